```python
import math
import jax, jax.numpy as jnp
from jax import lax
import numpy as np

D_MODEL = 1024
BATCH = 16
SEQ = 2048
DEPTH = 1
DEC_BATCH = 32
DEC_SEQ = 1
PAST_LEN = 16384
PAGE_SIZE = 128

D_MIX = D_MODEL
D_NSA = D_MIX // 2
D_CONV = D_MIX - D_NSA
HEAD_DIM = 64
N_HEADS = D_NSA // HEAD_DIM
N_KV = 2
GQ = N_HEADS // N_KV
CONV_W = 3
CMP_STRIDE = 16
CMP_RATIO = 2
CMP_LEN = CMP_STRIDE * CMP_RATIO
CMP_HIDDEN = 2 * HEAD_DIM
SLC_LEN = 64
N_SELECT = 16
WINDOW = 512
WIN_QBLOCK = 128
SLC_QBLOCK = 32
N_BUCKETS = 32
MAX_DISTANCE = 128
N_GROUPS = 4
N_EXP = 8
TOP_K = 2
D_FF = 256
ALPHA = (2 * DEPTH) ** 0.25
BETA = (8 * DEPTH) ** -0.25
NEG = -1e30
FORCE = 1e6
N_IN = D_NSA + 6 * N_KV * HEAD_DIM + 3 * N_HEADS + 3 * D_CONV

kernel_name = 'hymba_conv_nsa_hmoe_decode_step'


def _layernorm(x, g, b, eps=1e-5):
    xf = x.astype(jnp.float32)
    mu = jnp.mean(xf, -1, keepdims=True)
    var = jnp.mean(jnp.square(xf - mu), -1, keepdims=True)
    return ((xf - mu) * lax.rsqrt(var + eps)).astype(x.dtype) * g + b


def _rmsnorm(x, g, eps=1e-6):
    xf = x.astype(jnp.float32)
    return (xf * lax.rsqrt(jnp.mean(jnp.square(xf), -1, keepdims=True) + eps)).astype(x.dtype) * g


def _t5_bucket(dist):
    n = jnp.maximum(dist, 0)
    max_exact = N_BUCKETS // 2
    nf = jnp.maximum(n, 1).astype(jnp.float32)
    large = max_exact + (jnp.log(nf / max_exact) / math.log(MAX_DISTANCE / max_exact)
                         * (N_BUCKETS - max_exact)).astype(jnp.int32)
    large = jnp.minimum(large, N_BUCKETS - 1)
    return jnp.where(n < max_exact, n, large)


def _masked_softmax(s, mask):
    s = jnp.where(mask, s, NEG)
    return jax.nn.softmax(s, axis=-1) * mask.astype(jnp.float32)


def _attend(q, k, v, dist, mask, rel_bias):
    sq, sk = dist.shape
    bias = rel_bias[_t5_bucket(dist)].astype(jnp.float32)
    bias = bias.reshape(sq, sk, N_KV, GQ).transpose(2, 3, 0, 1)
    s = jnp.einsum('bqgrd,bkgd->bgrqk', q, k).astype(jnp.float32) + bias
    p = _masked_softmax(s, mask)
    o = jnp.einsum('bgrqk,bkgd->bqgrd', p.astype(v.dtype), v)
    return o, p


def _compress(k, pe, w1, w2):
    b, L, g, hd = k.shape
    n_chunks = L // CMP_STRIDE
    n_cmp = n_chunks - CMP_RATIO + 1
    ch = k[:, :n_chunks * CMP_STRIDE].reshape(b, n_chunks, CMP_STRIDE, g, hd)
    pe_r = pe.reshape(CMP_RATIO, CMP_STRIDE, hd)
    w1_r = w1.reshape(CMP_RATIO, CMP_STRIDE, hd, CMP_HIDDEN)
    h = sum(jnp.einsum('bcjgd,jde->bcge', ch[:, r:r + n_cmp] + pe_r[r][None, None, :, None, :], w1_r[r])
            for r in range(CMP_RATIO))
    return jnp.einsum('bcge,ed->bcgd', jax.nn.gelu(h), w2)


def _selected(q, ks, vs, idx, q_pos, rel_bias):
    b, sq, g, r, hd = q.shape
    L = ks.shape[1]
    n_slc = -(-L // SLC_LEN)
    pad = n_slc * SLC_LEN - L

    def blocks(t):
        t = jnp.pad(t, ((0, 0), (0, pad), (0, 0), (0, 0)))
        return t.reshape(b, n_slc, SLC_LEN, g, hd).transpose(0, 3, 1, 2, 4)

    kb, vb = blocks(ks), blocks(vs)
    n_sel = idx.shape[-1]
    qb = math.gcd(sq, SLC_QBLOCK)
    nb = sq // qb
    tbl = rel_bias.reshape(N_BUCKETS, g, r).transpose(1, 0, 2)
    gather = jax.vmap(jax.vmap(lambda blk, ix: blk[ix]))
    offs = jnp.arange(SLC_LEN, dtype=jnp.int32)

    def one(args):
        qq, ii, pp = args
        flat = ii.reshape(b, g, qb * n_sel)
        kg = gather(kb, flat).reshape(b, g, qb, n_sel * SLC_LEN, hd)
        vg = gather(vb, flat).reshape(b, g, qb, n_sel * SLC_LEN, hd)
        kpos = (ii[..., None] * SLC_LEN + offs).reshape(b, g, qb, n_sel * SLC_LEN)
        dist = pp[None, None, :, None] - kpos
        bias = jax.vmap(lambda t, bk: t[bk], in_axes=(0, 1), out_axes=1)(tbl, _t5_bucket(dist))
        s = jnp.einsum('bqgrd,bgqkd->bgrqk', qq, kg).astype(jnp.float32) \
            + bias.transpose(0, 1, 4, 2, 3).astype(jnp.float32)
        p = _masked_softmax(s, (dist >= 0)[:, :, None])
        return jnp.einsum('bgrqk,bgqkd->bqgrd', p.astype(vg.dtype), vg)

    qs = q.reshape(b, nb, qb, g, r, hd).transpose(1, 0, 2, 3, 4, 5)
    ids = idx.reshape(b, g, nb, qb, n_sel).transpose(2, 0, 1, 3, 4)
    ps = q_pos.reshape(nb, qb)
    o = lax.map(one, (qs, ids, ps))
    return o.transpose(1, 0, 2, 3, 4, 5).reshape(b, sq, g, r, hd)


def _nsa_global(q, kv, q_pos, pe_k, pe_v, w_ck1, w_ck2, w_cv1, w_cv2, rel_bias):
    L = kv.shape[1]
    kcmp = _compress(kv[:, :, 0], pe_k, w_ck1, w_ck2)
    vcmp = _compress(kv[:, :, 1], pe_v, w_cv1, w_cv2)
    n_cmp = kcmp.shape[1]
    cidx = jnp.arange(n_cmp, dtype=jnp.int32)
    ends = cidx * CMP_STRIDE + CMP_LEN - 1
    dist = q_pos[:, None] - ends[None, :]
    o_cmp, p_cmp = _attend(q, kcmp, vcmp, dist, dist >= 0, rel_bias)
    n_slc = -(-L // SLC_LEN)
    sidx = jnp.arange(n_slc, dtype=jnp.int32)
    c0 = cidx[:, None] * CMP_STRIDE
    s0 = sidx[None, :] * SLC_LEN
    shared_pos = jnp.maximum(jnp.minimum(c0 + CMP_LEN, s0 + SLC_LEN) - jnp.maximum(c0, s0), 0).astype(jnp.float32)
    imp = jnp.einsum('bgrqc,cs->bgqs', p_cmp, shared_pos)
    qblk = q_pos // SLC_LEN
    valid = sidx[None, :] <= qblk[:, None]
    forced = (sidx[None, :] == 0) | (sidx[None, :] == qblk[:, None]) | (sidx[None, :] == qblk[:, None] - 1)
    score = jnp.where(valid, imp + jnp.where(forced, FORCE, 0.0), -FORCE)
    _, idx = lax.top_k(score, min(N_SELECT, n_slc))
    o_slc = _selected(q, kv[:, :, 2], kv[:, :, 3], idx.astype(jnp.int32), q_pos, rel_bias)
    return o_cmp, o_slc


def _window_prompt(q, kw, vw, rel_bias):
    b, s = q.shape[:2]
    qb = math.gcd(s, WIN_QBLOCK)
    nb = s // qb
    span = WINDOW + qb

    def one(args):
        qq, pos = args
        kpos = pos[0] - WINDOW + jnp.arange(span, dtype=jnp.int32)
        idx = jnp.clip(kpos, 0, s - 1)
        dist = pos[:, None] - kpos[None, :]
        mask = (dist >= 0) & (dist < WINDOW) & (kpos >= 0)[None, :]
        o, _ = _attend(qq, kw[:, idx], vw[:, idx], dist, mask, rel_bias)
        return o

    qs = q.reshape(b, nb, qb, *q.shape[2:]).transpose(1, 0, 2, 3, 4, 5)
    ps = jnp.arange(s, dtype=jnp.int32).reshape(nb, qb)
    o = lax.map(one, (qs, ps))
    return o.transpose(1, 0, 2, 3, 4, 5).reshape(q.shape)


def _short_conv(u_ext, w):
    s = u_ext.shape[1] - (CONV_W - 1)
    return sum(u_ext[:, j:j + s] * w[j] for j in range(CONV_W))


def _hmoe(x, w_rg, b_rg, w_re, b_re, w_eg, w_eu, w_ed):
    lg = (x @ w_rg + b_rg).astype(jnp.float32)
    pg = jax.nn.softmax(lg, -1)
    oh_g = jax.nn.one_hot(jnp.argmax(lg, -1), N_GROUPS, dtype=jnp.float32)
    le = (x @ w_re + b_re).astype(jnp.float32).reshape(-1, N_GROUPS, N_EXP)
    pe = jax.nn.softmax(jnp.einsum('tg,tge->te', oh_g, le), -1)
    tv, ti = lax.top_k(pe, TOP_K)
    w = tv / jnp.sum(tv, -1, keepdims=True) * jnp.sum(pg * oh_g, -1, keepdims=True)
    gate_e = jnp.sum(jax.nn.one_hot(ti, N_EXP, dtype=jnp.float32) * w[..., None], 1)
    gate = (oh_g[:, :, None] * gate_e[:, None, :]).astype(x.dtype)
    out = jnp.zeros_like(x)
    for gi in range(N_GROUPS):
        h = jax.nn.silu(jnp.einsum('td,edf->tef', x, w_eg[gi])) * jnp.einsum('td,edf->tef', x, w_eu[gi])
        out = out + jnp.einsum('tef,efd->td', h * gate[:, gi, :, None], w_ed[gi])
    return out


def setup_inputs(seed: int = 0) -> dict:
    key = jax.random.key(seed)
    ks = jax.random.split(key, 32)
    f32 = jnp.float32

    def nrm(k, shape, scale):
        return jax.random.normal(k, shape, f32) * scale

    n_pages = PAST_LEN // PAGE_SIZE
    n_used = DEC_BATCH * n_pages
    n_phys = n_used + max(1, n_used // 4)
    win_buf = min(WINDOW, PAST_LEN)
    perm = jax.random.permutation(ks[0], n_phys)
    page_table = perm[:n_used].reshape(DEC_BATCH, n_pages).astype(jnp.int32)
    return {
        'x_prompt': nrm(ks[1], (BATCH, SEQ, D_MODEL), 1.0),
        'x_sample': nrm(ks[2], (DEC_BATCH, DEC_SEQ, D_MODEL), 1.0),
        'cache_kv': nrm(ks[3], (DEPTH, n_phys, PAGE_SIZE, 4, N_KV, HEAD_DIM), 1.0),
        'page_table': page_table,
        'cache_win': nrm(ks[4], (DEPTH, DEC_BATCH, win_buf, 2, N_KV, HEAD_DIM), 1.0),
        'state_conv': nrm(ks[5], (DEPTH, DEC_BATCH, CONV_W - 1, D_CONV), 1.0),
        'w_in': nrm(ks[6], (DEPTH, D_MODEL, N_IN), D_MODEL ** -0.5),
        'conv_w': nrm(ks[7], (DEPTH, CONV_W, D_CONV), CONV_W ** -0.5),
        'pe_k': nrm(ks[8], (DEPTH, CMP_LEN, HEAD_DIM), 0.1),
        'pe_v': nrm(ks[9], (DEPTH, CMP_LEN, HEAD_DIM), 0.1),
        'w_ck1': nrm(ks[10], (DEPTH, CMP_LEN, HEAD_DIM, CMP_HIDDEN), (CMP_LEN * HEAD_DIM) ** -0.5),
        'w_ck2': nrm(ks[11], (DEPTH, CMP_HIDDEN, HEAD_DIM), CMP_HIDDEN ** -0.5),
        'w_cv1': nrm(ks[12], (DEPTH, CMP_LEN, HEAD_DIM, CMP_HIDDEN), (CMP_LEN * HEAD_DIM) ** -0.5),
        'w_cv2': nrm(ks[13], (DEPTH, CMP_HIDDEN, HEAD_DIM), CMP_HIDDEN ** -0.5),
        'g_nsa': 1.0 + nrm(ks[14], (DEPTH, D_NSA), 0.05),
        'g_conv': 1.0 + nrm(ks[15], (DEPTH, D_CONV), 0.05),
        'w_out': nrm(ks[16], (DEPTH, D_MIX, D_MODEL), BETA * D_MIX ** -0.5),
        'ln1_g': 1.0 + nrm(ks[17], (DEPTH, D_MODEL), 0.05),
        'ln1_b': nrm(ks[18], (DEPTH, D_MODEL), 0.02),
        'w_rg': nrm(ks[19], (DEPTH, D_MODEL, N_GROUPS), D_MODEL ** -0.5),
        'b_rg': nrm(ks[20], (DEPTH, N_GROUPS), 0.01),
        'w_re': nrm(ks[21], (DEPTH, D_MODEL, N_GROUPS * N_EXP), D_MODEL ** -0.5),
        'b_re': nrm(ks[22], (DEPTH, N_GROUPS * N_EXP), 0.01),
        'w_eg': nrm(ks[23], (DEPTH, N_GROUPS, N_EXP, D_MODEL, D_FF), D_MODEL ** -0.5),
        'w_eu': nrm(ks[24], (DEPTH, N_GROUPS, N_EXP, D_MODEL, D_FF), D_MODEL ** -0.5),
        'w_ed': nrm(ks[25], (DEPTH, N_GROUPS, N_EXP, D_FF, D_MODEL), BETA * D_FF ** -0.5),
        'ln2_g': 1.0 + nrm(ks[26], (DEPTH, D_MODEL), 0.05),
        'ln2_b': nrm(ks[27], (DEPTH, D_MODEL), 0.02),
        'rel_bias': nrm(ks[28], (N_BUCKETS, N_HEADS), 0.2),
    }


def _split_cols(proj):
    sizes = [D_NSA] + [N_KV * HEAD_DIM] * 6 + [3 * N_HEADS] + [D_CONV] * 3
    offs = np.cumsum(sizes)[:-1].tolist()
    return jnp.split(proj, offs, axis=-1)


def reference(x_prompt, x_sample, cache_kv, page_table, cache_win, state_conv, w_in, conv_w, pe_k, pe_v,
              w_ck1, w_ck2, w_cv1, w_cv2, g_nsa, g_conv, w_out, ln1_g, ln1_b, w_rg, b_rg, w_re, b_re,
              w_eg, w_eu, w_ed, ln2_g, ln2_b, rel_bias):
    B, S, D = x_prompt.shape
    DB, DS = x_sample.shape[0], x_sample.shape[1]
    n_p, n_s = B * S, DB * DS
    n_pages = page_table.shape[1]
    past = n_pages * PAGE_SIZE
    win_buf = cache_win.shape[2]
    pos_p = jnp.arange(S, dtype=jnp.int32)
    pos_s = past + jnp.arange(DS, dtype=jnp.int32)
    xp, xs = x_prompt, x_sample
    kv_p_l, kv_s_l, win_p_l, win_s_l, conv_p_l, conv_s_l = [], [], [], [], [], []
    for l in range(DEPTH):
        X = jnp.concatenate([xp.reshape(n_p, D), xs.reshape(n_s, D)], axis=0)
        q, kc, vc, ksl, vsl, kw, vw, gt, cb, cc, chh = _split_cols(X @ w_in[l])
        q = q.reshape(-1, N_KV, GQ, HEAD_DIM) * HEAD_DIM ** -0.5
        kv_rows = jnp.stack([kc, vc, ksl, vsl], axis=1).reshape(-1, 4, N_KV, HEAD_DIM)
        win_rows = jnp.stack([kw, vw], axis=1).reshape(-1, 2, N_KV, HEAD_DIM)
        gates = jax.nn.sigmoid(gt).reshape(-1, N_KV, GQ, 3)
        u = cc * chh
        cmp_w = (pe_k[l], pe_v[l], w_ck1[l], w_ck2[l], w_cv1[l], w_cv2[l])

        q_p = q[:n_p].reshape(B, S, N_KV, GQ, HEAD_DIM)
        kv_p = kv_rows[:n_p].reshape(B, S, 4, N_KV, HEAD_DIM)
        win_p = win_rows[:n_p].reshape(B, S, 2, N_KV, HEAD_DIM)
        o_cmp_p, o_slc_p = _nsa_global(q_p, kv_p, pos_p, *cmp_w, rel_bias)
        o_win_p = _window_prompt(q_p, win_p[:, :, 0], win_p[:, :, 1], rel_bias)
        g_p = gates[:n_p].reshape(B, S, N_KV, GQ, 3)
        o_p = (g_p[..., 0:1] * o_cmp_p + g_p[..., 1:2] * o_slc_p + g_p[..., 2:3] * o_win_p).reshape(n_p, D_NSA)
        u_p_ext = jnp.pad(u[:n_p].reshape(B, S, D_CONV), ((0, 0), (CONV_W - 1, 0), (0, 0)))
        z_p = cb[:n_p].reshape(B, S, D_CONV) * _short_conv(u_p_ext, conv_w[l])

        q_s = q[n_p:].reshape(DB, DS, N_KV, GQ, HEAD_DIM)
        kv_s = kv_rows[n_p:].reshape(DB, DS, 4, N_KV, HEAD_DIM)
        win_s = win_rows[n_p:].reshape(DB, DS, 2, N_KV, HEAD_DIM)
        past_kv = cache_kv[l][page_table].reshape(DB, past, 4, N_KV, HEAD_DIM)
        kv_all = jnp.concatenate([past_kv, kv_s], axis=1)
        o_cmp_s, o_slc_s = _nsa_global(q_s, kv_all, pos_s, *cmp_w, rel_bias)
        win_all = jnp.concatenate([cache_win[l], win_s], axis=1)
        kpos_w = past - win_buf + jnp.arange(win_buf + DS, dtype=jnp.int32)
        dist_w = pos_s[:, None] - kpos_w[None, :]
        o_win_s, _ = _attend(q_s, win_all[:, :, 0], win_all[:, :, 1], dist_w,
                             (dist_w >= 0) & (dist_w < WINDOW), rel_bias)
        g_s = gates[n_p:].reshape(DB, DS, N_KV, GQ, 3)
        o_s = (g_s[..., 0:1] * o_cmp_s + g_s[..., 1:2] * o_slc_s + g_s[..., 2:3] * o_win_s).reshape(n_s, D_NSA)
        u_s_ext = jnp.concatenate([state_conv[l], u[n_p:].reshape(DB, DS, D_CONV)], axis=1)
        z_s = cb[n_p:].reshape(DB, DS, D_CONV) * _short_conv(u_s_ext, conv_w[l])

        o_nsa = jnp.concatenate([o_p, o_s], axis=0)
        o_conv = jnp.concatenate([z_p.reshape(n_p, D_CONV), z_s.reshape(n_s, D_CONV)], axis=0)
        mix = jnp.concatenate([_rmsnorm(o_nsa, g_nsa[l]), _rmsnorm(o_conv, g_conv[l])], axis=-1) @ w_out[l]
        X1 = _layernorm(ALPHA * X + mix, ln1_g[l], ln1_b[l])
        X2 = _layernorm(ALPHA * X1 + _hmoe(X1, w_rg[l], b_rg[l], w_re[l], b_re[l], w_eg[l], w_eu[l], w_ed[l]),
                        ln2_g[l], ln2_b[l])
        xp = X2[:n_p].reshape(B, S, D)
        xs = X2[n_p:].reshape(DB, DS, D)

        kv_p_l.append(kv_p)
        kv_s_l.append(kv_s)
        win_p_l.append(win_p[:, S - min(WINDOW, S):])
        win_s_l.append(win_all[:, win_all.shape[1] - min(WINDOW, past + DS):])
        conv_p_l.append(u_p_ext[:, u_p_ext.shape[1] - (CONV_W - 1):])
        conv_s_l.append(u_s_ext[:, u_s_ext.shape[1] - (CONV_W - 1):])
    return (xp, xs, jnp.stack(kv_p_l), jnp.stack(kv_s_l), jnp.stack(win_p_l), jnp.stack(win_s_l),
            jnp.stack(conv_p_l), jnp.stack(conv_s_l))
```

```python
import functools
import math

import numpy as np
import jax
import jax.numpy as jnp
from jax import lax
from jax.experimental import pallas as pl
from jax.experimental.pallas import tpu as pltpu

F32 = jnp.float32
BF16 = jnp.bfloat16

HEAD_DIM = 64
N_KV = 2
GQ = 4
N_HEADS = N_KV * GQ
CONV_W = 3
CMP_STRIDE = 16
CMP_LEN = 32
CMP_HIDDEN = 2 * HEAD_DIM
SLC_LEN = 64
N_SELECT = 16
WINDOW = 512
N_BUCKETS = 32
MAX_DISTANCE = 128
N_GROUPS = 4
N_EXP = 8
PAGE_SIZE = 128
NEG = -1e30
FORCE = 1e6

LANE = 128
SUBLANE = 8
VMEM_LIMIT = 52 * 1024 * 1024

TQ = 128
TM = 512
SAMPLE_PAGES = 32


def _cparams(n_axes):
    return pltpu.CompilerParams(dimension_semantics=("arbitrary",) * n_axes, vmem_limit_bytes=VMEM_LIMIT)


def _const_spec(shape):
    nd = len(shape)
    return pl.BlockSpec(shape, lambda *_, nd=nd: (0,) * nd)


def _qk(a, b):
    return lax.dot_general(a, b, (((1,), (1,)), ((), ())), preferred_element_type=F32)


def _mm(a, b):
    return jnp.dot(a, b, preferred_element_type=F32)


def _bucket_np(dist):
    n = np.maximum(dist, 0)
    max_exact = N_BUCKETS // 2
    nf = np.maximum(n, 1).astype(np.float32)
    large = max_exact + (np.log(nf / np.float32(max_exact)) / np.float32(math.log(MAX_DISTANCE / max_exact))
                         * np.float32(N_BUCKETS - max_exact)).astype(np.int32)
    large = np.minimum(large, N_BUCKETS - 1)
    return np.where(n < max_exact, n, large).astype(np.int32)


def _overlap_np(c, s):
    c0 = c * CMP_STRIDE
    s0 = s * SLC_LEN
    return np.maximum(np.minimum(c0 + CMP_LEN, s0 + SLC_LEN) - np.maximum(c0, s0), 0)


def _static_tables(seq, past, win_buf):
    nq = seq // TQ
    i = np.arange(LANE)[:, None]
    j = np.arange(LANE)[None, :]
    tiles = [_bucket_np(LANE * d + i - j) for d in range(2)]
    tiles += [_bucket_np(TQ * q + i - (CMP_STRIDE * j + CMP_STRIDE - 1)) for q in range(nq)]
    smp = np.full((LANE, LANE), 10 * MAX_DISTANCE, np.int64)
    k = np.arange(5 * LANE)
    smp[0:5] = np.where(k < win_buf, win_buf - k, 0).reshape(5, LANE)
    ci = np.arange(8 * LANE)
    smp[5:13] = np.maximum(past - (CMP_STRIDE * ci + CMP_STRIDE - 1), 0).reshape(8, LANE)
    nb = past // SLC_LEN
    smp[13] = past - (SLC_LEN * (nb - 2) + np.arange(LANE))
    tiles.append(_bucket_np(smp))
    idx = np.stack(tiles).astype(np.int32)

    def sp(nrows, ncols, n_slc):
        r = np.arange(nrows)[:, None]
        s = np.arange(ncols)[None, :]
        ov = _overlap_np(r - 1, s)
        return np.where((r >= 1) & (s < n_slc), ov, 0).astype(np.float32)

    sp_p = sp(seq // CMP_STRIDE, LANE, -(-seq // SLC_LEN))
    n_slc_s = -(-(past + 1) // SLC_LEN)
    sp_s = sp(past // CMP_STRIDE, 3 * LANE, n_slc_s)
    c = np.arange(seq // LANE)[:, None, None]
    s = np.arange(LANE)[None, :, None]
    kk = np.arange(LANE)[None, None, :]
    expand = (s == 2 * c + kk // SLC_LEN).astype(np.float32)
    return idx, sp_p, sp_s, expand


def _bias_kernel(rb_ref, idx_ref, out_ref):
    idx = idx_ref[0]
    accs = [jnp.zeros((LANE, LANE), F32) for _ in range(N_HEADS)]
    for b in range(N_BUCKETS):
        hit = idx == b
        for h in range(N_HEADS):
            accs[h] = jnp.where(hit, rb_ref[b, h], accs[h])
    for h in range(N_HEADS):
        out_ref[0, h] = accs[h]


def _bias_tables(rel_bias, idx):
    nt = idx.shape[0]
    return pl.pallas_call(
        _bias_kernel,
        grid=(nt,),
        in_specs=[pl.BlockSpec(memory_space=pltpu.SMEM),
                  pl.BlockSpec((1, LANE, LANE), lambda t: (t, 0, 0))],
        out_specs=pl.BlockSpec((1, N_HEADS, LANE, LANE), lambda t: (t, 0, 0, 0)),
        out_shape=jax.ShapeDtypeStruct((nt, N_HEADS, LANE, LANE), F32),
        compiler_params=_cparams(1),
        name="bias_tables",
    )(rel_bias, idx)


_Q0, _Q1 = 0, N_HEADS * LANE
_KV0, _KV1 = _Q1, _Q1 + 512
_WN0, _WN1 = _KV1, _KV1 + 256
_GT0, _GT1 = _WN1, _WN1 + 2 * LANE
_CV0, _CV1 = _GT1, _GT1 + 3 * 512


def _pack_in_weights(w):
    d = w.shape[0]
    d_nsa = N_HEADS * HEAD_DIM
    wq = w[:, :d_nsa].reshape(d, N_HEADS, HEAD_DIM)
    z = jnp.zeros_like(wq)
    lo = jnp.concatenate([wq, z], axis=-1)
    hi = jnp.concatenate([z, wq], axis=-1)
    grp = (jnp.arange(N_HEADS) // GQ)[None, :, None]
    wq_pad = jnp.where(grp == 0, lo, hi).reshape(d, N_HEADS * LANE)
    o = d_nsa
    w_kv = w[:, o:o + 512]
    w_win = w[:, o + 512:o + 768]
    wg = w[:, o + 768:o + 768 + 3 * N_HEADS].reshape(d, N_KV, 3 * GQ)
    wg_pad = jnp.pad(wg, ((0, 0), (0, 0), (0, LANE - 3 * GQ))).reshape(d, N_KV * LANE)
    w_conv = w[:, o + 768 + 3 * N_HEADS:]
    return jnp.concatenate([wq_pad, w_kv, w_win, wg_pad, w_conv], axis=1).astype(BF16)


def _rms(x, gain, n, eps=1e-6):
    ms = jnp.sum(x * x, axis=-1, keepdims=True) / n
    return x * lax.rsqrt(ms + eps) * gain


def _layernorm(y, gain, bias, eps=1e-5):
    mu = jnp.mean(y, axis=-1, keepdims=True)
    d = y - mu
    var = jnp.mean(d * d, axis=-1, keepdims=True)
    return d * lax.rsqrt(var + eps) * gain + bias


def _inproj_prompt_kernel(x_ref, w_ref, cw_ref, gc_ref,
                          q_ref, kv_ref, sk_ref, sv_ref, wk_ref, wv_ref, wtail_ref, gate_ref, zn_ref, ctail_ref,
                          uext_ref, *, tiles_per_batch):
    i = pl.program_id(0)
    tm = x_ref.shape[0]
    xb = x_ref[...].astype(BF16)

    def seg(a, b):
        return _mm(xb, w_ref[:, a:b])

    qp = seg(_Q0, _Q1) * (HEAD_DIM ** -0.5)
    for h in range(N_HEADS):
        q_ref[h] = qp[:, h * LANE:(h + 1) * LANE].astype(BF16)
    kv = seg(_KV0, _KV1)
    kv_ref[...] = kv
    sk_ref[...] = kv[:, 256:384].astype(BF16)
    sv_ref[...] = kv[:, 384:512].astype(BF16)
    win = seg(_WN0, _WN1)
    wk_ref[...] = win[:, 0:LANE].astype(BF16)
    wv_ref[...] = win[:, LANE:2 * LANE].astype(BF16)
    wtail_ref[0] = win
    gt = jax.nn.sigmoid(seg(_GT0, _GT1))
    gate_ref[0] = gt[:, 0:LANE]
    gate_ref[1] = gt[:, LANE:2 * LANE]

    conv = seg(_CV0, _CV1)
    cb = conv[:, 0:512]
    u = conv[:, 512:1024] * conv[:, 1024:1536]
    first = (i % tiles_per_batch) == 0

    @pl.when(first)
    def _():
        uext_ref[0:SUBLANE, :] = jnp.zeros((SUBLANE, 512), F32)

    @pl.when(jnp.logical_not(first))
    def _():
        uext_ref[0:SUBLANE, :] = uext_ref[tm:tm + SUBLANE, :]

    uext_ref[SUBLANE:tm + SUBLANE, :] = u
    um1 = uext_ref[SUBLANE - 1:tm + SUBLANE - 1, :]
    um2 = uext_ref[SUBLANE - 2:tm + SUBLANE - 2, :]
    z = cb * (um2 * cw_ref[0:1, :] + um1 * cw_ref[1:2, :] + u * cw_ref[2:3, :])
    zn_ref[...] = _rms(z, gc_ref[...], 512).astype(BF16)
    ctail_ref[0] = uext_ref[tm + SUBLANE - 2:tm + SUBLANE, :]


def _inproj_prompt(x, w, conv_w, g_conv, batch, seq):
    t, d = x.shape
    n_tiles = t // TM
    tpb = seq // TM
    outs = (
        jax.ShapeDtypeStruct((N_HEADS, t, LANE), BF16),
        jax.ShapeDtypeStruct((t, 512), F32),
        jax.ShapeDtypeStruct((t, LANE), BF16),
        jax.ShapeDtypeStruct((t, LANE), BF16),
        jax.ShapeDtypeStruct((t, LANE), BF16),
        jax.ShapeDtypeStruct((t, LANE), BF16),
        jax.ShapeDtypeStruct((batch, TM, 256), F32),
        jax.ShapeDtypeStruct((N_KV, t, LANE), F32),
        jax.ShapeDtypeStruct((t, 512), BF16),
        jax.ShapeDtypeStruct((batch, CONV_W - 1, 512), F32),
    )
    row = lambda n: pl.BlockSpec((TM, n), lambda i: (i, 0))
    return pl.pallas_call(
        functools.partial(_inproj_prompt_kernel, tiles_per_batch=tpb),
        grid=(n_tiles,),
        in_specs=[row(d), _const_spec(w.shape), _const_spec(conv_w.shape), _const_spec(g_conv.shape)],
        out_specs=(
            pl.BlockSpec((N_HEADS, TM, LANE), lambda i: (0, i, 0)),
            row(512), row(LANE), row(LANE), row(LANE), row(LANE),
            pl.BlockSpec((1, TM, 256), lambda i: (i // tpb, 0, 0)),
            pl.BlockSpec((N_KV, TM, LANE), lambda i: (0, i, 0)),
            row(512),
            pl.BlockSpec((1, CONV_W - 1, 512), lambda i: (i // tpb, 0, 0)),
        ),
        out_shape=outs,
        scratch_shapes=[pltpu.VMEM((TM + SUBLANE, 512), F32)],
        compiler_params=_cparams(1),
        name="inproj_prompt",
    )(x, w, conv_w, g_conv)


def _inproj_sample_kernel(x_ref, w_ref, cw_ref, gc_ref, s0_ref, s1_ref,
                          q_ref, kv_ref, win_ref, gate_ref, zn_ref, u_ref):
    xb = x_ref[...].astype(BF16)

    def seg(a, b):
        return _mm(xb, w_ref[:, a:b])

    q_ref[...] = (seg(_Q0, _Q1) * (HEAD_DIM ** -0.5)).astype(BF16)
    kv_ref[...] = seg(_KV0, _KV1)
    win_ref[...] = seg(_WN0, _WN1)
    gate_ref[...] = jax.nn.sigmoid(seg(_GT0, _GT1))
    conv = seg(_CV0, _CV1)
    cb = conv[:, 0:512]
    u = conv[:, 512:1024] * conv[:, 1024:1536]
    z = cb * (s0_ref[...] * cw_ref[0:1, :] + s1_ref[...] * cw_ref[1:2, :] + u * cw_ref[2:3, :])
    zn_ref[...] = _rms(z, gc_ref[...], 512).astype(BF16)
    u_ref[...] = u


def _inproj_sample(x, w, conv_w, g_conv, s0, s1):
    n = x.shape[0]
    outs = (
        jax.ShapeDtypeStruct((n, N_HEADS * LANE), BF16),
        jax.ShapeDtypeStruct((n, 512), F32),
        jax.ShapeDtypeStruct((n, 256), F32),
        jax.ShapeDtypeStruct((n, N_KV * LANE), F32),
        jax.ShapeDtypeStruct((n, 512), BF16),
        jax.ShapeDtypeStruct((n, 512), F32),
    )
    args = (x, w, conv_w, g_conv, s0, s1)
    return pl.pallas_call(
        _inproj_sample_kernel,
        grid=(1,),
        in_specs=[_const_spec(a.shape) for a in args],
        out_specs=tuple(_const_spec(o.shape) for o in outs),
        out_shape=outs,
        compiler_params=_cparams(1),
        name="inproj_sample",
    )(*args)


_CHUNK_LANES = CMP_STRIDE * 512


def _pack_compress_weights(pe, w1, w2):
    w = w1.reshape(2, CMP_STRIDE, HEAD_DIM, CMP_HIDDEN).transpose(1, 2, 0, 3).reshape(CMP_STRIDE, HEAD_DIM, 2 * CMP_HIDDEN)
    z = jnp.zeros_like(w)
    top = jnp.concatenate([w, z], axis=-1)
    bot = jnp.concatenate([z, w], axis=-1)
    w1bd = jnp.stack([top, bot], axis=1).reshape(CMP_STRIDE * 2 * HEAD_DIM, 4 * CMP_HIDDEN).astype(BF16)
    z2 = jnp.zeros_like(w2)
    w2bd = jnp.concatenate([jnp.concatenate([w2, z2], axis=1), jnp.concatenate([z2, w2], axis=1)], axis=0).astype(BF16)
    pe_rows = jnp.broadcast_to(pe.reshape(2, CMP_STRIDE, 1, HEAD_DIM), (2, CMP_STRIDE, N_KV, HEAD_DIM))
    pe_rows = pe_rows.reshape(2, CMP_STRIDE * N_KV * HEAD_DIM)
    pe_rows = jnp.pad(pe_rows, ((0, SUBLANE - 2), (0, 0)))
    return w1bd, w2bd, pe_rows


def _compress_kernel(*refs, n_pages, n_prefetch):
    refs = refs[n_prefetch:]
    page_refs = refs[:n_pages]
    w1_ref, w2_ref, pe_ref, out_ref, carry_ref = refs[n_pages:]
    step = pl.program_id(1)
    rows = page_refs[0].shape[1]
    m = n_pages * rows

    @pl.when(step == 0)
    def _():
        carry_ref[...] = jnp.zeros(carry_ref.shape, F32)

    row0 = lax.broadcasted_iota(jnp.int32, (m, CMP_HIDDEN), 0) == 0
    outs = []
    for t in range(2):
        pieces = [
            jnp.concatenate([pr[0, :, j * 512 + t * LANE:j * 512 + (t + 1) * LANE] for j in range(CMP_STRIDE)], axis=1)
            for pr in page_refs
        ]
        lhs = jnp.concatenate(pieces + [pe_ref[t]], axis=0).astype(BF16)
        a = _mm(lhs, w1_ref[t])
        hs = []
        for g in range(N_KV):
            c0 = g * 2 * CMP_HIDDEN
            a0 = a[0:m, c0:c0 + CMP_HIDDEN]
            a1 = a[0:m, c0 + CMP_HIDDEN:c0 + 2 * CMP_HIDDEN]
            pe_term = a[m:m + 1, c0:c0 + CMP_HIDDEN] + a[m + 1:m + 2, c0 + CMP_HIDDEN:c0 + 2 * CMP_HIDDEN]
            prev = carry_ref[t, 0:1, g * CMP_HIDDEN:(g + 1) * CMP_HIDDEN]
            shifted = jnp.where(row0, prev, pltpu.roll(a0, 1, axis=0))
            carry_ref[t, 0:1, g * CMP_HIDDEN:(g + 1) * CMP_HIDDEN] = a0[m - 1:m, :]
            hs.append(jax.nn.gelu(shifted + a1 + pe_term))
        outs.append(_mm(jnp.concatenate(hs, axis=1).astype(BF16), w2_ref[t]))
    out_ref[0] = jnp.concatenate(outs, axis=1).astype(BF16)


def _compress_call(page_specs, page_args, prefetch, grid, batch, n_chunks, m, w1, w2, pe, name):
    n_pages = len(page_specs)
    n_pf = len(prefetch)
    cspec = lambda shape: pl.BlockSpec(shape, lambda *_: (0,) * len(shape))
    grid_spec = pltpu.PrefetchScalarGridSpec(
        num_scalar_prefetch=n_pf,
        grid=grid,
        in_specs=list(page_specs) + [cspec(w1.shape), cspec(w2.shape), cspec(pe.shape)],
        out_specs=pl.BlockSpec((1, m, 256), lambda b, s, *_: (b, s, 0)),
        scratch_shapes=[pltpu.VMEM((2, SUBLANE, 2 * CMP_HIDDEN), F32)],
    )
    return pl.pallas_call(
        functools.partial(_compress_kernel, n_pages=n_pages, n_prefetch=n_pf),
        grid_spec=grid_spec,
        out_shape=jax.ShapeDtypeStruct((batch, n_chunks, 256), BF16),
        compiler_params=_cparams(2),
        name=name,
    )(*prefetch, *page_args, w1, w2, pe)


def _compress_prompt(kv, batch, seq, w1, w2, pe):
    n_chunks = seq // CMP_STRIDE
    view = kv.reshape(batch, n_chunks, _CHUNK_LANES)
    spec = pl.BlockSpec((1, n_chunks, _CHUNK_LANES), lambda b, s: (b, 0, 0))
    return _compress_call([spec], [view], (), (batch, 1), batch, n_chunks, n_chunks, w1, w2, pe, "compress_prompt")


def _compress_sample(cache, page_table, w1, w2, pe):
    n_phys = cache.shape[0]
    batch, n_pages = page_table.shape
    rows = PAGE_SIZE // CMP_STRIDE
    view = cache.reshape(n_phys, rows, _CHUNK_LANES)
    specs = [pl.BlockSpec((1, rows, _CHUNK_LANES), lambda b, s, pt, k=k: (pt[b, s * SAMPLE_PAGES + k], 0, 0))
             for k in range(SAMPLE_PAGES)]
    return _compress_call(specs, [view] * SAMPLE_PAGES, (page_table,), (batch, n_pages // SAMPLE_PAGES), batch,
                          n_pages * rows, SAMPLE_PAGES * rows, w1, w2, pe, "compress_sample")


def _tile4(x):
    return jnp.concatenate([x] * GQ, axis=0)


def _attn_prompt_kernel(c31_ref, q_ref, gate_ref, cmp_ref, sk_ref, sv_ref, wk_ref, wv_ref,
                        tabt_ref, tabc_ref, sp_ref, ex_ref, gn_ref, o_ref, ocat_ref):
    qi = pl.program_id(1)
    q0 = qi * TQ
    row = lax.broadcasted_iota(jnp.int32, (TQ, LANE), 0)
    col = lax.broadcasted_iota(jnp.int32, (TQ, LANE), 1)
    qpos = q0 + row
    tri = row >= col
    upper = col > row
    cmp_ok4 = _tile4((col >= 1) & (CMP_STRIDE * col + (CMP_STRIDE - 1) <= qpos))
    sidx = lax.broadcasted_iota(jnp.int32, (N_BUCKETS, TQ), 0)
    qblk = lax.shift_right_logical(q0 + lax.broadcasted_iota(jnp.int32, (N_BUCKETS, TQ), 1), 6)
    sel_valid = sidx <= qblk
    sel_forced = (sidx == 0) | (sidx == qblk) | (sidx == qblk - 1)

    def per_group(g, carry):
        heads = [GQ * g + r for r in range(GQ)]
        qs = jnp.concatenate([q_ref[h] for h in heads], axis=0)
        c31 = jnp.concatenate([jnp.full((TQ, 1), c31_ref[h], F32) for h in heads], axis=0)
        t0 = jnp.concatenate([tabt_ref[h, 0] for h in heads], axis=0)
        t1 = jnp.concatenate([tabt_ref[h, 1] for h in heads], axis=0)

        kc = cmp_ref[0, :, 0:LANE]
        vc = cmp_ref[0, :, LANE:2 * LANE]
        bias_c = jnp.concatenate([tabc_ref[0, h] for h in heads], axis=0)
        s = jnp.where(cmp_ok4, _qk(qs, kc) + bias_c, NEG)
        e = jnp.exp(s - jnp.max(s, axis=1, keepdims=True))
        p = (e / jnp.sum(e, axis=1, keepdims=True) * cmp_ok4.astype(F32)).astype(BF16)
        o_cmp = _mm(p, vc)
        imp4 = _mm(p, sp_ref[...])
        imp = imp4[0:TQ] + imp4[TQ:2 * TQ] + imp4[2 * TQ:3 * TQ] + imp4[3 * TQ:4 * TQ]

        imp_t = imp.T[0:N_BUCKETS, :]
        score = jnp.where(sel_valid, imp_t + jnp.where(sel_forced, FORCE, 0.0), -FORCE)
        rank = jnp.zeros((N_BUCKETS, TQ), F32)
        for other in range(N_BUCKETS):
            rw = score[other:other + 1, :]
            rank = rank + jnp.where(sidx > other, (rw >= score).astype(F32), (rw > score).astype(F32))
        sel_t = (rank < float(N_SELECT)).astype(F32)
        sel = jnp.concatenate([sel_t, jnp.zeros((LANE - N_BUCKETS, TQ), F32)], axis=0).T.astype(BF16)

        def flash_tile(c, state, bias, extra):
            m, l, acc = state
            start = pl.multiple_of(c * LANE, LANE)
            k = sk_ref[0, pl.ds(start, LANE), :]
            v = sv_ref[0, pl.ds(start, LANE), :]
            ok = _mm(sel, ex_ref[c]) > 0.5
            if extra is not None:
                ok = ok & extra
            ok4 = _tile4(ok)
            sc = jnp.where(ok4, _qk(qs, k) + bias, NEG)
            m_new = jnp.maximum(m, jnp.max(sc, axis=1, keepdims=True))
            alpha = jnp.exp(m - m_new)
            ee = jnp.where(ok4, jnp.exp(sc - m_new), 0.0)
            l = alpha * l + jnp.sum(ee, axis=1, keepdims=True)
            acc = alpha * acc + _mm(ee.astype(BF16), v)
            return m_new, l, acc

        state = (jnp.full((GQ * TQ, 1), NEG, F32), jnp.zeros((GQ * TQ, 1), F32), jnp.zeros((GQ * TQ, LANE), F32))
        state = lax.fori_loop(0, jnp.maximum(qi - 1, 0), lambda c, st: flash_tile(c, st, c31, None), state)
        state = flash_tile(jnp.maximum(qi - 1, 0), state, t1, qpos >= TQ)
        m, l, acc = flash_tile(qi, state, t0, tri)
        o_slc = acc / l

        n_w = WINDOW // LANE + 1
        scores, vals, oks = [], [], []
        for c in range(n_w):
            in_range = (qpos - row) >= (WINDOW - LANE * c)
            start = pl.multiple_of(jnp.maximum(q0 - WINDOW + LANE * c, 0), LANE)
            k = wk_ref[0, pl.ds(start, LANE), :]
            vals.append(wv_ref[0, pl.ds(start, LANE), :])
            if c == n_w - 1:
                bias, ok = t0, tri
            elif c == n_w - 2:
                bias, ok = t1, in_range
            elif c == 0:
                bias, ok = c31, upper & in_range
            else:
                bias, ok = c31, in_range
            ok4 = _tile4(ok)
            oks.append(ok4)
            scores.append(jnp.where(ok4, _qk(qs, k) + bias, NEG))
        m = scores[0].max(axis=1, keepdims=True)
        for sc in scores[1:]:
            m = jnp.maximum(m, sc.max(axis=1, keepdims=True))
        l = jnp.zeros((GQ * TQ, 1), F32)
        acc = jnp.zeros((GQ * TQ, LANE), F32)
        for sc, v, ok4 in zip(scores, vals, oks):
            ee = jnp.where(ok4, jnp.exp(sc - m), 0.0)
            l = l + jnp.sum(ee, axis=1, keepdims=True)
            acc = acc + _mm(ee.astype(BF16), v)
        o_win = acc / l

        gt = gate_ref[g]
        own = lax.shift_right_logical(col, 6) == g
        for r in range(GQ):
            rs = slice(r * TQ, (r + 1) * TQ)
            comb = gt[:, 3 * r:3 * r + 1] * o_cmp[rs] + gt[:, 3 * r + 1:3 * r + 2] * o_slc[rs] \
                + gt[:, 3 * r + 2:3 * r + 3] * o_win[rs]
            ocat_ref[GQ * g + r] = jnp.where(own, comb, 0.0)
        return carry

    lax.fori_loop(0, N_KV, per_group, 0)
    o = jnp.concatenate([ocat_ref[h] for h in range(N_HEADS)], axis=1)
    o_ref[...] = _rms(o, gn_ref[...], N_HEADS * HEAD_DIM).astype(BF16)


def _attn_prompt(c31, q, gates, cmpkv, sk, sv, wk, wv, tab_t, tab_c, sp, expand, gn_pad, batch, seq):
    nq = seq // TQ
    t = batch * seq
    per_b = lambda b, i: (b, 0, 0)
    return pl.pallas_call(
        _attn_prompt_kernel,
        grid=(batch, nq),
        in_specs=[
            pl.BlockSpec(memory_space=pltpu.SMEM),
            pl.BlockSpec((N_HEADS, TQ, LANE), lambda b, i: (0, b * nq + i, 0)),
            pl.BlockSpec((N_KV, TQ, LANE), lambda b, i: (0, b * nq + i, 0)),
            pl.BlockSpec((1, seq // CMP_STRIDE, 256), per_b),
            pl.BlockSpec((1, seq, LANE), per_b),
            pl.BlockSpec((1, seq, LANE), per_b),
            pl.BlockSpec((1, seq, LANE), per_b),
            pl.BlockSpec((1, seq, LANE), per_b),
            _const_spec(tab_t.shape),
            pl.BlockSpec((1, N_HEADS, LANE, LANE), lambda b, i: (i, 0, 0, 0)),
            _const_spec(sp.shape),
            _const_spec(expand.shape),
            _const_spec(gn_pad.shape),
        ],
        out_specs=pl.BlockSpec((TQ, N_HEADS * LANE), lambda b, i: (b * nq + i, 0)),
        out_shape=jax.ShapeDtypeStruct((t, N_HEADS * LANE), BF16),
        scratch_shapes=[pltpu.VMEM((N_HEADS, TQ, LANE), F32)],
        compiler_params=_cparams(2),
        name="attn_prompt",
    )(c31, q, gates, cmpkv, sk.reshape(batch, seq, LANE), sv.reshape(batch, seq, LANE),
      wk.reshape(batch, seq, LANE), wv.reshape(batch, seq, LANE), tab_t, tab_c, sp, expand, gn_pad)


def _sample_cw_kernel(q_ref, cmp_ref, cwin_ref, wnew_ref, tabc_ref, tabw_ref, sp_ref,
                      ocmp_ref, owin_ref, idx_ref, *, n_slc, qblk, win_buf):
    qs = q_ref[0]
    n_cmp_rows = cmp_ref.shape[1]
    col = lax.broadcasted_iota(jnp.int32, (N_HEADS, n_cmp_rows), 1)
    ok = col >= 1
    s = jnp.where(ok, _qk(qs, cmp_ref[0, :, 0:LANE]) + tabc_ref[...], NEG)
    e = jnp.exp(s - jnp.max(s, axis=1, keepdims=True))
    p = (e / jnp.sum(e, axis=1, keepdims=True) * ok.astype(F32)).astype(BF16)
    ocmp_ref[0] = _mm(p, cmp_ref[0, :, LANE:2 * LANE])
    imp8 = _mm(p, sp_ref[...])

    width = imp8.shape[1]
    head_grp = lax.shift_right_logical(lax.broadcasted_iota(jnp.int32, (N_HEADS, width), 0), 2)
    sidx = lax.broadcasted_iota(jnp.int32, (1, width), 1)
    sidx_f = sidx.astype(F32)
    forced = (sidx == 0) | (sidx == qblk) | (sidx == qblk - 1)
    lane = lax.broadcasted_iota(jnp.int32, (1, LANE), 1)
    for g in range(N_KV):
        imp = jnp.sum(jnp.where(head_grp == g, imp8, 0.0), axis=0, keepdims=True)
        score = jnp.where(sidx <= qblk, imp + jnp.where(forced, FORCE, 0.0), -FORCE)
        score = jnp.where(sidx < n_slc, score, -jnp.inf)
        picked = jnp.zeros((1, LANE), F32)
        for k in range(N_SELECT):
            best = jnp.max(score, axis=1, keepdims=True)
            ik = jnp.min(jnp.where(score == best, sidx_f, float(width)), axis=1, keepdims=True)
            picked = jnp.where(lane == k, ik, picked)
            score = jnp.where(sidx_f == ik, -jnp.inf, score)
        idx_ref[0, g:g + 1, :] = picked.astype(jnp.int32)

    n_keys = tabw_ref.shape[1]
    first_row = lax.broadcasted_iota(jnp.int32, (SUBLANE, LANE), 0) == 0
    pad = jnp.zeros((n_keys - win_buf - SUBLANE, LANE), F32)

    def keys(c0):
        new_tile = jnp.where(first_row, wnew_ref[0, :, c0:c0 + LANE], 0.0)
        return jnp.concatenate([cwin_ref[0, :, c0:c0 + LANE], new_tile, pad], axis=0).astype(BF16)

    colw = lax.broadcasted_iota(jnp.int32, (N_HEADS, n_keys), 1)
    okw = (colw > win_buf - WINDOW) & (colw <= win_buf)
    sw = jnp.where(okw, _qk(qs, keys(0)) + tabw_ref[...], NEG)
    ew = jnp.where(okw, jnp.exp(sw - jnp.max(sw, axis=1, keepdims=True)), 0.0)
    owin_ref[0] = _mm(ew.astype(BF16), keys(LANE)) / jnp.sum(ew, axis=1, keepdims=True)


def _sample_cw(q, cmpkv, cwin, wnew, tab_c, tab_w, sp, n_slc, qblk):
    n = q.shape[0]
    win_buf = cwin.shape[1]
    blk = lambda a: pl.BlockSpec((1,) + a.shape[1:], lambda b: (b,) + (0,) * (a.ndim - 1))
    outs = (jax.ShapeDtypeStruct((n, N_HEADS, LANE), F32), jax.ShapeDtypeStruct((n, N_HEADS, LANE), F32),
            jax.ShapeDtypeStruct((n, N_KV, LANE), jnp.int32))
    return pl.pallas_call(
        functools.partial(_sample_cw_kernel, n_slc=n_slc, qblk=qblk, win_buf=win_buf),
        grid=(n,),
        in_specs=[blk(q), blk(cmpkv), blk(cwin), blk(wnew), _const_spec(tab_c.shape), _const_spec(tab_w.shape),
                  _const_spec(sp.shape)],
        out_specs=tuple(pl.BlockSpec((1,) + o.shape[1:], lambda b: (b, 0, 0)) for o in outs),
        out_shape=outs,
        compiler_params=_cparams(1),
        name="sample_cmp_win",
    )(q, cmpkv, cwin, wnew, tab_c, tab_w, sp)


def _sample_slc_kernel(pg_ref, hf_ref, bid_ref, *refs, new_block, near_block):
    n_blk = N_KV * N_SELECT
    blocks = refs[:n_blk]
    (q_ref, kvnew_ref, ocmp_ref, owin_ref, gate_ref, tnear_ref, tlast_ref, c31_ref, rb0_ref, gn_ref,
     o_ref) = refs[n_blk:]
    b = pl.program_id(0)
    n_keys = N_SELECT * SLC_LEN
    lane = lax.broadcasted_iota(jnp.int32, (1, n_keys), 1)
    head_grp = lax.shift_right_logical(lax.broadcasted_iota(jnp.int32, (N_HEADS, LANE), 0), 2)
    lane_grp = lax.shift_right_logical(lax.broadcasted_iota(jnp.int32, (N_HEADS, LANE), 1), 6)
    knew = kvnew_ref[0, :, 256:384].astype(BF16).astype(F32)
    vnew = kvnew_ref[0, :, 384:512].astype(BF16).astype(F32)
    qs = q_ref[0]
    c31 = c31_ref[:, 0:1]
    o_slc = jnp.zeros((N_HEADS, LANE), F32)
    for g in range(N_KV):
        mine = blocks[g * N_SELECT:(g + 1) * N_SELECT]
        k = jnp.concatenate([blk[0, 0, :, 0:LANE] for blk in mine], axis=0).astype(BF16)
        v = jnp.concatenate([blk[0, 0, :, LANE:2 * LANE] for blk in mine], axis=0).astype(BF16)
        bid = jnp.zeros((1, n_keys), jnp.int32)
        has_new = bid_ref[b, g * N_SELECT] == new_block
        for kk in range(N_SELECT):
            bid = jnp.where(lax.shift_right_logical(lane, 6) == kk, bid_ref[b, g * N_SELECT + kk], bid)
            if kk:
                has_new = has_new | (bid_ref[b, g * N_SELECT + kk] == new_block)
        bias = jnp.where(bid == near_block + 1, tlast_ref[...], jnp.where(bid == near_block, tnear_ref[...], c31))
        ok = bid != new_block
        s = jnp.where(ok, _qk(qs, k) + bias, NEG)
        s_new = jnp.sum(qs.astype(F32) * knew, axis=1, keepdims=True) + rb0_ref[:, 0:1]
        s_new = jnp.where(has_new, s_new, NEG)
        m = jnp.maximum(jnp.max(s, axis=1, keepdims=True), s_new)
        e = jnp.where(ok, jnp.exp(s - m), 0.0)
        e_new = jnp.where(has_new, jnp.exp(s_new - m), 0.0)
        l = jnp.sum(e, axis=1, keepdims=True) + e_new
        og = (_mm(e.astype(BF16), v) + e_new.astype(BF16).astype(F32) * vnew) / l
        o_slc = jnp.where(head_grp == g, og, o_slc)
    gt = gate_ref[0]
    comb = gt[:, 0:1] * ocmp_ref[0] + gt[:, 1:2] * o_slc + gt[:, 2:3] * owin_ref[0]
    comb = jnp.where(lane_grp == head_grp, comb, 0.0)
    o = jnp.concatenate([comb[h:h + 1, :] for h in range(N_HEADS)], axis=1)
    o_ref[0] = _rms(o, gn_ref[...], N_HEADS * HEAD_DIM).astype(BF16)


def _sample_slc(pages, halves, bids, cache, q, kvnew, ocmp, owin, gates, t_near, t_last, c31, rb0, gn_pad,
                new_block, near_block):
    n = q.shape[0]
    n_phys = cache.shape[0]
    view = cache.reshape(n_phys, PAGE_SIZE // SLC_LEN, SLC_LEN, 512)
    n_blk = N_KV * N_SELECT
    blk_specs = [pl.BlockSpec((1, 1, SLC_LEN, 256), lambda b, pg, hf, bi, k=k: (pg[b, k], hf[b, k], 0, 1))
                 for k in range(n_blk)]
    per_b = lambda a: pl.BlockSpec((1,) + a.shape[1:], lambda b, *_: (b,) + (0,) * (a.ndim - 1))
    cst = lambda a: pl.BlockSpec(a.shape, lambda *_: (0,) * a.ndim)
    grid_spec = pltpu.PrefetchScalarGridSpec(
        num_scalar_prefetch=3,
        grid=(n,),
        in_specs=blk_specs + [per_b(q), per_b(kvnew), per_b(ocmp), per_b(owin), per_b(gates),
                              cst(t_near), cst(t_last), cst(c31), cst(rb0), cst(gn_pad)],
        out_specs=pl.BlockSpec((1, 1, N_HEADS * LANE), lambda b, *_: (b, 0, 0)),
    )
    return pl.pallas_call(
        functools.partial(_sample_slc_kernel, new_block=new_block, near_block=near_block),
        grid_spec=grid_spec,
        out_shape=jax.ShapeDtypeStruct((n, 1, N_HEADS * LANE), BF16),
        compiler_params=_cparams(1),
        name="sample_selected",
    )(pages, halves, bids, *([view] * n_blk), q, kvnew, ocmp, owin, gates, t_near, t_last, c31, rb0, gn_pad)


def _proj_kernel(on_ref, zn_ref, x_ref, wa_ref, wb_ref, g1_ref, b1_ref, wr_ref, br_ref,
                 x1_ref, x1b_ref, gate_ref, *, alpha):
    mix = _mm(on_ref[...], wa_ref[...]) + _mm(zn_ref[...], wb_ref[...])
    x1 = _layernorm(alpha * x_ref[...] + mix, g1_ref[...], b1_ref[...])
    x1_ref[...] = x1
    xb = x1.astype(BF16)
    x1b_ref[...] = xb
    logits = _mm(xb, wr_ref[...]) + br_ref[...]
    tm = logits.shape[0]
    lane = lax.broadcasted_iota(jnp.int32, (tm, LANE), 1).astype(F32)

    def first_argmax(v, vmax):
        return jnp.min(jnp.where(v == vmax, lane, float(LANE)), axis=1, keepdims=True)

    lg = jnp.where(lane < N_GROUPS, logits[:, 0:LANE], -jnp.inf)
    lg_max = jnp.max(lg, axis=1, keepdims=True)
    eg = jnp.exp(lg - lg_max)
    pg = eg / jnp.sum(eg, axis=1, keepdims=True)
    gidx = first_argmax(lg, lg_max)
    pg_sel = jnp.sum(jnp.where(lane == gidx, pg, 0.0), axis=1, keepdims=True)
    le = jnp.zeros((tm, LANE), F32)
    for gi in range(N_GROUPS):
        le = le + jnp.where(gidx == gi, logits[:, (gi + 1) * LANE:(gi + 2) * LANE], 0.0)
    le = jnp.where(lane < N_EXP, le, -jnp.inf)
    ee = jnp.exp(le - jnp.max(le, axis=1, keepdims=True))
    pe = jnp.where(lane < N_EXP, ee / jnp.sum(ee, axis=1, keepdims=True), -1.0)
    v1 = jnp.max(pe, axis=1, keepdims=True)
    i1 = first_argmax(pe, v1)
    pe2 = jnp.where(lane == i1, -1.0, pe)
    v2 = jnp.max(pe2, axis=1, keepdims=True)
    i2 = first_argmax(pe2, v2)
    tot = v1 + v2
    gate_e = jnp.where(lane == i1, v1 / tot * pg_sel, jnp.where(lane == i2, v2 / tot * pg_sel, 0.0))
    for gi in range(N_GROUPS):
        gate_ref[:, gi * LANE:(gi + 1) * LANE] = jnp.where(gidx == gi, gate_e, 0.0)


def _proj(on, zn, x, wa, wb, g1, b1, wr, br, alpha, tm):
    t, d = x.shape
    row = lambda n: pl.BlockSpec((tm, n), lambda i: (i, 0))
    outs = (jax.ShapeDtypeStruct((t, d), F32), jax.ShapeDtypeStruct((t, d), BF16),
            jax.ShapeDtypeStruct((t, N_GROUPS * LANE), F32))
    return pl.pallas_call(
        functools.partial(_proj_kernel, alpha=alpha),
        grid=(t // tm,),
        in_specs=[row(on.shape[1]), row(zn.shape[1]), row(d)] + [_const_spec(a.shape) for a in (wa, wb, g1, b1, wr, br)],
        out_specs=(row(d), row(d), row(N_GROUPS * LANE)),
        out_shape=outs,
        compiler_params=_cparams(1),
        name=f"proj_ln_router_{tm}",
    )(on, zn, x, wa, wb, g1, b1, wr, br)


def _moe_kernel(xb_ref, x1_ref, gate_ref, wgu_ref, wd_ref, g2_ref, b2_ref, out_ref, acc_ref, *, alpha, d_ff):
    gi = pl.program_id(1)

    @pl.when(gi == 0)
    def _():
        acc_ref[...] = jnp.zeros(acc_ref.shape, F32)

    x = xb_ref[...]
    gt = gate_ref[...]
    acc = acc_ref[...]
    for e in range(N_EXP):
        au = _mm(x, wgu_ref[0, e])
        h = jax.nn.silu(au[:, 0:d_ff]) * au[:, d_ff:2 * d_ff]
        acc = acc + _mm((h * gt[:, e:e + 1]).astype(BF16), wd_ref[0, e])
    acc_ref[...] = acc

    @pl.when(gi == N_GROUPS - 1)
    def _():
        out_ref[...] = _layernorm(alpha * x1_ref[...] + acc_ref[...], g2_ref[...], b2_ref[...])


def _moe(xb, x1, gate, wgu, wd, g2, b2, alpha, tm):
    t, d = x1.shape
    d_ff = wd.shape[2]
    return pl.pallas_call(
        functools.partial(_moe_kernel, alpha=alpha, d_ff=d_ff),
        grid=(t // tm, N_GROUPS),
        in_specs=[
            pl.BlockSpec((tm, d), lambda i, g: (i, 0)),
            pl.BlockSpec((tm, d), lambda i, g: (i, 0)),
            pl.BlockSpec((tm, LANE), lambda i, g: (i, g)),
            pl.BlockSpec((1, N_EXP, d, 2 * d_ff), lambda i, g: (g, 0, 0, 0)),
            pl.BlockSpec((1, N_EXP, d_ff, d), lambda i, g: (g, 0, 0, 0)),
            _const_spec(g2.shape), _const_spec(b2.shape),
        ],
        out_specs=pl.BlockSpec((tm, d), lambda i, g: (i, 0)),
        out_shape=jax.ShapeDtypeStruct((t, d), F32),
        scratch_shapes=[pltpu.VMEM((tm, d), F32)],
        compiler_params=_cparams(2),
        name=f"moe_ln_{tm}",
    )(xb, x1, gate, wgu, wd, g2, b2)


def _pad_head_lanes(v):
    vh = v.reshape(N_HEADS, HEAD_DIM)
    z = jnp.zeros_like(vh)
    grp = (jnp.arange(N_HEADS) // GQ)[:, None]
    return jnp.where(grp == 0, jnp.concatenate([vh, z], axis=1), jnp.concatenate([z, vh], axis=1)).reshape(1, -1)


def _pad_head_rows(w):
    wh = w.reshape(N_HEADS, HEAD_DIM, -1)
    z = jnp.zeros_like(wh)
    grp = (jnp.arange(N_HEADS) // GQ)[:, None, None]
    return jnp.where(grp == 0, jnp.concatenate([wh, z], axis=1), jnp.concatenate([z, wh], axis=1)).reshape(
        N_HEADS * LANE, -1)


def kernel(x_prompt, x_sample, cache_kv, page_table, cache_win, state_conv, w_in, conv_w, pe_k, pe_v, w_ck1, w_ck2, w_cv1, w_cv2, g_nsa, g_conv, w_out, ln1_g, ln1_b, w_rg, b_rg, w_re, b_re, w_eg, w_eu, w_ed, ln2_g, ln2_b, rel_bias):
    batch, seq, d_model = x_prompt.shape
    dec_batch, dec_seq = x_sample.shape[0], x_sample.shape[1]
    depth = w_in.shape[0]
    n_pages = page_table.shape[1]
    past = n_pages * PAGE_SIZE
    win_buf = cache_win.shape[2]
    d_nsa = N_HEADS * HEAD_DIM
    d_conv = w_out.shape[1] - d_nsa
    assert dec_seq == 1 and seq % TM == 0 and min(WINDOW, seq) == TM and win_buf == WINDOW
    assert d_conv == 512 and seq // CMP_STRIDE == LANE and n_pages % SAMPLE_PAGES == 0
    alpha = (2 * depth) ** 0.25
    n_slc_s = -(-(past + dec_seq) // SLC_LEN)
    qblk_s = past // SLC_LEN

    idx_np, sp_p_np, sp_s_np, expand_np = _static_tables(seq, past, win_buf)
    tabs = _bias_tables(rel_bias, jnp.asarray(idx_np))
    nq = seq // TQ
    tab_t = jnp.swapaxes(tabs[0:2], 0, 1)
    tab_c = tabs[2:2 + nq]
    smp = tabs[2 + nq]
    tab_w_s = smp[:, 0:5, :].reshape(N_HEADS, 5 * LANE)
    tab_c_s = smp[:, 5:13, :].reshape(N_HEADS, 8 * LANE)
    t_near = jnp.tile(smp[:, 13, 0:SLC_LEN], (1, N_SELECT))
    t_last = jnp.tile(smp[:, 13, SLC_LEN:2 * SLC_LEN], (1, N_SELECT))
    c31 = rel_bias[N_BUCKETS - 1]
    c31_rep = jnp.broadcast_to(c31[:, None], (N_HEADS, LANE))
    rb0_rep = jnp.broadcast_to(rel_bias[0][:, None], (N_HEADS, LANE))
    sp_p = jnp.asarray(sp_p_np, BF16)
    sp_s = jnp.asarray(sp_s_np, BF16)
    expand = jnp.asarray(expand_np, BF16)

    xp = x_prompt.reshape(batch * seq, d_model)
    xs = x_sample.reshape(dec_batch * dec_seq, d_model)
    outs = [[] for _ in range(6)]
    for l in range(depth):
        w_pack = _pack_in_weights(w_in[l])
        gc = g_conv[l].reshape(1, -1)
        gn_pad = _pad_head_lanes(g_nsa[l])
        wk1, wk2, pek = _pack_compress_weights(pe_k[l], w_ck1[l], w_ck2[l])
        wv1, wv2, pev = _pack_compress_weights(pe_v[l], w_cv1[l], w_cv2[l])
        w1 = jnp.stack([wk1, wv1])
        w2 = jnp.stack([wk2, wv2])
        pe = jnp.stack([pek, pev])
        wa = _pad_head_rows(w_out[l][:d_nsa]).astype(BF16)
        wb = w_out[l][d_nsa:].astype(BF16)
        wr = jnp.concatenate(
            [jnp.pad(w_rg[l], ((0, 0), (0, LANE - N_GROUPS)))]
            + [jnp.pad(w_re[l][:, gi * N_EXP:(gi + 1) * N_EXP], ((0, 0), (0, LANE - N_EXP))) for gi in range(N_GROUPS)],
            axis=1).astype(BF16)
        br = jnp.concatenate(
            [jnp.pad(b_rg[l], (0, LANE - N_GROUPS))]
            + [jnp.pad(b_re[l][gi * N_EXP:(gi + 1) * N_EXP], (0, LANE - N_EXP)) for gi in range(N_GROUPS)]).reshape(1, -1)
        wgu = jnp.concatenate([w_eg[l], w_eu[l]], axis=-1).astype(BF16)
        wd = w_ed[l].astype(BF16)
        g1, b1 = ln1_g[l].reshape(1, -1), ln1_b[l].reshape(1, -1)
        g2, b2 = ln2_g[l].reshape(1, -1), ln2_b[l].reshape(1, -1)

        q_p, kv_p, sk, sv, wk, wv, wtail, gates_p, zn_p, ctail = _inproj_prompt(xp, w_pack, conv_w[l], gc, batch, seq)
        cmp_p = _compress_prompt(kv_p, batch, seq, w1, w2, pe)
        on_p = _attn_prompt(c31, q_p, gates_p, cmp_p, sk, sv, wk, wv, tab_t, tab_c, sp_p, expand, gn_pad, batch, seq)
        x1_p, x1b_p, gate_p = _proj(on_p, zn_p, xp, wa, wb, g1, b1, wr, br, alpha, TM)
        y_p = _moe(x1b_p, x1_p, gate_p, wgu, wd, g2, b2, alpha, TM)

        st = state_conv[l]
        q_s, kv_s, win_s, gates_s, zn_s, u_s = _inproj_sample(xs, w_pack, conv_w[l], gc, st[:, 0], st[:, 1])
        cache = cache_kv[l]
        cmp_s = _compress_sample(cache, page_table, w1, w2, pe)
        q_s3 = q_s.reshape(dec_batch, N_HEADS, LANE)
        cwin = cache_win[l].reshape(dec_batch, win_buf, 256)
        ocmp, owin, sel_idx = _sample_cw(q_s3, cmp_s, cwin, win_s.reshape(dec_batch, 1, 256), tab_c_s, tab_w_s,
                                         sp_s, n_slc_s, qblk_s)
        bids = sel_idx[:, :, :N_SELECT].reshape(dec_batch, N_KV * N_SELECT)
        blk_pages = jnp.take_along_axis(page_table, jnp.minimum(bids // 2, n_pages - 1), axis=1)
        gates_s3 = jnp.pad(gates_s.reshape(dec_batch, N_KV, LANE)[:, :, :3 * GQ].reshape(dec_batch, N_HEADS, 3),
                           ((0, 0), (0, 0), (0, LANE - 3)))
        on_s = _sample_slc(blk_pages, bids % 2, bids, cache, q_s3, kv_s.reshape(dec_batch, 1, 512), ocmp, owin,
                           gates_s3, t_near, t_last, c31_rep, rb0_rep, gn_pad, n_slc_s - 1, qblk_s - 2)
        on_s = on_s.reshape(dec_batch, N_HEADS * LANE)
        x1_s, x1b_s, gate_s = _proj(on_s, zn_s, xs, wa, wb, g1, b1, wr, br, alpha, dec_batch)
        y_s = _moe(x1b_s, x1_s, gate_s, wgu, wd, g2, b2, alpha, dec_batch)

        outs[0].append(kv_p.reshape(batch, seq, 4, N_KV, HEAD_DIM))
        outs[1].append(kv_s.reshape(dec_batch, dec_seq, 4, N_KV, HEAD_DIM))
        outs[2].append(wtail.reshape(batch, TM, 2, N_KV, HEAD_DIM))
        win_all = jnp.concatenate([cache_win[l], win_s.reshape(dec_batch, dec_seq, 2, N_KV, HEAD_DIM)], axis=1)
        outs[3].append(win_all[:, win_all.shape[1] - min(WINDOW, past + dec_seq):])
        outs[4].append(ctail)
        outs[5].append(jnp.concatenate([st, u_s[:, None, :]], axis=1)[:, dec_seq:])
        xp, xs = y_p, y_s
    return (xp.reshape(batch, seq, d_model), xs.reshape(dec_batch, dec_seq, d_model),
            jnp.stack(outs[0]), jnp.stack(outs[1]), jnp.stack(outs[2]), jnp.stack(outs[3]),
            jnp.stack(outs[4]), jnp.stack(outs[5]))
```

```python
import functools
import math

import numpy as np
import jax
import jax.numpy as jnp
from jax import lax
from jax.experimental import pallas as pl
from jax.experimental.pallas import tpu as pltpu

F32 = jnp.float32
BF16 = jnp.bfloat16

HEAD_DIM = 64
N_KV = 2
GQ = 4
N_HEADS = N_KV * GQ
CONV_W = 3
CMP_STRIDE = 16
CMP_LEN = 32
CMP_HIDDEN = 2 * HEAD_DIM
SLC_LEN = 64
N_SELECT = 16
WINDOW = 512
N_BUCKETS = 32
MAX_DISTANCE = 128
N_GROUPS = 4
N_EXP = 8
PAGE_SIZE = 128
NEG = -1e30
FORCE = 1e6

LANE = 128
SUBLANE = 8
VMEM_LIMIT = 52 * 1024 * 1024

TQ = 128
TM = 512
SAMPLE_PAGES = 32


def _cparams(n_axes):
    return pltpu.CompilerParams(dimension_semantics=("arbitrary",) * n_axes, vmem_limit_bytes=VMEM_LIMIT)


def _const_spec(shape):
    nd = len(shape)
    return pl.BlockSpec(shape, lambda *_, nd=nd: (0,) * nd)


def _qk(a, b):
    return lax.dot_general(a, b, (((1,), (1,)), ((), ())), preferred_element_type=F32)


def _mm(a, b):
    return jnp.dot(a, b, preferred_element_type=F32)


def _bucket_np(dist):
    n = np.maximum(dist, 0)
    max_exact = N_BUCKETS // 2
    nf = np.maximum(n, 1).astype(np.float32)
    large = max_exact + (np.log(nf / np.float32(max_exact)) / np.float32(math.log(MAX_DISTANCE / max_exact))
                         * np.float32(N_BUCKETS - max_exact)).astype(np.int32)
    large = np.minimum(large, N_BUCKETS - 1)
    return np.where(n < max_exact, n, large).astype(np.int32)


def _overlap_np(c, s):
    c0 = c * CMP_STRIDE
    s0 = s * SLC_LEN
    return np.maximum(np.minimum(c0 + CMP_LEN, s0 + SLC_LEN) - np.maximum(c0, s0), 0)


def _static_tables(seq, past, win_buf):
    nq = seq // TQ
    i = np.arange(LANE)[:, None]
    j = np.arange(LANE)[None, :]
    tiles = [_bucket_np(LANE * d + i - j) for d in range(2)]
    tiles += [_bucket_np(TQ * q + i - (CMP_STRIDE * j + CMP_STRIDE - 1)) for q in range(nq)]
    smp = np.full((LANE, LANE), 10 * MAX_DISTANCE, np.int64)
    k = np.arange(5 * LANE)
    smp[0:5] = np.where(k < win_buf, win_buf - k, 0).reshape(5, LANE)
    ci = np.arange(8 * LANE)
    smp[5:13] = np.maximum(past - (CMP_STRIDE * ci + CMP_STRIDE - 1), 0).reshape(8, LANE)
    nb = past // SLC_LEN
    smp[13] = past - (SLC_LEN * (nb - 2) + np.arange(LANE))
    tiles.append(_bucket_np(smp))
    idx = np.stack(tiles).astype(np.int32)

    def sp(nrows, ncols, n_slc):
        r = np.arange(nrows)[:, None]
        s = np.arange(ncols)[None, :]
        ov = _overlap_np(r - 1, s)
        return np.where((r >= 1) & (s < n_slc), ov, 0).astype(np.float32)

    sp_p = sp(seq // CMP_STRIDE, LANE, -(-seq // SLC_LEN))
    n_slc_s = -(-(past + 1) // SLC_LEN)
    sp_s = sp(past // CMP_STRIDE, 3 * LANE, n_slc_s)
    c = np.arange(seq // LANE)[:, None, None]
    s = np.arange(LANE)[None, :, None]
    kk = np.arange(LANE)[None, None, :]
    expand = (s == 2 * c + kk // SLC_LEN).astype(np.float32)
    return idx, sp_p, sp_s, expand


def _bias_kernel(rb_ref, idx_ref, out_ref):
    idx = idx_ref[0]
    accs = [jnp.zeros((LANE, LANE), F32) for _ in range(N_HEADS)]
    for b in range(N_BUCKETS):
        hit = idx == b
        for h in range(N_HEADS):
            accs[h] = jnp.where(hit, rb_ref[b, h], accs[h])
    for h in range(N_HEADS):
        out_ref[0, h] = accs[h]


def _bias_tables(rel_bias, idx):
    nt = idx.shape[0]
    return pl.pallas_call(
        _bias_kernel,
        grid=(nt,),
        in_specs=[pl.BlockSpec(memory_space=pltpu.SMEM),
                  pl.BlockSpec((1, LANE, LANE), lambda t: (t, 0, 0))],
        out_specs=pl.BlockSpec((1, N_HEADS, LANE, LANE), lambda t: (t, 0, 0, 0)),
        out_shape=jax.ShapeDtypeStruct((nt, N_HEADS, LANE, LANE), F32),
        compiler_params=_cparams(1),
        name="bias_tables",
    )(rel_bias, idx)


_Q0, _Q1 = 0, N_HEADS * LANE
_KV0, _KV1 = _Q1, _Q1 + 512
_WN0, _WN1 = _KV1, _KV1 + 256
_GT0, _GT1 = _WN1, _WN1 + 2 * LANE
_CV0, _CV1 = _GT1, _GT1 + 3 * 512


def _pack_in_weights(w):
    d = w.shape[0]
    d_nsa = N_HEADS * HEAD_DIM
    wq = w[:, :d_nsa].reshape(d, N_HEADS, HEAD_DIM)
    z = jnp.zeros_like(wq)
    lo = jnp.concatenate([wq, z], axis=-1)
    hi = jnp.concatenate([z, wq], axis=-1)
    grp = (jnp.arange(N_HEADS) // GQ)[None, :, None]
    wq_pad = jnp.where(grp == 0, lo, hi).reshape(d, N_HEADS * LANE)
    o = d_nsa
    w_kv = w[:, o:o + 512]
    w_win = w[:, o + 512:o + 768]
    wg = w[:, o + 768:o + 768 + 3 * N_HEADS].reshape(d, N_KV, 3 * GQ)
    wg_pad = jnp.pad(wg, ((0, 0), (0, 0), (0, LANE - 3 * GQ))).reshape(d, N_KV * LANE)
    w_conv = w[:, o + 768 + 3 * N_HEADS:]
    return jnp.concatenate([wq_pad, w_kv, w_win, wg_pad, w_conv], axis=1).astype(BF16)


def _rms(x, gain, n, eps=1e-6):
    ms = jnp.sum(x * x, axis=-1, keepdims=True) / n
    return x * lax.rsqrt(ms + eps) * gain


def _layernorm(y, gain, bias, eps=1e-5):
    mu = jnp.mean(y, axis=-1, keepdims=True)
    d = y - mu
    var = jnp.mean(d * d, axis=-1, keepdims=True)
    return d * lax.rsqrt(var + eps) * gain + bias


def _inproj_prompt_kernel(x_ref, w_ref, cw_ref, gc_ref,
                          q_ref, kv_ref, sk_ref, sv_ref, wk_ref, wv_ref, wtail_ref, gate_ref, zn_ref, ctail_ref,
                          uext_ref, *, tiles_per_batch):
    i = pl.program_id(0)
    tm = x_ref.shape[0]
    xb = x_ref[...].astype(BF16)

    def seg(a, b):
        return _mm(xb, w_ref[:, a:b])

    qp = seg(_Q0, _Q1) * (HEAD_DIM ** -0.5)
    for h in range(N_HEADS):
        q_ref[h] = qp[:, h * LANE:(h + 1) * LANE].astype(BF16)
    kv = seg(_KV0, _KV1)
    kv_ref[0] = kv.T
    sk_ref[...] = kv[:, 256:384].astype(BF16)
    sv_ref[...] = kv[:, 384:512].astype(BF16)
    win = seg(_WN0, _WN1)
    wk_ref[...] = win[:, 0:LANE].astype(BF16)
    wv_ref[...] = win[:, LANE:2 * LANE].astype(BF16)
    wtail_ref[0] = win.T
    gt = jax.nn.sigmoid(seg(_GT0, _GT1))
    gate_ref[0] = gt[:, 0:LANE]
    gate_ref[1] = gt[:, LANE:2 * LANE]

    conv = seg(_CV0, _CV1)
    cb = conv[:, 0:512]
    u = conv[:, 512:1024] * conv[:, 1024:1536]
    first = (i % tiles_per_batch) == 0

    @pl.when(first)
    def _():
        uext_ref[0:SUBLANE, :] = jnp.zeros((SUBLANE, 512), F32)

    @pl.when(jnp.logical_not(first))
    def _():
        uext_ref[0:SUBLANE, :] = uext_ref[tm:tm + SUBLANE, :]

    uext_ref[SUBLANE:tm + SUBLANE, :] = u
    um1 = uext_ref[SUBLANE - 1:tm + SUBLANE - 1, :]
    um2 = uext_ref[SUBLANE - 2:tm + SUBLANE - 2, :]
    z = cb * (um2 * cw_ref[0:1, :] + um1 * cw_ref[1:2, :] + u * cw_ref[2:3, :])
    zn_ref[...] = _rms(z, gc_ref[...], 512).astype(BF16)
    ctail_ref[0] = uext_ref[tm + SUBLANE - 2:tm + SUBLANE, :]


def _inproj_prompt(x, w, conv_w, g_conv, batch, seq):
    t, d = x.shape
    n_tiles = t // TM
    tpb = seq // TM
    outs = (
        jax.ShapeDtypeStruct((N_HEADS, t, LANE), BF16),
        jax.ShapeDtypeStruct((batch, 512, seq), F32),
        jax.ShapeDtypeStruct((t, LANE), BF16),
        jax.ShapeDtypeStruct((t, LANE), BF16),
        jax.ShapeDtypeStruct((t, LANE), BF16),
        jax.ShapeDtypeStruct((t, LANE), BF16),
        jax.ShapeDtypeStruct((batch, 256, TM), F32),
        jax.ShapeDtypeStruct((N_KV, t, LANE), F32),
        jax.ShapeDtypeStruct((t, 512), BF16),
        jax.ShapeDtypeStruct((batch, CONV_W - 1, 512), F32),
    )
    row = lambda n: pl.BlockSpec((TM, n), lambda i: (i, 0))
    return pl.pallas_call(
        functools.partial(_inproj_prompt_kernel, tiles_per_batch=tpb),
        grid=(n_tiles,),
        in_specs=[row(d), _const_spec(w.shape), _const_spec(conv_w.shape), _const_spec(g_conv.shape)],
        out_specs=(
            pl.BlockSpec((N_HEADS, TM, LANE), lambda i: (0, i, 0)),
            pl.BlockSpec((1, 512, TM), lambda i: (i // tpb, 0, i % tpb)),
            row(LANE), row(LANE), row(LANE), row(LANE),
            pl.BlockSpec((1, 256, TM), lambda i: (i // tpb, 0, 0)),
            pl.BlockSpec((N_KV, TM, LANE), lambda i: (0, i, 0)),
            row(512),
            pl.BlockSpec((1, CONV_W - 1, 512), lambda i: (i // tpb, 0, 0)),
        ),
        out_shape=outs,
        scratch_shapes=[pltpu.VMEM((TM + SUBLANE, 512), F32)],
        compiler_params=_cparams(1),
        name="inproj_prompt",
    )(x, w, conv_w, g_conv)


def _inproj_sample_kernel(x_ref, w_ref, cw_ref, gc_ref, s0_ref, s1_ref,
                          q_ref, kv_ref, win_ref, gate_ref, zn_ref, u_ref):
    xb = x_ref[...].astype(BF16)

    def seg(a, b):
        return _mm(xb, w_ref[:, a:b])

    q_ref[...] = (seg(_Q0, _Q1) * (HEAD_DIM ** -0.5)).astype(BF16)
    kv_ref[...] = seg(_KV0, _KV1)
    win_ref[...] = seg(_WN0, _WN1)
    gate_ref[...] = jax.nn.sigmoid(seg(_GT0, _GT1))
    conv = seg(_CV0, _CV1)
    cb = conv[:, 0:512]
    u = conv[:, 512:1024] * conv[:, 1024:1536]
    z = cb * (s0_ref[...] * cw_ref[0:1, :] + s1_ref[...] * cw_ref[1:2, :] + u * cw_ref[2:3, :])
    zn_ref[...] = _rms(z, gc_ref[...], 512).astype(BF16)
    u_ref[...] = u


def _inproj_sample(x, w, conv_w, g_conv, s0, s1):
    n = x.shape[0]
    outs = (
        jax.ShapeDtypeStruct((n, N_HEADS * LANE), BF16),
        jax.ShapeDtypeStruct((n, 512), F32),
        jax.ShapeDtypeStruct((n, 256), F32),
        jax.ShapeDtypeStruct((n, N_KV * LANE), F32),
        jax.ShapeDtypeStruct((n, 512), BF16),
        jax.ShapeDtypeStruct((n, 512), F32),
    )
    args = (x, w, conv_w, g_conv, s0, s1)
    return pl.pallas_call(
        _inproj_sample_kernel,
        grid=(1,),
        in_specs=[_const_spec(a.shape) for a in args],
        out_specs=tuple(_const_spec(o.shape) for o in outs),
        out_shape=outs,
        compiler_params=_cparams(1),
        name="inproj_sample",
    )(*args)


def _pack_compress_weights(pe, w1, w2):
    w = w1.reshape(2, CMP_STRIDE, HEAD_DIM, CMP_HIDDEN).transpose(1, 2, 0, 3).reshape(CMP_STRIDE, HEAD_DIM, 2 * CMP_HIDDEN)
    z = jnp.zeros_like(w)
    top = jnp.concatenate([w, z], axis=-1)
    bot = jnp.concatenate([z, w], axis=-1)
    w1bd = jnp.stack([top, bot], axis=1).reshape(CMP_STRIDE * 2 * HEAD_DIM, 4 * CMP_HIDDEN).astype(BF16)
    z2 = jnp.zeros_like(w2)
    w2bd = jnp.concatenate([jnp.concatenate([w2, z2], axis=1), jnp.concatenate([z2, w2], axis=1)], axis=0).astype(BF16)
    pe_rows = jnp.broadcast_to(pe.reshape(2, CMP_STRIDE, 1, HEAD_DIM), (2, CMP_STRIDE, N_KV, HEAD_DIM))
    pe_rows = pe_rows.reshape(2, CMP_STRIDE * N_KV * HEAD_DIM)
    pe_rows = jnp.pad(pe_rows, ((0, SUBLANE - 2), (0, 0)))
    return w1bd, w2bd, pe_rows


def _compress_kernel(*refs, n_pages, n_prefetch):
    refs = refs[n_prefetch:]
    page_refs = refs[:n_pages]
    w1_ref, w2_ref, pe_ref, out_ref, carry_ref, tok_ref = refs[n_pages:]
    step = pl.program_id(1)
    m = n_pages * (PAGE_SIZE // CMP_STRIDE)

    @pl.when(step == 0)
    def _():
        carry_ref[...] = jnp.zeros(carry_ref.shape, F32)

    row0 = lax.broadcasted_iota(jnp.int32, (m, CMP_HIDDEN), 0) == 0
    outs = []
    for t in range(2):
        for p, pr in enumerate(page_refs):
            slab = pr[0, t].reshape(N_KV * HEAD_DIM, PAGE_SIZE)
            tok_ref[p * PAGE_SIZE:(p + 1) * PAGE_SIZE, :] = slab.T
        pieces = [tok_ref[pl.ds(j, m, stride=CMP_STRIDE), :] for j in range(CMP_STRIDE)]
        lhs = jnp.concatenate([jnp.concatenate(pieces, axis=1), pe_ref[t]], axis=0).astype(BF16)
        a = _mm(lhs, w1_ref[t])
        hs = []
        for g in range(N_KV):
            c0 = g * 2 * CMP_HIDDEN
            a0 = a[0:m, c0:c0 + CMP_HIDDEN]
            a1 = a[0:m, c0 + CMP_HIDDEN:c0 + 2 * CMP_HIDDEN]
            pe_term = a[m:m + 1, c0:c0 + CMP_HIDDEN] + a[m + 1:m + 2, c0 + CMP_HIDDEN:c0 + 2 * CMP_HIDDEN]
            prev = carry_ref[t, 0:1, g * CMP_HIDDEN:(g + 1) * CMP_HIDDEN]
            shifted = jnp.where(row0, prev, pltpu.roll(a0, 1, axis=0))
            carry_ref[t, 0:1, g * CMP_HIDDEN:(g + 1) * CMP_HIDDEN] = a0[m - 1:m, :]
            hs.append(jax.nn.gelu(shifted + a1 + pe_term))
        outs.append(_mm(jnp.concatenate(hs, axis=1).astype(BF16), w2_ref[t]))
    out_ref[0] = jnp.concatenate(outs, axis=1).astype(BF16)


def _compress_call(page_specs, page_args, prefetch, grid, batch, n_chunks, m, w1, w2, pe, name):
    n_pages = len(page_specs)
    n_pf = len(prefetch)
    cspec = lambda shape: pl.BlockSpec(shape, lambda *_: (0,) * len(shape))
    grid_spec = pltpu.PrefetchScalarGridSpec(
        num_scalar_prefetch=n_pf,
        grid=grid,
        in_specs=list(page_specs) + [cspec(w1.shape), cspec(w2.shape), cspec(pe.shape)],
        out_specs=pl.BlockSpec((1, m, 256), lambda b, s, *_: (b, s, 0)),
        scratch_shapes=[pltpu.VMEM((2, SUBLANE, 2 * CMP_HIDDEN), F32),
                        pltpu.VMEM((n_pages * PAGE_SIZE, N_KV * HEAD_DIM), F32)],
    )
    return pl.pallas_call(
        functools.partial(_compress_kernel, n_pages=n_pages, n_prefetch=n_pf),
        grid_spec=grid_spec,
        out_shape=jax.ShapeDtypeStruct((batch, n_chunks, 256), BF16),
        compiler_params=_cparams(2),
        name=name,
    )(*prefetch, *page_args, w1, w2, pe)


_PAGE_BLOCK = (1, 2, N_KV, HEAD_DIM, PAGE_SIZE)


def _compress_prompt(kv_t, batch, seq, w1, w2, pe):
    n_pages = seq // PAGE_SIZE
    rows = PAGE_SIZE // CMP_STRIDE
    specs = [pl.BlockSpec(_PAGE_BLOCK, lambda b, s, p=p: (b, 0, 0, 0, p)) for p in range(n_pages)]
    return _compress_call(specs, [kv_t] * n_pages, (), (batch, 1), batch, n_pages * rows, n_pages * rows,
                          w1, w2, pe, "compress_prompt")


def _compress_sample(cache_t, page_table, w1, w2, pe):
    batch, n_pages = page_table.shape
    rows = PAGE_SIZE // CMP_STRIDE
    specs = [pl.BlockSpec(_PAGE_BLOCK, lambda b, s, pt, k=k: (pt[b, s * SAMPLE_PAGES + k], 0, 0, 0, 0))
             for k in range(SAMPLE_PAGES)]
    return _compress_call(specs, [cache_t] * SAMPLE_PAGES, (page_table,), (batch, n_pages // SAMPLE_PAGES), batch,
                          n_pages * rows, SAMPLE_PAGES * rows, w1, w2, pe, "compress_sample")


def _tile4(x):
    return jnp.concatenate([x] * GQ, axis=0)


def _attn_prompt_kernel(c31_ref, q_ref, gate_ref, cmp_ref, sk_ref, sv_ref, wk_ref, wv_ref,
                        tabt_ref, tabc_ref, sp_ref, ex_ref, gn_ref, o_ref, ocat_ref):
    qi = pl.program_id(1)
    q0 = qi * TQ
    row = lax.broadcasted_iota(jnp.int32, (TQ, LANE), 0)
    col = lax.broadcasted_iota(jnp.int32, (TQ, LANE), 1)
    qpos = q0 + row
    tri = row >= col
    upper = col > row
    cmp_ok4 = _tile4((col >= 1) & (CMP_STRIDE * col + (CMP_STRIDE - 1) <= qpos))
    sidx = lax.broadcasted_iota(jnp.int32, (N_BUCKETS, TQ), 0)
    qblk = lax.shift_right_logical(q0 + lax.broadcasted_iota(jnp.int32, (N_BUCKETS, TQ), 1), 6)
    sel_valid = sidx <= qblk
    sel_forced = (sidx == 0) | (sidx == qblk) | (sidx == qblk - 1)

    def per_group(g, carry):
        heads = [GQ * g + r for r in range(GQ)]
        qs = jnp.concatenate([q_ref[h] for h in heads], axis=0)
        c31 = jnp.concatenate([jnp.full((TQ, 1), c31_ref[h], F32) for h in heads], axis=0)
        t0 = jnp.concatenate([tabt_ref[h, 0] for h in heads], axis=0)
        t1 = jnp.concatenate([tabt_ref[h, 1] for h in heads], axis=0)

        kc = cmp_ref[0, :, 0:LANE]
        vc = cmp_ref[0, :, LANE:2 * LANE]
        bias_c = jnp.concatenate([tabc_ref[0, h] for h in heads], axis=0)
        s = jnp.where(cmp_ok4, _qk(qs, kc) + bias_c, NEG)
        e = jnp.exp(s - jnp.max(s, axis=1, keepdims=True))
        p = (e / jnp.sum(e, axis=1, keepdims=True) * cmp_ok4.astype(F32)).astype(BF16)
        o_cmp = _mm(p, vc)
        imp4 = _mm(p, sp_ref[...])
        imp = imp4[0:TQ] + imp4[TQ:2 * TQ] + imp4[2 * TQ:3 * TQ] + imp4[3 * TQ:4 * TQ]

        imp_t = imp.T[0:N_BUCKETS, :]
        score = jnp.where(sel_valid, imp_t + jnp.where(sel_forced, FORCE, 0.0), -FORCE)
        rank = jnp.zeros((N_BUCKETS, TQ), F32)
        for other in range(N_BUCKETS):
            rw = score[other:other + 1, :]
            rank = rank + jnp.where(sidx > other, (rw >= score).astype(F32), (rw > score).astype(F32))
        sel_t = (rank < float(N_SELECT)).astype(F32)
        sel = jnp.concatenate([sel_t, jnp.zeros((LANE - N_BUCKETS, TQ), F32)], axis=0).T.astype(BF16)

        def flash_tile(c, state, bias, extra):
            m, l, acc = state
            start = pl.multiple_of(c * LANE, LANE)
            k = sk_ref[0, pl.ds(start, LANE), :]
            v = sv_ref[0, pl.ds(start, LANE), :]
            ok = _mm(sel, ex_ref[c]) > 0.5
            if extra is not None:
                ok = ok & extra
            ok4 = _tile4(ok)
            sc = jnp.where(ok4, _qk(qs, k) + bias, NEG)
            m_new = jnp.maximum(m, jnp.max(sc, axis=1, keepdims=True))
            alpha = jnp.exp(m - m_new)
            ee = jnp.where(ok4, jnp.exp(sc - m_new), 0.0)
            l = alpha * l + jnp.sum(ee, axis=1, keepdims=True)
            acc = alpha * acc + _mm(ee.astype(BF16), v)
            return m_new, l, acc

        state = (jnp.full((GQ * TQ, 1), NEG, F32), jnp.zeros((GQ * TQ, 1), F32), jnp.zeros((GQ * TQ, LANE), F32))
        state = lax.fori_loop(0, jnp.maximum(qi - 1, 0), lambda c, st: flash_tile(c, st, c31, None), state)
        state = flash_tile(jnp.maximum(qi - 1, 0), state, t1, qpos >= TQ)
        m, l, acc = flash_tile(qi, state, t0, tri)
        o_slc = acc / l

        n_w = WINDOW // LANE + 1
        scores, vals, oks = [], [], []
        for c in range(n_w):
            in_range = (qpos - row) >= (WINDOW - LANE * c)
            start = pl.multiple_of(jnp.maximum(q0 - WINDOW + LANE * c, 0), LANE)
            k = wk_ref[0, pl.ds(start, LANE), :]
            vals.append(wv_ref[0, pl.ds(start, LANE), :])
            if c == n_w - 1:
                bias, ok = t0, tri
            elif c == n_w - 2:
                bias, ok = t1, in_range
            elif c == 0:
                bias, ok = c31, upper & in_range
            else:
                bias, ok = c31, in_range
            ok4 = _tile4(ok)
            oks.append(ok4)
            scores.append(jnp.where(ok4, _qk(qs, k) + bias, NEG))
        m = scores[0].max(axis=1, keepdims=True)
        for sc in scores[1:]:
            m = jnp.maximum(m, sc.max(axis=1, keepdims=True))
        l = jnp.zeros((GQ * TQ, 1), F32)
        acc = jnp.zeros((GQ * TQ, LANE), F32)
        for sc, v, ok4 in zip(scores, vals, oks):
            ee = jnp.where(ok4, jnp.exp(sc - m), 0.0)
            l = l + jnp.sum(ee, axis=1, keepdims=True)
            acc = acc + _mm(ee.astype(BF16), v)
        o_win = acc / l

        gt = gate_ref[g]
        own = lax.shift_right_logical(col, 6) == g
        for r in range(GQ):
            rs = slice(r * TQ, (r + 1) * TQ)
            comb = gt[:, 3 * r:3 * r + 1] * o_cmp[rs] + gt[:, 3 * r + 1:3 * r + 2] * o_slc[rs] \
                + gt[:, 3 * r + 2:3 * r + 3] * o_win[rs]
            ocat_ref[GQ * g + r] = jnp.where(own, comb, 0.0)
        return carry

    lax.fori_loop(0, N_KV, per_group, 0)
    o = jnp.concatenate([ocat_ref[h] for h in range(N_HEADS)], axis=1)
    o_ref[...] = _rms(o, gn_ref[...], N_HEADS * HEAD_DIM).astype(BF16)


def _attn_prompt(c31, q, gates, cmpkv, sk, sv, wk, wv, tab_t, tab_c, sp, expand, gn_pad, batch, seq):
    nq = seq // TQ
    t = batch * seq
    per_b = lambda b, i: (b, 0, 0)
    return pl.pallas_call(
        _attn_prompt_kernel,
        grid=(batch, nq),
        in_specs=[
            pl.BlockSpec(memory_space=pltpu.SMEM),
            pl.BlockSpec((N_HEADS, TQ, LANE), lambda b, i: (0, b * nq + i, 0)),
            pl.BlockSpec((N_KV, TQ, LANE), lambda b, i: (0, b * nq + i, 0)),
            pl.BlockSpec((1, seq // CMP_STRIDE, 256), per_b),
            pl.BlockSpec((1, seq, LANE), per_b),
            pl.BlockSpec((1, seq, LANE), per_b),
            pl.BlockSpec((1, seq, LANE), per_b),
            pl.BlockSpec((1, seq, LANE), per_b),
            _const_spec(tab_t.shape),
            pl.BlockSpec((1, N_HEADS, LANE, LANE), lambda b, i: (i, 0, 0, 0)),
            _const_spec(sp.shape),
            _const_spec(expand.shape),
            _const_spec(gn_pad.shape),
        ],
        out_specs=pl.BlockSpec((TQ, N_HEADS * LANE), lambda b, i: (b * nq + i, 0)),
        out_shape=jax.ShapeDtypeStruct((t, N_HEADS * LANE), BF16),
        scratch_shapes=[pltpu.VMEM((N_HEADS, TQ, LANE), F32)],
        compiler_params=_cparams(2),
        name="attn_prompt",
    )(c31, q, gates, cmpkv, sk.reshape(batch, seq, LANE), sv.reshape(batch, seq, LANE),
      wk.reshape(batch, seq, LANE), wv.reshape(batch, seq, LANE), tab_t, tab_c, sp, expand, gn_pad)


def _sample_cw_kernel(q_ref, cmp_ref, cwin_ref, wnew_ref, tabc_ref, tabw_ref, rb0_ref, sp_ref,
                      ocmp_ref, owin_ref, idx_ref, *, n_slc, qblk, win_buf):
    qs = q_ref[0]
    n_cmp_rows = cmp_ref.shape[1]
    col = lax.broadcasted_iota(jnp.int32, (N_HEADS, n_cmp_rows), 1)
    ok = col >= 1
    s = jnp.where(ok, _qk(qs, cmp_ref[0, :, 0:LANE]) + tabc_ref[...], NEG)
    e = jnp.exp(s - jnp.max(s, axis=1, keepdims=True))
    p = (e / jnp.sum(e, axis=1, keepdims=True) * ok.astype(F32)).astype(BF16)
    ocmp_ref[0] = _mm(p, cmp_ref[0, :, LANE:2 * LANE])
    imp8 = _mm(p, sp_ref[...])

    width = imp8.shape[1]
    head_grp = lax.shift_right_logical(lax.broadcasted_iota(jnp.int32, (N_HEADS, width), 0), 2)
    sidx = lax.broadcasted_iota(jnp.int32, (1, width), 1)
    sidx_f = sidx.astype(F32)
    forced = (sidx == 0) | (sidx == qblk) | (sidx == qblk - 1)
    lane = lax.broadcasted_iota(jnp.int32, (1, LANE), 1)
    for g in range(N_KV):
        imp = jnp.sum(jnp.where(head_grp == g, imp8, 0.0), axis=0, keepdims=True)
        score = jnp.where(sidx <= qblk, imp + jnp.where(forced, FORCE, 0.0), -FORCE)
        score = jnp.where(sidx < n_slc, score, -jnp.inf)
        picked = jnp.zeros((1, LANE), F32)
        for k in range(N_SELECT):
            best = jnp.max(score, axis=1, keepdims=True)
            ik = jnp.min(jnp.where(score == best, sidx_f, float(width)), axis=1, keepdims=True)
            picked = jnp.where(lane == k, ik, picked)
            score = jnp.where(sidx_f == ik, -jnp.inf, score)
        idx_ref[0, g:g + 1, :] = picked.astype(jnp.int32)

    kt = cwin_ref[0, 0].reshape(N_KV * HEAD_DIM, win_buf).astype(BF16)
    vt = cwin_ref[0, 1].reshape(N_KV * HEAD_DIM, win_buf).astype(BF16)
    knew = wnew_ref[0, :, 0:LANE].astype(BF16).astype(F32)
    vnew = wnew_ref[0, :, LANE:2 * LANE].astype(BF16).astype(F32)
    colw = lax.broadcasted_iota(jnp.int32, (N_HEADS, win_buf), 1)
    okw = colw > win_buf - WINDOW
    sw = jnp.where(okw, _mm(qs, kt) + tabw_ref[...], NEG)
    s_new = jnp.sum(qs.astype(F32) * knew, axis=1, keepdims=True) + rb0_ref[:, 0:1]
    mw = jnp.maximum(jnp.max(sw, axis=1, keepdims=True), s_new)
    ew = jnp.where(okw, jnp.exp(sw - mw), 0.0)
    e_new = jnp.exp(s_new - mw)
    lw = jnp.sum(ew, axis=1, keepdims=True) + e_new
    owin_ref[0] = (_qk(ew.astype(BF16), vt) + e_new.astype(BF16).astype(F32) * vnew) / lw


def _sample_cw(q, cmpkv, cwin, wnew, tab_c, tab_w, rb0, sp, n_slc, qblk):
    n = q.shape[0]
    win_buf = cwin.shape[-1]
    blk = lambda a: pl.BlockSpec((1,) + a.shape[1:], lambda b: (b,) + (0,) * (a.ndim - 1))
    outs = (jax.ShapeDtypeStruct((n, N_HEADS, LANE), F32), jax.ShapeDtypeStruct((n, N_HEADS, LANE), F32),
            jax.ShapeDtypeStruct((n, N_KV, LANE), jnp.int32))
    return pl.pallas_call(
        functools.partial(_sample_cw_kernel, n_slc=n_slc, qblk=qblk, win_buf=win_buf),
        grid=(n,),
        in_specs=[blk(q), blk(cmpkv), blk(cwin), blk(wnew), _const_spec(tab_c.shape), _const_spec(tab_w.shape),
                  _const_spec(rb0.shape), _const_spec(sp.shape)],
        out_specs=tuple(pl.BlockSpec((1,) + o.shape[1:], lambda b: (b, 0, 0)) for o in outs),
        out_shape=outs,
        compiler_params=_cparams(1),
        name="sample_cmp_win",
    )(q, cmpkv, cwin, wnew, tab_c, tab_w, rb0, sp)


def _sample_slc_kernel(pg_ref, hf_ref, bid_ref, *refs, new_block, near_block):
    n_blk = N_KV * N_SELECT
    blocks = refs[:n_blk]
    (q_ref, kvnew_ref, ocmp_ref, owin_ref, gate_ref, tnear_ref, tlast_ref, c31_ref, rb0_ref, gn_ref,
     o_ref) = refs[n_blk:]
    b = pl.program_id(0)
    n_keys = N_SELECT * PAGE_SIZE
    lane = lax.broadcasted_iota(jnp.int32, (1, n_keys), 1)
    slot = lax.shift_right_logical(lane, 7)
    half = lax.shift_right_logical(lane, 6) & 1
    head_grp = lax.shift_right_logical(lax.broadcasted_iota(jnp.int32, (N_HEADS, LANE), 0), 2)
    lane_grp = lax.shift_right_logical(lax.broadcasted_iota(jnp.int32, (N_HEADS, LANE), 1), 6)
    knew = kvnew_ref[0, :, 256:384].astype(BF16).astype(F32)
    vnew = kvnew_ref[0, :, 384:512].astype(BF16).astype(F32)
    qs = q_ref[0]
    c31 = c31_ref[:, 0:1]
    o_slc = jnp.zeros((N_HEADS, LANE), F32)
    for g in range(N_KV):
        mine = blocks[g * N_SELECT:(g + 1) * N_SELECT]
        kt = jnp.concatenate([blk[0, 0].reshape(N_KV * HEAD_DIM, PAGE_SIZE) for blk in mine], axis=1).astype(BF16)
        vt = jnp.concatenate([blk[0, 1].reshape(N_KV * HEAD_DIM, PAGE_SIZE) for blk in mine], axis=1).astype(BF16)
        bid = jnp.zeros((1, n_keys), jnp.int32)
        hsel = jnp.zeros((1, n_keys), jnp.int32)
        has_new = bid_ref[b, g * N_SELECT] == new_block
        for kk in range(N_SELECT):
            bid = jnp.where(slot == kk, bid_ref[b, g * N_SELECT + kk], bid)
            hsel = jnp.where(slot == kk, hf_ref[b, g * N_SELECT + kk], hsel)
            if kk:
                has_new = has_new | (bid_ref[b, g * N_SELECT + kk] == new_block)
        bias = jnp.where(bid == near_block + 1, tlast_ref[...], jnp.where(bid == near_block, tnear_ref[...], c31))
        ok = (bid != new_block) & (half == hsel)
        s = jnp.where(ok, _mm(qs, kt) + bias, NEG)
        s_new = jnp.sum(qs.astype(F32) * knew, axis=1, keepdims=True) + rb0_ref[:, 0:1]
        s_new = jnp.where(has_new, s_new, NEG)
        m = jnp.maximum(jnp.max(s, axis=1, keepdims=True), s_new)
        e = jnp.where(ok, jnp.exp(s - m), 0.0)
        e_new = jnp.where(has_new, jnp.exp(s_new - m), 0.0)
        l = jnp.sum(e, axis=1, keepdims=True) + e_new
        og = (_qk(e.astype(BF16), vt) + e_new.astype(BF16).astype(F32) * vnew) / l
        o_slc = jnp.where(head_grp == g, og, o_slc)
    gt = gate_ref[0]
    comb = gt[:, 0:1] * ocmp_ref[0] + gt[:, 1:2] * o_slc + gt[:, 2:3] * owin_ref[0]
    comb = jnp.where(lane_grp == head_grp, comb, 0.0)
    o = jnp.concatenate([comb[h:h + 1, :] for h in range(N_HEADS)], axis=1)
    o_ref[0] = _rms(o, gn_ref[...], N_HEADS * HEAD_DIM).astype(BF16)


def _sample_slc(pages, halves, bids, cache_t, q, kvnew, ocmp, owin, gates, t_near, t_last, c31, rb0, gn_pad,
                new_block, near_block):
    n = q.shape[0]
    view = cache_t
    n_blk = N_KV * N_SELECT
    blk_specs = [pl.BlockSpec(_PAGE_BLOCK, lambda b, pg, hf, bi, k=k: (pg[b, k], 1, 0, 0, 0))
                 for k in range(n_blk)]
    per_b = lambda a: pl.BlockSpec((1,) + a.shape[1:], lambda b, *_: (b,) + (0,) * (a.ndim - 1))
    cst = lambda a: pl.BlockSpec(a.shape, lambda *_: (0,) * a.ndim)
    grid_spec = pltpu.PrefetchScalarGridSpec(
        num_scalar_prefetch=3,
        grid=(n,),
        in_specs=blk_specs + [per_b(q), per_b(kvnew), per_b(ocmp), per_b(owin), per_b(gates),
                              cst(t_near), cst(t_last), cst(c31), cst(rb0), cst(gn_pad)],
        out_specs=pl.BlockSpec((1, 1, N_HEADS * LANE), lambda b, *_: (b, 0, 0)),
    )
    return pl.pallas_call(
        functools.partial(_sample_slc_kernel, new_block=new_block, near_block=near_block),
        grid_spec=grid_spec,
        out_shape=jax.ShapeDtypeStruct((n, 1, N_HEADS * LANE), BF16),
        compiler_params=_cparams(1),
        name="sample_selected",
    )(pages, halves, bids, *([view] * n_blk), q, kvnew, ocmp, owin, gates, t_near, t_last, c31, rb0, gn_pad)


def _proj_kernel(on_ref, zn_ref, x_ref, wa_ref, wb_ref, g1_ref, b1_ref, wr_ref, br_ref,
                 x1_ref, x1b_ref, gate_ref, *, alpha):
    mix = _mm(on_ref[...], wa_ref[...]) + _mm(zn_ref[...], wb_ref[...])
    x1 = _layernorm(alpha * x_ref[...] + mix, g1_ref[...], b1_ref[...])
    x1_ref[...] = x1
    xb = x1.astype(BF16)
    x1b_ref[...] = xb
    logits = _mm(xb, wr_ref[...]) + br_ref[...]
    tm = logits.shape[0]
    lane = lax.broadcasted_iota(jnp.int32, (tm, LANE), 1).astype(F32)

    def first_argmax(v, vmax):
        return jnp.min(jnp.where(v == vmax, lane, float(LANE)), axis=1, keepdims=True)

    lg = jnp.where(lane < N_GROUPS, logits[:, 0:LANE], -jnp.inf)
    lg_max = jnp.max(lg, axis=1, keepdims=True)
    eg = jnp.exp(lg - lg_max)
    pg = eg / jnp.sum(eg, axis=1, keepdims=True)
    gidx = first_argmax(lg, lg_max)
    pg_sel = jnp.sum(jnp.where(lane == gidx, pg, 0.0), axis=1, keepdims=True)
    le = jnp.zeros((tm, LANE), F32)
    for gi in range(N_GROUPS):
        le = le + jnp.where(gidx == gi, logits[:, (gi + 1) * LANE:(gi + 2) * LANE], 0.0)
    le = jnp.where(lane < N_EXP, le, -jnp.inf)
    ee = jnp.exp(le - jnp.max(le, axis=1, keepdims=True))
    pe = jnp.where(lane < N_EXP, ee / jnp.sum(ee, axis=1, keepdims=True), -1.0)
    v1 = jnp.max(pe, axis=1, keepdims=True)
    i1 = first_argmax(pe, v1)
    pe2 = jnp.where(lane == i1, -1.0, pe)
    v2 = jnp.max(pe2, axis=1, keepdims=True)
    i2 = first_argmax(pe2, v2)
    tot = v1 + v2
    gate_e = jnp.where(lane == i1, v1 / tot * pg_sel, jnp.where(lane == i2, v2 / tot * pg_sel, 0.0))
    for gi in range(N_GROUPS):
        gate_ref[:, gi * LANE:(gi + 1) * LANE] = jnp.where(gidx == gi, gate_e, 0.0)


def _proj(on, zn, x, wa, wb, g1, b1, wr, br, alpha, tm):
    t, d = x.shape
    row = lambda n: pl.BlockSpec((tm, n), lambda i: (i, 0))
    outs = (jax.ShapeDtypeStruct((t, d), F32), jax.ShapeDtypeStruct((t, d), BF16),
            jax.ShapeDtypeStruct((t, N_GROUPS * LANE), F32))
    return pl.pallas_call(
        functools.partial(_proj_kernel, alpha=alpha),
        grid=(t // tm,),
        in_specs=[row(on.shape[1]), row(zn.shape[1]), row(d)] + [_const_spec(a.shape) for a in (wa, wb, g1, b1, wr, br)],
        out_specs=(row(d), row(d), row(N_GROUPS * LANE)),
        out_shape=outs,
        compiler_params=_cparams(1),
        name=f"proj_ln_router_{tm}",
    )(on, zn, x, wa, wb, g1, b1, wr, br)


def _moe_kernel(xb_ref, x1_ref, gate_ref, wgu_ref, wd_ref, g2_ref, b2_ref, out_ref, acc_ref, *, alpha, d_ff):
    gi = pl.program_id(1)

    @pl.when(gi == 0)
    def _():
        acc_ref[...] = jnp.zeros(acc_ref.shape, F32)

    x = xb_ref[...]
    gt = gate_ref[...]
    acc = acc_ref[...]
    for e in range(N_EXP):
        au = _mm(x, wgu_ref[0, e])
        h = jax.nn.silu(au[:, 0:d_ff]) * au[:, d_ff:2 * d_ff]
        acc = acc + _mm((h * gt[:, e:e + 1]).astype(BF16), wd_ref[0, e])
    acc_ref[...] = acc

    @pl.when(gi == N_GROUPS - 1)
    def _():
        out_ref[...] = _layernorm(alpha * x1_ref[...] + acc_ref[...], g2_ref[...], b2_ref[...])


def _moe(xb, x1, gate, wgu, wd, g2, b2, alpha, tm):
    t, d = x1.shape
    d_ff = wd.shape[2]
    return pl.pallas_call(
        functools.partial(_moe_kernel, alpha=alpha, d_ff=d_ff),
        grid=(t // tm, N_GROUPS),
        in_specs=[
            pl.BlockSpec((tm, d), lambda i, g: (i, 0)),
            pl.BlockSpec((tm, d), lambda i, g: (i, 0)),
            pl.BlockSpec((tm, LANE), lambda i, g: (i, g)),
            pl.BlockSpec((1, N_EXP, d, 2 * d_ff), lambda i, g: (g, 0, 0, 0)),
            pl.BlockSpec((1, N_EXP, d_ff, d), lambda i, g: (g, 0, 0, 0)),
            _const_spec(g2.shape), _const_spec(b2.shape),
        ],
        out_specs=pl.BlockSpec((tm, d), lambda i, g: (i, 0)),
        out_shape=jax.ShapeDtypeStruct((t, d), F32),
        scratch_shapes=[pltpu.VMEM((tm, d), F32)],
        compiler_params=_cparams(2),
        name=f"moe_ln_{tm}",
    )(xb, x1, gate, wgu, wd, g2, b2)


def _pad_head_lanes(v):
    vh = v.reshape(N_HEADS, HEAD_DIM)
    z = jnp.zeros_like(vh)
    grp = (jnp.arange(N_HEADS) // GQ)[:, None]
    return jnp.where(grp == 0, jnp.concatenate([vh, z], axis=1), jnp.concatenate([z, vh], axis=1)).reshape(1, -1)


def _pad_head_rows(w):
    wh = w.reshape(N_HEADS, HEAD_DIM, -1)
    z = jnp.zeros_like(wh)
    grp = (jnp.arange(N_HEADS) // GQ)[:, None, None]
    return jnp.where(grp == 0, jnp.concatenate([wh, z], axis=1), jnp.concatenate([z, wh], axis=1)).reshape(
        N_HEADS * LANE, -1)


def kernel(x_prompt, x_sample, cache_kv, page_table, cache_win, state_conv, w_in, conv_w, pe_k, pe_v, w_ck1, w_ck2, w_cv1, w_cv2, g_nsa, g_conv, w_out, ln1_g, ln1_b, w_rg, b_rg, w_re, b_re, w_eg, w_eu, w_ed, ln2_g, ln2_b, rel_bias):
    batch, seq, d_model = x_prompt.shape
    dec_batch, dec_seq = x_sample.shape[0], x_sample.shape[1]
    depth = w_in.shape[0]
    n_pages = page_table.shape[1]
    past = n_pages * PAGE_SIZE
    win_buf = cache_win.shape[2]
    d_nsa = N_HEADS * HEAD_DIM
    d_conv = w_out.shape[1] - d_nsa
    assert dec_seq == 1 and seq % TM == 0 and min(WINDOW, seq) == TM and win_buf == WINDOW
    assert d_conv == 512 and seq // CMP_STRIDE == LANE and n_pages % SAMPLE_PAGES == 0
    alpha = (2 * depth) ** 0.25
    n_slc_s = -(-(past + dec_seq) // SLC_LEN)
    qblk_s = past // SLC_LEN

    idx_np, sp_p_np, sp_s_np, expand_np = _static_tables(seq, past, win_buf)
    tabs = _bias_tables(rel_bias, jnp.asarray(idx_np))
    nq = seq // TQ
    tab_t = jnp.swapaxes(tabs[0:2], 0, 1)
    tab_c = tabs[2:2 + nq]
    smp = tabs[2 + nq]
    tab_w_s = smp[:, 0:5, :].reshape(N_HEADS, 5 * LANE)[:, :win_buf]
    tab_c_s = smp[:, 5:13, :].reshape(N_HEADS, 8 * LANE)
    reps = N_SELECT * PAGE_SIZE // SLC_LEN
    t_near = jnp.tile(smp[:, 13, 0:SLC_LEN], (1, reps))
    t_last = jnp.tile(smp[:, 13, SLC_LEN:2 * SLC_LEN], (1, reps))
    c31 = rel_bias[N_BUCKETS - 1]
    c31_rep = jnp.broadcast_to(c31[:, None], (N_HEADS, LANE))
    rb0_rep = jnp.broadcast_to(rel_bias[0][:, None], (N_HEADS, LANE))
    sp_p = jnp.asarray(sp_p_np, BF16)
    sp_s = jnp.asarray(sp_s_np, BF16)
    expand = jnp.asarray(expand_np, BF16)

    xp = x_prompt.reshape(batch * seq, d_model)
    xs = x_sample.reshape(dec_batch * dec_seq, d_model)
    outs = [[] for _ in range(6)]
    for l in range(depth):
        w_pack = _pack_in_weights(w_in[l])
        gc = g_conv[l].reshape(1, -1)
        gn_pad = _pad_head_lanes(g_nsa[l])
        wk1, wk2, pek = _pack_compress_weights(pe_k[l], w_ck1[l], w_ck2[l])
        wv1, wv2, pev = _pack_compress_weights(pe_v[l], w_cv1[l], w_cv2[l])
        w1 = jnp.stack([wk1, wv1])
        w2 = jnp.stack([wk2, wv2])
        pe = jnp.stack([pek, pev])
        wa = _pad_head_rows(w_out[l][:d_nsa]).astype(BF16)
        wb = w_out[l][d_nsa:].astype(BF16)
        wr = jnp.concatenate(
            [jnp.pad(w_rg[l], ((0, 0), (0, LANE - N_GROUPS)))]
            + [jnp.pad(w_re[l][:, gi * N_EXP:(gi + 1) * N_EXP], ((0, 0), (0, LANE - N_EXP))) for gi in range(N_GROUPS)],
            axis=1).astype(BF16)
        br = jnp.concatenate(
            [jnp.pad(b_rg[l], (0, LANE - N_GROUPS))]
            + [jnp.pad(b_re[l][gi * N_EXP:(gi + 1) * N_EXP], (0, LANE - N_EXP)) for gi in range(N_GROUPS)]).reshape(1, -1)
        wgu = jnp.concatenate([w_eg[l], w_eu[l]], axis=-1).astype(BF16)
        wd = w_ed[l].astype(BF16)
        g1, b1 = ln1_g[l].reshape(1, -1), ln1_b[l].reshape(1, -1)
        g2, b2 = ln2_g[l].reshape(1, -1), ln2_b[l].reshape(1, -1)

        q_p, kvt_p, sk, sv, wk, wv, wtail_t, gates_p, zn_p, ctail = _inproj_prompt(xp, w_pack, conv_w[l], gc, batch, seq)
        kvt_p = kvt_p.reshape(batch, 4, N_KV, HEAD_DIM, seq)
        cmp_p = _compress_prompt(kvt_p, batch, seq, w1, w2, pe)
        on_p = _attn_prompt(c31, q_p, gates_p, cmp_p, sk, sv, wk, wv, tab_t, tab_c, sp_p, expand, gn_pad, batch, seq)
        x1_p, x1b_p, gate_p = _proj(on_p, zn_p, xp, wa, wb, g1, b1, wr, br, alpha, TM)
        y_p = _moe(x1b_p, x1_p, gate_p, wgu, wd, g2, b2, alpha, TM)

        st = state_conv[l]
        q_s, kv_s, win_s, gates_s, zn_s, u_s = _inproj_sample(xs, w_pack, conv_w[l], gc, st[:, 0], st[:, 1])
        cache_t = jnp.transpose(cache_kv[l], (0, 2, 3, 4, 1))
        cwin_t = jnp.transpose(cache_win[l], (0, 2, 3, 4, 1))
        cmp_s = _compress_sample(cache_t, page_table, w1, w2, pe)
        q_s3 = q_s.reshape(dec_batch, N_HEADS, LANE)
        ocmp, owin, sel_idx = _sample_cw(q_s3, cmp_s, cwin_t, win_s.reshape(dec_batch, 1, 256), tab_c_s, tab_w_s,
                                         rb0_rep, sp_s, n_slc_s, qblk_s)
        bids = sel_idx[:, :, :N_SELECT].reshape(dec_batch, N_KV * N_SELECT)
        blk_pages = jnp.take_along_axis(page_table, jnp.minimum(bids // 2, n_pages - 1), axis=1)
        gates_s3 = jnp.pad(gates_s.reshape(dec_batch, N_KV, LANE)[:, :, :3 * GQ].reshape(dec_batch, N_HEADS, 3),
                           ((0, 0), (0, 0), (0, LANE - 3)))
        on_s = _sample_slc(blk_pages, bids % 2, bids, cache_t, q_s3, kv_s.reshape(dec_batch, 1, 512), ocmp, owin,
                           gates_s3, t_near, t_last, c31_rep, rb0_rep, gn_pad, n_slc_s - 1, qblk_s - 2)
        on_s = on_s.reshape(dec_batch, N_HEADS * LANE)
        x1_s, x1b_s, gate_s = _proj(on_s, zn_s, xs, wa, wb, g1, b1, wr, br, alpha, dec_batch)
        y_s = _moe(x1b_s, x1_s, gate_s, wgu, wd, g2, b2, alpha, dec_batch)

        to_token_major = lambda a: jnp.transpose(a, (0, 4, 1, 2, 3))
        outs[0].append(to_token_major(kvt_p))
        outs[1].append(kv_s.reshape(dec_batch, dec_seq, 4, N_KV, HEAD_DIM))
        outs[2].append(to_token_major(wtail_t.reshape(batch, 2, N_KV, HEAD_DIM, TM)))
        win_all_t = jnp.concatenate([cwin_t, win_s.reshape(dec_batch, 2, N_KV, HEAD_DIM, dec_seq)], axis=-1)
        outs[3].append(to_token_major(win_all_t[..., win_all_t.shape[-1] - min(WINDOW, past + dec_seq):]))
        outs[4].append(ctail)
        outs[5].append(jnp.concatenate([st, u_s[:, None, :]], axis=1)[:, dec_seq:])
        xp, xs = y_p, y_s
    return (xp.reshape(batch, seq, d_model), xs.reshape(dec_batch, dec_seq, d_model),
            jnp.stack(outs[0]), jnp.stack(outs[1]), jnp.stack(outs[2]), jnp.stack(outs[3]),
            jnp.stack(outs[4]), jnp.stack(outs[5]))
```

```python
import functools
import math

import numpy as np
import jax
import jax.numpy as jnp
from jax import lax
from jax.experimental import pallas as pl
from jax.experimental.pallas import tpu as pltpu

F32 = jnp.float32
BF16 = jnp.bfloat16

HEAD_DIM = 64
N_KV = 2
GQ = 4
N_HEADS = N_KV * GQ
CONV_W = 3
CMP_STRIDE = 16
CMP_LEN = 32
CMP_HIDDEN = 2 * HEAD_DIM
SLC_LEN = 64
N_SELECT = 16
WINDOW = 512
N_BUCKETS = 32
MAX_DISTANCE = 128
N_GROUPS = 4
N_EXP = 8
PAGE_SIZE = 128
NEG = -1e30
FORCE = 1e6

LANE = 128
SUBLANE = 8
VMEM_LIMIT = 52 * 1024 * 1024

TQ = 128
TM = 512
FAR = 512
TAIL = WINDOW + TQ
SAMPLE_PAGES = 32

KEY_MASK = 2.0 ** 100
TABLE_MASK = 2 * NEG
MASKED_BUCKET = N_BUCKETS
SEL_LANE0 = HEAD_DIM
PAD_LANE = HEAD_DIM + 32


def _cparams(n_axes):
    return pltpu.CompilerParams(dimension_semantics=("arbitrary",) * n_axes, vmem_limit_bytes=VMEM_LIMIT)


def _const_spec(shape):
    nd = len(shape)
    return pl.BlockSpec(shape, lambda *_, nd=nd: (0,) * nd)


def _qk(a, b):
    return lax.dot_general(a, b, (((1,), (1,)), ((), ())), preferred_element_type=F32)


def _mm(a, b):
    return jnp.dot(a, b, preferred_element_type=F32)


def _bucket_np(dist):
    n = np.maximum(dist, 0)
    max_exact = N_BUCKETS // 2
    nf = np.maximum(n, 1).astype(np.float32)
    large = max_exact + (np.log(nf / np.float32(max_exact)) / np.float32(math.log(MAX_DISTANCE / max_exact))
                         * np.float32(N_BUCKETS - max_exact)).astype(np.int32)
    large = np.minimum(large, N_BUCKETS - 1)
    return np.where(n < max_exact, n, large).astype(np.int32)


def _overlap_np(c, s):
    c0 = c * CMP_STRIDE
    s0 = s * SLC_LEN
    return np.maximum(np.minimum(c0 + CMP_LEN, s0 + SLC_LEN) - np.maximum(c0, s0), 0)


_N_NEAR_TILES = 3


def _static_tables(seq, past, win_buf):
    nq = seq // TQ
    i = np.arange(LANE)[:, None]
    j = np.arange(LANE)[None, :]
    far_bucket = N_BUCKETS - 1
    tiles = [_bucket_np(LANE + i - j),
             np.where(i >= j, _bucket_np(i - j), MASKED_BUCKET),
             np.where(j > i, far_bucket, MASKED_BUCKET) + 0 * i]
    tiles += [_bucket_np(TQ * q + i - (CMP_STRIDE * j + CMP_STRIDE - 1)) for q in range(nq)]
    smp = np.full((LANE, LANE), 10 * MAX_DISTANCE, np.int64)
    k = np.arange(5 * LANE)
    smp[0:5] = np.where(k < win_buf, win_buf - k, 0).reshape(5, LANE)
    ci = np.arange(8 * LANE)
    smp[5:13] = np.maximum(past - (CMP_STRIDE * ci + CMP_STRIDE - 1), 0).reshape(8, LANE)
    nb = past // SLC_LEN
    smp[13] = past - (SLC_LEN * (nb - 2) + np.arange(LANE))
    smp[14] = 0
    tiles.append(_bucket_np(smp))
    idx = np.stack(tiles).astype(np.int32)

    def sp(nrows, ncols, n_slc):
        r = np.arange(nrows)[:, None]
        s = np.arange(ncols)[None, :]
        ov = _overlap_np(r - 1, s)
        return np.where((r >= 1) & (s < n_slc), ov, 0).astype(np.float32)

    sp_p = sp(seq // CMP_STRIDE, LANE, -(-seq // SLC_LEN))
    n_slc_s = -(-(past + 1) // SLC_LEN)
    sp_s = sp(past // CMP_STRIDE, 3 * LANE, n_slc_s)
    return idx, sp_p, sp_s


def _bias_kernel(rb_ref, idx_ref, out_ref):
    idx = idx_ref[0]
    accs = [jnp.full((LANE, LANE), TABLE_MASK, F32) for _ in range(N_HEADS)]
    for b in range(N_BUCKETS):
        hit = idx == b
        for h in range(N_HEADS):
            accs[h] = jnp.where(hit, rb_ref[b, h] - rb_ref[N_BUCKETS - 1, h], accs[h])
    for h in range(N_HEADS):
        out_ref[0, h] = accs[h]


def _bias_tables(rel_bias, idx):
    nt = idx.shape[0]
    return pl.pallas_call(
        _bias_kernel,
        grid=(nt,),
        in_specs=[pl.BlockSpec(memory_space=pltpu.SMEM),
                  pl.BlockSpec((1, LANE, LANE), lambda t: (t, 0, 0))],
        out_specs=pl.BlockSpec((1, N_HEADS, LANE, LANE), lambda t: (t, 0, 0, 0)),
        out_shape=jax.ShapeDtypeStruct((nt, N_HEADS, LANE, LANE), F32),
        compiler_params=_cparams(1),
        name="bias_tables",
    )(rel_bias, idx)


_Q0, _Q1 = 0, N_HEADS * LANE
_KV0, _KV1 = _Q1, _Q1 + 512
_WN0, _WN1 = _KV1, _KV1 + 256
_GT0, _GT1 = _WN1, _WN1 + 2 * LANE
_CV0, _CV1 = _GT1, _GT1 + 3 * 512


def _pack_in_weights(w):
    d = w.shape[0]
    d_nsa = N_HEADS * HEAD_DIM
    wq = w[:, :d_nsa].reshape(d, N_HEADS, HEAD_DIM)
    wq_pad = jnp.pad(wq, ((0, 0), (0, 0), (0, LANE - HEAD_DIM))).reshape(d, N_HEADS * LANE)
    o = d_nsa
    w_kv = w[:, o:o + 512]
    w_win = w[:, o + 512:o + 768]
    wg = w[:, o + 768:o + 768 + 3 * N_HEADS].reshape(d, N_KV, 3 * GQ)
    wg_pad = jnp.pad(wg, ((0, 0), (0, 0), (0, LANE - 3 * GQ))).reshape(d, N_KV * LANE)
    w_conv = w[:, o + 768 + 3 * N_HEADS:]
    return jnp.concatenate([wq_pad, w_kv, w_win, wg_pad, w_conv], axis=1).astype(BF16)


def _rms(x, gain, n, eps=1e-6):
    ms = jnp.sum(x * x, axis=-1, keepdims=True) / n
    return x * lax.rsqrt(ms + eps) * gain


def _layernorm(y, gain, bias, eps=1e-5):
    mu = jnp.mean(y, axis=-1, keepdims=True)
    d = y - mu
    var = jnp.mean(d * d, axis=-1, keepdims=True)
    return d * lax.rsqrt(var + eps) * gain + bias


def _inproj_prompt_kernel(x_ref, w_ref, cw_ref, gc_ref,
                          q_ref, kv_ref, ks_ref, sv_ref, kw_ref, wv_ref, wtail_ref, gate_ref, zn_ref, ctail_ref,
                          uext_ref, *, tiles_per_batch):
    i = pl.program_id(0)
    tm = x_ref.shape[0]
    xb = x_ref[...].astype(BF16)

    def seg(a, b):
        return _mm(xb, w_ref[:, a:b])

    qp = seg(_Q0, _Q1) * (HEAD_DIM ** -0.5)
    for h in range(N_HEADS):
        q_ref[h] = qp[:, h * LANE:(h + 1) * LANE].astype(BF16)
    kv = seg(_KV0, _KV1)
    kv_ref[0] = kv.T
    win = seg(_WN0, _WN1)
    wtail_ref[0] = win.T
    pos = (i % tiles_per_batch) * tm + lax.broadcasted_iota(jnp.int32, (tm, LANE - HEAD_DIM), 0)
    lane = lax.broadcasted_iota(jnp.int32, (tm, LANE - HEAD_DIM), 1)
    blk_flag = jnp.where(lane == lax.shift_right_logical(pos, 6), KEY_MASK, 0.0).astype(BF16)
    no_flag = jnp.zeros((tm, LANE - HEAD_DIM), BF16)
    for g in range(N_KV):
        ks_ref[g] = jnp.concatenate([kv[:, 256 + g * HEAD_DIM:256 + (g + 1) * HEAD_DIM].astype(BF16), blk_flag], axis=1)
        kw_ref[g] = jnp.concatenate([win[:, g * HEAD_DIM:(g + 1) * HEAD_DIM].astype(BF16), no_flag], axis=1)
    sv_ref[...] = kv[:, 384:512].astype(BF16)
    wv_ref[...] = win[:, LANE:2 * LANE].astype(BF16)
    gt = jax.nn.sigmoid(seg(_GT0, _GT1))
    gate_ref[0] = gt[:, 0:LANE]
    gate_ref[1] = gt[:, LANE:2 * LANE]

    conv = seg(_CV0, _CV1)
    cb = conv[:, 0:512]
    u = conv[:, 512:1024] * conv[:, 1024:1536]
    first = (i % tiles_per_batch) == 0

    @pl.when(first)
    def _():
        uext_ref[0:SUBLANE, :] = jnp.zeros((SUBLANE, 512), F32)

    @pl.when(jnp.logical_not(first))
    def _():
        uext_ref[0:SUBLANE, :] = uext_ref[tm:tm + SUBLANE, :]

    uext_ref[SUBLANE:tm + SUBLANE, :] = u
    um1 = uext_ref[SUBLANE - 1:tm + SUBLANE - 1, :]
    um2 = uext_ref[SUBLANE - 2:tm + SUBLANE - 2, :]
    z = cb * (um2 * cw_ref[0:1, :] + um1 * cw_ref[1:2, :] + u * cw_ref[2:3, :])
    zn_ref[...] = _rms(z, gc_ref[...], 512).astype(BF16)
    ctail_ref[0] = uext_ref[tm + SUBLANE - 2:tm + SUBLANE, :]


def _inproj_prompt(x, w, conv_w, g_conv, batch, seq):
    t, d = x.shape
    n_tiles = t // TM
    tpb = seq // TM
    outs = (
        jax.ShapeDtypeStruct((N_HEADS, t, LANE), BF16),
        jax.ShapeDtypeStruct((batch, 512, seq), F32),
        jax.ShapeDtypeStruct((N_KV, t, LANE), BF16),
        jax.ShapeDtypeStruct((t, LANE), BF16),
        jax.ShapeDtypeStruct((N_KV, t, LANE), BF16),
        jax.ShapeDtypeStruct((t, LANE), BF16),
        jax.ShapeDtypeStruct((batch, 256, TM), F32),
        jax.ShapeDtypeStruct((N_KV, t, LANE), F32),
        jax.ShapeDtypeStruct((t, 512), BF16),
        jax.ShapeDtypeStruct((batch, CONV_W - 1, 512), F32),
    )
    row = lambda n: pl.BlockSpec((TM, n), lambda i: (i, 0))
    planes = lambda n: pl.BlockSpec((n, TM, LANE), lambda i: (0, i, 0))
    return pl.pallas_call(
        functools.partial(_inproj_prompt_kernel, tiles_per_batch=tpb),
        grid=(n_tiles,),
        in_specs=[row(d), _const_spec(w.shape), _const_spec(conv_w.shape), _const_spec(g_conv.shape)],
        out_specs=(
            planes(N_HEADS),
            pl.BlockSpec((1, 512, TM), lambda i: (i // tpb, 0, i % tpb)),
            planes(N_KV), row(LANE), planes(N_KV), row(LANE),
            pl.BlockSpec((1, 256, TM), lambda i: (i // tpb, 0, 0)),
            planes(N_KV),
            row(512),
            pl.BlockSpec((1, CONV_W - 1, 512), lambda i: (i // tpb, 0, 0)),
        ),
        out_shape=outs,
        scratch_shapes=[pltpu.VMEM((TM + SUBLANE, 512), F32)],
        compiler_params=_cparams(1),
        name="inproj_prompt",
    )(x, w, conv_w, g_conv)


def _inproj_sample_kernel(x_ref, w_ref, cw_ref, gc_ref, s0_ref, s1_ref,
                          qlo_ref, qgl_ref, kv_ref, win_ref, gate_ref, zn_ref, u_ref):
    xb = x_ref[...].astype(BF16)

    def seg(a, b):
        return _mm(xb, w_ref[:, a:b])

    qp = seg(_Q0, _Q1) * (HEAD_DIM ** -0.5)
    qlo_ref[...] = qp.astype(BF16)
    blocks = [qp[:, h * LANE:(h + 1) * LANE] for h in range(N_HEADS)]
    blocks = [b if h < GQ else pltpu.roll(b, HEAD_DIM, axis=1) for h, b in enumerate(blocks)]
    qgl_ref[...] = jnp.concatenate(blocks, axis=1).astype(BF16)
    kv_ref[...] = seg(_KV0, _KV1)
    win_ref[...] = seg(_WN0, _WN1)
    gate_ref[...] = jax.nn.sigmoid(seg(_GT0, _GT1))
    conv = seg(_CV0, _CV1)
    cb = conv[:, 0:512]
    u = conv[:, 512:1024] * conv[:, 1024:1536]
    z = cb * (s0_ref[...] * cw_ref[0:1, :] + s1_ref[...] * cw_ref[1:2, :] + u * cw_ref[2:3, :])
    zn_ref[...] = _rms(z, gc_ref[...], 512).astype(BF16)
    u_ref[...] = u


def _inproj_sample(x, w, conv_w, g_conv, s0, s1):
    n = x.shape[0]
    outs = (
        jax.ShapeDtypeStruct((n, N_HEADS * LANE), BF16),
        jax.ShapeDtypeStruct((n, N_HEADS * LANE), BF16),
        jax.ShapeDtypeStruct((n, 512), F32),
        jax.ShapeDtypeStruct((n, 256), F32),
        jax.ShapeDtypeStruct((n, N_KV * LANE), F32),
        jax.ShapeDtypeStruct((n, 512), BF16),
        jax.ShapeDtypeStruct((n, 512), F32),
    )
    args = (x, w, conv_w, g_conv, s0, s1)
    return pl.pallas_call(
        _inproj_sample_kernel,
        grid=(1,),
        in_specs=[_const_spec(a.shape) for a in args],
        out_specs=tuple(_const_spec(o.shape) for o in outs),
        out_shape=outs,
        compiler_params=_cparams(1),
        name="inproj_sample",
    )(*args)


def _pack_compress_weights(pe, w1, w2, per_group_out):
    w = w1.reshape(2, CMP_STRIDE, HEAD_DIM, CMP_HIDDEN).transpose(1, 2, 0, 3).reshape(CMP_STRIDE, HEAD_DIM, 2 * CMP_HIDDEN)
    z = jnp.zeros_like(w)
    top = jnp.concatenate([w, z], axis=-1)
    bot = jnp.concatenate([z, w], axis=-1)
    w1bd = jnp.stack([top, bot], axis=1).reshape(CMP_STRIDE * 2 * HEAD_DIM, 4 * CMP_HIDDEN).astype(BF16)
    z2 = jnp.zeros_like(w2)
    if per_group_out:
        w2bd = jnp.concatenate([jnp.concatenate([w2, z2, z2, z2], axis=1),
                                jnp.concatenate([z2, z2, w2, z2], axis=1)], axis=0).astype(BF16)
    else:
        w2bd = jnp.concatenate([jnp.concatenate([w2, z2], axis=1), jnp.concatenate([z2, w2], axis=1)], axis=0).astype(BF16)
    pe_rows = jnp.broadcast_to(pe.reshape(2, CMP_STRIDE, 1, HEAD_DIM), (2, CMP_STRIDE, N_KV, HEAD_DIM))
    pe_rows = pe_rows.reshape(2, CMP_STRIDE * N_KV * HEAD_DIM)
    pe_rows = jnp.pad(pe_rows, ((0, SUBLANE - 2), (0, 0)))
    return w1bd, w2bd, pe_rows


def _compress_kernel(*refs, n_pages, n_prefetch):
    refs = refs[n_prefetch:]
    page_refs = refs[:n_pages]
    w1_ref, w2k_ref, w2v_ref, pe_ref, out_ref, carry_ref, tok_ref = refs[n_pages:]
    step = pl.program_id(1)
    m = n_pages * (PAGE_SIZE // CMP_STRIDE)

    @pl.when(step == 0)
    def _():
        carry_ref[...] = jnp.zeros(carry_ref.shape, F32)

    row0 = lax.broadcasted_iota(jnp.int32, (m, CMP_HIDDEN), 0) == 0
    for t, w2_ref in enumerate((w2k_ref, w2v_ref)):
        for p, pr in enumerate(page_refs):
            slab = pr[0, t].reshape(N_KV * HEAD_DIM, PAGE_SIZE)
            tok_ref[p * PAGE_SIZE:(p + 1) * PAGE_SIZE, :] = slab.T
        pieces = [tok_ref[pl.ds(j, m, stride=CMP_STRIDE), :] for j in range(CMP_STRIDE)]
        lhs = jnp.concatenate([jnp.concatenate(pieces, axis=1), pe_ref[t]], axis=0).astype(BF16)
        a = _mm(lhs, w1_ref[t])
        hs = []
        for g in range(N_KV):
            c0 = g * 2 * CMP_HIDDEN
            a0 = a[0:m, c0:c0 + CMP_HIDDEN]
            a1 = a[0:m, c0 + CMP_HIDDEN:c0 + 2 * CMP_HIDDEN]
            pe_term = a[m:m + 1, c0:c0 + CMP_HIDDEN] + a[m + 1:m + 2, c0 + CMP_HIDDEN:c0 + 2 * CMP_HIDDEN]
            prev = carry_ref[t, 0:1, g * CMP_HIDDEN:(g + 1) * CMP_HIDDEN]
            shifted = jnp.where(row0, prev, pltpu.roll(a0, 1, axis=0))
            carry_ref[t, 0:1, g * CMP_HIDDEN:(g + 1) * CMP_HIDDEN] = a0[m - 1:m, :]
            hs.append(jax.nn.gelu(shifted + a1 + pe_term))
        out = _mm(jnp.concatenate(hs, axis=1).astype(BF16), w2_ref[...]).astype(BF16)
        if t == 0:
            out_ref[0, 0] = out[:, 0:LANE]
            out_ref[0, 1] = out[:, LANE:2 * LANE]
        else:
            out_ref[0, 2] = out


def _compress_call(page_specs, page_args, prefetch, grid, batch, n_chunks, m, w1, w2k, w2v, pe, name):
    n_pages = len(page_specs)
    n_pf = len(prefetch)
    cspec = lambda shape: pl.BlockSpec(shape, lambda *_: (0,) * len(shape))
    grid_spec = pltpu.PrefetchScalarGridSpec(
        num_scalar_prefetch=n_pf,
        grid=grid,
        in_specs=list(page_specs) + [cspec(w1.shape), cspec(w2k.shape), cspec(w2v.shape), cspec(pe.shape)],
        out_specs=pl.BlockSpec((1, 3, m, LANE), lambda b, s, *_: (b, 0, s, 0)),
        scratch_shapes=[pltpu.VMEM((2, SUBLANE, 2 * CMP_HIDDEN), F32),
                        pltpu.VMEM((n_pages * PAGE_SIZE, N_KV * HEAD_DIM), F32)],
    )
    return pl.pallas_call(
        functools.partial(_compress_kernel, n_pages=n_pages, n_prefetch=n_pf),
        grid_spec=grid_spec,
        out_shape=jax.ShapeDtypeStruct((batch, 3, n_chunks, LANE), BF16),
        compiler_params=_cparams(2),
        name=name,
    )(*prefetch, *page_args, w1, w2k, w2v, pe)


_PAGE_BLOCK = (1, 2, N_KV, HEAD_DIM, PAGE_SIZE)


def _compress_prompt(kv_t, batch, seq, w1, w2k, w2v, pe):
    n_pages = seq // PAGE_SIZE
    rows = PAGE_SIZE // CMP_STRIDE
    specs = [pl.BlockSpec(_PAGE_BLOCK, lambda b, s, p=p: (b, 0, 0, 0, p)) for p in range(n_pages)]
    return _compress_call(specs, [kv_t] * n_pages, (), (batch, 1), batch, n_pages * rows, n_pages * rows,
                          w1, w2k, w2v, pe, "compress_prompt")


def _compress_sample(cache_t, page_table, w1, w2k, w2v, pe):
    batch, n_pages = page_table.shape
    rows = PAGE_SIZE // CMP_STRIDE
    specs = [pl.BlockSpec(_PAGE_BLOCK, lambda b, s, pt, k=k: (pt[b, s * SAMPLE_PAGES + k], 0, 0, 0, 0))
             for k in range(SAMPLE_PAGES)]
    return _compress_call(specs, [cache_t] * SAMPLE_PAGES, (page_table,), (batch, n_pages // SAMPLE_PAGES), batch,
                          n_pages * rows, SAMPLE_PAGES * rows, w1, w2k, w2v, pe, "compress_sample")


def _tile4(x):
    return jnp.concatenate([x] * GQ, axis=0)


def _softmax_fold(state, pieces, v):
    m, l, acc = state
    m_new = m
    for pc in pieces:
        m_new = jnp.maximum(m_new, jnp.max(pc, axis=1, keepdims=True))
    alpha = jnp.exp(m - m_new)
    es = [jnp.exp(pc - m_new) for pc in pieces]
    l = alpha * l
    for e in es:
        l = l + jnp.sum(e, axis=1, keepdims=True)
    e_all = es[0] if len(es) == 1 else jnp.concatenate(es, axis=1)
    acc = alpha * acc + _mm(e_all.astype(BF16), v)
    return m_new, l, acc


def _attn_prompt_kernel(q_ref, gate_ref, cmp_ref, ks_ref, sv_ref, kw_ref, wv_ref,
                        tabt_ref, tabc_ref, sp_ref, gn_ref, o_ref, ocat_ref, *, n_slc):
    qi = pl.program_id(1)
    q0 = pl.multiple_of(qi * TQ, TQ)
    row = lax.broadcasted_iota(jnp.int32, (TQ, LANE), 0)
    col = lax.broadcasted_iota(jnp.int32, (TQ, LANE), 1)
    cmp_ok4 = _tile4((col >= 1) & (CMP_STRIDE * col + (CMP_STRIDE - 1) <= q0 + row))
    sidx = lax.broadcasted_iota(jnp.int32, (n_slc, TQ), 0)
    qblk = lax.shift_right_logical(q0 + lax.broadcasted_iota(jnp.int32, (n_slc, TQ), 1), 6)
    sel_valid = sidx <= qblk
    sel_forced = (sidx == 0) | (sidx == qblk) | (sidx == qblk - 1)
    lane_row = lax.broadcasted_iota(jnp.int32, (LANE, TQ), 0)
    n_far = lax.shift_right_logical(jnp.maximum(qi - 1, 0), 2)
    up4 = _tile4(tabt_ref[0, :, 2 * LANE:3 * LANE])
    fresh = (jnp.full((GQ * TQ, 1), NEG, F32), jnp.zeros((GQ * TQ, 1), F32), jnp.zeros((GQ * TQ, LANE), F32))

    def per_group(g, carry):
        heads = [GQ * g + r for r in range(GQ)]
        qs = jnp.concatenate([q_ref[h] for h in heads], axis=0)
        near = jnp.concatenate([tabt_ref[h, :, 0:2 * LANE] for h in heads], axis=0)

        bias_c = jnp.concatenate([tabc_ref[0, h] for h in heads], axis=0)
        s = jnp.where(cmp_ok4, _qk(qs, cmp_ref[0, g]) + bias_c, NEG)
        e = jnp.exp(s - jnp.max(s, axis=1, keepdims=True))
        p = (e / jnp.sum(e, axis=1, keepdims=True) * cmp_ok4.astype(F32)).astype(BF16)
        o_cmp = _mm(p, cmp_ref[0, 2])
        imp4 = _mm(p, sp_ref[...])
        imp = imp4[0:TQ] + imp4[TQ:2 * TQ] + imp4[2 * TQ:3 * TQ] + imp4[3 * TQ:4 * TQ]

        score = jnp.where(sel_valid, imp.T[0:n_slc, :] + jnp.where(sel_forced, FORCE, 0.0), -FORCE)
        rank = jnp.zeros((n_slc, TQ), F32)
        for k in range(1, n_slc):
            other = pltpu.roll(score, k, axis=0)
            rank = rank + jnp.where(sidx >= k, (other >= score).astype(F32), (other > score).astype(F32))
        sel_t = (rank < float(N_SELECT)).astype(F32)
        aug_t = jnp.concatenate([jnp.zeros((SEL_LANE0, TQ), F32), sel_t - 1.0,
                                 jnp.zeros((LANE - SEL_LANE0 - n_slc, TQ), F32)], axis=0)
        aug_t = jnp.where(lane_row == PAD_LANE, -1.0, aug_t)
        qa = qs + _tile4(aug_t.T.astype(BF16))

        def far_tile(it, state):
            start = pl.multiple_of(q0 - FAR * (n_far - it), LANE)
            sc = _qk(qa, ks_ref[g, 0, pl.ds(start, FAR), :])
            return _softmax_fold(state, [sc], sv_ref[0, pl.ds(start, FAR), :])

        state = lax.fori_loop(0, n_far, far_tile, fresh)
        sc = _qk(qa, ks_ref[g, 0, pl.ds(q0, TAIL), :])
        pieces = [sc[:, 0:TAIL - 2 * LANE], sc[:, TAIL - 2 * LANE:TAIL] + near]
        _, l, acc = _softmax_fold(state, pieces, sv_ref[0, pl.ds(q0, TAIL), :])
        o_slc = acc / l

        sc = _qk(qa, kw_ref[g, 0, pl.ds(q0, TAIL), :])
        pieces = [sc[:, 0:LANE] + up4, sc[:, LANE:TAIL - 2 * LANE], sc[:, TAIL - 2 * LANE:TAIL] + near]
        _, l, acc = _softmax_fold(fresh, pieces, wv_ref[0, pl.ds(q0, TAIL), :])
        o_win = acc / l

        gt = gate_ref[g]
        own = lax.shift_right_logical(col, 6) == g
        for r in range(GQ):
            rs = slice(r * TQ, (r + 1) * TQ)
            comb = gt[:, 3 * r:3 * r + 1] * o_cmp[rs] + gt[:, 3 * r + 1:3 * r + 2] * o_slc[rs] \
                + gt[:, 3 * r + 2:3 * r + 3] * o_win[rs]
            ocat_ref[GQ * g + r] = jnp.where(own, comb, 0.0)
        return carry

    lax.fori_loop(0, N_KV, per_group, 0)
    o = jnp.concatenate([ocat_ref[h] for h in range(N_HEADS)], axis=1)
    o_ref[...] = _rms(o, gn_ref[...], N_HEADS * HEAD_DIM).astype(BF16)


def _attn_prompt(q, gates, cmpkv, ks, sv, kw, wv, tab_t, tab_c, sp, gn_pad, batch, seq):
    nq = seq // TQ
    t = batch * seq
    rows = ks.shape[2]
    n_slc = seq // SLC_LEN
    assert WINDOW >= FAR - LANE and SEL_LANE0 + n_slc <= PAD_LANE < LANE
    return pl.pallas_call(
        functools.partial(_attn_prompt_kernel, n_slc=n_slc),
        grid=(batch, nq),
        in_specs=[
            pl.BlockSpec((N_HEADS, TQ, LANE), lambda b, i: (0, b * nq + i, 0)),
            pl.BlockSpec((N_KV, TQ, LANE), lambda b, i: (0, b * nq + i, 0)),
            pl.BlockSpec((1, 3, seq // CMP_STRIDE, LANE), lambda b, i: (b, 0, 0, 0)),
            pl.BlockSpec((N_KV, 1, rows, LANE), lambda b, i: (0, b, 0, 0)),
            pl.BlockSpec((1, rows, LANE), lambda b, i: (b, 0, 0)),
            pl.BlockSpec((N_KV, 1, rows, LANE), lambda b, i: (0, b, 0, 0)),
            pl.BlockSpec((1, rows, LANE), lambda b, i: (b, 0, 0)),
            _const_spec(tab_t.shape),
            pl.BlockSpec((1, N_HEADS, LANE, LANE), lambda b, i: (i, 0, 0, 0)),
            _const_spec(sp.shape),
            _const_spec(gn_pad.shape),
        ],
        out_specs=pl.BlockSpec((TQ, N_HEADS * LANE), lambda b, i: (b * nq + i, 0)),
        out_shape=jax.ShapeDtypeStruct((t, N_HEADS * LANE), BF16),
        scratch_shapes=[pltpu.VMEM((N_HEADS, TQ, LANE), F32)],
        compiler_params=_cparams(2),
        name="attn_prompt",
    )(q, gates, cmpkv, ks, sv, kw, wv, tab_t, tab_c, sp, gn_pad)


def _sample_cw_kernel(qlo_ref, qgl_ref, cmp_ref, cwin_ref, wnew_ref, tabc_ref, tabw_ref, rb0_ref, sp_ref,
                      ocmp_ref, owin_ref, idx_ref, *, n_slc, qblk, win_buf):
    qlo = qlo_ref[0]
    qgl = qgl_ref[0]
    n_cmp_rows = cmp_ref.shape[2]
    col = lax.broadcasted_iota(jnp.int32, (N_HEADS, n_cmp_rows), 1)
    grp_of_head = lax.shift_right_logical(lax.broadcasted_iota(jnp.int32, (N_HEADS, n_cmp_rows), 0), 2)
    ok = col >= 1
    raw = jnp.where(grp_of_head == 0, _qk(qlo, cmp_ref[0, 0]), _qk(qlo, cmp_ref[0, 1]))
    s = jnp.where(ok, raw + tabc_ref[...], NEG)
    e = jnp.exp(s - jnp.max(s, axis=1, keepdims=True))
    p = (e / jnp.sum(e, axis=1, keepdims=True) * ok.astype(F32)).astype(BF16)
    ocmp_ref[0] = _mm(p, cmp_ref[0, 2])
    imp8 = _mm(p, sp_ref[...])

    width = imp8.shape[1]
    head_grp = lax.shift_right_logical(lax.broadcasted_iota(jnp.int32, (N_HEADS, width), 0), 2)
    sidx = lax.broadcasted_iota(jnp.int32, (1, width), 1)
    sidx_f = sidx.astype(F32)
    forced = (sidx == 0) | (sidx == qblk) | (sidx == qblk - 1)
    lane = lax.broadcasted_iota(jnp.int32, (1, LANE), 1)
    for g in range(N_KV):
        imp = jnp.sum(jnp.where(head_grp == g, imp8, 0.0), axis=0, keepdims=True)
        score = jnp.where(sidx <= qblk, imp + jnp.where(forced, FORCE, 0.0), -FORCE)
        score = jnp.where(sidx < n_slc, score, -jnp.inf)
        picked = jnp.zeros((1, LANE), F32)
        for k in range(N_SELECT):
            best = jnp.max(score, axis=1, keepdims=True)
            ik = jnp.min(jnp.where(score == best, sidx_f, float(width)), axis=1, keepdims=True)
            picked = jnp.where(lane == k, ik, picked)
            score = jnp.where(sidx_f == ik, -jnp.inf, score)
        idx_ref[0, g:g + 1, :] = picked.astype(jnp.int32)

    kt = cwin_ref[0, 0].reshape(N_KV * HEAD_DIM, win_buf).astype(BF16)
    vt = cwin_ref[0, 1].reshape(N_KV * HEAD_DIM, win_buf).astype(BF16)
    knew = wnew_ref[0, :, 0:LANE].astype(BF16).astype(F32)
    vnew = wnew_ref[0, :, LANE:2 * LANE].astype(BF16).astype(F32)
    colw = lax.broadcasted_iota(jnp.int32, (N_HEADS, win_buf), 1)
    okw = colw > win_buf - WINDOW
    sw = jnp.where(okw, _mm(qgl, kt) + tabw_ref[...], NEG)
    s_new = jnp.sum(qgl.astype(F32) * knew, axis=1, keepdims=True) + rb0_ref[:, 0:1]
    mw = jnp.maximum(jnp.max(sw, axis=1, keepdims=True), s_new)
    ew = jnp.where(okw, jnp.exp(sw - mw), 0.0)
    e_new = jnp.exp(s_new - mw)
    lw = jnp.sum(ew, axis=1, keepdims=True) + e_new
    owin_ref[0] = (_qk(ew.astype(BF16), vt) + e_new.astype(BF16).astype(F32) * vnew) / lw


def _sample_cw(qlo, qgl, cmpkv, cwin, wnew, tab_c, tab_w, rb0, sp, n_slc, qblk):
    n = qlo.shape[0]
    win_buf = cwin.shape[-1]
    blk = lambda a: pl.BlockSpec((1,) + a.shape[1:], lambda b: (b,) + (0,) * (a.ndim - 1))
    outs = (jax.ShapeDtypeStruct((n, N_HEADS, LANE), F32), jax.ShapeDtypeStruct((n, N_HEADS, LANE), F32),
            jax.ShapeDtypeStruct((n, N_KV, LANE), jnp.int32))
    return pl.pallas_call(
        functools.partial(_sample_cw_kernel, n_slc=n_slc, qblk=qblk, win_buf=win_buf),
        grid=(n,),
        in_specs=[blk(qlo), blk(qgl), blk(cmpkv), blk(cwin), blk(wnew), _const_spec(tab_c.shape),
                  _const_spec(tab_w.shape), _const_spec(rb0.shape), _const_spec(sp.shape)],
        out_specs=tuple(pl.BlockSpec((1,) + o.shape[1:], lambda b: (b, 0, 0)) for o in outs),
        out_shape=outs,
        compiler_params=_cparams(1),
        name="sample_cmp_win",
    )(qlo, qgl, cmpkv, cwin, wnew, tab_c, tab_w, rb0, sp)


def _sample_slc_kernel(pg_ref, hf_ref, bid_ref, *refs, new_block, near_block):
    n_blk = N_KV * N_SELECT
    blocks = refs[:n_blk]
    (q_ref, kvnew_ref, ocmp_ref, owin_ref, gate_ref, tnear_ref, tlast_ref, rb0_ref, gn_ref, o_ref) = refs[n_blk:]
    b = pl.program_id(0)
    n_keys = N_SELECT * PAGE_SIZE
    lane = lax.broadcasted_iota(jnp.int32, (1, n_keys), 1)
    slot = lax.shift_right_logical(lane, 7)
    half = lax.shift_right_logical(lane, 6) & 1
    head_grp = lax.shift_right_logical(lax.broadcasted_iota(jnp.int32, (N_HEADS, LANE), 0), 2)
    lane_grp = lax.shift_right_logical(lax.broadcasted_iota(jnp.int32, (N_HEADS, LANE), 1), 6)
    knew = kvnew_ref[0, :, 256:384].astype(BF16).astype(F32)
    vnew = kvnew_ref[0, :, 384:512].astype(BF16).astype(F32)
    qs = q_ref[0]
    o_slc = jnp.zeros((N_HEADS, LANE), F32)
    for g in range(N_KV):
        mine = blocks[g * N_SELECT:(g + 1) * N_SELECT]
        kt = jnp.concatenate([blk[0, 0].reshape(N_KV * HEAD_DIM, PAGE_SIZE) for blk in mine], axis=1).astype(BF16)
        vt = jnp.concatenate([blk[0, 1].reshape(N_KV * HEAD_DIM, PAGE_SIZE) for blk in mine], axis=1).astype(BF16)
        bid = jnp.zeros((1, n_keys), jnp.int32)
        hsel = jnp.zeros((1, n_keys), jnp.int32)
        has_new = bid_ref[b, g * N_SELECT] == new_block
        for kk in range(N_SELECT):
            bid = jnp.where(slot == kk, bid_ref[b, g * N_SELECT + kk], bid)
            hsel = jnp.where(slot == kk, hf_ref[b, g * N_SELECT + kk], hsel)
            if kk:
                has_new = has_new | (bid_ref[b, g * N_SELECT + kk] == new_block)
        bias = jnp.where(bid == near_block + 1, tlast_ref[...], jnp.where(bid == near_block, tnear_ref[...], 0.0))
        ok = (bid != new_block) & (half == hsel)
        s = jnp.where(ok, _mm(qs, kt) + bias, NEG)
        s_new = jnp.sum(qs.astype(F32) * knew, axis=1, keepdims=True) + rb0_ref[:, 0:1]
        s_new = jnp.where(has_new, s_new, NEG)
        m = jnp.maximum(jnp.max(s, axis=1, keepdims=True), s_new)
        e = jnp.where(ok, jnp.exp(s - m), 0.0)
        e_new = jnp.where(has_new, jnp.exp(s_new - m), 0.0)
        l = jnp.sum(e, axis=1, keepdims=True) + e_new
        og = (_qk(e.astype(BF16), vt) + e_new.astype(BF16).astype(F32) * vnew) / l
        o_slc = jnp.where(head_grp == g, og, o_slc)
    gt = gate_ref[0]
    comb = gt[:, 0:1] * ocmp_ref[0] + gt[:, 1:2] * o_slc + gt[:, 2:3] * owin_ref[0]
    comb = jnp.where(lane_grp == head_grp, comb, 0.0)
    o = jnp.concatenate([comb[h:h + 1, :] for h in range(N_HEADS)], axis=1)
    o_ref[0] = _rms(o, gn_ref[...], N_HEADS * HEAD_DIM).astype(BF16)


def _sample_slc(pages, halves, bids, cache_t, q, kvnew, ocmp, owin, gates, t_near, t_last, rb0, gn_pad,
                new_block, near_block):
    n = q.shape[0]
    n_blk = N_KV * N_SELECT
    blk_specs = [pl.BlockSpec(_PAGE_BLOCK, lambda b, pg, hf, bi, k=k: (pg[b, k], 1, 0, 0, 0))
                 for k in range(n_blk)]
    per_b = lambda a: pl.BlockSpec((1,) + a.shape[1:], lambda b, *_: (b,) + (0,) * (a.ndim - 1))
    cst = lambda a: pl.BlockSpec(a.shape, lambda *_: (0,) * a.ndim)
    grid_spec = pltpu.PrefetchScalarGridSpec(
        num_scalar_prefetch=3,
        grid=(n,),
        in_specs=blk_specs + [per_b(q), per_b(kvnew), per_b(ocmp), per_b(owin), per_b(gates),
                              cst(t_near), cst(t_last), cst(rb0), cst(gn_pad)],
        out_specs=pl.BlockSpec((1, 1, N_HEADS * LANE), lambda b, *_: (b, 0, 0)),
    )
    return pl.pallas_call(
        functools.partial(_sample_slc_kernel, new_block=new_block, near_block=near_block),
        grid_spec=grid_spec,
        out_shape=jax.ShapeDtypeStruct((n, 1, N_HEADS * LANE), BF16),
        compiler_params=_cparams(1),
        name="sample_selected",
    )(pages, halves, bids, *([cache_t] * n_blk), q, kvnew, ocmp, owin, gates, t_near, t_last, rb0, gn_pad)


def _proj_kernel(on_ref, zn_ref, x_ref, wa_ref, wb_ref, g1_ref, b1_ref, wr_ref, br_ref,
                 x1_ref, x1b_ref, gate_ref, *, alpha):
    mix = _mm(on_ref[...], wa_ref[...]) + _mm(zn_ref[...], wb_ref[...])
    x1 = _layernorm(alpha * x_ref[...] + mix, g1_ref[...], b1_ref[...])
    x1_ref[...] = x1
    xb = x1.astype(BF16)
    x1b_ref[...] = xb
    logits = _mm(xb, wr_ref[...]) + br_ref[...]
    tm = logits.shape[0]
    lane = lax.broadcasted_iota(jnp.int32, (tm, LANE), 1).astype(F32)

    def first_argmax(v, vmax):
        return jnp.min(jnp.where(v == vmax, lane, float(LANE)), axis=1, keepdims=True)

    lg = jnp.where(lane < N_GROUPS, logits[:, 0:LANE], -jnp.inf)
    lg_max = jnp.max(lg, axis=1, keepdims=True)
    eg = jnp.exp(lg - lg_max)
    pg = eg / jnp.sum(eg, axis=1, keepdims=True)
    gidx = first_argmax(lg, lg_max)
    pg_sel = jnp.sum(jnp.where(lane == gidx, pg, 0.0), axis=1, keepdims=True)
    le = jnp.zeros((tm, LANE), F32)
    for gi in range(N_GROUPS):
        le = le + jnp.where(gidx == gi, logits[:, (gi + 1) * LANE:(gi + 2) * LANE], 0.0)
    le = jnp.where(lane < N_EXP, le, -jnp.inf)
    ee = jnp.exp(le - jnp.max(le, axis=1, keepdims=True))
    pe = jnp.where(lane < N_EXP, ee / jnp.sum(ee, axis=1, keepdims=True), -1.0)
    v1 = jnp.max(pe, axis=1, keepdims=True)
    i1 = first_argmax(pe, v1)
    pe2 = jnp.where(lane == i1, -1.0, pe)
    v2 = jnp.max(pe2, axis=1, keepdims=True)
    i2 = first_argmax(pe2, v2)
    tot = v1 + v2
    gate_e = jnp.where(lane == i1, v1 / tot * pg_sel, jnp.where(lane == i2, v2 / tot * pg_sel, 0.0))
    for gi in range(N_GROUPS):
        gate_ref[:, gi * LANE:(gi + 1) * LANE] = jnp.where(gidx == gi, gate_e, 0.0)


def _proj(on, zn, x, wa, wb, g1, b1, wr, br, alpha, tm):
    t, d = x.shape
    row = lambda n: pl.BlockSpec((tm, n), lambda i: (i, 0))
    outs = (jax.ShapeDtypeStruct((t, d), F32), jax.ShapeDtypeStruct((t, d), BF16),
            jax.ShapeDtypeStruct((t, N_GROUPS * LANE), F32))
    return pl.pallas_call(
        functools.partial(_proj_kernel, alpha=alpha),
        grid=(t // tm,),
        in_specs=[row(on.shape[1]), row(zn.shape[1]), row(d)] + [_const_spec(a.shape) for a in (wa, wb, g1, b1, wr, br)],
        out_specs=(row(d), row(d), row(N_GROUPS * LANE)),
        out_shape=outs,
        compiler_params=_cparams(1),
        name=f"proj_ln_router_{tm}",
    )(on, zn, x, wa, wb, g1, b1, wr, br)


def _moe_kernel(xb_ref, x1_ref, gate_ref, wgu_ref, wd_ref, g2_ref, b2_ref, out_ref, acc_ref, *, alpha, d_ff):
    gi = pl.program_id(1)

    @pl.when(gi == 0)
    def _():
        acc_ref[...] = jnp.zeros(acc_ref.shape, F32)

    x = xb_ref[...]
    gt = gate_ref[...]
    acc = acc_ref[...]
    for e in range(N_EXP):
        au = _mm(x, wgu_ref[0, e])
        h = jax.nn.silu(au[:, 0:d_ff]) * au[:, d_ff:2 * d_ff]
        acc = acc + _mm((h * gt[:, e:e + 1]).astype(BF16), wd_ref[0, e])
    acc_ref[...] = acc

    @pl.when(gi == N_GROUPS - 1)
    def _():
        out_ref[...] = _layernorm(alpha * x1_ref[...] + acc_ref[...], g2_ref[...], b2_ref[...])


def _moe(xb, x1, gate, wgu, wd, g2, b2, alpha, tm):
    t, d = x1.shape
    d_ff = wd.shape[2]
    return pl.pallas_call(
        functools.partial(_moe_kernel, alpha=alpha, d_ff=d_ff),
        grid=(t // tm, N_GROUPS),
        in_specs=[
            pl.BlockSpec((tm, d), lambda i, g: (i, 0)),
            pl.BlockSpec((tm, d), lambda i, g: (i, 0)),
            pl.BlockSpec((tm, LANE), lambda i, g: (i, g)),
            pl.BlockSpec((1, N_EXP, d, 2 * d_ff), lambda i, g: (g, 0, 0, 0)),
            pl.BlockSpec((1, N_EXP, d_ff, d), lambda i, g: (g, 0, 0, 0)),
            _const_spec(g2.shape), _const_spec(b2.shape),
        ],
        out_specs=pl.BlockSpec((tm, d), lambda i, g: (i, 0)),
        out_shape=jax.ShapeDtypeStruct((t, d), F32),
        scratch_shapes=[pltpu.VMEM((tm, d), F32)],
        compiler_params=_cparams(2),
        name=f"moe_ln_{tm}",
    )(xb, x1, gate, wgu, wd, g2, b2)


def _pad_head_lanes(v):
    vh = v.reshape(N_HEADS, HEAD_DIM)
    z = jnp.zeros_like(vh)
    grp = (jnp.arange(N_HEADS) // GQ)[:, None]
    return jnp.where(grp == 0, jnp.concatenate([vh, z], axis=1), jnp.concatenate([z, vh], axis=1)).reshape(1, -1)


def _pad_head_rows(w):
    wh = w.reshape(N_HEADS, HEAD_DIM, -1)
    z = jnp.zeros_like(wh)
    grp = (jnp.arange(N_HEADS) // GQ)[:, None, None]
    return jnp.where(grp == 0, jnp.concatenate([wh, z], axis=1), jnp.concatenate([z, wh], axis=1)).reshape(
        N_HEADS * LANE, -1)


def _front_pad(a, axis, pad_row):
    shape = list(a.shape)
    shape[axis] = WINDOW
    return jnp.concatenate([jnp.broadcast_to(pad_row.astype(a.dtype), shape), a], axis=axis)


def kernel(x_prompt, x_sample, cache_kv, page_table, cache_win, state_conv, w_in, conv_w, pe_k, pe_v, w_ck1, w_ck2, w_cv1, w_cv2, g_nsa, g_conv, w_out, ln1_g, ln1_b, w_rg, b_rg, w_re, b_re, w_eg, w_eu, w_ed, ln2_g, ln2_b, rel_bias):
    batch, seq, d_model = x_prompt.shape
    dec_batch, dec_seq = x_sample.shape[0], x_sample.shape[1]
    depth = w_in.shape[0]
    n_pages = page_table.shape[1]
    past = n_pages * PAGE_SIZE
    win_buf = cache_win.shape[2]
    d_nsa = N_HEADS * HEAD_DIM
    d_conv = w_out.shape[1] - d_nsa
    assert dec_seq == 1 and seq % TM == 0 and min(WINDOW, seq) == TM and win_buf == WINDOW
    assert d_conv == 512 and seq // CMP_STRIDE == LANE and n_pages % SAMPLE_PAGES == 0
    alpha = (2 * depth) ** 0.25
    n_slc_s = -(-(past + dec_seq) // SLC_LEN)
    qblk_s = past // SLC_LEN

    idx_np, sp_p_np, sp_s_np = _static_tables(seq, past, win_buf)
    tabs = _bias_tables(rel_bias, jnp.asarray(idx_np))
    nq = seq // TQ
    tab_t = jnp.transpose(tabs[0:_N_NEAR_TILES], (1, 2, 0, 3)).reshape(N_HEADS, LANE, _N_NEAR_TILES * LANE)
    tab_c = tabs[_N_NEAR_TILES:_N_NEAR_TILES + nq]
    smp = tabs[_N_NEAR_TILES + nq]
    tab_w_s = smp[:, 0:5, :].reshape(N_HEADS, 5 * LANE)[:, :win_buf]
    tab_c_s = smp[:, 5:13, :].reshape(N_HEADS, 8 * LANE)
    reps = N_SELECT * PAGE_SIZE // SLC_LEN
    t_near = jnp.tile(smp[:, 13, 0:SLC_LEN], (1, reps))
    t_last = jnp.tile(smp[:, 13, SLC_LEN:2 * SLC_LEN], (1, reps))
    rb0_rep = jnp.broadcast_to(smp[:, 14, 0:1], (N_HEADS, LANE))
    sp_p = jnp.asarray(sp_p_np, BF16)
    sp_s = jnp.asarray(sp_s_np, BF16)
    key_pad = jnp.zeros((LANE,), F32).at[PAD_LANE].set(KEY_MASK)
    val_pad = jnp.zeros((LANE,), F32)

    xp = x_prompt.reshape(batch * seq, d_model)
    xs = x_sample.reshape(dec_batch * dec_seq, d_model)
    outs = [[] for _ in range(6)]
    for l in range(depth):
        w_pack = _pack_in_weights(w_in[l])
        gc = g_conv[l].reshape(1, -1)
        gn_pad = _pad_head_lanes(g_nsa[l])
        w1k, w2k, pek = _pack_compress_weights(pe_k[l], w_ck1[l], w_ck2[l], True)
        w1v, w2v, pev = _pack_compress_weights(pe_v[l], w_cv1[l], w_cv2[l], False)
        w1 = jnp.stack([w1k, w1v])
        pe = jnp.stack([pek, pev])
        wa = _pad_head_rows(w_out[l][:d_nsa]).astype(BF16)
        wb = w_out[l][d_nsa:].astype(BF16)
        wr = jnp.concatenate(
            [jnp.pad(w_rg[l], ((0, 0), (0, LANE - N_GROUPS)))]
            + [jnp.pad(w_re[l][:, gi * N_EXP:(gi + 1) * N_EXP], ((0, 0), (0, LANE - N_EXP))) for gi in range(N_GROUPS)],
            axis=1).astype(BF16)
        br = jnp.concatenate(
            [jnp.pad(b_rg[l], (0, LANE - N_GROUPS))]
            + [jnp.pad(b_re[l][gi * N_EXP:(gi + 1) * N_EXP], (0, LANE - N_EXP)) for gi in range(N_GROUPS)]).reshape(1, -1)
        wgu = jnp.concatenate([w_eg[l], w_eu[l]], axis=-1).astype(BF16)
        wd = w_ed[l].astype(BF16)
        g1, b1 = ln1_g[l].reshape(1, -1), ln1_b[l].reshape(1, -1)
        g2, b2 = ln2_g[l].reshape(1, -1), ln2_b[l].reshape(1, -1)

        q_p, kvt_p, ks, sv, kw, wv, wtail_t, gates_p, zn_p, ctail = _inproj_prompt(xp, w_pack, conv_w[l], gc, batch, seq)
        kvt_p = kvt_p.reshape(batch, 4, N_KV, HEAD_DIM, seq)
        cmp_p = _compress_prompt(kvt_p, batch, seq, w1, w2k, w2v, pe)
        ks = _front_pad(ks.reshape(N_KV, batch, seq, LANE), 2, key_pad)
        kw = _front_pad(kw.reshape(N_KV, batch, seq, LANE), 2, key_pad)
        sv = _front_pad(sv.reshape(batch, seq, LANE), 1, val_pad)
        wv = _front_pad(wv.reshape(batch, seq, LANE), 1, val_pad)
        on_p = _attn_prompt(q_p, gates_p, cmp_p, ks, sv, kw, wv, tab_t, tab_c, sp_p, gn_pad, batch, seq)
        x1_p, x1b_p, gate_p = _proj(on_p, zn_p, xp, wa, wb, g1, b1, wr, br, alpha, TM)
        y_p = _moe(x1b_p, x1_p, gate_p, wgu, wd, g2, b2, alpha, TM)

        st = state_conv[l]
        qlo_s, qgl_s, kv_s, win_s, gates_s, zn_s, u_s = _inproj_sample(xs, w_pack, conv_w[l], gc, st[:, 0], st[:, 1])
        cache_t = jnp.transpose(cache_kv[l], (0, 2, 3, 4, 1))
        cwin_t = jnp.transpose(cache_win[l], (0, 2, 3, 4, 1))
        cmp_s = _compress_sample(cache_t, page_table, w1, w2k, w2v, pe)
        qlo_s3 = qlo_s.reshape(dec_batch, N_HEADS, LANE)
        qgl_s3 = qgl_s.reshape(dec_batch, N_HEADS, LANE)
        ocmp, owin, sel_idx = _sample_cw(qlo_s3, qgl_s3, cmp_s, cwin_t, win_s.reshape(dec_batch, 1, 256), tab_c_s,
                                         tab_w_s, rb0_rep, sp_s, n_slc_s, qblk_s)
        bids = sel_idx[:, :, :N_SELECT].reshape(dec_batch, N_KV * N_SELECT)
        blk_pages = jnp.take_along_axis(page_table, jnp.minimum(bids // 2, n_pages - 1), axis=1)
        gates_s3 = jnp.pad(gates_s.reshape(dec_batch, N_KV, LANE)[:, :, :3 * GQ].reshape(dec_batch, N_HEADS, 3),
                           ((0, 0), (0, 0), (0, LANE - 3)))
        on_s = _sample_slc(blk_pages, bids % 2, bids, cache_t, qgl_s3, kv_s.reshape(dec_batch, 1, 512), ocmp, owin,
                           gates_s3, t_near, t_last, rb0_rep, gn_pad, n_slc_s - 1, qblk_s - 2)
        on_s = on_s.reshape(dec_batch, N_HEADS * LANE)
        x1_s, x1b_s, gate_s = _proj(on_s, zn_s, xs, wa, wb, g1, b1, wr, br, alpha, dec_batch)
        y_s = _moe(x1b_s, x1_s, gate_s, wgu, wd, g2, b2, alpha, dec_batch)

        to_token_major = lambda a: jnp.transpose(a, (0, 4, 1, 2, 3))
        outs[0].append(to_token_major(kvt_p))
        outs[1].append(kv_s.reshape(dec_batch, dec_seq, 4, N_KV, HEAD_DIM))
        outs[2].append(to_token_major(wtail_t.reshape(batch, 2, N_KV, HEAD_DIM, TM)))
        win_all_t = jnp.concatenate([cwin_t, win_s.reshape(dec_batch, 2, N_KV, HEAD_DIM, dec_seq)], axis=-1)
        outs[3].append(to_token_major(win_all_t[..., win_all_t.shape[-1] - min(WINDOW, past + dec_seq):]))
        outs[4].append(ctail)
        outs[5].append(jnp.concatenate([st, u_s[:, None, :]], axis=1)[:, dec_seq:])
        xp, xs = y_p, y_s
    return (xp.reshape(batch, seq, d_model), xs.reshape(dec_batch, dec_seq, d_model),
            jnp.stack(outs[0]), jnp.stack(outs[1]), jnp.stack(outs[2]), jnp.stack(outs[3]),
            jnp.stack(outs[4]), jnp.stack(outs[5]))
```

```python
import functools
import math

import numpy as np
import jax
import jax.numpy as jnp
from jax import lax
from jax.experimental import pallas as pl
from jax.experimental.pallas import tpu as pltpu

F32 = jnp.float32
BF16 = jnp.bfloat16

HEAD_DIM = 64
N_KV = 2
GQ = 4
N_HEADS = N_KV * GQ
CONV_W = 3
CMP_STRIDE = 16
CMP_LEN = 32
CMP_HIDDEN = 2 * HEAD_DIM
SLC_LEN = 64
N_SELECT = 16
WINDOW = 512
N_BUCKETS = 32
MAX_DISTANCE = 128
N_GROUPS = 4
N_EXP = 8
PAGE_SIZE = 128
NEG = -1e30
FORCE = 1e6

LANE = 128
SUBLANE = 8
VMEM_LIMIT = 52 * 1024 * 1024

TQ = 128
TM = 512
FAR = 512
TAIL = WINDOW + TQ
SAMPLE_PAGES = 32
TMOE = 1024
MOE_CHUNK = 128
MOE_SUB = 64
SEG_ALIGN = 16
SORT_ROWS = TMOE + N_GROUPS * SEG_ALIGN + MOE_CHUNK + 64
EXPERT_SPLIT = 2
GROUP_LANE = N_EXP

KEY_MASK = 2.0 ** 100
TABLE_MASK = 2 * NEG
MASKED_BUCKET = N_BUCKETS
SEL_LANE0 = HEAD_DIM
PAD_LANE = HEAD_DIM + 32


def _cparams(n_axes):
    return pltpu.CompilerParams(dimension_semantics=("arbitrary",) * n_axes, vmem_limit_bytes=VMEM_LIMIT)


def _const_spec(shape):
    nd = len(shape)
    return pl.BlockSpec(shape, lambda *_, nd=nd: (0,) * nd)


def _qk(a, b):
    return lax.dot_general(a, b, (((1,), (1,)), ((), ())), preferred_element_type=F32)


def _mm(a, b):
    return jnp.dot(a, b, preferred_element_type=F32)


def _bucket_np(dist):
    n = np.maximum(dist, 0)
    max_exact = N_BUCKETS // 2
    nf = np.maximum(n, 1).astype(np.float32)
    large = max_exact + (np.log(nf / np.float32(max_exact)) / np.float32(math.log(MAX_DISTANCE / max_exact))
                         * np.float32(N_BUCKETS - max_exact)).astype(np.int32)
    large = np.minimum(large, N_BUCKETS - 1)
    return np.where(n < max_exact, n, large).astype(np.int32)


def _overlap_np(c, s):
    c0 = c * CMP_STRIDE
    s0 = s * SLC_LEN
    return np.maximum(np.minimum(c0 + CMP_LEN, s0 + SLC_LEN) - np.maximum(c0, s0), 0)


_N_NEAR_TILES = 3


def _static_tables(seq, past, win_buf):
    nq = seq // TQ
    i = np.arange(LANE)[:, None]
    j = np.arange(LANE)[None, :]
    far_bucket = N_BUCKETS - 1
    tiles = [_bucket_np(LANE + i - j),
             np.where(i >= j, _bucket_np(i - j), MASKED_BUCKET),
             np.where(j > i, far_bucket, MASKED_BUCKET) + 0 * i]
    tiles += [_bucket_np(TQ * q + i - (CMP_STRIDE * j + CMP_STRIDE - 1)) for q in range(nq)]
    smp = np.full((LANE, LANE), 10 * MAX_DISTANCE, np.int64)
    k = np.arange(5 * LANE)
    smp[0:5] = np.where(k < win_buf, win_buf - k, 0).reshape(5, LANE)
    ci = np.arange(8 * LANE)
    smp[5:13] = np.maximum(past - (CMP_STRIDE * ci + CMP_STRIDE - 1), 0).reshape(8, LANE)
    nb = past // SLC_LEN
    smp[13] = past - (SLC_LEN * (nb - 2) + np.arange(LANE))
    smp[14] = 0
    tiles.append(_bucket_np(smp))
    idx = np.stack(tiles).astype(np.int32)

    def sp(nrows, ncols, n_slc):
        r = np.arange(nrows)[:, None]
        s = np.arange(ncols)[None, :]
        ov = _overlap_np(r - 1, s)
        return np.where((r >= 1) & (s < n_slc), ov, 0).astype(np.float32)

    sp_p = sp(seq // CMP_STRIDE, LANE, -(-seq // SLC_LEN))
    n_slc_s = -(-(past + 1) // SLC_LEN)
    sp_s = sp(past // CMP_STRIDE, 3 * LANE, n_slc_s)
    return idx, sp_p, sp_s


def _bias_kernel(rb_ref, idx_ref, out_ref):
    idx = idx_ref[0]
    accs = [jnp.full((LANE, LANE), TABLE_MASK, F32) for _ in range(N_HEADS)]
    for b in range(N_BUCKETS):
        hit = idx == b
        for h in range(N_HEADS):
            accs[h] = jnp.where(hit, rb_ref[b, h] - rb_ref[N_BUCKETS - 1, h], accs[h])
    for h in range(N_HEADS):
        out_ref[0, h] = accs[h]


def _bias_tables(rel_bias, idx):
    nt = idx.shape[0]
    return pl.pallas_call(
        _bias_kernel,
        grid=(nt,),
        in_specs=[pl.BlockSpec(memory_space=pltpu.SMEM),
                  pl.BlockSpec((1, LANE, LANE), lambda t: (t, 0, 0))],
        out_specs=pl.BlockSpec((1, N_HEADS, LANE, LANE), lambda t: (t, 0, 0, 0)),
        out_shape=jax.ShapeDtypeStruct((nt, N_HEADS, LANE, LANE), F32),
        compiler_params=_cparams(1),
        name="bias_tables",
    )(rel_bias, idx)


_Q0, _Q1 = 0, N_HEADS * LANE
_KV0, _KV1 = _Q1, _Q1 + 512
_WN0, _WN1 = _KV1, _KV1 + 256
_GT0, _GT1 = _WN1, _WN1 + 2 * LANE
_CV0, _CV1 = _GT1, _GT1 + 3 * 512


def _pack_in_weights(w):
    d = w.shape[0]
    d_nsa = N_HEADS * HEAD_DIM
    wq = w[:, :d_nsa].reshape(d, N_HEADS, HEAD_DIM)
    wq_pad = jnp.pad(wq, ((0, 0), (0, 0), (0, LANE - HEAD_DIM))).reshape(d, N_HEADS * LANE)
    o = d_nsa
    w_kv = w[:, o:o + 512]
    w_win = w[:, o + 512:o + 768]
    wg = w[:, o + 768:o + 768 + 3 * N_HEADS].reshape(d, N_KV, 3 * GQ)
    wg_pad = jnp.pad(wg, ((0, 0), (0, 0), (0, LANE - 3 * GQ))).reshape(d, N_KV * LANE)
    w_conv = w[:, o + 768 + 3 * N_HEADS:]
    return jnp.concatenate([wq_pad, w_kv, w_win, wg_pad, w_conv], axis=1).astype(BF16)


def _rms(x, gain, n, eps=1e-6):
    ms = jnp.sum(x * x, axis=-1, keepdims=True) / n
    return x * lax.rsqrt(ms + eps) * gain


def _layernorm(y, gain, bias, eps=1e-5):
    mu = jnp.mean(y, axis=-1, keepdims=True)
    d = y - mu
    var = jnp.mean(d * d, axis=-1, keepdims=True)
    return d * lax.rsqrt(var + eps) * gain + bias


def _inproj_prompt_kernel(x_ref, w_ref, cw_ref, gc_ref,
                          q_ref, kv_ref, ks_ref, sv_ref, kw_ref, wv_ref, wtail_ref, gate_ref, zn_ref, ctail_ref,
                          uext_ref, *, tiles_per_batch):
    i = pl.program_id(0)
    tm = x_ref.shape[0]
    xb = x_ref[...].astype(BF16)

    def seg(a, b):
        return _mm(xb, w_ref[:, a:b])

    qp = seg(_Q0, _Q1) * (HEAD_DIM ** -0.5)
    for h in range(N_HEADS):
        q_ref[h] = qp[:, h * LANE:(h + 1) * LANE].astype(BF16)
    kv = seg(_KV0, _KV1)
    kv_ref[0] = kv.T
    win = seg(_WN0, _WN1)
    wtail_ref[0] = win.T
    pos = (i % tiles_per_batch) * tm + lax.broadcasted_iota(jnp.int32, (tm, LANE - HEAD_DIM), 0)
    lane = lax.broadcasted_iota(jnp.int32, (tm, LANE - HEAD_DIM), 1)
    blk_flag = jnp.where(lane == lax.shift_right_logical(pos, 6), KEY_MASK, 0.0).astype(BF16)
    no_flag = jnp.zeros((tm, LANE - HEAD_DIM), BF16)
    for g in range(N_KV):
        ks_ref[g] = jnp.concatenate([kv[:, 256 + g * HEAD_DIM:256 + (g + 1) * HEAD_DIM].astype(BF16), blk_flag], axis=1)
        kw_ref[g] = jnp.concatenate([win[:, g * HEAD_DIM:(g + 1) * HEAD_DIM].astype(BF16), no_flag], axis=1)
    sv_ref[...] = kv[:, 384:512].astype(BF16)
    wv_ref[...] = win[:, LANE:2 * LANE].astype(BF16)
    gt = jax.nn.sigmoid(seg(_GT0, _GT1))
    gate_ref[0] = gt[:, 0:LANE]
    gate_ref[1] = gt[:, LANE:2 * LANE]

    conv = seg(_CV0, _CV1)
    cb = conv[:, 0:512]
    u = conv[:, 512:1024] * conv[:, 1024:1536]
    first = (i % tiles_per_batch) == 0

    @pl.when(first)
    def _():
        uext_ref[0:SUBLANE, :] = jnp.zeros((SUBLANE, 512), F32)

    @pl.when(jnp.logical_not(first))
    def _():
        uext_ref[0:SUBLANE, :] = uext_ref[tm:tm + SUBLANE, :]

    uext_ref[SUBLANE:tm + SUBLANE, :] = u
    um1 = uext_ref[SUBLANE - 1:tm + SUBLANE - 1, :]
    um2 = uext_ref[SUBLANE - 2:tm + SUBLANE - 2, :]
    z = cb * (um2 * cw_ref[0:1, :] + um1 * cw_ref[1:2, :] + u * cw_ref[2:3, :])
    zn_ref[...] = _rms(z, gc_ref[...], 512).astype(BF16)
    ctail_ref[0] = uext_ref[tm + SUBLANE - 2:tm + SUBLANE, :]


def _inproj_prompt(x, w, conv_w, g_conv, batch, seq):
    t, d = x.shape
    n_tiles = t // TM
    tpb = seq // TM
    outs = (
        jax.ShapeDtypeStruct((N_HEADS, t, LANE), BF16),
        jax.ShapeDtypeStruct((batch, 512, seq), F32),
        jax.ShapeDtypeStruct((N_KV, t, LANE), BF16),
        jax.ShapeDtypeStruct((t, LANE), BF16),
        jax.ShapeDtypeStruct((N_KV, t, LANE), BF16),
        jax.ShapeDtypeStruct((t, LANE), BF16),
        jax.ShapeDtypeStruct((batch, 256, TM), F32),
        jax.ShapeDtypeStruct((N_KV, t, LANE), F32),
        jax.ShapeDtypeStruct((t, 512), BF16),
        jax.ShapeDtypeStruct((batch, CONV_W - 1, 512), F32),
    )
    row = lambda n: pl.BlockSpec((TM, n), lambda i: (i, 0))
    planes = lambda n: pl.BlockSpec((n, TM, LANE), lambda i: (0, i, 0))
    return pl.pallas_call(
        functools.partial(_inproj_prompt_kernel, tiles_per_batch=tpb),
        grid=(n_tiles,),
        in_specs=[row(d), _const_spec(w.shape), _const_spec(conv_w.shape), _const_spec(g_conv.shape)],
        out_specs=(
            planes(N_HEADS),
            pl.BlockSpec((1, 512, TM), lambda i: (i // tpb, 0, i % tpb)),
            planes(N_KV), row(LANE), planes(N_KV), row(LANE),
            pl.BlockSpec((1, 256, TM), lambda i: (i // tpb, 0, 0)),
            planes(N_KV),
            row(512),
            pl.BlockSpec((1, CONV_W - 1, 512), lambda i: (i // tpb, 0, 0)),
        ),
        out_shape=outs,
        scratch_shapes=[pltpu.VMEM((TM + SUBLANE, 512), F32)],
        compiler_params=_cparams(1),
        name="inproj_prompt",
    )(x, w, conv_w, g_conv)


def _inproj_sample_kernel(x_ref, w_ref, cw_ref, gc_ref, s0_ref, s1_ref,
                          qlo_ref, qgl_ref, kv_ref, win_ref, gate_ref, zn_ref, u_ref):
    xb = x_ref[...].astype(BF16)

    def seg(a, b):
        return _mm(xb, w_ref[:, a:b])

    qp = seg(_Q0, _Q1) * (HEAD_DIM ** -0.5)
    qlo_ref[...] = qp.astype(BF16)
    blocks = [qp[:, h * LANE:(h + 1) * LANE] for h in range(N_HEADS)]
    blocks = [b if h < GQ else pltpu.roll(b, HEAD_DIM, axis=1) for h, b in enumerate(blocks)]
    qgl_ref[...] = jnp.concatenate(blocks, axis=1).astype(BF16)
    kv_ref[...] = seg(_KV0, _KV1)
    win_ref[...] = seg(_WN0, _WN1)
    gate_ref[...] = jax.nn.sigmoid(seg(_GT0, _GT1))
    conv = seg(_CV0, _CV1)
    cb = conv[:, 0:512]
    u = conv[:, 512:1024] * conv[:, 1024:1536]
    z = cb * (s0_ref[...] * cw_ref[0:1, :] + s1_ref[...] * cw_ref[1:2, :] + u * cw_ref[2:3, :])
    zn_ref[...] = _rms(z, gc_ref[...], 512).astype(BF16)
    u_ref[...] = u


def _inproj_sample(x, w, conv_w, g_conv, s0, s1):
    n = x.shape[0]
    outs = (
        jax.ShapeDtypeStruct((n, N_HEADS * LANE), BF16),
        jax.ShapeDtypeStruct((n, N_HEADS * LANE), BF16),
        jax.ShapeDtypeStruct((n, 512), F32),
        jax.ShapeDtypeStruct((n, 256), F32),
        jax.ShapeDtypeStruct((n, N_KV * LANE), F32),
        jax.ShapeDtypeStruct((n, 512), BF16),
        jax.ShapeDtypeStruct((n, 512), F32),
    )
    args = (x, w, conv_w, g_conv, s0, s1)
    return pl.pallas_call(
        _inproj_sample_kernel,
        grid=(1,),
        in_specs=[_const_spec(a.shape) for a in args],
        out_specs=tuple(_const_spec(o.shape) for o in outs),
        out_shape=outs,
        compiler_params=_cparams(1),
        name="inproj_sample",
    )(*args)


def _pack_compress_weights(pe, w1, w2, per_group_out):
    w = w1.reshape(2, CMP_STRIDE, HEAD_DIM, CMP_HIDDEN).transpose(1, 2, 0, 3).reshape(CMP_STRIDE, HEAD_DIM, 2 * CMP_HIDDEN)
    z = jnp.zeros_like(w)
    top = jnp.concatenate([w, z], axis=-1)
    bot = jnp.concatenate([z, w], axis=-1)
    w1bd = jnp.stack([top, bot], axis=1).reshape(CMP_STRIDE * 2 * HEAD_DIM, 4 * CMP_HIDDEN).astype(BF16)
    z2 = jnp.zeros_like(w2)
    if per_group_out:
        w2bd = jnp.concatenate([jnp.concatenate([w2, z2, z2, z2], axis=1),
                                jnp.concatenate([z2, z2, w2, z2], axis=1)], axis=0).astype(BF16)
    else:
        w2bd = jnp.concatenate([jnp.concatenate([w2, z2], axis=1), jnp.concatenate([z2, w2], axis=1)], axis=0).astype(BF16)
    pe_rows = jnp.broadcast_to(pe.reshape(2, CMP_STRIDE, 1, HEAD_DIM), (2, CMP_STRIDE, N_KV, HEAD_DIM))
    pe_rows = pe_rows.reshape(2, CMP_STRIDE * N_KV * HEAD_DIM)
    pe_rows = jnp.pad(pe_rows, ((0, SUBLANE - 2), (0, 0)))
    return w1bd, w2bd, pe_rows


def _compress_kernel(*refs, n_pages, n_prefetch):
    refs = refs[n_prefetch:]
    page_refs = refs[:n_pages]
    w1_ref, w2k_ref, w2v_ref, pe_ref, out_ref, carry_ref, tok_ref = refs[n_pages:]
    step = pl.program_id(1)
    m = n_pages * (PAGE_SIZE // CMP_STRIDE)

    @pl.when(step == 0)
    def _():
        carry_ref[...] = jnp.zeros(carry_ref.shape, F32)

    row0 = lax.broadcasted_iota(jnp.int32, (m, CMP_HIDDEN), 0) == 0
    for t, w2_ref in enumerate((w2k_ref, w2v_ref)):
        for p, pr in enumerate(page_refs):
            slab = pr[0, t].reshape(N_KV * HEAD_DIM, PAGE_SIZE)
            tok_ref[p * PAGE_SIZE:(p + 1) * PAGE_SIZE, :] = slab.T
        pieces = [tok_ref[pl.ds(j, m, stride=CMP_STRIDE), :] for j in range(CMP_STRIDE)]
        lhs = jnp.concatenate([jnp.concatenate(pieces, axis=1), pe_ref[t]], axis=0).astype(BF16)
        a = _mm(lhs, w1_ref[t])
        hs = []
        for g in range(N_KV):
            c0 = g * 2 * CMP_HIDDEN
            a0 = a[0:m, c0:c0 + CMP_HIDDEN]
            a1 = a[0:m, c0 + CMP_HIDDEN:c0 + 2 * CMP_HIDDEN]
            pe_term = a[m:m + 1, c0:c0 + CMP_HIDDEN] + a[m + 1:m + 2, c0 + CMP_HIDDEN:c0 + 2 * CMP_HIDDEN]
            prev = carry_ref[t, 0:1, g * CMP_HIDDEN:(g + 1) * CMP_HIDDEN]
            shifted = jnp.where(row0, prev, pltpu.roll(a0, 1, axis=0))
            carry_ref[t, 0:1, g * CMP_HIDDEN:(g + 1) * CMP_HIDDEN] = a0[m - 1:m, :]
            hs.append(jax.nn.gelu(shifted + a1 + pe_term))
        out = _mm(jnp.concatenate(hs, axis=1).astype(BF16), w2_ref[...]).astype(BF16)
        if t == 0:
            out_ref[0, 0] = out[:, 0:LANE]
            out_ref[0, 1] = out[:, LANE:2 * LANE]
        else:
            out_ref[0, 2] = out


def _compress_call(page_specs, page_args, prefetch, grid, batch, n_chunks, m, w1, w2k, w2v, pe, name):
    n_pages = len(page_specs)
    n_pf = len(prefetch)
    cspec = lambda shape: pl.BlockSpec(shape, lambda *_: (0,) * len(shape))
    grid_spec = pltpu.PrefetchScalarGridSpec(
        num_scalar_prefetch=n_pf,
        grid=grid,
        in_specs=list(page_specs) + [cspec(w1.shape), cspec(w2k.shape), cspec(w2v.shape), cspec(pe.shape)],
        out_specs=pl.BlockSpec((1, 3, m, LANE), lambda b, s, *_: (b, 0, s, 0)),
        scratch_shapes=[pltpu.VMEM((2, SUBLANE, 2 * CMP_HIDDEN), F32),
                        pltpu.VMEM((n_pages * PAGE_SIZE, N_KV * HEAD_DIM), F32)],
    )
    return pl.pallas_call(
        functools.partial(_compress_kernel, n_pages=n_pages, n_prefetch=n_pf),
        grid_spec=grid_spec,
        out_shape=jax.ShapeDtypeStruct((batch, 3, n_chunks, LANE), BF16),
        compiler_params=_cparams(2),
        name=name,
    )(*prefetch, *page_args, w1, w2k, w2v, pe)


_PAGE_BLOCK = (1, 2, N_KV, HEAD_DIM, PAGE_SIZE)


def _compress_prompt(kv_t, batch, seq, w1, w2k, w2v, pe):
    n_pages = seq // PAGE_SIZE
    rows = PAGE_SIZE // CMP_STRIDE
    specs = [pl.BlockSpec(_PAGE_BLOCK, lambda b, s, p=p: (b, 0, 0, 0, p)) for p in range(n_pages)]
    return _compress_call(specs, [kv_t] * n_pages, (), (batch, 1), batch, n_pages * rows, n_pages * rows,
                          w1, w2k, w2v, pe, "compress_prompt")


def _compress_sample(cache_t, page_table, w1, w2k, w2v, pe):
    batch, n_pages = page_table.shape
    rows = PAGE_SIZE // CMP_STRIDE
    specs = [pl.BlockSpec(_PAGE_BLOCK, lambda b, s, pt, k=k: (pt[b, s * SAMPLE_PAGES + k], 0, 0, 0, 0))
             for k in range(SAMPLE_PAGES)]
    return _compress_call(specs, [cache_t] * SAMPLE_PAGES, (page_table,), (batch, n_pages // SAMPLE_PAGES), batch,
                          n_pages * rows, SAMPLE_PAGES * rows, w1, w2k, w2v, pe, "compress_sample")


def _tile4(x):
    return jnp.concatenate([x] * GQ, axis=0)


def _softmax_fold(state, pieces, v):
    m, l, acc = state
    m_new = m
    for pc in pieces:
        m_new = jnp.maximum(m_new, jnp.max(pc, axis=1, keepdims=True))
    alpha = jnp.exp(m - m_new)
    es = [jnp.exp(pc - m_new) for pc in pieces]
    l = alpha * l
    for e in es:
        l = l + jnp.sum(e, axis=1, keepdims=True)
    e_all = es[0] if len(es) == 1 else jnp.concatenate(es, axis=1)
    acc = alpha * acc + _mm(e_all.astype(BF16), v)
    return m_new, l, acc


def _attn_prompt_kernel(q_ref, gate_ref, cmp_ref, ks_ref, sv_ref, kw_ref, wv_ref,
                        tabt_ref, tabc_ref, sp_ref, gn_ref, o_ref, ocat_ref, *, n_slc):
    qi = pl.program_id(1)
    q0 = pl.multiple_of(qi * TQ, TQ)
    row = lax.broadcasted_iota(jnp.int32, (TQ, LANE), 0)
    col = lax.broadcasted_iota(jnp.int32, (TQ, LANE), 1)
    cmp_ok4 = _tile4((col >= 1) & (CMP_STRIDE * col + (CMP_STRIDE - 1) <= q0 + row))
    sidx = lax.broadcasted_iota(jnp.int32, (n_slc, TQ), 0)
    qblk = lax.shift_right_logical(q0 + lax.broadcasted_iota(jnp.int32, (n_slc, TQ), 1), 6)
    sel_valid = sidx <= qblk
    sel_forced = (sidx == 0) | (sidx == qblk) | (sidx == qblk - 1)
    lane_row = lax.broadcasted_iota(jnp.int32, (LANE, TQ), 0)
    n_far = lax.shift_right_logical(jnp.maximum(qi - 1, 0), 2)
    up4 = _tile4(tabt_ref[0, :, 2 * LANE:3 * LANE])
    fresh = (jnp.full((GQ * TQ, 1), NEG, F32), jnp.zeros((GQ * TQ, 1), F32), jnp.zeros((GQ * TQ, LANE), F32))

    def per_group(g, carry):
        heads = [GQ * g + r for r in range(GQ)]
        qs = jnp.concatenate([q_ref[h] for h in heads], axis=0)
        near = jnp.concatenate([tabt_ref[h, :, 0:2 * LANE] for h in heads], axis=0)

        bias_c = jnp.concatenate([tabc_ref[0, h] for h in heads], axis=0)
        s = jnp.where(cmp_ok4, _qk(qs, cmp_ref[0, g]) + bias_c, NEG)
        e = jnp.exp(s - jnp.max(s, axis=1, keepdims=True))
        p = (e / jnp.sum(e, axis=1, keepdims=True) * cmp_ok4.astype(F32)).astype(BF16)
        o_cmp = _mm(p, cmp_ref[0, 2])
        imp4 = _mm(p, sp_ref[...])
        imp = imp4[0:TQ] + imp4[TQ:2 * TQ] + imp4[2 * TQ:3 * TQ] + imp4[3 * TQ:4 * TQ]

        score = jnp.where(sel_valid, imp.T[0:n_slc, :] + jnp.where(sel_forced, FORCE, 0.0), -FORCE)
        rank = jnp.zeros((n_slc, TQ), F32)
        for k in range(1, n_slc):
            other = pltpu.roll(score, k, axis=0)
            rank = rank + jnp.where(sidx >= k, (other >= score).astype(F32), (other > score).astype(F32))
        sel_t = (rank < float(N_SELECT)).astype(F32)
        aug_t = jnp.concatenate([jnp.zeros((SEL_LANE0, TQ), F32), sel_t - 1.0,
                                 jnp.zeros((LANE - SEL_LANE0 - n_slc, TQ), F32)], axis=0)
        aug_t = jnp.where(lane_row == PAD_LANE, -1.0, aug_t)
        qa = qs + _tile4(aug_t.T.astype(BF16))

        def far_tile(it, state):
            start = pl.multiple_of(q0 - FAR * (n_far - it), LANE)
            sc = _qk(qa, ks_ref[g, 0, pl.ds(start, FAR), :])
            return _softmax_fold(state, [sc], sv_ref[0, pl.ds(start, FAR), :])

        state = lax.fori_loop(0, n_far, far_tile, fresh)
        sc = _qk(qa, ks_ref[g, 0, pl.ds(q0, TAIL), :])
        pieces = [sc[:, 0:TAIL - 2 * LANE], sc[:, TAIL - 2 * LANE:TAIL] + near]
        _, l, acc = _softmax_fold(state, pieces, sv_ref[0, pl.ds(q0, TAIL), :])
        o_slc = acc / l

        sc = _qk(qa, kw_ref[g, 0, pl.ds(q0, TAIL), :])
        pieces = [sc[:, 0:LANE] + up4, sc[:, LANE:TAIL - 2 * LANE], sc[:, TAIL - 2 * LANE:TAIL] + near]
        _, l, acc = _softmax_fold(fresh, pieces, wv_ref[0, pl.ds(q0, TAIL), :])
        o_win = acc / l

        gt = gate_ref[g]
        own = lax.shift_right_logical(col, 6) == g
        for r in range(GQ):
            rs = slice(r * TQ, (r + 1) * TQ)
            comb = gt[:, 3 * r:3 * r + 1] * o_cmp[rs] + gt[:, 3 * r + 1:3 * r + 2] * o_slc[rs] \
                + gt[:, 3 * r + 2:3 * r + 3] * o_win[rs]
            ocat_ref[GQ * g + r] = jnp.where(own, comb, 0.0)
        return carry

    lax.fori_loop(0, N_KV, per_group, 0)
    o = jnp.concatenate([ocat_ref[h] for h in range(N_HEADS)], axis=1)
    o_ref[...] = _rms(o, gn_ref[...], N_HEADS * HEAD_DIM).astype(BF16)


def _attn_prompt(q, gates, cmpkv, ks, sv, kw, wv, tab_t, tab_c, sp, gn_pad, batch, seq):
    nq = seq // TQ
    t = batch * seq
    rows = ks.shape[2]
    n_slc = seq // SLC_LEN
    assert WINDOW >= FAR - LANE and SEL_LANE0 + n_slc <= PAD_LANE < LANE
    return pl.pallas_call(
        functools.partial(_attn_prompt_kernel, n_slc=n_slc),
        grid=(batch, nq),
        in_specs=[
            pl.BlockSpec((N_HEADS, TQ, LANE), lambda b, i: (0, b * nq + i, 0)),
            pl.BlockSpec((N_KV, TQ, LANE), lambda b, i: (0, b * nq + i, 0)),
            pl.BlockSpec((1, 3, seq // CMP_STRIDE, LANE), lambda b, i: (b, 0, 0, 0)),
            pl.BlockSpec((N_KV, 1, rows, LANE), lambda b, i: (0, b, 0, 0)),
            pl.BlockSpec((1, rows, LANE), lambda b, i: (b, 0, 0)),
            pl.BlockSpec((N_KV, 1, rows, LANE), lambda b, i: (0, b, 0, 0)),
            pl.BlockSpec((1, rows, LANE), lambda b, i: (b, 0, 0)),
            _const_spec(tab_t.shape),
            pl.BlockSpec((1, N_HEADS, LANE, LANE), lambda b, i: (i, 0, 0, 0)),
            _const_spec(sp.shape),
            _const_spec(gn_pad.shape),
        ],
        out_specs=pl.BlockSpec((TQ, N_HEADS * LANE), lambda b, i: (b * nq + i, 0)),
        out_shape=jax.ShapeDtypeStruct((t, N_HEADS * LANE), BF16),
        scratch_shapes=[pltpu.VMEM((N_HEADS, TQ, LANE), F32)],
        compiler_params=_cparams(2),
        name="attn_prompt",
    )(q, gates, cmpkv, ks, sv, kw, wv, tab_t, tab_c, sp, gn_pad)


def _sample_cw_kernel(qlo_ref, qgl_ref, cmp_ref, cwin_ref, wnew_ref, tabc_ref, tabw_ref, rb0_ref, sp_ref,
                      ocmp_ref, owin_ref, idx_ref, *, n_slc, qblk, win_buf):
    qlo = qlo_ref[0]
    qgl = qgl_ref[0]
    n_cmp_rows = cmp_ref.shape[2]
    col = lax.broadcasted_iota(jnp.int32, (N_HEADS, n_cmp_rows), 1)
    grp_of_head = lax.shift_right_logical(lax.broadcasted_iota(jnp.int32, (N_HEADS, n_cmp_rows), 0), 2)
    ok = col >= 1
    raw = jnp.where(grp_of_head == 0, _qk(qlo, cmp_ref[0, 0]), _qk(qlo, cmp_ref[0, 1]))
    s = jnp.where(ok, raw + tabc_ref[...], NEG)
    e = jnp.exp(s - jnp.max(s, axis=1, keepdims=True))
    p = (e / jnp.sum(e, axis=1, keepdims=True) * ok.astype(F32)).astype(BF16)
    ocmp_ref[0] = _mm(p, cmp_ref[0, 2])
    imp8 = _mm(p, sp_ref[...])

    width = imp8.shape[1]
    head_grp = lax.shift_right_logical(lax.broadcasted_iota(jnp.int32, (N_HEADS, width), 0), 2)
    sidx = lax.broadcasted_iota(jnp.int32, (1, width), 1)
    sidx_f = sidx.astype(F32)
    forced = (sidx == 0) | (sidx == qblk) | (sidx == qblk - 1)
    lane = lax.broadcasted_iota(jnp.int32, (1, LANE), 1)
    for g in range(N_KV):
        imp = jnp.sum(jnp.where(head_grp == g, imp8, 0.0), axis=0, keepdims=True)
        score = jnp.where(sidx <= qblk, imp + jnp.where(forced, FORCE, 0.0), -FORCE)
        score = jnp.where(sidx < n_slc, score, -jnp.inf)
        picked = jnp.zeros((1, LANE), F32)
        for k in range(N_SELECT):
            best = jnp.max(score, axis=1, keepdims=True)
            ik = jnp.min(jnp.where(score == best, sidx_f, float(width)), axis=1, keepdims=True)
            picked = jnp.where(lane == k, ik, picked)
            score = jnp.where(sidx_f == ik, -jnp.inf, score)
        idx_ref[0, g:g + 1, :] = picked.astype(jnp.int32)

    kt = cwin_ref[0, 0].reshape(N_KV * HEAD_DIM, win_buf).astype(BF16)
    vt = cwin_ref[0, 1].reshape(N_KV * HEAD_DIM, win_buf).astype(BF16)
    knew = wnew_ref[0, :, 0:LANE].astype(BF16).astype(F32)
    vnew = wnew_ref[0, :, LANE:2 * LANE].astype(BF16).astype(F32)
    colw = lax.broadcasted_iota(jnp.int32, (N_HEADS, win_buf), 1)
    okw = colw > win_buf - WINDOW
    sw = jnp.where(okw, _mm(qgl, kt) + tabw_ref[...], NEG)
    s_new = jnp.sum(qgl.astype(F32) * knew, axis=1, keepdims=True) + rb0_ref[:, 0:1]
    mw = jnp.maximum(jnp.max(sw, axis=1, keepdims=True), s_new)
    ew = jnp.where(okw, jnp.exp(sw - mw), 0.0)
    e_new = jnp.exp(s_new - mw)
    lw = jnp.sum(ew, axis=1, keepdims=True) + e_new
    owin_ref[0] = (_qk(ew.astype(BF16), vt) + e_new.astype(BF16).astype(F32) * vnew) / lw


def _sample_cw(qlo, qgl, cmpkv, cwin, wnew, tab_c, tab_w, rb0, sp, n_slc, qblk):
    n = qlo.shape[0]
    win_buf = cwin.shape[-1]
    blk = lambda a: pl.BlockSpec((1,) + a.shape[1:], lambda b: (b,) + (0,) * (a.ndim - 1))
    outs = (jax.ShapeDtypeStruct((n, N_HEADS, LANE), F32), jax.ShapeDtypeStruct((n, N_HEADS, LANE), F32),
            jax.ShapeDtypeStruct((n, N_KV, LANE), jnp.int32))
    return pl.pallas_call(
        functools.partial(_sample_cw_kernel, n_slc=n_slc, qblk=qblk, win_buf=win_buf),
        grid=(n,),
        in_specs=[blk(qlo), blk(qgl), blk(cmpkv), blk(cwin), blk(wnew), _const_spec(tab_c.shape),
                  _const_spec(tab_w.shape), _const_spec(rb0.shape), _const_spec(sp.shape)],
        out_specs=tuple(pl.BlockSpec((1,) + o.shape[1:], lambda b: (b, 0, 0)) for o in outs),
        out_shape=outs,
        compiler_params=_cparams(1),
        name="sample_cmp_win",
    )(qlo, qgl, cmpkv, cwin, wnew, tab_c, tab_w, rb0, sp)


def _sample_slc_kernel(pg_ref, hf_ref, bid_ref, *refs, new_block, near_block):
    n_blk = N_KV * N_SELECT
    blocks = refs[:n_blk]
    (q_ref, kvnew_ref, ocmp_ref, owin_ref, gate_ref, tnear_ref, tlast_ref, rb0_ref, gn_ref, o_ref) = refs[n_blk:]
    b = pl.program_id(0)
    n_keys = N_SELECT * PAGE_SIZE
    lane = lax.broadcasted_iota(jnp.int32, (1, n_keys), 1)
    slot = lax.shift_right_logical(lane, 7)
    half = lax.shift_right_logical(lane, 6) & 1
    head_grp = lax.shift_right_logical(lax.broadcasted_iota(jnp.int32, (N_HEADS, LANE), 0), 2)
    lane_grp = lax.shift_right_logical(lax.broadcasted_iota(jnp.int32, (N_HEADS, LANE), 1), 6)
    knew = kvnew_ref[0, :, 256:384].astype(BF16).astype(F32)
    vnew = kvnew_ref[0, :, 384:512].astype(BF16).astype(F32)
    qs = q_ref[0]
    o_slc = jnp.zeros((N_HEADS, LANE), F32)
    for g in range(N_KV):
        mine = blocks[g * N_SELECT:(g + 1) * N_SELECT]
        kt = jnp.concatenate([blk[0, 0].reshape(N_KV * HEAD_DIM, PAGE_SIZE) for blk in mine], axis=1).astype(BF16)
        vt = jnp.concatenate([blk[0, 1].reshape(N_KV * HEAD_DIM, PAGE_SIZE) for blk in mine], axis=1).astype(BF16)
        bid = jnp.zeros((1, n_keys), jnp.int32)
        hsel = jnp.zeros((1, n_keys), jnp.int32)
        has_new = bid_ref[b, g * N_SELECT] == new_block
        for kk in range(N_SELECT):
            bid = jnp.where(slot == kk, bid_ref[b, g * N_SELECT + kk], bid)
            hsel = jnp.where(slot == kk, hf_ref[b, g * N_SELECT + kk], hsel)
            if kk:
                has_new = has_new | (bid_ref[b, g * N_SELECT + kk] == new_block)
        bias = jnp.where(bid == near_block + 1, tlast_ref[...], jnp.where(bid == near_block, tnear_ref[...], 0.0))
        ok = (bid != new_block) & (half == hsel)
        s = jnp.where(ok, _mm(qs, kt) + bias, NEG)
        s_new = jnp.sum(qs.astype(F32) * knew, axis=1, keepdims=True) + rb0_ref[:, 0:1]
        s_new = jnp.where(has_new, s_new, NEG)
        m = jnp.maximum(jnp.max(s, axis=1, keepdims=True), s_new)
        e = jnp.where(ok, jnp.exp(s - m), 0.0)
        e_new = jnp.where(has_new, jnp.exp(s_new - m), 0.0)
        l = jnp.sum(e, axis=1, keepdims=True) + e_new
        og = (_qk(e.astype(BF16), vt) + e_new.astype(BF16).astype(F32) * vnew) / l
        o_slc = jnp.where(head_grp == g, og, o_slc)
    gt = gate_ref[0]
    comb = gt[:, 0:1] * ocmp_ref[0] + gt[:, 1:2] * o_slc + gt[:, 2:3] * owin_ref[0]
    comb = jnp.where(lane_grp == head_grp, comb, 0.0)
    o = jnp.concatenate([comb[h:h + 1, :] for h in range(N_HEADS)], axis=1)
    o_ref[0] = _rms(o, gn_ref[...], N_HEADS * HEAD_DIM).astype(BF16)


def _sample_slc(pages, halves, bids, cache_t, q, kvnew, ocmp, owin, gates, t_near, t_last, rb0, gn_pad,
                new_block, near_block):
    n = q.shape[0]
    n_blk = N_KV * N_SELECT
    blk_specs = [pl.BlockSpec(_PAGE_BLOCK, lambda b, pg, hf, bi, k=k: (pg[b, k], 1, 0, 0, 0))
                 for k in range(n_blk)]
    per_b = lambda a: pl.BlockSpec((1,) + a.shape[1:], lambda b, *_: (b,) + (0,) * (a.ndim - 1))
    cst = lambda a: pl.BlockSpec(a.shape, lambda *_: (0,) * a.ndim)
    grid_spec = pltpu.PrefetchScalarGridSpec(
        num_scalar_prefetch=3,
        grid=(n,),
        in_specs=blk_specs + [per_b(q), per_b(kvnew), per_b(ocmp), per_b(owin), per_b(gates),
                              cst(t_near), cst(t_last), cst(rb0), cst(gn_pad)],
        out_specs=pl.BlockSpec((1, 1, N_HEADS * LANE), lambda b, *_: (b, 0, 0)),
    )
    return pl.pallas_call(
        functools.partial(_sample_slc_kernel, new_block=new_block, near_block=near_block),
        grid_spec=grid_spec,
        out_shape=jax.ShapeDtypeStruct((n, 1, N_HEADS * LANE), BF16),
        compiler_params=_cparams(1),
        name="sample_selected",
    )(pages, halves, bids, *([cache_t] * n_blk), q, kvnew, ocmp, owin, gates, t_near, t_last, rb0, gn_pad)


def _proj_kernel(on_ref, zn_ref, x_ref, wa_ref, wb_ref, g1_ref, b1_ref, wr_ref, br_ref,
                 x1_ref, x1b_ref, gate_ref, *, alpha):
    mix = _mm(on_ref[...], wa_ref[...]) + _mm(zn_ref[...], wb_ref[...])
    x1 = _layernorm(alpha * x_ref[...] + mix, g1_ref[...], b1_ref[...])
    x1_ref[...] = x1
    xb = x1.astype(BF16)
    x1b_ref[...] = xb
    logits = _mm(xb, wr_ref[...]) + br_ref[...]
    tm = logits.shape[0]
    lane = lax.broadcasted_iota(jnp.int32, (tm, LANE), 1).astype(F32)

    def first_argmax(v, vmax):
        return jnp.min(jnp.where(v == vmax, lane, float(LANE)), axis=1, keepdims=True)

    lg = jnp.where(lane < N_GROUPS, logits[:, 0:LANE], -jnp.inf)
    lg_max = jnp.max(lg, axis=1, keepdims=True)
    eg = jnp.exp(lg - lg_max)
    pg = eg / jnp.sum(eg, axis=1, keepdims=True)
    gidx = first_argmax(lg, lg_max)
    pg_sel = jnp.sum(jnp.where(lane == gidx, pg, 0.0), axis=1, keepdims=True)
    le = jnp.zeros((tm, LANE), F32)
    for gi in range(N_GROUPS):
        le = le + jnp.where(gidx == gi, logits[:, (gi + 1) * LANE:(gi + 2) * LANE], 0.0)
    le = jnp.where(lane < N_EXP, le, -jnp.inf)
    ee = jnp.exp(le - jnp.max(le, axis=1, keepdims=True))
    pe = jnp.where(lane < N_EXP, ee / jnp.sum(ee, axis=1, keepdims=True), -1.0)
    v1 = jnp.max(pe, axis=1, keepdims=True)
    i1 = first_argmax(pe, v1)
    pe2 = jnp.where(lane == i1, -1.0, pe)
    v2 = jnp.max(pe2, axis=1, keepdims=True)
    i2 = first_argmax(pe2, v2)
    tot = v1 + v2
    gate_e = jnp.where(lane == i1, v1 / tot * pg_sel, jnp.where(lane == i2, v2 / tot * pg_sel, 0.0))
    gate_ref[...] = jnp.where(lane == GROUP_LANE, gidx, gate_e)


def _proj(on, zn, x, wa, wb, g1, b1, wr, br, alpha, tm):
    t, d = x.shape
    row = lambda n: pl.BlockSpec((tm, n), lambda i: (i, 0))
    outs = (jax.ShapeDtypeStruct((t, d), F32), jax.ShapeDtypeStruct((t, d), BF16),
            jax.ShapeDtypeStruct((t, LANE), F32))
    return pl.pallas_call(
        functools.partial(_proj_kernel, alpha=alpha),
        grid=(t // tm,),
        in_specs=[row(on.shape[1]), row(zn.shape[1]), row(d)] + [_const_spec(a.shape) for a in (wa, wb, g1, b1, wr, br)],
        out_specs=(row(d), row(d), row(LANE)),
        out_shape=outs,
        compiler_params=_cparams(1),
        name=f"proj_ln_router_{tm}",
    )(on, zn, x, wa, wb, g1, b1, wr, br)


def _moe_kernel(xb_ref, x1_ref, gate_ref, wgu_ref, wd_ref, g2_ref, b2_ref, out_ref, acc_ref, *, alpha, d_ff):
    gi = pl.program_id(1)

    @pl.when(gi == 0)
    def _():
        acc_ref[...] = jnp.zeros(acc_ref.shape, F32)

    x = xb_ref[...]
    gt = gate_ref[...]
    acc = acc_ref[...]
    for e in range(N_EXP):
        au = _mm(x, wgu_ref[0, e])
        h = jax.nn.silu(au[:, 0:d_ff]) * au[:, d_ff:2 * d_ff]
        acc = acc + _mm((h * gt[:, e:e + 1]).astype(BF16), wd_ref[0, e])
    acc_ref[...] = acc

    @pl.when(gi == N_GROUPS - 1)
    def _():
        out_ref[...] = _layernorm(alpha * x1_ref[...] + acc_ref[...], g2_ref[...], b2_ref[...])


def _moe(xb, x1, gate, wgu, wd, g2, b2, alpha, tm):
    t, d = x1.shape
    d_ff = wd.shape[2]
    return pl.pallas_call(
        functools.partial(_moe_kernel, alpha=alpha, d_ff=d_ff),
        grid=(t // tm, N_GROUPS),
        in_specs=[
            pl.BlockSpec((tm, d), lambda i, g: (i, 0)),
            pl.BlockSpec((tm, d), lambda i, g: (i, 0)),
            pl.BlockSpec((tm, LANE), lambda i, g: (i, g)),
            pl.BlockSpec((1, N_EXP, d, 2 * d_ff), lambda i, g: (g, 0, 0, 0)),
            pl.BlockSpec((1, N_EXP, d_ff, d), lambda i, g: (g, 0, 0, 0)),
            _const_spec(g2.shape), _const_spec(b2.shape),
        ],
        out_specs=pl.BlockSpec((tm, d), lambda i, g: (i, 0)),
        out_shape=jax.ShapeDtypeStruct((t, d), F32),
        scratch_shapes=[pltpu.VMEM((tm, d), F32)],
        compiler_params=_cparams(2),
        name=f"moe_ln_{tm}",
    )(xb, x1, gate, wgu, wd, g2, b2)


def _split_bf16(x):
    hi = x.astype(BF16)
    return hi, (x - hi.astype(F32)).astype(BF16)


def _moe_sorted_kernel(cnt_ref, off_ref, x1_ref, gate_ref, drow_ref, dcol_ref, wgu_ref, wd_ref, g2_ref, b2_ref,
                       out_ref, xs_ref, gs_ref, ys_ref, *, alpha, d_ff):
    i = pl.program_id(0)
    gi = pl.program_id(1)
    half = pl.program_id(2)
    n_half = N_EXP // EXPERT_SPLIT
    tmoe = x1_ref.shape[0]
    rows = xs_ref.shape[0]

    @pl.when((i == 0) & (gi == 0) & (half == 0))
    def _():
        ys_ref[...] = jnp.zeros(ys_ref.shape, F32)

    @pl.when((gi == 0) & (half == 0))
    def _():
        perm = (lax.broadcasted_iota(jnp.int32, (rows, tmoe), 0) == drow_ref[0]).astype(BF16)
        xs_ref[...] = _mm(perm, x1_ref[...].astype(BF16)).astype(BF16)
        g_hi, g_lo = _split_bf16(gate_ref[...])
        gs_ref[...] = _mm(perm, g_hi) + _mm(perm, g_lo)

    n = cnt_ref[i * N_GROUPS + gi]
    off = off_ref[i * N_GROUPS + gi]

    def run_chunk(base, size):
        xc = xs_ref[pl.ds(base, size), :]
        gc = gs_ref[pl.ds(base, size), :]
        gc = jnp.where(half == 0, gc, pltpu.roll(gc, LANE - n_half, axis=1))
        y = jnp.zeros((size, out_ref.shape[1]), F32)
        for e in range(n_half):
            au = _mm(xc, wgu_ref[0, e])
            h = jax.nn.silu(au[:, 0:d_ff]) * au[:, d_ff:2 * d_ff]
            y = y + _mm((h * gc[:, e:e + 1]).astype(BF16), wd_ref[0, e])

        @pl.when(half == 0)
        def _():
            ys_ref[pl.ds(base, size), :] = y

        @pl.when(half != 0)
        def _():
            ys_ref[pl.ds(base, size), :] = ys_ref[pl.ds(base, size), :] + y

    n_full = lax.shift_right_logical(n, 7)
    n_sub = lax.shift_right_logical(n - n_full * MOE_CHUNK + (MOE_SUB - 1), 6)

    def full_body(c, carry):
        run_chunk(pl.multiple_of(off + c * MOE_CHUNK, SEG_ALIGN), MOE_CHUNK)
        return carry

    def sub_body(c, carry):
        run_chunk(pl.multiple_of(off + n_full * MOE_CHUNK + c * MOE_SUB, SEG_ALIGN), MOE_SUB)
        return carry

    lax.fori_loop(0, n_full, full_body, 0)
    lax.fori_loop(0, n_sub, sub_body, 0)

    @pl.when((gi == N_GROUPS - 1) & (half == EXPERT_SPLIT - 1))
    def _():
        unperm = (dcol_ref[...] == lax.broadcasted_iota(jnp.int32, (tmoe, rows), 1)).astype(BF16)
        y_hi, y_lo = _split_bf16(ys_ref[...])
        moe = _mm(unperm, y_hi) + _mm(unperm, y_lo)
        out_ref[...] = _layernorm(alpha * x1_ref[...] + moe, g2_ref[...], b2_ref[...])


def _moe_sorted(x1, gate, wgu, wd, g2, b2, alpha):
    t, d = x1.shape
    d_ff = wd.shape[2]
    n_tiles = t // TMOE
    n_half = N_EXP // EXPERT_SPLIT
    gid = gate[:, GROUP_LANE].astype(jnp.int32).reshape(n_tiles, TMOE)
    onehot = (gid[:, :, None] == jnp.arange(N_GROUPS)[None, None, :]).astype(jnp.int32)
    cnt = jnp.sum(onehot, axis=1)
    rank = jnp.sum((jnp.cumsum(onehot, axis=1) - onehot) * onehot, axis=2)
    seg = (cnt + SEG_ALIGN - 1) // SEG_ALIGN * SEG_ALIGN
    off = jnp.cumsum(seg, axis=1) - seg
    dest = jnp.take_along_axis(off, gid, axis=1) + rank
    wgu_h = wgu.reshape(N_GROUPS * EXPERT_SPLIT, n_half, d, 2 * d_ff)
    wd_h = wd.reshape(N_GROUPS * EXPERT_SPLIT, n_half, d_ff, d)
    grid_spec = pltpu.PrefetchScalarGridSpec(
        num_scalar_prefetch=2,
        grid=(n_tiles, N_GROUPS, EXPERT_SPLIT),
        in_specs=[
            pl.BlockSpec((TMOE, d), lambda i, g, h, *_: (i, 0)),
            pl.BlockSpec((TMOE, LANE), lambda i, g, h, *_: (i, 0)),
            pl.BlockSpec((1, 1, TMOE), lambda i, g, h, *_: (i, 0, 0)),
            pl.BlockSpec((TMOE, 1), lambda i, g, h, *_: (i, 0)),
            pl.BlockSpec((1, n_half, d, 2 * d_ff), lambda i, g, h, *_: (g * EXPERT_SPLIT + h, 0, 0, 0)),
            pl.BlockSpec((1, n_half, d_ff, d), lambda i, g, h, *_: (g * EXPERT_SPLIT + h, 0, 0, 0)),
            pl.BlockSpec(g2.shape, lambda *_: (0, 0)), pl.BlockSpec(b2.shape, lambda *_: (0, 0)),
        ],
        out_specs=pl.BlockSpec((TMOE, d), lambda i, g, h, *_: (i, 0)),
        scratch_shapes=[pltpu.VMEM((SORT_ROWS, d), BF16), pltpu.VMEM((SORT_ROWS, LANE), F32),
                        pltpu.VMEM((SORT_ROWS, d), F32)],
    )
    return pl.pallas_call(
        functools.partial(_moe_sorted_kernel, alpha=alpha, d_ff=d_ff),
        grid_spec=grid_spec,
        out_shape=jax.ShapeDtypeStruct((t, d), F32),
        compiler_params=_cparams(3),
        name="moe_sorted_ln",
    )(cnt.reshape(-1), off.reshape(-1), x1, gate, dest.reshape(n_tiles, 1, TMOE), dest.reshape(t, 1), wgu_h, wd_h,
      g2, b2)


def _pad_head_lanes(v):
    vh = v.reshape(N_HEADS, HEAD_DIM)
    z = jnp.zeros_like(vh)
    grp = (jnp.arange(N_HEADS) // GQ)[:, None]
    return jnp.where(grp == 0, jnp.concatenate([vh, z], axis=1), jnp.concatenate([z, vh], axis=1)).reshape(1, -1)


def _pad_head_rows(w):
    wh = w.reshape(N_HEADS, HEAD_DIM, -1)
    z = jnp.zeros_like(wh)
    grp = (jnp.arange(N_HEADS) // GQ)[:, None, None]
    return jnp.where(grp == 0, jnp.concatenate([wh, z], axis=1), jnp.concatenate([z, wh], axis=1)).reshape(
        N_HEADS * LANE, -1)


def _front_pad(a, axis, pad_row):
    shape = list(a.shape)
    shape[axis] = WINDOW
    return jnp.concatenate([jnp.broadcast_to(pad_row.astype(a.dtype), shape), a], axis=axis)


def kernel(x_prompt, x_sample, cache_kv, page_table, cache_win, state_conv, w_in, conv_w, pe_k, pe_v, w_ck1, w_ck2, w_cv1, w_cv2, g_nsa, g_conv, w_out, ln1_g, ln1_b, w_rg, b_rg, w_re, b_re, w_eg, w_eu, w_ed, ln2_g, ln2_b, rel_bias):
    batch, seq, d_model = x_prompt.shape
    dec_batch, dec_seq = x_sample.shape[0], x_sample.shape[1]
    depth = w_in.shape[0]
    n_pages = page_table.shape[1]
    past = n_pages * PAGE_SIZE
    win_buf = cache_win.shape[2]
    d_nsa = N_HEADS * HEAD_DIM
    d_conv = w_out.shape[1] - d_nsa
    assert dec_seq == 1 and seq % TM == 0 and min(WINDOW, seq) == TM and win_buf == WINDOW
    assert d_conv == 512 and seq // CMP_STRIDE == LANE and n_pages % SAMPLE_PAGES == 0
    alpha = (2 * depth) ** 0.25
    n_slc_s = -(-(past + dec_seq) // SLC_LEN)
    qblk_s = past // SLC_LEN

    idx_np, sp_p_np, sp_s_np = _static_tables(seq, past, win_buf)
    tabs = _bias_tables(rel_bias, jnp.asarray(idx_np))
    nq = seq // TQ
    tab_t = jnp.transpose(tabs[0:_N_NEAR_TILES], (1, 2, 0, 3)).reshape(N_HEADS, LANE, _N_NEAR_TILES * LANE)
    tab_c = tabs[_N_NEAR_TILES:_N_NEAR_TILES + nq]
    smp = tabs[_N_NEAR_TILES + nq]
    tab_w_s = smp[:, 0:5, :].reshape(N_HEADS, 5 * LANE)[:, :win_buf]
    tab_c_s = smp[:, 5:13, :].reshape(N_HEADS, 8 * LANE)
    reps = N_SELECT * PAGE_SIZE // SLC_LEN
    t_near = jnp.tile(smp[:, 13, 0:SLC_LEN], (1, reps))
    t_last = jnp.tile(smp[:, 13, SLC_LEN:2 * SLC_LEN], (1, reps))
    rb0_rep = jnp.broadcast_to(smp[:, 14, 0:1], (N_HEADS, LANE))
    sp_p = jnp.asarray(sp_p_np, BF16)
    sp_s = jnp.asarray(sp_s_np, BF16)
    key_pad = jnp.zeros((LANE,), F32).at[PAD_LANE].set(KEY_MASK)
    val_pad = jnp.zeros((LANE,), F32)

    xp = x_prompt.reshape(batch * seq, d_model)
    xs = x_sample.reshape(dec_batch * dec_seq, d_model)
    outs = [[] for _ in range(6)]
    for l in range(depth):
        w_pack = _pack_in_weights(w_in[l])
        gc = g_conv[l].reshape(1, -1)
        gn_pad = _pad_head_lanes(g_nsa[l])
        w1k, w2k, pek = _pack_compress_weights(pe_k[l], w_ck1[l], w_ck2[l], True)
        w1v, w2v, pev = _pack_compress_weights(pe_v[l], w_cv1[l], w_cv2[l], False)
        w1 = jnp.stack([w1k, w1v])
        pe = jnp.stack([pek, pev])
        wa = _pad_head_rows(w_out[l][:d_nsa]).astype(BF16)
        wb = w_out[l][d_nsa:].astype(BF16)
        wr = jnp.concatenate(
            [jnp.pad(w_rg[l], ((0, 0), (0, LANE - N_GROUPS)))]
            + [jnp.pad(w_re[l][:, gi * N_EXP:(gi + 1) * N_EXP], ((0, 0), (0, LANE - N_EXP))) for gi in range(N_GROUPS)],
            axis=1).astype(BF16)
        br = jnp.concatenate(
            [jnp.pad(b_rg[l], (0, LANE - N_GROUPS))]
            + [jnp.pad(b_re[l][gi * N_EXP:(gi + 1) * N_EXP], (0, LANE - N_EXP)) for gi in range(N_GROUPS)]).reshape(1, -1)
        wgu = jnp.concatenate([w_eg[l], w_eu[l]], axis=-1).astype(BF16)
        wd = w_ed[l].astype(BF16)
        g1, b1 = ln1_g[l].reshape(1, -1), ln1_b[l].reshape(1, -1)
        g2, b2 = ln2_g[l].reshape(1, -1), ln2_b[l].reshape(1, -1)

        q_p, kvt_p, ks, sv, kw, wv, wtail_t, gates_p, zn_p, ctail = _inproj_prompt(xp, w_pack, conv_w[l], gc, batch, seq)
        kvt_p = kvt_p.reshape(batch, 4, N_KV, HEAD_DIM, seq)
        cmp_p = _compress_prompt(kvt_p, batch, seq, w1, w2k, w2v, pe)
        ks = _front_pad(ks.reshape(N_KV, batch, seq, LANE), 2, key_pad)
        kw = _front_pad(kw.reshape(N_KV, batch, seq, LANE), 2, key_pad)
        sv = _front_pad(sv.reshape(batch, seq, LANE), 1, val_pad)
        wv = _front_pad(wv.reshape(batch, seq, LANE), 1, val_pad)
        on_p = _attn_prompt(q_p, gates_p, cmp_p, ks, sv, kw, wv, tab_t, tab_c, sp_p, gn_pad, batch, seq)
        x1_p, x1b_p, gate_p = _proj(on_p, zn_p, xp, wa, wb, g1, b1, wr, br, alpha, TM)
        y_p = _moe_sorted(x1_p, gate_p, wgu, wd, g2, b2, alpha)

        st = state_conv[l]
        qlo_s, qgl_s, kv_s, win_s, gates_s, zn_s, u_s = _inproj_sample(xs, w_pack, conv_w[l], gc, st[:, 0], st[:, 1])
        cache_t = jnp.transpose(cache_kv[l], (0, 2, 3, 4, 1))
        cwin_t = jnp.transpose(cache_win[l], (0, 2, 3, 4, 1))
        cmp_s = _compress_sample(cache_t, page_table, w1, w2k, w2v, pe)
        qlo_s3 = qlo_s.reshape(dec_batch, N_HEADS, LANE)
        qgl_s3 = qgl_s.reshape(dec_batch, N_HEADS, LANE)
        ocmp, owin, sel_idx = _sample_cw(qlo_s3, qgl_s3, cmp_s, cwin_t, win_s.reshape(dec_batch, 1, 256), tab_c_s,
                                         tab_w_s, rb0_rep, sp_s, n_slc_s, qblk_s)
        bids = sel_idx[:, :, :N_SELECT].reshape(dec_batch, N_KV * N_SELECT)
        blk_pages = jnp.take_along_axis(page_table, jnp.minimum(bids // 2, n_pages - 1), axis=1)
        gates_s3 = jnp.pad(gates_s.reshape(dec_batch, N_KV, LANE)[:, :, :3 * GQ].reshape(dec_batch, N_HEADS, 3),
                           ((0, 0), (0, 0), (0, LANE - 3)))
        on_s = _sample_slc(blk_pages, bids % 2, bids, cache_t, qgl_s3, kv_s.reshape(dec_batch, 1, 512), ocmp, owin,
                           gates_s3, t_near, t_last, rb0_rep, gn_pad, n_slc_s - 1, qblk_s - 2)
        on_s = on_s.reshape(dec_batch, N_HEADS * LANE)
        x1_s, x1b_s, gate_s = _proj(on_s, zn_s, xs, wa, wb, g1, b1, wr, br, alpha, dec_batch)
        gid_s = gate_s[:, GROUP_LANE:GROUP_LANE + 1]
        gate4_s = jnp.concatenate([jnp.where(gid_s == gi, gate_s, 0.0) for gi in range(N_GROUPS)], axis=1)
        y_s = _moe(x1b_s, x1_s, gate4_s, wgu, wd, g2, b2, alpha, dec_batch)

        to_token_major = lambda a: jnp.transpose(a, (0, 4, 1, 2, 3))
        outs[0].append(to_token_major(kvt_p))
        outs[1].append(kv_s.reshape(dec_batch, dec_seq, 4, N_KV, HEAD_DIM))
        outs[2].append(to_token_major(wtail_t.reshape(batch, 2, N_KV, HEAD_DIM, TM)))
        win_all_t = jnp.concatenate([cwin_t, win_s.reshape(dec_batch, 2, N_KV, HEAD_DIM, dec_seq)], axis=-1)
        outs[3].append(to_token_major(win_all_t[..., win_all_t.shape[-1] - min(WINDOW, past + dec_seq):]))
        outs[4].append(ctail)
        outs[5].append(jnp.concatenate([st, u_s[:, None, :]], axis=1)[:, dec_seq:])
        xp, xs = y_p, y_s
    return (xp.reshape(batch, seq, d_model), xs.reshape(dec_batch, dec_seq, d_model),
            jnp.stack(outs[0]), jnp.stack(outs[1]), jnp.stack(outs[2]), jnp.stack(outs[3]),
            jnp.stack(outs[4]), jnp.stack(outs[5]))
```

```python
import functools
import math

import numpy as np
import jax
import jax.numpy as jnp
from jax import lax
from jax.experimental import pallas as pl
from jax.experimental.pallas import tpu as pltpu

F32 = jnp.float32
BF16 = jnp.bfloat16

HEAD_DIM = 64
N_KV = 2
GQ = 4
N_HEADS = N_KV * GQ
CONV_W = 3
CMP_STRIDE = 16
CMP_LEN = 32
CMP_HIDDEN = 2 * HEAD_DIM
SLC_LEN = 64
N_SELECT = 16
WINDOW = 512
N_BUCKETS = 32
MAX_DISTANCE = 128
N_GROUPS = 4
N_EXP = 8
PAGE_SIZE = 128
NEG = -1e30
FORCE = 1e6

LANE = 128
SUBLANE = 8
VMEM_LIMIT = 52 * 1024 * 1024

TQ = 128
TM = 512
FAR = 512
TAIL = WINDOW + TQ
SAMPLE_PAGES = 32
TMOE = 1024
MOE_CHUNK = 384
SEG_ALIGN = 16
PERM_ROWS = TMOE + LANE
SORT_ROWS = PERM_ROWS + MOE_CHUNK
EXPERT_SPLIT = 2
GROUP_LANE = N_EXP

KEY_MASK = 2.0 ** 100
TABLE_MASK = 2 * NEG
MASKED_BUCKET = N_BUCKETS
SEL_LANE0 = HEAD_DIM
PAD_LANE = HEAD_DIM + 32


def _cparams(n_axes):
    return pltpu.CompilerParams(dimension_semantics=("arbitrary",) * n_axes, vmem_limit_bytes=VMEM_LIMIT)


def _const_spec(shape):
    nd = len(shape)
    return pl.BlockSpec(shape, lambda *_, nd=nd: (0,) * nd)


def _qk(a, b):
    return lax.dot_general(a, b, (((1,), (1,)), ((), ())), preferred_element_type=F32)


def _mm(a, b):
    return jnp.dot(a, b, preferred_element_type=F32)


def _bucket_np(dist):
    n = np.maximum(dist, 0)
    max_exact = N_BUCKETS // 2
    nf = np.maximum(n, 1).astype(np.float32)
    large = max_exact + (np.log(nf / np.float32(max_exact)) / np.float32(math.log(MAX_DISTANCE / max_exact))
                         * np.float32(N_BUCKETS - max_exact)).astype(np.int32)
    large = np.minimum(large, N_BUCKETS - 1)
    return np.where(n < max_exact, n, large).astype(np.int32)


def _overlap_np(c, s):
    c0 = c * CMP_STRIDE
    s0 = s * SLC_LEN
    return np.maximum(np.minimum(c0 + CMP_LEN, s0 + SLC_LEN) - np.maximum(c0, s0), 0)


_N_NEAR_TILES = 3


def _static_tables(seq, past, win_buf):
    nq = seq // TQ
    i = np.arange(LANE)[:, None]
    j = np.arange(LANE)[None, :]
    far_bucket = N_BUCKETS - 1
    tiles = [_bucket_np(LANE + i - j),
             np.where(i >= j, _bucket_np(i - j), MASKED_BUCKET),
             np.where(j > i, far_bucket, MASKED_BUCKET) + 0 * i]
    tiles += [_bucket_np(TQ * q + i - (CMP_STRIDE * j + CMP_STRIDE - 1)) for q in range(nq)]
    smp = np.full((LANE, LANE), 10 * MAX_DISTANCE, np.int64)
    k = np.arange(5 * LANE)
    smp[0:5] = np.where(k < win_buf, win_buf - k, 0).reshape(5, LANE)
    ci = np.arange(8 * LANE)
    smp[5:13] = np.maximum(past - (CMP_STRIDE * ci + CMP_STRIDE - 1), 0).reshape(8, LANE)
    nb = past // SLC_LEN
    smp[13] = past - (SLC_LEN * (nb - 2) + np.arange(LANE))
    smp[14] = 0
    tiles.append(_bucket_np(smp))
    idx = np.stack(tiles).astype(np.int32)

    def sp(nrows, ncols, n_slc):
        r = np.arange(nrows)[:, None]
        s = np.arange(ncols)[None, :]
        ov = _overlap_np(r - 1, s)
        return np.where((r >= 1) & (s < n_slc), ov, 0).astype(np.float32)

    sp_p = sp(seq // CMP_STRIDE, LANE, -(-seq // SLC_LEN))
    n_slc_s = -(-(past + 1) // SLC_LEN)
    sp_s = sp(past // CMP_STRIDE, 3 * LANE, n_slc_s)
    return idx, sp_p, sp_s


def _bias_kernel(rb_ref, idx_ref, out_ref):
    idx = idx_ref[0]
    accs = [jnp.full((LANE, LANE), TABLE_MASK, F32) for _ in range(N_HEADS)]
    for b in range(N_BUCKETS):
        hit = idx == b
        for h in range(N_HEADS):
            accs[h] = jnp.where(hit, rb_ref[b, h] - rb_ref[N_BUCKETS - 1, h], accs[h])
    for h in range(N_HEADS):
        out_ref[0, h] = accs[h]


def _bias_tables(rel_bias, idx):
    nt = idx.shape[0]
    return pl.pallas_call(
        _bias_kernel,
        grid=(nt,),
        in_specs=[pl.BlockSpec(memory_space=pltpu.SMEM),
                  pl.BlockSpec((1, LANE, LANE), lambda t: (t, 0, 0))],
        out_specs=pl.BlockSpec((1, N_HEADS, LANE, LANE), lambda t: (t, 0, 0, 0)),
        out_shape=jax.ShapeDtypeStruct((nt, N_HEADS, LANE, LANE), F32),
        compiler_params=_cparams(1),
        name="bias_tables",
    )(rel_bias, idx)


_Q0, _Q1 = 0, N_HEADS * LANE
_KV0, _KV1 = _Q1, _Q1 + 512
_WN0, _WN1 = _KV1, _KV1 + 256
_GT0, _GT1 = _WN1, _WN1 + 2 * LANE
_CV0, _CV1 = _GT1, _GT1 + 3 * 512


def _pack_in_weights(w):
    d = w.shape[0]
    d_nsa = N_HEADS * HEAD_DIM
    wq = w[:, :d_nsa].reshape(d, N_HEADS, HEAD_DIM)
    wq_pad = jnp.pad(wq, ((0, 0), (0, 0), (0, LANE - HEAD_DIM))).reshape(d, N_HEADS * LANE)
    o = d_nsa
    w_kv = w[:, o:o + 512]
    w_win = w[:, o + 512:o + 768]
    wg = w[:, o + 768:o + 768 + 3 * N_HEADS].reshape(d, N_KV, 3 * GQ)
    wg_pad = jnp.pad(wg, ((0, 0), (0, 0), (0, LANE - 3 * GQ))).reshape(d, N_KV * LANE)
    w_conv = w[:, o + 768 + 3 * N_HEADS:]
    return jnp.concatenate([wq_pad, w_kv, w_win, wg_pad, w_conv], axis=1).astype(BF16)


def _rms(x, gain, n, eps=1e-6):
    ms = jnp.sum(x * x, axis=-1, keepdims=True) / n
    return x * lax.rsqrt(ms + eps) * gain


def _layernorm(y, gain, bias, eps=1e-5):
    mu = jnp.mean(y, axis=-1, keepdims=True)
    d = y - mu
    var = jnp.mean(d * d, axis=-1, keepdims=True)
    return d * lax.rsqrt(var + eps) * gain + bias


def _inproj_prompt_kernel(x_ref, w_ref, cw_ref, gc_ref,
                          q_ref, kv_ref, ks_ref, sv_ref, kw_ref, wv_ref, wtail_ref, gate_ref, zn_ref, ctail_ref,
                          uext_ref, *, tiles_per_batch):
    i = pl.program_id(0)
    tm = x_ref.shape[0]
    xb = x_ref[...].astype(BF16)

    def seg(a, b):
        return _mm(xb, w_ref[:, a:b])

    qp = seg(_Q0, _Q1) * (HEAD_DIM ** -0.5)
    for h in range(N_HEADS):
        q_ref[h] = qp[:, h * LANE:(h + 1) * LANE].astype(BF16)
    kv = seg(_KV0, _KV1)
    kv_ref[0] = kv.T
    win = seg(_WN0, _WN1)
    wtail_ref[0] = win.T
    pos = (i % tiles_per_batch) * tm + lax.broadcasted_iota(jnp.int32, (tm, LANE - HEAD_DIM), 0)
    lane = lax.broadcasted_iota(jnp.int32, (tm, LANE - HEAD_DIM), 1)
    blk_flag = jnp.where(lane == lax.shift_right_logical(pos, 6), KEY_MASK, 0.0).astype(BF16)
    no_flag = jnp.zeros((tm, LANE - HEAD_DIM), BF16)
    for g in range(N_KV):
        ks_ref[g] = jnp.concatenate([kv[:, 256 + g * HEAD_DIM:256 + (g + 1) * HEAD_DIM].astype(BF16), blk_flag], axis=1)
        kw_ref[g] = jnp.concatenate([win[:, g * HEAD_DIM:(g + 1) * HEAD_DIM].astype(BF16), no_flag], axis=1)
    sv_ref[...] = kv[:, 384:512].astype(BF16)
    wv_ref[...] = win[:, LANE:2 * LANE].astype(BF16)
    gt = jax.nn.sigmoid(seg(_GT0, _GT1))
    gate_ref[0] = gt[:, 0:LANE]
    gate_ref[1] = gt[:, LANE:2 * LANE]

    conv = seg(_CV0, _CV1)
    cb = conv[:, 0:512]
    u = conv[:, 512:1024] * conv[:, 1024:1536]
    first = (i % tiles_per_batch) == 0

    @pl.when(first)
    def _():
        uext_ref[0:SUBLANE, :] = jnp.zeros((SUBLANE, 512), F32)

    @pl.when(jnp.logical_not(first))
    def _():
        uext_ref[0:SUBLANE, :] = uext_ref[tm:tm + SUBLANE, :]

    uext_ref[SUBLANE:tm + SUBLANE, :] = u
    um1 = uext_ref[SUBLANE - 1:tm + SUBLANE - 1, :]
    um2 = uext_ref[SUBLANE - 2:tm + SUBLANE - 2, :]
    z = cb * (um2 * cw_ref[0:1, :] + um1 * cw_ref[1:2, :] + u * cw_ref[2:3, :])
    zn_ref[...] = _rms(z, gc_ref[...], 512).astype(BF16)
    ctail_ref[0] = uext_ref[tm + SUBLANE - 2:tm + SUBLANE, :]


def _inproj_prompt(x, w, conv_w, g_conv, batch, seq):
    t, d = x.shape
    n_tiles = t // TM
    tpb = seq // TM
    outs = (
        jax.ShapeDtypeStruct((N_HEADS, t, LANE), BF16),
        jax.ShapeDtypeStruct((batch, 512, seq), F32),
        jax.ShapeDtypeStruct((N_KV, t, LANE), BF16),
        jax.ShapeDtypeStruct((t, LANE), BF16),
        jax.ShapeDtypeStruct((N_KV, t, LANE), BF16),
        jax.ShapeDtypeStruct((t, LANE), BF16),
        jax.ShapeDtypeStruct((batch, 256, TM), F32),
        jax.ShapeDtypeStruct((N_KV, t, LANE), F32),
        jax.ShapeDtypeStruct((t, 512), BF16),
        jax.ShapeDtypeStruct((batch, CONV_W - 1, 512), F32),
    )
    row = lambda n: pl.BlockSpec((TM, n), lambda i: (i, 0))
    planes = lambda n: pl.BlockSpec((n, TM, LANE), lambda i: (0, i, 0))
    return pl.pallas_call(
        functools.partial(_inproj_prompt_kernel, tiles_per_batch=tpb),
        grid=(n_tiles,),
        in_specs=[row(d), _const_spec(w.shape), _const_spec(conv_w.shape), _const_spec(g_conv.shape)],
        out_specs=(
            planes(N_HEADS),
            pl.BlockSpec((1, 512, TM), lambda i: (i // tpb, 0, i % tpb)),
            planes(N_KV), row(LANE), planes(N_KV), row(LANE),
            pl.BlockSpec((1, 256, TM), lambda i: (i // tpb, 0, 0)),
            planes(N_KV),
            row(512),
            pl.BlockSpec((1, CONV_W - 1, 512), lambda i: (i // tpb, 0, 0)),
        ),
        out_shape=outs,
        scratch_shapes=[pltpu.VMEM((TM + SUBLANE, 512), F32)],
        compiler_params=_cparams(1),
        name="inproj_prompt",
    )(x, w, conv_w, g_conv)


def _inproj_sample_kernel(x_ref, w_ref, cw_ref, gc_ref, s0_ref, s1_ref,
                          qlo_ref, qgl_ref, kv_ref, win_ref, gate_ref, zn_ref, u_ref):
    xb = x_ref[...].astype(BF16)

    def seg(a, b):
        return _mm(xb, w_ref[:, a:b])

    qp = seg(_Q0, _Q1) * (HEAD_DIM ** -0.5)
    qlo_ref[...] = qp.astype(BF16)
    blocks = [qp[:, h * LANE:(h + 1) * LANE] for h in range(N_HEADS)]
    blocks = [b if h < GQ else pltpu.roll(b, HEAD_DIM, axis=1) for h, b in enumerate(blocks)]
    qgl_ref[...] = jnp.concatenate(blocks, axis=1).astype(BF16)
    kv_ref[...] = seg(_KV0, _KV1)
    win_ref[...] = seg(_WN0, _WN1)
    gate_ref[...] = jax.nn.sigmoid(seg(_GT0, _GT1))
    conv = seg(_CV0, _CV1)
    cb = conv[:, 0:512]
    u = conv[:, 512:1024] * conv[:, 1024:1536]
    z = cb * (s0_ref[...] * cw_ref[0:1, :] + s1_ref[...] * cw_ref[1:2, :] + u * cw_ref[2:3, :])
    zn_ref[...] = _rms(z, gc_ref[...], 512).astype(BF16)
    u_ref[...] = u


def _inproj_sample(x, w, conv_w, g_conv, s0, s1):
    n = x.shape[0]
    outs = (
        jax.ShapeDtypeStruct((n, N_HEADS * LANE), BF16),
        jax.ShapeDtypeStruct((n, N_HEADS * LANE), BF16),
        jax.ShapeDtypeStruct((n, 512), F32),
        jax.ShapeDtypeStruct((n, 256), F32),
        jax.ShapeDtypeStruct((n, N_KV * LANE), F32),
        jax.ShapeDtypeStruct((n, 512), BF16),
        jax.ShapeDtypeStruct((n, 512), F32),
    )
    args = (x, w, conv_w, g_conv, s0, s1)
    return pl.pallas_call(
        _inproj_sample_kernel,
        grid=(1,),
        in_specs=[_const_spec(a.shape) for a in args],
        out_specs=tuple(_const_spec(o.shape) for o in outs),
        out_shape=outs,
        compiler_params=_cparams(1),
        name="inproj_sample",
    )(*args)


def _pack_compress_weights(pe, w1, w2, per_group_out):
    w = w1.reshape(2, CMP_STRIDE, HEAD_DIM, CMP_HIDDEN).transpose(1, 2, 0, 3).reshape(CMP_STRIDE, HEAD_DIM, 2 * CMP_HIDDEN)
    z = jnp.zeros_like(w)
    top = jnp.concatenate([w, z], axis=-1)
    bot = jnp.concatenate([z, w], axis=-1)
    w1bd = jnp.stack([top, bot], axis=1).reshape(CMP_STRIDE * 2 * HEAD_DIM, 4 * CMP_HIDDEN).astype(BF16)
    z2 = jnp.zeros_like(w2)
    if per_group_out:
        w2bd = jnp.concatenate([jnp.concatenate([w2, z2, z2, z2], axis=1),
                                jnp.concatenate([z2, z2, w2, z2], axis=1)], axis=0).astype(BF16)
    else:
        w2bd = jnp.concatenate([jnp.concatenate([w2, z2], axis=1), jnp.concatenate([z2, w2], axis=1)], axis=0).astype(BF16)
    pe_rows = jnp.broadcast_to(pe.reshape(2, CMP_STRIDE, 1, HEAD_DIM), (2, CMP_STRIDE, N_KV, HEAD_DIM))
    pe_rows = pe_rows.reshape(2, CMP_STRIDE * N_KV * HEAD_DIM)
    pe_rows = jnp.pad(pe_rows, ((0, SUBLANE - 2), (0, 0)))
    return w1bd, w2bd, pe_rows


def _compress_kernel(*refs, n_pages, n_prefetch):
    refs = refs[n_prefetch:]
    page_refs = refs[:n_pages]
    w1_ref, w2k_ref, w2v_ref, pe_ref, out_ref, carry_ref, tok_ref = refs[n_pages:]
    step = pl.program_id(1)
    m = n_pages * (PAGE_SIZE // CMP_STRIDE)

    @pl.when(step == 0)
    def _():
        carry_ref[...] = jnp.zeros(carry_ref.shape, F32)

    row0 = lax.broadcasted_iota(jnp.int32, (m, CMP_HIDDEN), 0) == 0
    for t, w2_ref in enumerate((w2k_ref, w2v_ref)):
        for p, pr in enumerate(page_refs):
            slab = pr[0, t].reshape(N_KV * HEAD_DIM, PAGE_SIZE)
            tok_ref[p * PAGE_SIZE:(p + 1) * PAGE_SIZE, :] = slab.T
        pieces = [tok_ref[pl.ds(j, m, stride=CMP_STRIDE), :] for j in range(CMP_STRIDE)]
        lhs = jnp.concatenate([jnp.concatenate(pieces, axis=1), pe_ref[t]], axis=0).astype(BF16)
        a = _mm(lhs, w1_ref[t])
        hs = []
        for g in range(N_KV):
            c0 = g * 2 * CMP_HIDDEN
            a0 = a[0:m, c0:c0 + CMP_HIDDEN]
            a1 = a[0:m, c0 + CMP_HIDDEN:c0 + 2 * CMP_HIDDEN]
            pe_term = a[m:m + 1, c0:c0 + CMP_HIDDEN] + a[m + 1:m + 2, c0 + CMP_HIDDEN:c0 + 2 * CMP_HIDDEN]
            prev = carry_ref[t, 0:1, g * CMP_HIDDEN:(g + 1) * CMP_HIDDEN]
            shifted = jnp.where(row0, prev, pltpu.roll(a0, 1, axis=0))
            carry_ref[t, 0:1, g * CMP_HIDDEN:(g + 1) * CMP_HIDDEN] = a0[m - 1:m, :]
            hs.append(jax.nn.gelu(shifted + a1 + pe_term))
        out = _mm(jnp.concatenate(hs, axis=1).astype(BF16), w2_ref[...]).astype(BF16)
        if t == 0:
            out_ref[0, 0] = out[:, 0:LANE]
            out_ref[0, 1] = out[:, LANE:2 * LANE]
        else:
            out_ref[0, 2] = out


def _compress_call(page_specs, page_args, prefetch, grid, batch, n_chunks, m, w1, w2k, w2v, pe, name):
    n_pages = len(page_specs)
    n_pf = len(prefetch)
    cspec = lambda shape: pl.BlockSpec(shape, lambda *_: (0,) * len(shape))
    grid_spec = pltpu.PrefetchScalarGridSpec(
        num_scalar_prefetch=n_pf,
        grid=grid,
        in_specs=list(page_specs) + [cspec(w1.shape), cspec(w2k.shape), cspec(w2v.shape), cspec(pe.shape)],
        out_specs=pl.BlockSpec((1, 3, m, LANE), lambda b, s, *_: (b, 0, s, 0)),
        scratch_shapes=[pltpu.VMEM((2, SUBLANE, 2 * CMP_HIDDEN), F32),
                        pltpu.VMEM((n_pages * PAGE_SIZE, N_KV * HEAD_DIM), F32)],
    )
    return pl.pallas_call(
        functools.partial(_compress_kernel, n_pages=n_pages, n_prefetch=n_pf),
        grid_spec=grid_spec,
        out_shape=jax.ShapeDtypeStruct((batch, 3, n_chunks, LANE), BF16),
        compiler_params=_cparams(2),
        name=name,
    )(*prefetch, *page_args, w1, w2k, w2v, pe)


_PAGE_BLOCK = (1, 2, N_KV, HEAD_DIM, PAGE_SIZE)


def _compress_prompt(kv_t, batch, seq, w1, w2k, w2v, pe):
    n_pages = seq // PAGE_SIZE
    rows = PAGE_SIZE // CMP_STRIDE
    specs = [pl.BlockSpec(_PAGE_BLOCK, lambda b, s, p=p: (b, 0, 0, 0, p)) for p in range(n_pages)]
    return _compress_call(specs, [kv_t] * n_pages, (), (batch, 1), batch, n_pages * rows, n_pages * rows,
                          w1, w2k, w2v, pe, "compress_prompt")


def _compress_sample(cache_t, page_table, w1, w2k, w2v, pe):
    batch, n_pages = page_table.shape
    rows = PAGE_SIZE // CMP_STRIDE
    specs = [pl.BlockSpec(_PAGE_BLOCK, lambda b, s, pt, k=k: (pt[b, s * SAMPLE_PAGES + k], 0, 0, 0, 0))
             for k in range(SAMPLE_PAGES)]
    return _compress_call(specs, [cache_t] * SAMPLE_PAGES, (page_table,), (batch, n_pages // SAMPLE_PAGES), batch,
                          n_pages * rows, SAMPLE_PAGES * rows, w1, w2k, w2v, pe, "compress_sample")


def _tile4(x):
    return jnp.concatenate([x] * GQ, axis=0)


def _softmax_fold(state, pieces, v):
    m, l, acc = state
    m_new = m
    for pc in pieces:
        m_new = jnp.maximum(m_new, jnp.max(pc, axis=1, keepdims=True))
    alpha = jnp.exp(m - m_new)
    es = [jnp.exp(pc - m_new) for pc in pieces]
    l = alpha * l
    for e in es:
        l = l + jnp.sum(e, axis=1, keepdims=True)
    e_all = es[0] if len(es) == 1 else jnp.concatenate(es, axis=1)
    acc = alpha * acc + _mm(e_all.astype(BF16), v)
    return m_new, l, acc


def _attn_prompt_kernel(q_ref, gate_ref, cmp_ref, ks_ref, sv_ref, kw_ref, wv_ref,
                        tabt_ref, tabc_ref, sp_ref, gn_ref, o_ref, ocat_ref, *, n_slc):
    qi = pl.program_id(1)
    q0 = pl.multiple_of(qi * TQ, TQ)
    row = lax.broadcasted_iota(jnp.int32, (TQ, LANE), 0)
    col = lax.broadcasted_iota(jnp.int32, (TQ, LANE), 1)
    cmp_ok4 = _tile4((col >= 1) & (CMP_STRIDE * col + (CMP_STRIDE - 1) <= q0 + row))
    sidx = lax.broadcasted_iota(jnp.int32, (n_slc, TQ), 0)
    qblk = lax.shift_right_logical(q0 + lax.broadcasted_iota(jnp.int32, (n_slc, TQ), 1), 6)
    sel_valid = sidx <= qblk
    sel_forced = (sidx == 0) | (sidx == qblk) | (sidx == qblk - 1)
    lane_row = lax.broadcasted_iota(jnp.int32, (LANE, TQ), 0)
    n_far = lax.shift_right_logical(jnp.maximum(qi - 1, 0), 2)
    up4 = _tile4(tabt_ref[0, :, 2 * LANE:3 * LANE])
    fresh = (jnp.full((GQ * TQ, 1), NEG, F32), jnp.zeros((GQ * TQ, 1), F32), jnp.zeros((GQ * TQ, LANE), F32))

    def per_group(g, carry):
        heads = [GQ * g + r for r in range(GQ)]
        qs = jnp.concatenate([q_ref[h] for h in heads], axis=0)
        near = jnp.concatenate([tabt_ref[h, :, 0:2 * LANE] for h in heads], axis=0)

        bias_c = jnp.concatenate([tabc_ref[0, h] for h in heads], axis=0)
        s = jnp.where(cmp_ok4, _qk(qs, cmp_ref[0, g]) + bias_c, NEG)
        e = jnp.exp(s - jnp.max(s, axis=1, keepdims=True))
        p = (e / jnp.sum(e, axis=1, keepdims=True) * cmp_ok4.astype(F32)).astype(BF16)
        o_cmp = _mm(p, cmp_ref[0, 2])
        imp4 = _mm(p, sp_ref[...])
        imp = imp4[0:TQ] + imp4[TQ:2 * TQ] + imp4[2 * TQ:3 * TQ] + imp4[3 * TQ:4 * TQ]

        score = jnp.where(sel_valid, imp.T[0:n_slc, :] + jnp.where(sel_forced, FORCE, 0.0), -FORCE)
        rank = jnp.zeros((n_slc, TQ), F32)
        for k in range(1, n_slc):
            other = pltpu.roll(score, k, axis=0)
            rank = rank + jnp.where(sidx >= k, (other >= score).astype(F32), (other > score).astype(F32))
        sel_t = (rank < float(N_SELECT)).astype(F32)
        aug_t = jnp.concatenate([jnp.zeros((SEL_LANE0, TQ), F32), sel_t - 1.0,
                                 jnp.zeros((LANE - SEL_LANE0 - n_slc, TQ), F32)], axis=0)
        aug_t = jnp.where(lane_row == PAD_LANE, -1.0, aug_t)
        qa = qs + _tile4(aug_t.T.astype(BF16))

        def far_tile(it, state):
            start = pl.multiple_of(q0 - FAR * (n_far - it), LANE)
            sc = _qk(qa, ks_ref[g, 0, pl.ds(start, FAR), :])
            return _softmax_fold(state, [sc], sv_ref[0, pl.ds(start, FAR), :])

        state = lax.fori_loop(0, n_far, far_tile, fresh)
        sc = _qk(qa, ks_ref[g, 0, pl.ds(q0, TAIL), :])
        pieces = [sc[:, 0:TAIL - 2 * LANE], sc[:, TAIL - 2 * LANE:TAIL] + near]
        _, l, acc = _softmax_fold(state, pieces, sv_ref[0, pl.ds(q0, TAIL), :])
        o_slc = acc / l

        sc = _qk(qa, kw_ref[g, 0, pl.ds(q0, TAIL), :])
        pieces = [sc[:, 0:LANE] + up4, sc[:, LANE:TAIL - 2 * LANE], sc[:, TAIL - 2 * LANE:TAIL] + near]
        _, l, acc = _softmax_fold(fresh, pieces, wv_ref[0, pl.ds(q0, TAIL), :])
        o_win = acc / l

        gt = gate_ref[g]
        own = lax.shift_right_logical(col, 6) == g
        for r in range(GQ):
            rs = slice(r * TQ, (r + 1) * TQ)
            comb = gt[:, 3 * r:3 * r + 1] * o_cmp[rs] + gt[:, 3 * r + 1:3 * r + 2] * o_slc[rs] \
                + gt[:, 3 * r + 2:3 * r + 3] * o_win[rs]
            ocat_ref[GQ * g + r] = jnp.where(own, comb, 0.0)
        return carry

    lax.fori_loop(0, N_KV, per_group, 0)
    o = jnp.concatenate([ocat_ref[h] for h in range(N_HEADS)], axis=1)
    o_ref[...] = _rms(o, gn_ref[...], N_HEADS * HEAD_DIM).astype(BF16)


def _attn_prompt(q, gates, cmpkv, ks, sv, kw, wv, tab_t, tab_c, sp, gn_pad, batch, seq):
    nq = seq // TQ
    t = batch * seq
    rows = ks.shape[2]
    n_slc = seq // SLC_LEN
    assert WINDOW >= FAR - LANE and SEL_LANE0 + n_slc <= PAD_LANE < LANE
    return pl.pallas_call(
        functools.partial(_attn_prompt_kernel, n_slc=n_slc),
        grid=(batch, nq),
        in_specs=[
            pl.BlockSpec((N_HEADS, TQ, LANE), lambda b, i: (0, b * nq + i, 0)),
            pl.BlockSpec((N_KV, TQ, LANE), lambda b, i: (0, b * nq + i, 0)),
            pl.BlockSpec((1, 3, seq // CMP_STRIDE, LANE), lambda b, i: (b, 0, 0, 0)),
            pl.BlockSpec((N_KV, 1, rows, LANE), lambda b, i: (0, b, 0, 0)),
            pl.BlockSpec((1, rows, LANE), lambda b, i: (b, 0, 0)),
            pl.BlockSpec((N_KV, 1, rows, LANE), lambda b, i: (0, b, 0, 0)),
            pl.BlockSpec((1, rows, LANE), lambda b, i: (b, 0, 0)),
            _const_spec(tab_t.shape),
            pl.BlockSpec((1, N_HEADS, LANE, LANE), lambda b, i: (i, 0, 0, 0)),
            _const_spec(sp.shape),
            _const_spec(gn_pad.shape),
        ],
        out_specs=pl.BlockSpec((TQ, N_HEADS * LANE), lambda b, i: (b * nq + i, 0)),
        out_shape=jax.ShapeDtypeStruct((t, N_HEADS * LANE), BF16),
        scratch_shapes=[pltpu.VMEM((N_HEADS, TQ, LANE), F32)],
        compiler_params=_cparams(2),
        name="attn_prompt",
    )(q, gates, cmpkv, ks, sv, kw, wv, tab_t, tab_c, sp, gn_pad)


def _sample_cw_kernel(qlo_ref, qgl_ref, cmp_ref, cwin_ref, wnew_ref, tabc_ref, tabw_ref, rb0_ref, sp_ref,
                      ocmp_ref, owin_ref, idx_ref, *, n_slc, qblk, win_buf):
    qlo = qlo_ref[0]
    qgl = qgl_ref[0]
    n_cmp_rows = cmp_ref.shape[2]
    col = lax.broadcasted_iota(jnp.int32, (N_HEADS, n_cmp_rows), 1)
    grp_of_head = lax.shift_right_logical(lax.broadcasted_iota(jnp.int32, (N_HEADS, n_cmp_rows), 0), 2)
    ok = col >= 1
    raw = jnp.where(grp_of_head == 0, _qk(qlo, cmp_ref[0, 0]), _qk(qlo, cmp_ref[0, 1]))
    s = jnp.where(ok, raw + tabc_ref[...], NEG)
    e = jnp.exp(s - jnp.max(s, axis=1, keepdims=True))
    p = (e / jnp.sum(e, axis=1, keepdims=True) * ok.astype(F32)).astype(BF16)
    ocmp_ref[0] = _mm(p, cmp_ref[0, 2])
    imp8 = _mm(p, sp_ref[...])

    width = imp8.shape[1]
    head_grp = lax.shift_right_logical(lax.broadcasted_iota(jnp.int32, (N_HEADS, width), 0), 2)
    sidx = lax.broadcasted_iota(jnp.int32, (1, width), 1)
    sidx_f = sidx.astype(F32)
    forced = (sidx == 0) | (sidx == qblk) | (sidx == qblk - 1)
    lane = lax.broadcasted_iota(jnp.int32, (1, LANE), 1)
    for g in range(N_KV):
        imp = jnp.sum(jnp.where(head_grp == g, imp8, 0.0), axis=0, keepdims=True)
        score = jnp.where(sidx <= qblk, imp + jnp.where(forced, FORCE, 0.0), -FORCE)
        score = jnp.where(sidx < n_slc, score, -jnp.inf)
        picked = jnp.zeros((1, LANE), F32)
        for k in range(N_SELECT):
            best = jnp.max(score, axis=1, keepdims=True)
            ik = jnp.min(jnp.where(score == best, sidx_f, float(width)), axis=1, keepdims=True)
            picked = jnp.where(lane == k, ik, picked)
            score = jnp.where(sidx_f == ik, -jnp.inf, score)
        idx_ref[0, g:g + 1, :] = picked.astype(jnp.int32)

    kt = cwin_ref[0, 0].reshape(N_KV * HEAD_DIM, win_buf).astype(BF16)
    vt = cwin_ref[0, 1].reshape(N_KV * HEAD_DIM, win_buf).astype(BF16)
    knew = wnew_ref[0, :, 0:LANE].astype(BF16).astype(F32)
    vnew = wnew_ref[0, :, LANE:2 * LANE].astype(BF16).astype(F32)
    colw = lax.broadcasted_iota(jnp.int32, (N_HEADS, win_buf), 1)
    okw = colw > win_buf - WINDOW
    sw = jnp.where(okw, _mm(qgl, kt) + tabw_ref[...], NEG)
    s_new = jnp.sum(qgl.astype(F32) * knew, axis=1, keepdims=True) + rb0_ref[:, 0:1]
    mw = jnp.maximum(jnp.max(sw, axis=1, keepdims=True), s_new)
    ew = jnp.where(okw, jnp.exp(sw - mw), 0.0)
    e_new = jnp.exp(s_new - mw)
    lw = jnp.sum(ew, axis=1, keepdims=True) + e_new
    owin_ref[0] = (_qk(ew.astype(BF16), vt) + e_new.astype(BF16).astype(F32) * vnew) / lw


def _sample_cw(qlo, qgl, cmpkv, cwin, wnew, tab_c, tab_w, rb0, sp, n_slc, qblk):
    n = qlo.shape[0]
    win_buf = cwin.shape[-1]
    blk = lambda a: pl.BlockSpec((1,) + a.shape[1:], lambda b: (b,) + (0,) * (a.ndim - 1))
    outs = (jax.ShapeDtypeStruct((n, N_HEADS, LANE), F32), jax.ShapeDtypeStruct((n, N_HEADS, LANE), F32),
            jax.ShapeDtypeStruct((n, N_KV, LANE), jnp.int32))
    return pl.pallas_call(
        functools.partial(_sample_cw_kernel, n_slc=n_slc, qblk=qblk, win_buf=win_buf),
        grid=(n,),
        in_specs=[blk(qlo), blk(qgl), blk(cmpkv), blk(cwin), blk(wnew), _const_spec(tab_c.shape),
                  _const_spec(tab_w.shape), _const_spec(rb0.shape), _const_spec(sp.shape)],
        out_specs=tuple(pl.BlockSpec((1,) + o.shape[1:], lambda b: (b, 0, 0)) for o in outs),
        out_shape=outs,
        compiler_params=_cparams(1),
        name="sample_cmp_win",
    )(qlo, qgl, cmpkv, cwin, wnew, tab_c, tab_w, rb0, sp)


def _sample_slc_kernel(pg_ref, hf_ref, bid_ref, *refs, new_block, near_block):
    n_blk = N_KV * N_SELECT
    blocks = refs[:n_blk]
    (q_ref, kvnew_ref, ocmp_ref, owin_ref, gate_ref, tnear_ref, tlast_ref, rb0_ref, gn_ref, o_ref) = refs[n_blk:]
    b = pl.program_id(0)
    n_keys = N_SELECT * PAGE_SIZE
    lane = lax.broadcasted_iota(jnp.int32, (1, n_keys), 1)
    slot = lax.shift_right_logical(lane, 7)
    half = lax.shift_right_logical(lane, 6) & 1
    head_grp = lax.shift_right_logical(lax.broadcasted_iota(jnp.int32, (N_HEADS, LANE), 0), 2)
    lane_grp = lax.shift_right_logical(lax.broadcasted_iota(jnp.int32, (N_HEADS, LANE), 1), 6)
    knew = kvnew_ref[0, :, 256:384].astype(BF16).astype(F32)
    vnew = kvnew_ref[0, :, 384:512].astype(BF16).astype(F32)
    qs = q_ref[0]
    o_slc = jnp.zeros((N_HEADS, LANE), F32)
    for g in range(N_KV):
        mine = blocks[g * N_SELECT:(g + 1) * N_SELECT]
        kt = jnp.concatenate([blk[0, 0].reshape(N_KV * HEAD_DIM, PAGE_SIZE) for blk in mine], axis=1).astype(BF16)
        vt = jnp.concatenate([blk[0, 1].reshape(N_KV * HEAD_DIM, PAGE_SIZE) for blk in mine], axis=1).astype(BF16)
        bid = jnp.zeros((1, n_keys), jnp.int32)
        hsel = jnp.zeros((1, n_keys), jnp.int32)
        has_new = bid_ref[b, g * N_SELECT] == new_block
        for kk in range(N_SELECT):
            bid = jnp.where(slot == kk, bid_ref[b, g * N_SELECT + kk], bid)
            hsel = jnp.where(slot == kk, hf_ref[b, g * N_SELECT + kk], hsel)
            if kk:
                has_new = has_new | (bid_ref[b, g * N_SELECT + kk] == new_block)
        bias = jnp.where(bid == near_block + 1, tlast_ref[...], jnp.where(bid == near_block, tnear_ref[...], 0.0))
        ok = (bid != new_block) & (half == hsel)
        s = jnp.where(ok, _mm(qs, kt) + bias, NEG)
        s_new = jnp.sum(qs.astype(F32) * knew, axis=1, keepdims=True) + rb0_ref[:, 0:1]
        s_new = jnp.where(has_new, s_new, NEG)
        m = jnp.maximum(jnp.max(s, axis=1, keepdims=True), s_new)
        e = jnp.where(ok, jnp.exp(s - m), 0.0)
        e_new = jnp.where(has_new, jnp.exp(s_new - m), 0.0)
        l = jnp.sum(e, axis=1, keepdims=True) + e_new
        og = (_qk(e.astype(BF16), vt) + e_new.astype(BF16).astype(F32) * vnew) / l
        o_slc = jnp.where(head_grp == g, og, o_slc)
    gt = gate_ref[0]
    comb = gt[:, 0:1] * ocmp_ref[0] + gt[:, 1:2] * o_slc + gt[:, 2:3] * owin_ref[0]
    comb = jnp.where(lane_grp == head_grp, comb, 0.0)
    o = jnp.concatenate([comb[h:h + 1, :] for h in range(N_HEADS)], axis=1)
    o_ref[0] = _rms(o, gn_ref[...], N_HEADS * HEAD_DIM).astype(BF16)


def _sample_slc(pages, halves, bids, cache_t, q, kvnew, ocmp, owin, gates, t_near, t_last, rb0, gn_pad,
                new_block, near_block):
    n = q.shape[0]
    n_blk = N_KV * N_SELECT
    blk_specs = [pl.BlockSpec(_PAGE_BLOCK, lambda b, pg, hf, bi, k=k: (pg[b, k], 1, 0, 0, 0))
                 for k in range(n_blk)]
    per_b = lambda a: pl.BlockSpec((1,) + a.shape[1:], lambda b, *_: (b,) + (0,) * (a.ndim - 1))
    cst = lambda a: pl.BlockSpec(a.shape, lambda *_: (0,) * a.ndim)
    grid_spec = pltpu.PrefetchScalarGridSpec(
        num_scalar_prefetch=3,
        grid=(n,),
        in_specs=blk_specs + [per_b(q), per_b(kvnew), per_b(ocmp), per_b(owin), per_b(gates),
                              cst(t_near), cst(t_last), cst(rb0), cst(gn_pad)],
        out_specs=pl.BlockSpec((1, 1, N_HEADS * LANE), lambda b, *_: (b, 0, 0)),
    )
    return pl.pallas_call(
        functools.partial(_sample_slc_kernel, new_block=new_block, near_block=near_block),
        grid_spec=grid_spec,
        out_shape=jax.ShapeDtypeStruct((n, 1, N_HEADS * LANE), BF16),
        compiler_params=_cparams(1),
        name="sample_selected",
    )(pages, halves, bids, *([cache_t] * n_blk), q, kvnew, ocmp, owin, gates, t_near, t_last, rb0, gn_pad)


def _proj_kernel(on_ref, zn_ref, x_ref, wa_ref, wb_ref, g1_ref, b1_ref, wr_ref, br_ref,
                 x1_ref, gate_ref, *, alpha):
    mix = _mm(on_ref[...], wa_ref[...]) + _mm(zn_ref[...], wb_ref[...])
    x1 = _layernorm(alpha * x_ref[...] + mix, g1_ref[...], b1_ref[...])
    x1_ref[...] = x1
    logits = _mm(x1.astype(BF16), wr_ref[...]) + br_ref[...]
    tm = logits.shape[0]
    lane = lax.broadcasted_iota(jnp.int32, (tm, LANE), 1).astype(F32)

    def first_argmax(v, vmax):
        return jnp.min(jnp.where(v == vmax, lane, float(LANE)), axis=1, keepdims=True)

    lg = jnp.where(lane < N_GROUPS, logits[:, 0:LANE], -jnp.inf)
    lg_max = jnp.max(lg, axis=1, keepdims=True)
    eg = jnp.exp(lg - lg_max)
    pg = eg / jnp.sum(eg, axis=1, keepdims=True)
    gidx = first_argmax(lg, lg_max)
    pg_sel = jnp.sum(jnp.where(lane == gidx, pg, 0.0), axis=1, keepdims=True)
    le = jnp.zeros((tm, LANE), F32)
    for gi in range(N_GROUPS):
        le = le + jnp.where(gidx == gi, logits[:, (gi + 1) * LANE:(gi + 2) * LANE], 0.0)
    le = jnp.where(lane < N_EXP, le, -jnp.inf)
    ee = jnp.exp(le - jnp.max(le, axis=1, keepdims=True))
    pe = jnp.where(lane < N_EXP, ee / jnp.sum(ee, axis=1, keepdims=True), -1.0)
    v1 = jnp.max(pe, axis=1, keepdims=True)
    i1 = first_argmax(pe, v1)
    pe2 = jnp.where(lane == i1, -1.0, pe)
    v2 = jnp.max(pe2, axis=1, keepdims=True)
    i2 = first_argmax(pe2, v2)
    tot = v1 + v2
    gate_e = jnp.where(lane == i1, v1 / tot * pg_sel, jnp.where(lane == i2, v2 / tot * pg_sel, 0.0))
    gate_ref[...] = jnp.where(lane == GROUP_LANE, gidx, gate_e)


def _proj(on, zn, x, wa, wb, g1, b1, wr, br, alpha, tm):
    t, d = x.shape
    row = lambda n: pl.BlockSpec((tm, n), lambda i: (i, 0))
    outs = (jax.ShapeDtypeStruct((t, d), F32), jax.ShapeDtypeStruct((t, LANE), F32))
    return pl.pallas_call(
        functools.partial(_proj_kernel, alpha=alpha),
        grid=(t // tm,),
        in_specs=[row(on.shape[1]), row(zn.shape[1]), row(d)] + [_const_spec(a.shape) for a in (wa, wb, g1, b1, wr, br)],
        out_specs=(row(d), row(LANE)),
        out_shape=outs,
        compiler_params=_cparams(1),
        name=f"proj_ln_router_{tm}",
    )(on, zn, x, wa, wb, g1, b1, wr, br)


def _moe_kernel(x1_ref, gate_ref, wgu_ref, wd_ref, g2_ref, b2_ref, out_ref, acc_ref, *, alpha, d_ff):
    gi = pl.program_id(1)

    @pl.when(gi == 0)
    def _():
        acc_ref[...] = jnp.zeros(acc_ref.shape, F32)

    x = x1_ref[...].astype(BF16)
    gate = gate_ref[...]
    gt = jnp.where(gate[:, GROUP_LANE:GROUP_LANE + 1] == gi.astype(F32), gate, 0.0)
    acc = acc_ref[...]
    for e in range(N_EXP):
        au = _mm(x, wgu_ref[0, e])
        h = jax.nn.silu(au[:, 0:d_ff]) * au[:, d_ff:2 * d_ff]
        acc = acc + _mm((h * gt[:, e:e + 1]).astype(BF16), wd_ref[0, e])
    acc_ref[...] = acc

    @pl.when(gi == N_GROUPS - 1)
    def _():
        out_ref[...] = _layernorm(alpha * x1_ref[...] + acc_ref[...], g2_ref[...], b2_ref[...])


def _moe(x1, gate, wgu, wd, g2, b2, alpha, tm):
    t, d = x1.shape
    d_ff = wd.shape[2]
    return pl.pallas_call(
        functools.partial(_moe_kernel, alpha=alpha, d_ff=d_ff),
        grid=(t // tm, N_GROUPS),
        in_specs=[
            pl.BlockSpec((tm, d), lambda i, g: (i, 0)),
            pl.BlockSpec((tm, LANE), lambda i, g: (i, 0)),
            pl.BlockSpec((1, N_EXP, d, 2 * d_ff), lambda i, g: (g, 0, 0, 0)),
            pl.BlockSpec((1, N_EXP, d_ff, d), lambda i, g: (g, 0, 0, 0)),
            _const_spec(g2.shape), _const_spec(b2.shape),
        ],
        out_specs=pl.BlockSpec((tm, d), lambda i, g: (i, 0)),
        out_shape=jax.ShapeDtypeStruct((t, d), F32),
        scratch_shapes=[pltpu.VMEM((tm, d), F32)],
        compiler_params=_cparams(2),
        name=f"moe_ln_{tm}",
    )(x1, gate, wgu, wd, g2, b2)


def _split_bf16(x):
    hi = x.astype(BF16)
    return hi, (x - hi.astype(F32)).astype(BF16)


def _moe_sorted_kernel(cnt_ref, off_ref, x1_ref, gate_ref, drow_ref, dcol_ref, wgu_ref, wd_ref, g2_ref, b2_ref,
                       out_ref, xs_ref, gs_ref, ys_ref, *, alpha, d_ff):
    i = pl.program_id(0)
    gi = pl.program_id(1)
    half = pl.program_id(2)
    n_half = N_EXP // EXPERT_SPLIT
    tmoe = x1_ref.shape[0]

    @pl.when((i == 0) & (gi == 0) & (half == 0))
    def _():
        xs_ref[...] = jnp.zeros(xs_ref.shape, BF16)
        gs_ref[...] = jnp.zeros(gs_ref.shape, F32)
        ys_ref[...] = jnp.zeros(ys_ref.shape, F32)

    @pl.when((gi == 0) & (half == 0))
    def _():
        perm = (lax.broadcasted_iota(jnp.int32, (PERM_ROWS, tmoe), 0) == drow_ref[0]).astype(BF16)
        xs_ref[0:PERM_ROWS, :] = _mm(perm, x1_ref[...].astype(BF16)).astype(BF16)
        g_hi, g_lo = _split_bf16(gate_ref[...])
        gs_ref[0:PERM_ROWS, :] = _mm(perm, g_hi) + _mm(perm, g_lo)

    n = cnt_ref[i * N_GROUPS + gi]
    off = off_ref[i * N_GROUPS + gi]

    def run_pass(c, carry):
        base = pl.multiple_of(off + c * MOE_CHUNK, SEG_ALIGN)
        xc = xs_ref[pl.ds(base, MOE_CHUNK), :]
        gc = gs_ref[pl.ds(base, MOE_CHUNK), :]
        gc = jnp.where(half == 0, gc, pltpu.roll(gc, LANE - n_half, axis=1))
        y = jnp.zeros((MOE_CHUNK, out_ref.shape[1]), F32)
        for e in range(n_half):
            au = _mm(xc, wgu_ref[0, e])
            h = jax.nn.silu(au[:, 0:d_ff]) * au[:, d_ff:2 * d_ff]
            y = y + _mm((h * gc[:, e:e + 1]).astype(BF16), wd_ref[0, e])

        @pl.when(half == 0)
        def _():
            ys_ref[pl.ds(base, MOE_CHUNK), :] = y

        @pl.when(half != 0)
        def _():
            ys_ref[pl.ds(base, MOE_CHUNK), :] = ys_ref[pl.ds(base, MOE_CHUNK), :] + y

        return carry

    lax.fori_loop(0, (n + MOE_CHUNK - 1) // MOE_CHUNK, run_pass, 0)

    @pl.when((gi == N_GROUPS - 1) & (half == EXPERT_SPLIT - 1))
    def _():
        unperm = (dcol_ref[...] == lax.broadcasted_iota(jnp.int32, (tmoe, PERM_ROWS), 1)).astype(BF16)
        y_hi, y_lo = _split_bf16(ys_ref[0:PERM_ROWS, :])
        moe = _mm(unperm, y_hi) + _mm(unperm, y_lo)
        out_ref[...] = _layernorm(alpha * x1_ref[...] + moe, g2_ref[...], b2_ref[...])


def _moe_sorted(x1, gate, wgu, wd, g2, b2, alpha):
    t, d = x1.shape
    d_ff = wd.shape[2]
    n_tiles = t // TMOE
    n_half = N_EXP // EXPERT_SPLIT
    gid = gate[:, GROUP_LANE].astype(jnp.int32).reshape(n_tiles, TMOE)
    onehot = (gid[:, :, None] == jnp.arange(N_GROUPS)[None, None, :]).astype(jnp.int32)
    cnt = jnp.sum(onehot, axis=1)
    rank = jnp.sum((jnp.cumsum(onehot, axis=1) - onehot) * onehot, axis=2)
    seg = (cnt + SEG_ALIGN - 1) // SEG_ALIGN * SEG_ALIGN
    off = jnp.cumsum(seg, axis=1) - seg
    dest = jnp.sum(onehot * off[:, None, :], axis=2) + rank
    assert N_GROUPS * (SEG_ALIGN - 1) <= PERM_ROWS - TMOE
    wgu_h = wgu.reshape(N_GROUPS * EXPERT_SPLIT, n_half, d, 2 * d_ff)
    wd_h = wd.reshape(N_GROUPS * EXPERT_SPLIT, n_half, d_ff, d)
    grid_spec = pltpu.PrefetchScalarGridSpec(
        num_scalar_prefetch=2,
        grid=(n_tiles, N_GROUPS, EXPERT_SPLIT),
        in_specs=[
            pl.BlockSpec((TMOE, d), lambda i, g, h, *_: (i, 0)),
            pl.BlockSpec((TMOE, LANE), lambda i, g, h, *_: (i, 0)),
            pl.BlockSpec((1, 1, TMOE), lambda i, g, h, *_: (i, 0, 0)),
            pl.BlockSpec((TMOE, 1), lambda i, g, h, *_: (i, 0)),
            pl.BlockSpec((1, n_half, d, 2 * d_ff), lambda i, g, h, *_: (g * EXPERT_SPLIT + h, 0, 0, 0)),
            pl.BlockSpec((1, n_half, d_ff, d), lambda i, g, h, *_: (g * EXPERT_SPLIT + h, 0, 0, 0)),
            pl.BlockSpec(g2.shape, lambda *_: (0, 0)), pl.BlockSpec(b2.shape, lambda *_: (0, 0)),
        ],
        out_specs=pl.BlockSpec((TMOE, d), lambda i, g, h, *_: (i, 0)),
        scratch_shapes=[pltpu.VMEM((SORT_ROWS, d), BF16), pltpu.VMEM((SORT_ROWS, LANE), F32),
                        pltpu.VMEM((SORT_ROWS, d), F32)],
    )
    return pl.pallas_call(
        functools.partial(_moe_sorted_kernel, alpha=alpha, d_ff=d_ff),
        grid_spec=grid_spec,
        out_shape=jax.ShapeDtypeStruct((t, d), F32),
        compiler_params=_cparams(3),
        name="moe_sorted_ln",
    )(cnt.reshape(-1), off.reshape(-1), x1, gate, dest.reshape(n_tiles, 1, TMOE), dest.reshape(t, 1), wgu_h, wd_h,
      g2, b2)


def _pad_head_lanes(v):
    vh = v.reshape(N_HEADS, HEAD_DIM)
    z = jnp.zeros_like(vh)
    grp = (jnp.arange(N_HEADS) // GQ)[:, None]
    return jnp.where(grp == 0, jnp.concatenate([vh, z], axis=1), jnp.concatenate([z, vh], axis=1)).reshape(1, -1)


def _pad_head_rows(w):
    wh = w.reshape(N_HEADS, HEAD_DIM, -1)
    z = jnp.zeros_like(wh)
    grp = (jnp.arange(N_HEADS) // GQ)[:, None, None]
    return jnp.where(grp == 0, jnp.concatenate([wh, z], axis=1), jnp.concatenate([z, wh], axis=1)).reshape(
        N_HEADS * LANE, -1)


def _front_pad(a, axis, pad_row):
    shape = list(a.shape)
    shape[axis] = WINDOW
    return jnp.concatenate([jnp.broadcast_to(pad_row.astype(a.dtype), shape), a], axis=axis)


def kernel(x_prompt, x_sample, cache_kv, page_table, cache_win, state_conv, w_in, conv_w, pe_k, pe_v, w_ck1, w_ck2, w_cv1, w_cv2, g_nsa, g_conv, w_out, ln1_g, ln1_b, w_rg, b_rg, w_re, b_re, w_eg, w_eu, w_ed, ln2_g, ln2_b, rel_bias):
    batch, seq, d_model = x_prompt.shape
    dec_batch, dec_seq = x_sample.shape[0], x_sample.shape[1]
    depth = w_in.shape[0]
    n_pages = page_table.shape[1]
    past = n_pages * PAGE_SIZE
    win_buf = cache_win.shape[2]
    d_nsa = N_HEADS * HEAD_DIM
    d_conv = w_out.shape[1] - d_nsa
    assert dec_seq == 1 and seq % TM == 0 and min(WINDOW, seq) == TM and win_buf == WINDOW
    assert d_conv == 512 and seq // CMP_STRIDE == LANE and n_pages % SAMPLE_PAGES == 0
    alpha = (2 * depth) ** 0.25
    n_slc_s = -(-(past + dec_seq) // SLC_LEN)
    qblk_s = past // SLC_LEN

    idx_np, sp_p_np, sp_s_np = _static_tables(seq, past, win_buf)
    tabs = _bias_tables(rel_bias, jnp.asarray(idx_np))
    nq = seq // TQ
    tab_t = jnp.transpose(tabs[0:_N_NEAR_TILES], (1, 2, 0, 3)).reshape(N_HEADS, LANE, _N_NEAR_TILES * LANE)
    tab_c = tabs[_N_NEAR_TILES:_N_NEAR_TILES + nq]
    smp = tabs[_N_NEAR_TILES + nq]
    tab_w_s = smp[:, 0:5, :].reshape(N_HEADS, 5 * LANE)[:, :win_buf]
    tab_c_s = smp[:, 5:13, :].reshape(N_HEADS, 8 * LANE)
    reps = N_SELECT * PAGE_SIZE // SLC_LEN
    t_near = jnp.tile(smp[:, 13, 0:SLC_LEN], (1, reps))
    t_last = jnp.tile(smp[:, 13, SLC_LEN:2 * SLC_LEN], (1, reps))
    rb0_rep = jnp.broadcast_to(smp[:, 14, 0:1], (N_HEADS, LANE))
    sp_p = jnp.asarray(sp_p_np, BF16)
    sp_s = jnp.asarray(sp_s_np, BF16)
    key_pad = jnp.zeros((LANE,), F32).at[PAD_LANE].set(KEY_MASK)
    val_pad = jnp.zeros((LANE,), F32)

    xp = x_prompt.reshape(batch * seq, d_model)
    xs = x_sample.reshape(dec_batch * dec_seq, d_model)
    outs = [[] for _ in range(6)]
    for l in range(depth):
        w_pack = _pack_in_weights(w_in[l])
        gc = g_conv[l].reshape(1, -1)
        gn_pad = _pad_head_lanes(g_nsa[l])
        w1k, w2k, pek = _pack_compress_weights(pe_k[l], w_ck1[l], w_ck2[l], True)
        w1v, w2v, pev = _pack_compress_weights(pe_v[l], w_cv1[l], w_cv2[l], False)
        w1 = jnp.stack([w1k, w1v])
        pe = jnp.stack([pek, pev])
        wa = _pad_head_rows(w_out[l][:d_nsa]).astype(BF16)
        wb = w_out[l][d_nsa:].astype(BF16)
        wr = jnp.concatenate(
            [jnp.pad(w_rg[l], ((0, 0), (0, LANE - N_GROUPS)))]
            + [jnp.pad(w_re[l][:, gi * N_EXP:(gi + 1) * N_EXP], ((0, 0), (0, LANE - N_EXP))) for gi in range(N_GROUPS)],
            axis=1).astype(BF16)
        br = jnp.concatenate(
            [jnp.pad(b_rg[l], (0, LANE - N_GROUPS))]
            + [jnp.pad(b_re[l][gi * N_EXP:(gi + 1) * N_EXP], (0, LANE - N_EXP)) for gi in range(N_GROUPS)]).reshape(1, -1)
        wgu = jnp.concatenate([w_eg[l], w_eu[l]], axis=-1).astype(BF16)
        wd = w_ed[l].astype(BF16)
        g1, b1 = ln1_g[l].reshape(1, -1), ln1_b[l].reshape(1, -1)
        g2, b2 = ln2_g[l].reshape(1, -1), ln2_b[l].reshape(1, -1)

        q_p, kvt_p, ks, sv, kw, wv, wtail_t, gates_p, zn_p, ctail = _inproj_prompt(xp, w_pack, conv_w[l], gc, batch, seq)
        kvt_p = kvt_p.reshape(batch, 4, N_KV, HEAD_DIM, seq)
        cmp_p = _compress_prompt(kvt_p, batch, seq, w1, w2k, w2v, pe)
        ks = _front_pad(ks.reshape(N_KV, batch, seq, LANE), 2, key_pad)
        kw = _front_pad(kw.reshape(N_KV, batch, seq, LANE), 2, key_pad)
        sv = _front_pad(sv.reshape(batch, seq, LANE), 1, val_pad)
        wv = _front_pad(wv.reshape(batch, seq, LANE), 1, val_pad)
        on_p = _attn_prompt(q_p, gates_p, cmp_p, ks, sv, kw, wv, tab_t, tab_c, sp_p, gn_pad, batch, seq)
        x1_p, gate_p = _proj(on_p, zn_p, xp, wa, wb, g1, b1, wr, br, alpha, TM)
        y_p = _moe_sorted(x1_p, gate_p, wgu, wd, g2, b2, alpha)

        st = state_conv[l]
        qlo_s, qgl_s, kv_s, win_s, gates_s, zn_s, u_s = _inproj_sample(xs, w_pack, conv_w[l], gc, st[:, 0], st[:, 1])
        cache_t = jnp.transpose(cache_kv[l], (0, 2, 3, 4, 1))
        cwin_t = jnp.transpose(cache_win[l], (0, 2, 3, 4, 1))
        cmp_s = _compress_sample(cache_t, page_table, w1, w2k, w2v, pe)
        qlo_s3 = qlo_s.reshape(dec_batch, N_HEADS, LANE)
        qgl_s3 = qgl_s.reshape(dec_batch, N_HEADS, LANE)
        ocmp, owin, sel_idx = _sample_cw(qlo_s3, qgl_s3, cmp_s, cwin_t, win_s.reshape(dec_batch, 1, 256), tab_c_s,
                                         tab_w_s, rb0_rep, sp_s, n_slc_s, qblk_s)
        bids = sel_idx[:, :, :N_SELECT].reshape(dec_batch, N_KV * N_SELECT)
        blk_pages = jnp.take_along_axis(page_table, jnp.minimum(bids // 2, n_pages - 1), axis=1)
        gates_s3 = jnp.pad(gates_s.reshape(dec_batch, N_KV, LANE)[:, :, :3 * GQ].reshape(dec_batch, N_HEADS, 3),
                           ((0, 0), (0, 0), (0, LANE - 3)))
        on_s = _sample_slc(blk_pages, bids % 2, bids, cache_t, qgl_s3, kv_s.reshape(dec_batch, 1, 512), ocmp, owin,
                           gates_s3, t_near, t_last, rb0_rep, gn_pad, n_slc_s - 1, qblk_s - 2)
        on_s = on_s.reshape(dec_batch, N_HEADS * LANE)
        x1_s, gate_s = _proj(on_s, zn_s, xs, wa, wb, g1, b1, wr, br, alpha, dec_batch)
        y_s = _moe(x1_s, gate_s, wgu, wd, g2, b2, alpha, dec_batch)

        to_token_major = lambda a: jnp.transpose(a, (0, 4, 1, 2, 3))
        outs[0].append(to_token_major(kvt_p))
        outs[1].append(kv_s.reshape(dec_batch, dec_seq, 4, N_KV, HEAD_DIM))
        outs[2].append(to_token_major(wtail_t.reshape(batch, 2, N_KV, HEAD_DIM, TM)))
        win_all_t = jnp.concatenate([cwin_t, win_s.reshape(dec_batch, 2, N_KV, HEAD_DIM, dec_seq)], axis=-1)
        outs[3].append(to_token_major(win_all_t[..., win_all_t.shape[-1] - min(WINDOW, past + dec_seq):]))
        outs[4].append(ctail)
        outs[5].append(jnp.concatenate([st, u_s[:, None, :]], axis=1)[:, dec_seq:])
        xp, xs = y_p, y_s
    return (xp.reshape(batch, seq, d_model), xs.reshape(dec_batch, dec_seq, d_model),
            jnp.stack(outs[0]), jnp.stack(outs[1]), jnp.stack(outs[2]), jnp.stack(outs[3]),
            jnp.stack(outs[4]), jnp.stack(outs[5]))
```

```python
import functools
import math

import numpy as np
import jax
import jax.numpy as jnp
from jax import lax
from jax.experimental import pallas as pl
from jax.experimental.pallas import tpu as pltpu

F32 = jnp.float32
BF16 = jnp.bfloat16

HEAD_DIM = 64
N_KV = 2
GQ = 4
N_HEADS = N_KV * GQ
CONV_W = 3
CMP_STRIDE = 16
CMP_LEN = 32
CMP_HIDDEN = 2 * HEAD_DIM
SLC_LEN = 64
N_SELECT = 16
WINDOW = 512
N_BUCKETS = 32
MAX_DISTANCE = 128
N_GROUPS = 4
N_EXP = 8
PAGE_SIZE = 128
NEG = -1e30
FORCE = 1e6

LANE = 128
SUBLANE = 8
VMEM_LIMIT = 52 * 1024 * 1024

TQ = 128
TM = 512
FAR = 512
TAIL = WINDOW + TQ
SAMPLE_PAGES = 32
TMOE = 1024
MOE_CHUNK = 320
SEG_ALIGN = 16
PERM_ROWS = TMOE + LANE
SORT_ROWS = PERM_ROWS + MOE_CHUNK
EXPERT_SPLIT = 2
GROUP_LANE = N_EXP

KEY_MASK = 2.0 ** 100
TABLE_MASK = 2 * NEG
MASKED_BUCKET = N_BUCKETS
SEL_LANE0 = HEAD_DIM
PAD_LANE = HEAD_DIM + 32


def _cparams(n_axes):
    return pltpu.CompilerParams(dimension_semantics=("arbitrary",) * n_axes, vmem_limit_bytes=VMEM_LIMIT)


def _const_spec(shape):
    nd = len(shape)
    return pl.BlockSpec(shape, lambda *_, nd=nd: (0,) * nd)


def _qk(a, b):
    return lax.dot_general(a, b, (((1,), (1,)), ((), ())), preferred_element_type=F32)


def _mm(a, b):
    return jnp.dot(a, b, preferred_element_type=F32)


def _bucket_np(dist):
    n = np.maximum(dist, 0)
    max_exact = N_BUCKETS // 2
    nf = np.maximum(n, 1).astype(np.float32)
    large = max_exact + (np.log(nf / np.float32(max_exact)) / np.float32(math.log(MAX_DISTANCE / max_exact))
                         * np.float32(N_BUCKETS - max_exact)).astype(np.int32)
    large = np.minimum(large, N_BUCKETS - 1)
    return np.where(n < max_exact, n, large).astype(np.int32)


def _overlap_np(c, s):
    c0 = c * CMP_STRIDE
    s0 = s * SLC_LEN
    return np.maximum(np.minimum(c0 + CMP_LEN, s0 + SLC_LEN) - np.maximum(c0, s0), 0)


_N_NEAR_TILES = 3


def _static_tables(seq, past, win_buf):
    nq = seq // TQ
    i = np.arange(LANE)[:, None]
    j = np.arange(LANE)[None, :]
    far_bucket = N_BUCKETS - 1
    tiles = [_bucket_np(LANE + i - j),
             np.where(i >= j, _bucket_np(i - j), MASKED_BUCKET),
             np.where(j > i, far_bucket, MASKED_BUCKET) + 0 * i]
    tiles += [_bucket_np(TQ * q + i - (CMP_STRIDE * j + CMP_STRIDE - 1)) for q in range(nq)]
    smp = np.full((LANE, LANE), 10 * MAX_DISTANCE, np.int64)
    k = np.arange(5 * LANE)
    smp[0:5] = np.where(k < win_buf, win_buf - k, 0).reshape(5, LANE)
    ci = np.arange(8 * LANE)
    smp[5:13] = np.maximum(past - (CMP_STRIDE * ci + CMP_STRIDE - 1), 0).reshape(8, LANE)
    nb = past // SLC_LEN
    smp[13] = past - (SLC_LEN * (nb - 2) + np.arange(LANE))
    smp[14] = 0
    tiles.append(_bucket_np(smp))
    idx = np.stack(tiles).astype(np.int32)

    def sp(nrows, ncols, n_slc):
        r = np.arange(nrows)[:, None]
        s = np.arange(ncols)[None, :]
        ov = _overlap_np(r - 1, s)
        return np.where((r >= 1) & (s < n_slc), ov, 0).astype(np.float32)

    sp_p = sp(seq // CMP_STRIDE, LANE, -(-seq // SLC_LEN))
    n_slc_s = -(-(past + 1) // SLC_LEN)
    sp_s = sp(past // CMP_STRIDE, 3 * LANE, n_slc_s)
    return idx, sp_p, sp_s


def _bias_kernel(rb_ref, idx_ref, out_ref):
    idx = idx_ref[0]
    accs = [jnp.full((LANE, LANE), TABLE_MASK, F32) for _ in range(N_HEADS)]
    for b in range(N_BUCKETS):
        hit = idx == b
        for h in range(N_HEADS):
            accs[h] = jnp.where(hit, rb_ref[b, h] - rb_ref[N_BUCKETS - 1, h], accs[h])
    for h in range(N_HEADS):
        out_ref[0, h] = accs[h]


def _bias_tables(rel_bias, idx):
    nt = idx.shape[0]
    return pl.pallas_call(
        _bias_kernel,
        grid=(nt,),
        in_specs=[pl.BlockSpec(memory_space=pltpu.SMEM),
                  pl.BlockSpec((1, LANE, LANE), lambda t: (t, 0, 0))],
        out_specs=pl.BlockSpec((1, N_HEADS, LANE, LANE), lambda t: (t, 0, 0, 0)),
        out_shape=jax.ShapeDtypeStruct((nt, N_HEADS, LANE, LANE), F32),
        compiler_params=_cparams(1),
        name="bias_tables",
    )(rel_bias, idx)


_Q0, _Q1 = 0, N_HEADS * LANE
_KV0, _KV1 = _Q1, _Q1 + 512
_WN0, _WN1 = _KV1, _KV1 + 256
_GT0, _GT1 = _WN1, _WN1 + 2 * LANE
_CV0, _CV1 = _GT1, _GT1 + 3 * 512


def _pack_in_weights(w):
    d = w.shape[0]
    d_nsa = N_HEADS * HEAD_DIM
    wq = w[:, :d_nsa].reshape(d, N_HEADS, HEAD_DIM)
    wq_pad = jnp.pad(wq, ((0, 0), (0, 0), (0, LANE - HEAD_DIM))).reshape(d, N_HEADS * LANE)
    o = d_nsa
    w_kv = w[:, o:o + 512]
    w_win = w[:, o + 512:o + 768]
    wg = w[:, o + 768:o + 768 + 3 * N_HEADS].reshape(d, N_KV, 3 * GQ)
    wg_pad = jnp.pad(wg, ((0, 0), (0, 0), (0, LANE - 3 * GQ))).reshape(d, N_KV * LANE)
    w_conv = w[:, o + 768 + 3 * N_HEADS:]
    return jnp.concatenate([wq_pad, w_kv, w_win, wg_pad, w_conv], axis=1).astype(BF16)


def _rms(x, gain, n, eps=1e-6):
    ms = jnp.sum(x * x, axis=-1, keepdims=True) / n
    return x * lax.rsqrt(ms + eps) * gain


def _layernorm(y, gain, bias, eps=1e-5):
    mu = jnp.mean(y, axis=-1, keepdims=True)
    d = y - mu
    var = jnp.mean(d * d, axis=-1, keepdims=True)
    return d * lax.rsqrt(var + eps) * gain + bias


def _inproj_prompt_kernel(x_ref, w_ref, cw_ref, gc_ref,
                          q_ref, kv_ref, ks_ref, sv_ref, kw_ref, wv_ref, wtail_ref, gate_ref, zn_ref, ctail_ref,
                          uext_ref, *, tiles_per_batch):
    i = pl.program_id(0)
    tm = x_ref.shape[0]
    xb = x_ref[...].astype(BF16)

    def seg(a, b):
        return _mm(xb, w_ref[:, a:b])

    qp = seg(_Q0, _Q1) * (HEAD_DIM ** -0.5)
    for h in range(N_HEADS):
        q_ref[h] = qp[:, h * LANE:(h + 1) * LANE].astype(BF16)
    kv = seg(_KV0, _KV1)
    kv_ref[0] = kv.T
    win = seg(_WN0, _WN1)
    wtail_ref[0] = win.T
    pos = (i % tiles_per_batch) * tm + lax.broadcasted_iota(jnp.int32, (tm, LANE - HEAD_DIM), 0)
    lane = lax.broadcasted_iota(jnp.int32, (tm, LANE - HEAD_DIM), 1)
    blk_flag = jnp.where(lane == lax.shift_right_logical(pos, 6), KEY_MASK, 0.0).astype(BF16)
    no_flag = jnp.zeros((tm, LANE - HEAD_DIM), BF16)
    for g in range(N_KV):
        ks_ref[g] = jnp.concatenate([kv[:, 256 + g * HEAD_DIM:256 + (g + 1) * HEAD_DIM].astype(BF16), blk_flag], axis=1)
        kw_ref[g] = jnp.concatenate([win[:, g * HEAD_DIM:(g + 1) * HEAD_DIM].astype(BF16), no_flag], axis=1)
    sv_ref[...] = kv[:, 384:512].astype(BF16)
    wv_ref[...] = win[:, LANE:2 * LANE].astype(BF16)
    gt = jax.nn.sigmoid(seg(_GT0, _GT1))
    gate_ref[0] = gt[:, 0:LANE]
    gate_ref[1] = gt[:, LANE:2 * LANE]

    conv = seg(_CV0, _CV1)
    cb = conv[:, 0:512]
    u = conv[:, 512:1024] * conv[:, 1024:1536]
    first = (i % tiles_per_batch) == 0

    @pl.when(first)
    def _():
        uext_ref[0:SUBLANE, :] = jnp.zeros((SUBLANE, 512), F32)

    @pl.when(jnp.logical_not(first))
    def _():
        uext_ref[0:SUBLANE, :] = uext_ref[tm:tm + SUBLANE, :]

    uext_ref[SUBLANE:tm + SUBLANE, :] = u
    um1 = uext_ref[SUBLANE - 1:tm + SUBLANE - 1, :]
    um2 = uext_ref[SUBLANE - 2:tm + SUBLANE - 2, :]
    z = cb * (um2 * cw_ref[0:1, :] + um1 * cw_ref[1:2, :] + u * cw_ref[2:3, :])
    zn_ref[...] = _rms(z, gc_ref[...], 512).astype(BF16)
    ctail_ref[0] = uext_ref[tm + SUBLANE - 2:tm + SUBLANE, :]


def _inproj_prompt(x, w, conv_w, g_conv, batch, seq):
    t, d = x.shape
    n_tiles = t // TM
    tpb = seq // TM
    outs = (
        jax.ShapeDtypeStruct((N_HEADS, t, LANE), BF16),
        jax.ShapeDtypeStruct((batch, 512, seq), F32),
        jax.ShapeDtypeStruct((N_KV, t, LANE), BF16),
        jax.ShapeDtypeStruct((t, LANE), BF16),
        jax.ShapeDtypeStruct((N_KV, t, LANE), BF16),
        jax.ShapeDtypeStruct((t, LANE), BF16),
        jax.ShapeDtypeStruct((batch, 256, TM), F32),
        jax.ShapeDtypeStruct((N_KV, t, LANE), F32),
        jax.ShapeDtypeStruct((t, 512), BF16),
        jax.ShapeDtypeStruct((batch, CONV_W - 1, 512), F32),
    )
    row = lambda n: pl.BlockSpec((TM, n), lambda i: (i, 0))
    planes = lambda n: pl.BlockSpec((n, TM, LANE), lambda i: (0, i, 0))
    return pl.pallas_call(
        functools.partial(_inproj_prompt_kernel, tiles_per_batch=tpb),
        grid=(n_tiles,),
        in_specs=[row(d), _const_spec(w.shape), _const_spec(conv_w.shape), _const_spec(g_conv.shape)],
        out_specs=(
            planes(N_HEADS),
            pl.BlockSpec((1, 512, TM), lambda i: (i // tpb, 0, i % tpb)),
            planes(N_KV), row(LANE), planes(N_KV), row(LANE),
            pl.BlockSpec((1, 256, TM), lambda i: (i // tpb, 0, 0)),
            planes(N_KV),
            row(512),
            pl.BlockSpec((1, CONV_W - 1, 512), lambda i: (i // tpb, 0, 0)),
        ),
        out_shape=outs,
        scratch_shapes=[pltpu.VMEM((TM + SUBLANE, 512), F32)],
        compiler_params=_cparams(1),
        name="inproj_prompt",
    )(x, w, conv_w, g_conv)


def _inproj_sample_kernel(x_ref, w_ref, cw_ref, gc_ref, s0_ref, s1_ref,
                          qlo_ref, qgl_ref, kv_ref, win_ref, gate_ref, zn_ref, u_ref):
    xb = x_ref[...].astype(BF16)

    def seg(a, b):
        return _mm(xb, w_ref[:, a:b])

    qp = seg(_Q0, _Q1) * (HEAD_DIM ** -0.5)
    qlo_ref[...] = qp.astype(BF16)
    blocks = [qp[:, h * LANE:(h + 1) * LANE] for h in range(N_HEADS)]
    blocks = [b if h < GQ else pltpu.roll(b, HEAD_DIM, axis=1) for h, b in enumerate(blocks)]
    qgl_ref[...] = jnp.concatenate(blocks, axis=1).astype(BF16)
    kv_ref[...] = seg(_KV0, _KV1)
    win_ref[...] = seg(_WN0, _WN1)
    gate_ref[...] = jax.nn.sigmoid(seg(_GT0, _GT1))
    conv = seg(_CV0, _CV1)
    cb = conv[:, 0:512]
    u = conv[:, 512:1024] * conv[:, 1024:1536]
    z = cb * (s0_ref[...] * cw_ref[0:1, :] + s1_ref[...] * cw_ref[1:2, :] + u * cw_ref[2:3, :])
    zn_ref[...] = _rms(z, gc_ref[...], 512).astype(BF16)
    u_ref[...] = u


def _inproj_sample(x, w, conv_w, g_conv, s0, s1):
    n = x.shape[0]
    outs = (
        jax.ShapeDtypeStruct((n, N_HEADS * LANE), BF16),
        jax.ShapeDtypeStruct((n, N_HEADS * LANE), BF16),
        jax.ShapeDtypeStruct((n, 512), F32),
        jax.ShapeDtypeStruct((n, 256), F32),
        jax.ShapeDtypeStruct((n, N_KV * LANE), F32),
        jax.ShapeDtypeStruct((n, 512), BF16),
        jax.ShapeDtypeStruct((n, 512), F32),
    )
    args = (x, w, conv_w, g_conv, s0, s1)
    return pl.pallas_call(
        _inproj_sample_kernel,
        grid=(1,),
        in_specs=[_const_spec(a.shape) for a in args],
        out_specs=tuple(_const_spec(o.shape) for o in outs),
        out_shape=outs,
        compiler_params=_cparams(1),
        name="inproj_sample",
    )(*args)


def _pack_compress_weights(pe, w1, w2, per_group_out):
    w = w1.reshape(2, CMP_STRIDE, HEAD_DIM, CMP_HIDDEN).transpose(1, 2, 0, 3).reshape(CMP_STRIDE, HEAD_DIM, 2 * CMP_HIDDEN)
    z = jnp.zeros_like(w)
    top = jnp.concatenate([w, z], axis=-1)
    bot = jnp.concatenate([z, w], axis=-1)
    w1bd = jnp.stack([top, bot], axis=1).reshape(CMP_STRIDE * 2 * HEAD_DIM, 4 * CMP_HIDDEN).astype(BF16)
    z2 = jnp.zeros_like(w2)
    if per_group_out:
        w2bd = jnp.concatenate([jnp.concatenate([w2, z2, z2, z2], axis=1),
                                jnp.concatenate([z2, z2, w2, z2], axis=1)], axis=0).astype(BF16)
    else:
        w2bd = jnp.concatenate([jnp.concatenate([w2, z2], axis=1), jnp.concatenate([z2, w2], axis=1)], axis=0).astype(BF16)
    pe_rows = jnp.broadcast_to(pe.reshape(2, CMP_STRIDE, 1, HEAD_DIM), (2, CMP_STRIDE, N_KV, HEAD_DIM))
    pe_rows = pe_rows.reshape(2, CMP_STRIDE * N_KV * HEAD_DIM)
    pe_rows = jnp.pad(pe_rows, ((0, SUBLANE - 2), (0, 0)))
    return w1bd, w2bd, pe_rows


def _compress_kernel(*refs, n_pages, n_prefetch):
    refs = refs[n_prefetch:]
    page_refs = refs[:n_pages]
    w1_ref, w2k_ref, w2v_ref, pe_ref, out_ref, carry_ref, tok_ref = refs[n_pages:]
    step = pl.program_id(1)
    m = n_pages * (PAGE_SIZE // CMP_STRIDE)

    @pl.when(step == 0)
    def _():
        carry_ref[...] = jnp.zeros(carry_ref.shape, F32)

    row0 = lax.broadcasted_iota(jnp.int32, (m, CMP_HIDDEN), 0) == 0
    for t, w2_ref in enumerate((w2k_ref, w2v_ref)):
        for p, pr in enumerate(page_refs):
            slab = pr[0, t].reshape(N_KV * HEAD_DIM, PAGE_SIZE)
            tok_ref[p * PAGE_SIZE:(p + 1) * PAGE_SIZE, :] = slab.T
        pieces = [tok_ref[pl.ds(j, m, stride=CMP_STRIDE), :] for j in range(CMP_STRIDE)]
        lhs = jnp.concatenate([jnp.concatenate(pieces, axis=1), pe_ref[t]], axis=0).astype(BF16)
        a = _mm(lhs, w1_ref[t])
        hs = []
        for g in range(N_KV):
            c0 = g * 2 * CMP_HIDDEN
            a0 = a[0:m, c0:c0 + CMP_HIDDEN]
            a1 = a[0:m, c0 + CMP_HIDDEN:c0 + 2 * CMP_HIDDEN]
            pe_term = a[m:m + 1, c0:c0 + CMP_HIDDEN] + a[m + 1:m + 2, c0 + CMP_HIDDEN:c0 + 2 * CMP_HIDDEN]
            prev = carry_ref[t, 0:1, g * CMP_HIDDEN:(g + 1) * CMP_HIDDEN]
            shifted = jnp.where(row0, prev, pltpu.roll(a0, 1, axis=0))
            carry_ref[t, 0:1, g * CMP_HIDDEN:(g + 1) * CMP_HIDDEN] = a0[m - 1:m, :]
            hs.append(jax.nn.gelu(shifted + a1 + pe_term))
        out = _mm(jnp.concatenate(hs, axis=1).astype(BF16), w2_ref[...]).astype(BF16)
        if t == 0:
            out_ref[0, 0] = out[:, 0:LANE]
            out_ref[0, 1] = out[:, LANE:2 * LANE]
        else:
            out_ref[0, 2] = out


def _compress_call(page_specs, page_args, prefetch, grid, batch, n_chunks, m, w1, w2k, w2v, pe, name):
    n_pages = len(page_specs)
    n_pf = len(prefetch)
    cspec = lambda shape: pl.BlockSpec(shape, lambda *_: (0,) * len(shape))
    grid_spec = pltpu.PrefetchScalarGridSpec(
        num_scalar_prefetch=n_pf,
        grid=grid,
        in_specs=list(page_specs) + [cspec(w1.shape), cspec(w2k.shape), cspec(w2v.shape), cspec(pe.shape)],
        out_specs=pl.BlockSpec((1, 3, m, LANE), lambda b, s, *_: (b, 0, s, 0)),
        scratch_shapes=[pltpu.VMEM((2, SUBLANE, 2 * CMP_HIDDEN), F32),
                        pltpu.VMEM((n_pages * PAGE_SIZE, N_KV * HEAD_DIM), F32)],
    )
    return pl.pallas_call(
        functools.partial(_compress_kernel, n_pages=n_pages, n_prefetch=n_pf),
        grid_spec=grid_spec,
        out_shape=jax.ShapeDtypeStruct((batch, 3, n_chunks, LANE), BF16),
        compiler_params=_cparams(2),
        name=name,
    )(*prefetch, *page_args, w1, w2k, w2v, pe)


_PAGE_BLOCK = (1, 2, N_KV, HEAD_DIM, PAGE_SIZE)


def _compress_prompt(kv_t, batch, seq, w1, w2k, w2v, pe):
    n_pages = seq // PAGE_SIZE
    rows = PAGE_SIZE // CMP_STRIDE
    specs = [pl.BlockSpec(_PAGE_BLOCK, lambda b, s, p=p: (b, 0, 0, 0, p)) for p in range(n_pages)]
    return _compress_call(specs, [kv_t] * n_pages, (), (batch, 1), batch, n_pages * rows, n_pages * rows,
                          w1, w2k, w2v, pe, "compress_prompt")


def _compress_sample(cache_t, page_table, w1, w2k, w2v, pe):
    batch, n_pages = page_table.shape
    rows = PAGE_SIZE // CMP_STRIDE
    specs = [pl.BlockSpec(_PAGE_BLOCK, lambda b, s, pt, k=k: (pt[b, s * SAMPLE_PAGES + k], 0, 0, 0, 0))
             for k in range(SAMPLE_PAGES)]
    return _compress_call(specs, [cache_t] * SAMPLE_PAGES, (page_table,), (batch, n_pages // SAMPLE_PAGES), batch,
                          n_pages * rows, SAMPLE_PAGES * rows, w1, w2k, w2v, pe, "compress_sample")


def _tile4(x):
    return jnp.concatenate([x] * GQ, axis=0)


def _softmax_fold(state, pieces, v):
    m, l, acc = state
    m_new = m
    for pc in pieces:
        m_new = jnp.maximum(m_new, jnp.max(pc, axis=1, keepdims=True))
    alpha = jnp.exp(m - m_new)
    es = [jnp.exp(pc - m_new) for pc in pieces]
    l = alpha * l
    for e in es:
        l = l + jnp.sum(e, axis=1, keepdims=True)
    e_all = es[0] if len(es) == 1 else jnp.concatenate(es, axis=1)
    acc = alpha * acc + _mm(e_all.astype(BF16), v)
    return m_new, l, acc


def _attn_prompt_kernel(q_ref, gate_ref, cmp_ref, ks_ref, sv_ref, kw_ref, wv_ref,
                        tabt_ref, tabc_ref, sp_ref, gn_ref, o_ref, *, n_slc):
    qi = pl.program_id(1)
    q0 = pl.multiple_of(qi * TQ, TQ)
    row = lax.broadcasted_iota(jnp.int32, (TQ, LANE), 0)
    col = lax.broadcasted_iota(jnp.int32, (TQ, LANE), 1)
    cmp_ok4 = _tile4((col >= 1) & (CMP_STRIDE * col + (CMP_STRIDE - 1) <= q0 + row))
    sidx = lax.broadcasted_iota(jnp.int32, (n_slc, TQ), 0)
    qblk = lax.shift_right_logical(q0 + lax.broadcasted_iota(jnp.int32, (n_slc, TQ), 1), 6)
    sel_valid = sidx <= qblk
    sel_forced = (sidx == 0) | (sidx == qblk) | (sidx == qblk - 1)
    lane_row = lax.broadcasted_iota(jnp.int32, (LANE, TQ), 0)
    n_far = lax.shift_right_logical(jnp.maximum(qi - 1, 0), 2)
    up4 = _tile4(tabt_ref[0, :, 2 * LANE:3 * LANE])
    fresh = (jnp.full((GQ * TQ, 1), NEG, F32), jnp.zeros((GQ * TQ, 1), F32), jnp.zeros((GQ * TQ, LANE), F32))

    groups = range(N_KV)
    qa, near, o_cmp = [], [], []
    for g in groups:
        heads = [GQ * g + r for r in range(GQ)]
        qs = jnp.concatenate([q_ref[h] for h in heads], axis=0)
        near.append(jnp.concatenate([tabt_ref[h, :, 0:2 * LANE] for h in heads], axis=0))

        bias_c = jnp.concatenate([tabc_ref[0, h] for h in heads], axis=0)
        s = jnp.where(cmp_ok4, _qk(qs, cmp_ref[0, g]) + bias_c, NEG)
        e = jnp.exp(s - jnp.max(s, axis=1, keepdims=True))
        p = (e / jnp.sum(e, axis=1, keepdims=True) * cmp_ok4.astype(F32)).astype(BF16)
        o_cmp.append(_mm(p, cmp_ref[0, 2]))
        imp4 = _mm(p, sp_ref[...])
        imp = imp4[0:TQ] + imp4[TQ:2 * TQ] + imp4[2 * TQ:3 * TQ] + imp4[3 * TQ:4 * TQ]

        score = jnp.where(sel_valid, imp.T[0:n_slc, :] + jnp.where(sel_forced, FORCE, 0.0), -FORCE)
        rank = jnp.zeros((n_slc, TQ), F32)
        for k in range(1, n_slc):
            other = pltpu.roll(score, k, axis=0)
            rank = rank + jnp.where(sidx >= k, (other >= score).astype(F32), (other > score).astype(F32))
        sel_t = (rank < float(N_SELECT)).astype(F32)
        aug_t = jnp.concatenate([jnp.zeros((SEL_LANE0, TQ), F32), sel_t - 1.0,
                                 jnp.zeros((LANE - SEL_LANE0 - n_slc, TQ), F32)], axis=0)
        aug_t = jnp.where(lane_row == PAD_LANE, -1.0, aug_t)
        qa.append(qs + _tile4(aug_t.T.astype(BF16)))

    def far_tile(it, states):
        start = pl.multiple_of(q0 - FAR * (n_far - it), LANE)
        v = sv_ref[0, pl.ds(start, FAR), :]
        return tuple(_softmax_fold(states[g], [_qk(qa[g], ks_ref[g, 0, pl.ds(start, FAR), :])], v) for g in groups)

    states = lax.fori_loop(0, n_far, far_tile, (fresh,) * N_KV)
    v_tail = sv_ref[0, pl.ds(q0, TAIL), :]
    w_tail = wv_ref[0, pl.ds(q0, TAIL), :]
    o_slc, o_win = [], []
    for g in groups:
        sc = _qk(qa[g], ks_ref[g, 0, pl.ds(q0, TAIL), :])
        pieces = [sc[:, 0:TAIL - 2 * LANE], sc[:, TAIL - 2 * LANE:TAIL] + near[g]]
        _, l, acc = _softmax_fold(states[g], pieces, v_tail)
        o_slc.append(acc / l)
        sc = _qk(qa[g], kw_ref[g, 0, pl.ds(q0, TAIL), :])
        pieces = [sc[:, 0:LANE] + up4, sc[:, LANE:TAIL - 2 * LANE], sc[:, TAIL - 2 * LANE:TAIL] + near[g]]
        _, l, acc = _softmax_fold(fresh, pieces, w_tail)
        o_win.append(acc / l)

    outs = []
    for g in groups:
        gt = gate_ref[g]
        own = lax.shift_right_logical(col, 6) == g
        for r in range(GQ):
            rs = slice(r * TQ, (r + 1) * TQ)
            comb = gt[:, 3 * r:3 * r + 1] * o_cmp[g][rs] + gt[:, 3 * r + 1:3 * r + 2] * o_slc[g][rs] \
                + gt[:, 3 * r + 2:3 * r + 3] * o_win[g][rs]
            outs.append(jnp.where(own, comb, 0.0))
    o = jnp.concatenate(outs, axis=1)
    o_ref[...] = _rms(o, gn_ref[...], N_HEADS * HEAD_DIM).astype(BF16)


def _attn_prompt(q, gates, cmpkv, ks, sv, kw, wv, tab_t, tab_c, sp, gn_pad, batch, seq):
    nq = seq // TQ
    t = batch * seq
    rows = ks.shape[2]
    n_slc = seq // SLC_LEN
    assert WINDOW >= FAR - LANE and SEL_LANE0 + n_slc <= PAD_LANE < LANE
    return pl.pallas_call(
        functools.partial(_attn_prompt_kernel, n_slc=n_slc),
        grid=(batch, nq),
        in_specs=[
            pl.BlockSpec((N_HEADS, TQ, LANE), lambda b, i: (0, b * nq + i, 0)),
            pl.BlockSpec((N_KV, TQ, LANE), lambda b, i: (0, b * nq + i, 0)),
            pl.BlockSpec((1, 3, seq // CMP_STRIDE, LANE), lambda b, i: (b, 0, 0, 0)),
            pl.BlockSpec((N_KV, 1, rows, LANE), lambda b, i: (0, b, 0, 0)),
            pl.BlockSpec((1, rows, LANE), lambda b, i: (b, 0, 0)),
            pl.BlockSpec((N_KV, 1, rows, LANE), lambda b, i: (0, b, 0, 0)),
            pl.BlockSpec((1, rows, LANE), lambda b, i: (b, 0, 0)),
            _const_spec(tab_t.shape),
            pl.BlockSpec((1, N_HEADS, LANE, LANE), lambda b, i: (i, 0, 0, 0)),
            _const_spec(sp.shape),
            _const_spec(gn_pad.shape),
        ],
        out_specs=pl.BlockSpec((TQ, N_HEADS * LANE), lambda b, i: (b * nq + i, 0)),
        out_shape=jax.ShapeDtypeStruct((t, N_HEADS * LANE), BF16),
        compiler_params=_cparams(2),
        name="attn_prompt",
    )(q, gates, cmpkv, ks, sv, kw, wv, tab_t, tab_c, sp, gn_pad)


def _sample_cw_kernel(qlo_ref, qgl_ref, cmp_ref, cwin_ref, wnew_ref, tabc_ref, tabw_ref, rb0_ref, sp_ref,
                      ocmp_ref, owin_ref, idx_ref, *, n_slc, qblk, win_buf):
    qlo = qlo_ref[0]
    qgl = qgl_ref[0]
    n_cmp_rows = cmp_ref.shape[2]
    col = lax.broadcasted_iota(jnp.int32, (N_HEADS, n_cmp_rows), 1)
    grp_of_head = lax.shift_right_logical(lax.broadcasted_iota(jnp.int32, (N_HEADS, n_cmp_rows), 0), 2)
    ok = col >= 1
    raw = jnp.where(grp_of_head == 0, _qk(qlo, cmp_ref[0, 0]), _qk(qlo, cmp_ref[0, 1]))
    s = jnp.where(ok, raw + tabc_ref[...], NEG)
    e = jnp.exp(s - jnp.max(s, axis=1, keepdims=True))
    p = (e / jnp.sum(e, axis=1, keepdims=True) * ok.astype(F32)).astype(BF16)
    ocmp_ref[0] = _mm(p, cmp_ref[0, 2])
    imp8 = _mm(p, sp_ref[...])

    width = imp8.shape[1]
    head_grp = lax.shift_right_logical(lax.broadcasted_iota(jnp.int32, (N_HEADS, width), 0), 2)
    sidx = lax.broadcasted_iota(jnp.int32, (1, width), 1)
    sidx_f = sidx.astype(F32)
    forced = (sidx == 0) | (sidx == qblk) | (sidx == qblk - 1)
    lane = lax.broadcasted_iota(jnp.int32, (1, LANE), 1)
    for g in range(N_KV):
        imp = jnp.sum(jnp.where(head_grp == g, imp8, 0.0), axis=0, keepdims=True)
        score = jnp.where(sidx <= qblk, imp + jnp.where(forced, FORCE, 0.0), -FORCE)
        score = jnp.where(sidx < n_slc, score, -jnp.inf)
        picked = jnp.zeros((1, LANE), F32)
        for k in range(N_SELECT):
            best = jnp.max(score, axis=1, keepdims=True)
            ik = jnp.min(jnp.where(score == best, sidx_f, float(width)), axis=1, keepdims=True)
            picked = jnp.where(lane == k, ik, picked)
            score = jnp.where(sidx_f == ik, -jnp.inf, score)
        idx_ref[0, g:g + 1, :] = picked.astype(jnp.int32)

    kt = cwin_ref[0, 0].reshape(N_KV * HEAD_DIM, win_buf).astype(BF16)
    vt = cwin_ref[0, 1].reshape(N_KV * HEAD_DIM, win_buf).astype(BF16)
    knew = wnew_ref[0, :, 0:LANE].astype(BF16).astype(F32)
    vnew = wnew_ref[0, :, LANE:2 * LANE].astype(BF16).astype(F32)
    colw = lax.broadcasted_iota(jnp.int32, (N_HEADS, win_buf), 1)
    okw = colw > win_buf - WINDOW
    sw = jnp.where(okw, _mm(qgl, kt) + tabw_ref[...], NEG)
    s_new = jnp.sum(qgl.astype(F32) * knew, axis=1, keepdims=True) + rb0_ref[:, 0:1]
    mw = jnp.maximum(jnp.max(sw, axis=1, keepdims=True), s_new)
    ew = jnp.where(okw, jnp.exp(sw - mw), 0.0)
    e_new = jnp.exp(s_new - mw)
    lw = jnp.sum(ew, axis=1, keepdims=True) + e_new
    owin_ref[0] = (_qk(ew.astype(BF16), vt) + e_new.astype(BF16).astype(F32) * vnew) / lw


def _sample_cw(qlo, qgl, cmpkv, cwin, wnew, tab_c, tab_w, rb0, sp, n_slc, qblk):
    n = qlo.shape[0]
    win_buf = cwin.shape[-1]
    blk = lambda a: pl.BlockSpec((1,) + a.shape[1:], lambda b: (b,) + (0,) * (a.ndim - 1))
    outs = (jax.ShapeDtypeStruct((n, N_HEADS, LANE), F32), jax.ShapeDtypeStruct((n, N_HEADS, LANE), F32),
            jax.ShapeDtypeStruct((n, N_KV, LANE), jnp.int32))
    return pl.pallas_call(
        functools.partial(_sample_cw_kernel, n_slc=n_slc, qblk=qblk, win_buf=win_buf),
        grid=(n,),
        in_specs=[blk(qlo), blk(qgl), blk(cmpkv), blk(cwin), blk(wnew), _const_spec(tab_c.shape),
                  _const_spec(tab_w.shape), _const_spec(rb0.shape), _const_spec(sp.shape)],
        out_specs=tuple(pl.BlockSpec((1,) + o.shape[1:], lambda b: (b, 0, 0)) for o in outs),
        out_shape=outs,
        compiler_params=_cparams(1),
        name="sample_cmp_win",
    )(qlo, qgl, cmpkv, cwin, wnew, tab_c, tab_w, rb0, sp)


def _sample_slc_kernel(pg_ref, hf_ref, bid_ref, *refs, new_block, near_block):
    n_blk = N_KV * N_SELECT
    blocks = refs[:n_blk]
    (q_ref, kvnew_ref, ocmp_ref, owin_ref, gate_ref, tnear_ref, tlast_ref, rb0_ref, gn_ref, o_ref) = refs[n_blk:]
    b = pl.program_id(0)
    n_keys = N_SELECT * PAGE_SIZE
    lane = lax.broadcasted_iota(jnp.int32, (1, n_keys), 1)
    slot = lax.shift_right_logical(lane, 7)
    half = lax.shift_right_logical(lane, 6) & 1
    head_grp = lax.shift_right_logical(lax.broadcasted_iota(jnp.int32, (N_HEADS, LANE), 0), 2)
    lane_grp = lax.shift_right_logical(lax.broadcasted_iota(jnp.int32, (N_HEADS, LANE), 1), 6)
    knew = kvnew_ref[0, :, 256:384].astype(BF16).astype(F32)
    vnew = kvnew_ref[0, :, 384:512].astype(BF16).astype(F32)
    qs = q_ref[0]
    o_slc = jnp.zeros((N_HEADS, LANE), F32)
    for g in range(N_KV):
        mine = blocks[g * N_SELECT:(g + 1) * N_SELECT]
        kt = jnp.concatenate([blk[0, 0].reshape(N_KV * HEAD_DIM, PAGE_SIZE) for blk in mine], axis=1).astype(BF16)
        vt = jnp.concatenate([blk[0, 1].reshape(N_KV * HEAD_DIM, PAGE_SIZE) for blk in mine], axis=1).astype(BF16)
        bid = jnp.zeros((1, n_keys), jnp.int32)
        hsel = jnp.zeros((1, n_keys), jnp.int32)
        has_new = bid_ref[b, g * N_SELECT] == new_block
        for kk in range(N_SELECT):
            bid = jnp.where(slot == kk, bid_ref[b, g * N_SELECT + kk], bid)
            hsel = jnp.where(slot == kk, hf_ref[b, g * N_SELECT + kk], hsel)
            if kk:
                has_new = has_new | (bid_ref[b, g * N_SELECT + kk] == new_block)
        bias = jnp.where(bid == near_block + 1, tlast_ref[...], jnp.where(bid == near_block, tnear_ref[...], 0.0))
        ok = (bid != new_block) & (half == hsel)
        s = jnp.where(ok, _mm(qs, kt) + bias, NEG)
        s_new = jnp.sum(qs.astype(F32) * knew, axis=1, keepdims=True) + rb0_ref[:, 0:1]
        s_new = jnp.where(has_new, s_new, NEG)
        m = jnp.maximum(jnp.max(s, axis=1, keepdims=True), s_new)
        e = jnp.where(ok, jnp.exp(s - m), 0.0)
        e_new = jnp.where(has_new, jnp.exp(s_new - m), 0.0)
        l = jnp.sum(e, axis=1, keepdims=True) + e_new
        og = (_qk(e.astype(BF16), vt) + e_new.astype(BF16).astype(F32) * vnew) / l
        o_slc = jnp.where(head_grp == g, og, o_slc)
    gt = gate_ref[0]
    comb = gt[:, 0:1] * ocmp_ref[0] + gt[:, 1:2] * o_slc + gt[:, 2:3] * owin_ref[0]
    comb = jnp.where(lane_grp == head_grp, comb, 0.0)
    o = jnp.concatenate([comb[h:h + 1, :] for h in range(N_HEADS)], axis=1)
    o_ref[0] = _rms(o, gn_ref[...], N_HEADS * HEAD_DIM).astype(BF16)


def _sample_slc(pages, halves, bids, cache_t, q, kvnew, ocmp, owin, gates, t_near, t_last, rb0, gn_pad,
                new_block, near_block):
    n = q.shape[0]
    n_blk = N_KV * N_SELECT
    blk_specs = [pl.BlockSpec(_PAGE_BLOCK, lambda b, pg, hf, bi, k=k: (pg[b, k], 1, 0, 0, 0))
                 for k in range(n_blk)]
    per_b = lambda a: pl.BlockSpec((1,) + a.shape[1:], lambda b, *_: (b,) + (0,) * (a.ndim - 1))
    cst = lambda a: pl.BlockSpec(a.shape, lambda *_: (0,) * a.ndim)
    grid_spec = pltpu.PrefetchScalarGridSpec(
        num_scalar_prefetch=3,
        grid=(n,),
        in_specs=blk_specs + [per_b(q), per_b(kvnew), per_b(ocmp), per_b(owin), per_b(gates),
                              cst(t_near), cst(t_last), cst(rb0), cst(gn_pad)],
        out_specs=pl.BlockSpec((1, 1, N_HEADS * LANE), lambda b, *_: (b, 0, 0)),
    )
    return pl.pallas_call(
        functools.partial(_sample_slc_kernel, new_block=new_block, near_block=near_block),
        grid_spec=grid_spec,
        out_shape=jax.ShapeDtypeStruct((n, 1, N_HEADS * LANE), BF16),
        compiler_params=_cparams(1),
        name="sample_selected",
    )(pages, halves, bids, *([cache_t] * n_blk), q, kvnew, ocmp, owin, gates, t_near, t_last, rb0, gn_pad)


def _proj_kernel(on_ref, zn_ref, x_ref, wa_ref, wb_ref, g1_ref, b1_ref, wr_ref, br_ref,
                 x1_ref, gate_ref, *, alpha):
    mix = _mm(on_ref[...], wa_ref[...]) + _mm(zn_ref[...], wb_ref[...])
    x1 = _layernorm(alpha * x_ref[...] + mix, g1_ref[...], b1_ref[...])
    x1_ref[...] = x1
    logits = _mm(x1.astype(BF16), wr_ref[...]) + br_ref[...]
    tm = logits.shape[0]
    lane = lax.broadcasted_iota(jnp.int32, (tm, LANE), 1).astype(F32)

    def first_argmax(v, vmax):
        return jnp.min(jnp.where(v == vmax, lane, float(LANE)), axis=1, keepdims=True)

    lg = jnp.where(lane < N_GROUPS, logits[:, 0:LANE], -jnp.inf)
    lg_max = jnp.max(lg, axis=1, keepdims=True)
    eg = jnp.exp(lg - lg_max)
    pg = eg / jnp.sum(eg, axis=1, keepdims=True)
    gidx = first_argmax(lg, lg_max)
    pg_sel = jnp.sum(jnp.where(lane == gidx, pg, 0.0), axis=1, keepdims=True)
    le = jnp.zeros((tm, LANE), F32)
    for gi in range(N_GROUPS):
        le = le + jnp.where(gidx == gi, logits[:, (gi + 1) * LANE:(gi + 2) * LANE], 0.0)
    le = jnp.where(lane < N_EXP, le, -jnp.inf)
    ee = jnp.exp(le - jnp.max(le, axis=1, keepdims=True))
    pe = jnp.where(lane < N_EXP, ee / jnp.sum(ee, axis=1, keepdims=True), -1.0)
    v1 = jnp.max(pe, axis=1, keepdims=True)
    i1 = first_argmax(pe, v1)
    pe2 = jnp.where(lane == i1, -1.0, pe)
    v2 = jnp.max(pe2, axis=1, keepdims=True)
    i2 = first_argmax(pe2, v2)
    tot = v1 + v2
    gate_e = jnp.where(lane == i1, v1 / tot * pg_sel, jnp.where(lane == i2, v2 / tot * pg_sel, 0.0))
    gate_ref[...] = jnp.where(lane == GROUP_LANE, gidx, gate_e)


def _proj(on, zn, x, wa, wb, g1, b1, wr, br, alpha, tm):
    t, d = x.shape
    row = lambda n: pl.BlockSpec((tm, n), lambda i: (i, 0))
    outs = (jax.ShapeDtypeStruct((t, d), F32), jax.ShapeDtypeStruct((t, LANE), F32))
    return pl.pallas_call(
        functools.partial(_proj_kernel, alpha=alpha),
        grid=(t // tm,),
        in_specs=[row(on.shape[1]), row(zn.shape[1]), row(d)] + [_const_spec(a.shape) for a in (wa, wb, g1, b1, wr, br)],
        out_specs=(row(d), row(LANE)),
        out_shape=outs,
        compiler_params=_cparams(1),
        name=f"proj_ln_router_{tm}",
    )(on, zn, x, wa, wb, g1, b1, wr, br)


def _moe_kernel(x1_ref, gate_ref, wgu_ref, wd_ref, g2_ref, b2_ref, out_ref, acc_ref, *, alpha, d_ff):
    gi = pl.program_id(1)

    @pl.when(gi == 0)
    def _():
        acc_ref[...] = jnp.zeros(acc_ref.shape, F32)

    x = x1_ref[...].astype(BF16)
    gate = gate_ref[...]
    gt = jnp.where(gate[:, GROUP_LANE:GROUP_LANE + 1] == gi.astype(F32), gate, 0.0)
    acc = acc_ref[...]
    for e in range(N_EXP):
        au = _mm(x, wgu_ref[0, e])
        h = jax.nn.silu(au[:, 0:d_ff]) * au[:, d_ff:2 * d_ff]
        acc = acc + _mm((h * gt[:, e:e + 1]).astype(BF16), wd_ref[0, e])
    acc_ref[...] = acc

    @pl.when(gi == N_GROUPS - 1)
    def _():
        out_ref[...] = _layernorm(alpha * x1_ref[...] + acc_ref[...], g2_ref[...], b2_ref[...])


def _moe(x1, gate, wgu, wd, g2, b2, alpha, tm):
    t, d = x1.shape
    d_ff = wd.shape[2]
    return pl.pallas_call(
        functools.partial(_moe_kernel, alpha=alpha, d_ff=d_ff),
        grid=(t // tm, N_GROUPS),
        in_specs=[
            pl.BlockSpec((tm, d), lambda i, g: (i, 0)),
            pl.BlockSpec((tm, LANE), lambda i, g: (i, 0)),
            pl.BlockSpec((1, N_EXP, d, 2 * d_ff), lambda i, g: (g, 0, 0, 0)),
            pl.BlockSpec((1, N_EXP, d_ff, d), lambda i, g: (g, 0, 0, 0)),
            _const_spec(g2.shape), _const_spec(b2.shape),
        ],
        out_specs=pl.BlockSpec((tm, d), lambda i, g: (i, 0)),
        out_shape=jax.ShapeDtypeStruct((t, d), F32),
        scratch_shapes=[pltpu.VMEM((tm, d), F32)],
        compiler_params=_cparams(2),
        name=f"moe_ln_{tm}",
    )(x1, gate, wgu, wd, g2, b2)


def _split_bf16(x):
    hi = x.astype(BF16)
    return hi, (x - hi.astype(F32)).astype(BF16)


def _moe_sorted_kernel(cnt_ref, off_ref, x1_ref, gate_ref, drow_ref, dcol_ref, wgu_ref, wd_ref, g2_ref, b2_ref,
                       out_ref, xs_ref, gs_ref, ys_ref, *, alpha, d_ff):
    i = pl.program_id(0)
    gi = pl.program_id(1)
    half = pl.program_id(2)
    n_half = N_EXP // EXPERT_SPLIT
    tmoe = x1_ref.shape[0]

    @pl.when((i == 0) & (gi == 0) & (half == 0))
    def _():
        xs_ref[...] = jnp.zeros(xs_ref.shape, BF16)
        gs_ref[...] = jnp.zeros(gs_ref.shape, F32)
        ys_ref[...] = jnp.zeros(ys_ref.shape, F32)

    @pl.when((gi == 0) & (half == 0))
    def _():
        perm = (lax.broadcasted_iota(jnp.int32, (PERM_ROWS, tmoe), 0) == drow_ref[0]).astype(BF16)
        xs_ref[0:PERM_ROWS, :] = _mm(perm, x1_ref[...].astype(BF16)).astype(BF16)
        g_hi, g_lo = _split_bf16(gate_ref[...])
        gs_ref[0:PERM_ROWS, :] = _mm(perm, g_hi) + _mm(perm, g_lo)

    n = cnt_ref[i * N_GROUPS + gi]
    off = off_ref[i * N_GROUPS + gi]

    def run_pass(c, carry):
        base = pl.multiple_of(off + c * MOE_CHUNK, SEG_ALIGN)
        xc = xs_ref[pl.ds(base, MOE_CHUNK), :]
        gc = gs_ref[pl.ds(base, MOE_CHUNK), :]
        gc = jnp.where(half == 0, gc, pltpu.roll(gc, LANE - n_half, axis=1))
        y = jnp.zeros((MOE_CHUNK, out_ref.shape[1]), F32)
        for e in range(n_half):
            au = _mm(xc, wgu_ref[0, e])
            h = jax.nn.silu(au[:, 0:d_ff]) * au[:, d_ff:2 * d_ff]
            y = y + _mm((h * gc[:, e:e + 1]).astype(BF16), wd_ref[0, e])

        @pl.when(half == 0)
        def _():
            ys_ref[pl.ds(base, MOE_CHUNK), :] = y

        @pl.when(half != 0)
        def _():
            ys_ref[pl.ds(base, MOE_CHUNK), :] = ys_ref[pl.ds(base, MOE_CHUNK), :] + y

        return carry

    lax.fori_loop(0, (n + MOE_CHUNK - 1) // MOE_CHUNK, run_pass, 0)

    @pl.when((gi == N_GROUPS - 1) & (half == EXPERT_SPLIT - 1))
    def _():
        unperm = (dcol_ref[...] == lax.broadcasted_iota(jnp.int32, (tmoe, PERM_ROWS), 1)).astype(BF16)
        y_hi, y_lo = _split_bf16(ys_ref[0:PERM_ROWS, :])
        moe = _mm(unperm, y_hi) + _mm(unperm, y_lo)
        out_ref[...] = _layernorm(alpha * x1_ref[...] + moe, g2_ref[...], b2_ref[...])


def _moe_sorted(x1, gate, wgu, wd, g2, b2, alpha):
    t, d = x1.shape
    d_ff = wd.shape[2]
    n_tiles = t // TMOE
    n_half = N_EXP // EXPERT_SPLIT
    gid = gate[:, GROUP_LANE].astype(jnp.int32).reshape(n_tiles, TMOE)
    onehot = (gid[:, :, None] == jnp.arange(N_GROUPS)[None, None, :]).astype(jnp.int32)
    cnt = jnp.sum(onehot, axis=1)
    rank = jnp.sum((jnp.cumsum(onehot, axis=1) - onehot) * onehot, axis=2)
    seg = (cnt + SEG_ALIGN - 1) // SEG_ALIGN * SEG_ALIGN
    off = jnp.cumsum(seg, axis=1) - seg
    dest = jnp.sum(onehot * off[:, None, :], axis=2) + rank
    assert N_GROUPS * (SEG_ALIGN - 1) <= PERM_ROWS - TMOE
    wgu_h = wgu.reshape(N_GROUPS * EXPERT_SPLIT, n_half, d, 2 * d_ff)
    wd_h = wd.reshape(N_GROUPS * EXPERT_SPLIT, n_half, d_ff, d)
    grid_spec = pltpu.PrefetchScalarGridSpec(
        num_scalar_prefetch=2,
        grid=(n_tiles, N_GROUPS, EXPERT_SPLIT),
        in_specs=[
            pl.BlockSpec((TMOE, d), lambda i, g, h, *_: (i, 0)),
            pl.BlockSpec((TMOE, LANE), lambda i, g, h, *_: (i, 0)),
            pl.BlockSpec((1, 1, TMOE), lambda i, g, h, *_: (i, 0, 0)),
            pl.BlockSpec((TMOE, 1), lambda i, g, h, *_: (i, 0)),
            pl.BlockSpec((1, n_half, d, 2 * d_ff), lambda i, g, h, *_: (g * EXPERT_SPLIT + h, 0, 0, 0)),
            pl.BlockSpec((1, n_half, d_ff, d), lambda i, g, h, *_: (g * EXPERT_SPLIT + h, 0, 0, 0)),
            pl.BlockSpec(g2.shape, lambda *_: (0, 0)), pl.BlockSpec(b2.shape, lambda *_: (0, 0)),
        ],
        out_specs=pl.BlockSpec((TMOE, d), lambda i, g, h, *_: (i, 0)),
        scratch_shapes=[pltpu.VMEM((SORT_ROWS, d), BF16), pltpu.VMEM((SORT_ROWS, LANE), F32),
                        pltpu.VMEM((SORT_ROWS, d), F32)],
    )
    return pl.pallas_call(
        functools.partial(_moe_sorted_kernel, alpha=alpha, d_ff=d_ff),
        grid_spec=grid_spec,
        out_shape=jax.ShapeDtypeStruct((t, d), F32),
        compiler_params=_cparams(3),
        name="moe_sorted_ln",
    )(cnt.reshape(-1), off.reshape(-1), x1, gate, dest.reshape(n_tiles, 1, TMOE), dest.reshape(t, 1), wgu_h, wd_h,
      g2, b2)


def _pad_head_lanes(v):
    vh = v.reshape(N_HEADS, HEAD_DIM)
    z = jnp.zeros_like(vh)
    grp = (jnp.arange(N_HEADS) // GQ)[:, None]
    return jnp.where(grp == 0, jnp.concatenate([vh, z], axis=1), jnp.concatenate([z, vh], axis=1)).reshape(1, -1)


def _pad_head_rows(w):
    wh = w.reshape(N_HEADS, HEAD_DIM, -1)
    z = jnp.zeros_like(wh)
    grp = (jnp.arange(N_HEADS) // GQ)[:, None, None]
    return jnp.where(grp == 0, jnp.concatenate([wh, z], axis=1), jnp.concatenate([z, wh], axis=1)).reshape(
        N_HEADS * LANE, -1)


def _front_pad(a, axis, pad_row):
    shape = list(a.shape)
    shape[axis] = WINDOW
    return jnp.concatenate([jnp.broadcast_to(pad_row.astype(a.dtype), shape), a], axis=axis)


def kernel(x_prompt, x_sample, cache_kv, page_table, cache_win, state_conv, w_in, conv_w, pe_k, pe_v, w_ck1, w_ck2, w_cv1, w_cv2, g_nsa, g_conv, w_out, ln1_g, ln1_b, w_rg, b_rg, w_re, b_re, w_eg, w_eu, w_ed, ln2_g, ln2_b, rel_bias):
    batch, seq, d_model = x_prompt.shape
    dec_batch, dec_seq = x_sample.shape[0], x_sample.shape[1]
    depth = w_in.shape[0]
    n_pages = page_table.shape[1]
    past = n_pages * PAGE_SIZE
    win_buf = cache_win.shape[2]
    d_nsa = N_HEADS * HEAD_DIM
    d_conv = w_out.shape[1] - d_nsa
    assert dec_seq == 1 and seq % TM == 0 and min(WINDOW, seq) == TM and win_buf == WINDOW
    assert d_conv == 512 and seq // CMP_STRIDE == LANE and n_pages % SAMPLE_PAGES == 0
    alpha = (2 * depth) ** 0.25
    n_slc_s = -(-(past + dec_seq) // SLC_LEN)
    qblk_s = past // SLC_LEN

    idx_np, sp_p_np, sp_s_np = _static_tables(seq, past, win_buf)
    tabs = _bias_tables(rel_bias, jnp.asarray(idx_np))
    nq = seq // TQ
    tab_t = jnp.transpose(tabs[0:_N_NEAR_TILES], (1, 2, 0, 3)).reshape(N_HEADS, LANE, _N_NEAR_TILES * LANE)
    tab_c = tabs[_N_NEAR_TILES:_N_NEAR_TILES + nq]
    smp = tabs[_N_NEAR_TILES + nq]
    tab_w_s = smp[:, 0:5, :].reshape(N_HEADS, 5 * LANE)[:, :win_buf]
    tab_c_s = smp[:, 5:13, :].reshape(N_HEADS, 8 * LANE)
    reps = N_SELECT * PAGE_SIZE // SLC_LEN
    t_near = jnp.tile(smp[:, 13, 0:SLC_LEN], (1, reps))
    t_last = jnp.tile(smp[:, 13, SLC_LEN:2 * SLC_LEN], (1, reps))
    rb0_rep = jnp.broadcast_to(smp[:, 14, 0:1], (N_HEADS, LANE))
    sp_p = jnp.asarray(sp_p_np, BF16)
    sp_s = jnp.asarray(sp_s_np, BF16)
    key_pad = jnp.zeros((LANE,), F32).at[PAD_LANE].set(KEY_MASK)
    val_pad = jnp.zeros((LANE,), F32)

    xp = x_prompt.reshape(batch * seq, d_model)
    xs = x_sample.reshape(dec_batch * dec_seq, d_model)
    outs = [[] for _ in range(6)]
    for l in range(depth):
        w_pack = _pack_in_weights(w_in[l])
        gc = g_conv[l].reshape(1, -1)
        gn_pad = _pad_head_lanes(g_nsa[l])
        w1k, w2k, pek = _pack_compress_weights(pe_k[l], w_ck1[l], w_ck2[l], True)
        w1v, w2v, pev = _pack_compress_weights(pe_v[l], w_cv1[l], w_cv2[l], False)
        w1 = jnp.stack([w1k, w1v])
        pe = jnp.stack([pek, pev])
        wa = _pad_head_rows(w_out[l][:d_nsa]).astype(BF16)
        wb = w_out[l][d_nsa:].astype(BF16)
        wr = jnp.concatenate(
            [jnp.pad(w_rg[l], ((0, 0), (0, LANE - N_GROUPS)))]
            + [jnp.pad(w_re[l][:, gi * N_EXP:(gi + 1) * N_EXP], ((0, 0), (0, LANE - N_EXP))) for gi in range(N_GROUPS)],
            axis=1).astype(BF16)
        br = jnp.concatenate(
            [jnp.pad(b_rg[l], (0, LANE - N_GROUPS))]
            + [jnp.pad(b_re[l][gi * N_EXP:(gi + 1) * N_EXP], (0, LANE - N_EXP)) for gi in range(N_GROUPS)]).reshape(1, -1)
        wgu = jnp.concatenate([w_eg[l], w_eu[l]], axis=-1).astype(BF16)
        wd = w_ed[l].astype(BF16)
        g1, b1 = ln1_g[l].reshape(1, -1), ln1_b[l].reshape(1, -1)
        g2, b2 = ln2_g[l].reshape(1, -1), ln2_b[l].reshape(1, -1)

        q_p, kvt_p, ks, sv, kw, wv, wtail_t, gates_p, zn_p, ctail = _inproj_prompt(xp, w_pack, conv_w[l], gc, batch, seq)
        kvt_p = kvt_p.reshape(batch, 4, N_KV, HEAD_DIM, seq)
        cmp_p = _compress_prompt(kvt_p, batch, seq, w1, w2k, w2v, pe)
        ks = _front_pad(ks.reshape(N_KV, batch, seq, LANE), 2, key_pad)
        kw = _front_pad(kw.reshape(N_KV, batch, seq, LANE), 2, key_pad)
        sv = _front_pad(sv.reshape(batch, seq, LANE), 1, val_pad)
        wv = _front_pad(wv.reshape(batch, seq, LANE), 1, val_pad)
        on_p = _attn_prompt(q_p, gates_p, cmp_p, ks, sv, kw, wv, tab_t, tab_c, sp_p, gn_pad, batch, seq)
        x1_p, gate_p = _proj(on_p, zn_p, xp, wa, wb, g1, b1, wr, br, alpha, TM)
        y_p = _moe_sorted(x1_p, gate_p, wgu, wd, g2, b2, alpha)

        st = state_conv[l]
        qlo_s, qgl_s, kv_s, win_s, gates_s, zn_s, u_s = _inproj_sample(xs, w_pack, conv_w[l], gc, st[:, 0], st[:, 1])
        cache_t = jnp.transpose(cache_kv[l], (0, 2, 3, 4, 1))
        cwin_t = jnp.transpose(cache_win[l], (0, 2, 3, 4, 1))
        cmp_s = _compress_sample(cache_t, page_table, w1, w2k, w2v, pe)
        qlo_s3 = qlo_s.reshape(dec_batch, N_HEADS, LANE)
        qgl_s3 = qgl_s.reshape(dec_batch, N_HEADS, LANE)
        ocmp, owin, sel_idx = _sample_cw(qlo_s3, qgl_s3, cmp_s, cwin_t, win_s.reshape(dec_batch, 1, 256), tab_c_s,
                                         tab_w_s, rb0_rep, sp_s, n_slc_s, qblk_s)
        bids = sel_idx[:, :, :N_SELECT].reshape(dec_batch, N_KV * N_SELECT)
        blk_pages = jnp.take_along_axis(page_table, jnp.minimum(bids // 2, n_pages - 1), axis=1)
        gates_s3 = jnp.pad(gates_s.reshape(dec_batch, N_KV, LANE)[:, :, :3 * GQ].reshape(dec_batch, N_HEADS, 3),
                           ((0, 0), (0, 0), (0, LANE - 3)))
        on_s = _sample_slc(blk_pages, bids % 2, bids, cache_t, qgl_s3, kv_s.reshape(dec_batch, 1, 512), ocmp, owin,
                           gates_s3, t_near, t_last, rb0_rep, gn_pad, n_slc_s - 1, qblk_s - 2)
        on_s = on_s.reshape(dec_batch, N_HEADS * LANE)
        x1_s, gate_s = _proj(on_s, zn_s, xs, wa, wb, g1, b1, wr, br, alpha, dec_batch)
        y_s = _moe(x1_s, gate_s, wgu, wd, g2, b2, alpha, dec_batch)

        to_token_major = lambda a: jnp.transpose(a, (0, 4, 1, 2, 3))
        outs[0].append(to_token_major(kvt_p))
        outs[1].append(kv_s.reshape(dec_batch, dec_seq, 4, N_KV, HEAD_DIM))
        outs[2].append(to_token_major(wtail_t.reshape(batch, 2, N_KV, HEAD_DIM, TM)))
        win_all_t = jnp.concatenate([cwin_t, win_s.reshape(dec_batch, 2, N_KV, HEAD_DIM, dec_seq)], axis=-1)
        outs[3].append(to_token_major(win_all_t[..., win_all_t.shape[-1] - min(WINDOW, past + dec_seq):]))
        outs[4].append(ctail)
        outs[5].append(jnp.concatenate([st, u_s[:, None, :]], axis=1)[:, dec_seq:])
        xp, xs = y_p, y_s
    return (xp.reshape(batch, seq, d_model), xs.reshape(dec_batch, dec_seq, d_model),
            jnp.stack(outs[0]), jnp.stack(outs[1]), jnp.stack(outs[2]), jnp.stack(outs[3]),
            jnp.stack(outs[4]), jnp.stack(outs[5]))
```

```python
import functools
import math

import numpy as np
import jax
import jax.numpy as jnp
from jax import lax
from jax.experimental import pallas as pl
from jax.experimental.pallas import tpu as pltpu

F32 = jnp.float32
BF16 = jnp.bfloat16

HEAD_DIM = 64
N_KV = 2
GQ = 4
N_HEADS = N_KV * GQ
CONV_W = 3
CMP_STRIDE = 16
CMP_LEN = 32
CMP_HIDDEN = 2 * HEAD_DIM
SLC_LEN = 64
N_SELECT = 16
WINDOW = 512
N_BUCKETS = 32
MAX_DISTANCE = 128
N_GROUPS = 4
N_EXP = 8
PAGE_SIZE = 128
NEG = -1e30
FORCE = 1e6

LANE = 128
SUBLANE = 8
VMEM_LIMIT = 52 * 1024 * 1024

TQ = 128
TM = 512
FAR = 512
TAIL = WINDOW + TQ
SAMPLE_PAGES = 64
TMOE = 1024
MOE_CHUNK = 320
SEG_ALIGN = 16
PERM_ROWS = TMOE + LANE
SORT_ROWS = PERM_ROWS + MOE_CHUNK
EXPERT_SPLIT = 2
GROUP_LANE = N_EXP

KEY_MASK = 2.0 ** 100
TABLE_MASK = 2 * NEG
MASKED_BUCKET = N_BUCKETS
SEL_LANE0 = HEAD_DIM
PAD_LANE = HEAD_DIM + 32


def _cparams(n_axes):
    return pltpu.CompilerParams(dimension_semantics=("arbitrary",) * n_axes, vmem_limit_bytes=VMEM_LIMIT)


def _const_spec(shape):
    nd = len(shape)
    return pl.BlockSpec(shape, lambda *_, nd=nd: (0,) * nd)


def _qk(a, b):
    return lax.dot_general(a, b, (((1,), (1,)), ((), ())), preferred_element_type=F32)


def _mm(a, b):
    return jnp.dot(a, b, preferred_element_type=F32)


def _bucket_np(dist):
    n = np.maximum(dist, 0)
    max_exact = N_BUCKETS // 2
    nf = np.maximum(n, 1).astype(np.float32)
    large = max_exact + (np.log(nf / np.float32(max_exact)) / np.float32(math.log(MAX_DISTANCE / max_exact))
                         * np.float32(N_BUCKETS - max_exact)).astype(np.int32)
    large = np.minimum(large, N_BUCKETS - 1)
    return np.where(n < max_exact, n, large).astype(np.int32)


def _overlap_np(c, s):
    c0 = c * CMP_STRIDE
    s0 = s * SLC_LEN
    return np.maximum(np.minimum(c0 + CMP_LEN, s0 + SLC_LEN) - np.maximum(c0, s0), 0)


_N_NEAR_TILES = 3


def _static_tables(seq, past, win_buf):
    nq = seq // TQ
    i = np.arange(LANE)[:, None]
    j = np.arange(LANE)[None, :]
    far_bucket = N_BUCKETS - 1
    tiles = [_bucket_np(LANE + j - i),
             np.where(j >= i, _bucket_np(j - i), MASKED_BUCKET),
             np.where(i > j, far_bucket, MASKED_BUCKET) + 0 * i]
    tiles += [_bucket_np(TQ * q + j - (CMP_STRIDE * i + CMP_STRIDE - 1)) for q in range(nq)]
    smp = np.full((LANE, LANE), 10 * MAX_DISTANCE, np.int64)
    k = np.arange(5 * LANE)
    smp[0:5] = np.where(k < win_buf, win_buf - k, 0).reshape(5, LANE)
    ci = np.arange(8 * LANE)
    smp[5:13] = np.maximum(past - (CMP_STRIDE * ci + CMP_STRIDE - 1), 0).reshape(8, LANE)
    nb = past // SLC_LEN
    smp[13] = past - (SLC_LEN * (nb - 2) + np.arange(LANE))
    smp[14] = 0
    tiles.append(_bucket_np(smp))
    idx = np.stack(tiles).astype(np.int32)

    def sp(nrows, ncols, n_slc):
        r = np.arange(nrows)[:, None]
        s = np.arange(ncols)[None, :]
        ov = _overlap_np(r - 1, s)
        return np.where((r >= 1) & (s < n_slc), ov, 0).astype(np.float32)

    sp_p = sp(seq // CMP_STRIDE, LANE, -(-seq // SLC_LEN)).T
    n_slc_s = -(-(past + 1) // SLC_LEN)
    sp_s = sp(past // CMP_STRIDE, 3 * LANE, n_slc_s)
    return idx, sp_p, sp_s


def _bias_kernel(rb_ref, idx_ref, out_ref):
    idx = idx_ref[0]
    accs = [jnp.full((LANE, LANE), TABLE_MASK, F32) for _ in range(N_HEADS)]
    for b in range(N_BUCKETS):
        hit = idx == b
        for h in range(N_HEADS):
            accs[h] = jnp.where(hit, rb_ref[b, h] - rb_ref[N_BUCKETS - 1, h], accs[h])
    for h in range(N_HEADS):
        out_ref[0, h] = accs[h]


def _bias_tables(rel_bias, idx):
    nt = idx.shape[0]
    return pl.pallas_call(
        _bias_kernel,
        grid=(nt,),
        in_specs=[pl.BlockSpec(memory_space=pltpu.SMEM),
                  pl.BlockSpec((1, LANE, LANE), lambda t: (t, 0, 0))],
        out_specs=pl.BlockSpec((1, N_HEADS, LANE, LANE), lambda t: (t, 0, 0, 0)),
        out_shape=jax.ShapeDtypeStruct((nt, N_HEADS, LANE, LANE), F32),
        compiler_params=_cparams(1),
        name="bias_tables",
    )(rel_bias, idx)


_Q0, _Q1 = 0, N_HEADS * LANE
_KV0, _KV1 = _Q1, _Q1 + 512
_WN0, _WN1 = _KV1, _KV1 + 256
_GT0, _GT1 = _WN1, _WN1 + 2 * LANE
_CV0, _CV1 = _GT1, _GT1 + 3 * 512


def _pack_in_weights(w):
    d = w.shape[0]
    d_nsa = N_HEADS * HEAD_DIM
    wq = w[:, :d_nsa].reshape(d, N_HEADS, HEAD_DIM)
    wq_pad = jnp.pad(wq, ((0, 0), (0, 0), (0, LANE - HEAD_DIM))).reshape(d, N_HEADS * LANE)
    o = d_nsa
    w_kv = w[:, o:o + 512]
    w_win = w[:, o + 512:o + 768]
    wg = w[:, o + 768:o + 768 + 3 * N_HEADS].reshape(d, N_KV, 3 * GQ)
    wg_pad = jnp.pad(wg, ((0, 0), (0, 0), (0, LANE - 3 * GQ))).reshape(d, N_KV * LANE)
    w_conv = w[:, o + 768 + 3 * N_HEADS:]
    return jnp.concatenate([wq_pad, w_kv, w_win, wg_pad, w_conv], axis=1).astype(BF16)


def _rms(x, gain, n, eps=1e-6):
    ms = jnp.sum(x * x, axis=-1, keepdims=True) / n
    return x * lax.rsqrt(ms + eps) * gain


def _layernorm(y, gain, bias, eps=1e-5):
    mu = jnp.mean(y, axis=-1, keepdims=True)
    d = y - mu
    var = jnp.mean(d * d, axis=-1, keepdims=True)
    return d * lax.rsqrt(var + eps) * gain + bias


def _inproj_prompt_kernel(x_ref, w_ref, cw_ref, gc_ref,
                          q_ref, kv_ref, ks_ref, sv_ref, kw_ref, wv_ref, wtail_ref, gate_ref, zn_ref, ctail_ref,
                          uext_ref, *, tiles_per_batch):
    i = pl.program_id(0)
    tm = x_ref.shape[0]
    xb = x_ref[...].astype(BF16)

    def seg(a, b):
        return _mm(xb, w_ref[:, a:b])

    qp = seg(_Q0, _Q1) * (HEAD_DIM ** -0.5)
    for h in range(N_HEADS):
        q_ref[h] = qp[:, h * LANE:(h + 1) * LANE].T.astype(BF16)
    kv = seg(_KV0, _KV1)
    kv_t = kv.T
    kv_ref[0] = kv_t
    win = seg(_WN0, _WN1)
    win_t = win.T
    wtail_ref[0] = win_t
    pos = (i % tiles_per_batch) * tm + lax.broadcasted_iota(jnp.int32, (tm, LANE - HEAD_DIM), 0)
    lane = lax.broadcasted_iota(jnp.int32, (tm, LANE - HEAD_DIM), 1)
    blk_flag = jnp.where(lane == lax.shift_right_logical(pos, 6), KEY_MASK, 0.0).astype(BF16)
    no_flag = jnp.zeros((tm, LANE - HEAD_DIM), BF16)
    for g in range(N_KV):
        ks_ref[g] = jnp.concatenate([kv[:, 256 + g * HEAD_DIM:256 + (g + 1) * HEAD_DIM].astype(BF16), blk_flag], axis=1)
        kw_ref[g] = jnp.concatenate([win[:, g * HEAD_DIM:(g + 1) * HEAD_DIM].astype(BF16), no_flag], axis=1)
    sv_ref[0] = kv_t[384:512, :].astype(BF16)
    wv_ref[0] = win_t[LANE:2 * LANE, :].astype(BF16)
    gt_t = jax.nn.sigmoid(seg(_GT0, _GT1)).T
    gate_ref[0] = gt_t[0:LANE, :]
    gate_ref[1] = gt_t[LANE:2 * LANE, :]

    conv = seg(_CV0, _CV1)
    cb = conv[:, 0:512]
    u = conv[:, 512:1024] * conv[:, 1024:1536]
    first = (i % tiles_per_batch) == 0

    @pl.when(first)
    def _():
        uext_ref[0:SUBLANE, :] = jnp.zeros((SUBLANE, 512), F32)

    @pl.when(jnp.logical_not(first))
    def _():
        uext_ref[0:SUBLANE, :] = uext_ref[tm:tm + SUBLANE, :]

    uext_ref[SUBLANE:tm + SUBLANE, :] = u
    um1 = uext_ref[SUBLANE - 1:tm + SUBLANE - 1, :]
    um2 = uext_ref[SUBLANE - 2:tm + SUBLANE - 2, :]
    z = cb * (um2 * cw_ref[0:1, :] + um1 * cw_ref[1:2, :] + u * cw_ref[2:3, :])
    zn_ref[...] = _rms(z, gc_ref[...], 512).astype(BF16)
    ctail_ref[0] = uext_ref[tm + SUBLANE - 2:tm + SUBLANE, :]


def _inproj_prompt(x, w, conv_w, g_conv, batch, seq):
    t, d = x.shape
    n_tiles = t // TM
    tpb = seq // TM
    outs = (
        jax.ShapeDtypeStruct((N_HEADS, LANE, t), BF16),
        jax.ShapeDtypeStruct((batch, 512, seq), F32),
        jax.ShapeDtypeStruct((N_KV, t, LANE), BF16),
        jax.ShapeDtypeStruct((batch, LANE, seq), BF16),
        jax.ShapeDtypeStruct((N_KV, t, LANE), BF16),
        jax.ShapeDtypeStruct((batch, LANE, seq), BF16),
        jax.ShapeDtypeStruct((batch, 256, TM), F32),
        jax.ShapeDtypeStruct((N_KV, LANE, t), F32),
        jax.ShapeDtypeStruct((t, 512), BF16),
        jax.ShapeDtypeStruct((batch, CONV_W - 1, 512), F32),
    )
    row = lambda n: pl.BlockSpec((TM, n), lambda i: (i, 0))
    planes = lambda n: pl.BlockSpec((n, TM, LANE), lambda i: (0, i, 0))
    planes_t = lambda n: pl.BlockSpec((n, LANE, TM), lambda i: (0, 0, i))
    per_batch_t = pl.BlockSpec((1, LANE, TM), lambda i: (i // tpb, 0, i % tpb))
    return pl.pallas_call(
        functools.partial(_inproj_prompt_kernel, tiles_per_batch=tpb),
        grid=(n_tiles,),
        in_specs=[row(d), _const_spec(w.shape), _const_spec(conv_w.shape), _const_spec(g_conv.shape)],
        out_specs=(
            planes_t(N_HEADS),
            pl.BlockSpec((1, 512, TM), lambda i: (i // tpb, 0, i % tpb)),
            planes(N_KV), per_batch_t, planes(N_KV), per_batch_t,
            pl.BlockSpec((1, 256, TM), lambda i: (i // tpb, 0, 0)),
            planes_t(N_KV),
            row(512),
            pl.BlockSpec((1, CONV_W - 1, 512), lambda i: (i // tpb, 0, 0)),
        ),
        out_shape=outs,
        scratch_shapes=[pltpu.VMEM((TM + SUBLANE, 512), F32)],
        compiler_params=_cparams(1),
        name="inproj_prompt",
    )(x, w, conv_w, g_conv)


def _inproj_sample_kernel(x_ref, w_ref, cw_ref, gc_ref, s0_ref, s1_ref,
                          qlo_ref, qgl_ref, kv_ref, win_ref, gate_ref, zn_ref, u_ref):
    xb = x_ref[...].astype(BF16)

    def seg(a, b):
        return _mm(xb, w_ref[:, a:b])

    qp = seg(_Q0, _Q1) * (HEAD_DIM ** -0.5)
    qlo_ref[...] = qp.astype(BF16)
    blocks = [qp[:, h * LANE:(h + 1) * LANE] for h in range(N_HEADS)]
    blocks = [b if h < GQ else pltpu.roll(b, HEAD_DIM, axis=1) for h, b in enumerate(blocks)]
    qgl_ref[...] = jnp.concatenate(blocks, axis=1).astype(BF16)
    kv_ref[...] = seg(_KV0, _KV1)
    win_ref[...] = seg(_WN0, _WN1)
    gate_ref[...] = jax.nn.sigmoid(seg(_GT0, _GT1))
    conv = seg(_CV0, _CV1)
    cb = conv[:, 0:512]
    u = conv[:, 512:1024] * conv[:, 1024:1536]
    z = cb * (s0_ref[...] * cw_ref[0:1, :] + s1_ref[...] * cw_ref[1:2, :] + u * cw_ref[2:3, :])
    zn_ref[...] = _rms(z, gc_ref[...], 512).astype(BF16)
    u_ref[...] = u


def _inproj_sample(x, w, conv_w, g_conv, s0, s1):
    n = x.shape[0]
    outs = (
        jax.ShapeDtypeStruct((n, N_HEADS * LANE), BF16),
        jax.ShapeDtypeStruct((n, N_HEADS * LANE), BF16),
        jax.ShapeDtypeStruct((n, 512), F32),
        jax.ShapeDtypeStruct((n, 256), F32),
        jax.ShapeDtypeStruct((n, N_KV * LANE), F32),
        jax.ShapeDtypeStruct((n, 512), BF16),
        jax.ShapeDtypeStruct((n, 512), F32),
    )
    args = (x, w, conv_w, g_conv, s0, s1)
    return pl.pallas_call(
        _inproj_sample_kernel,
        grid=(1,),
        in_specs=[_const_spec(a.shape) for a in args],
        out_specs=tuple(_const_spec(o.shape) for o in outs),
        out_shape=outs,
        compiler_params=_cparams(1),
        name="inproj_sample",
    )(*args)


def _pack_compress_weights(pe, w1, w2, per_group_out):
    w = w1.reshape(2, CMP_STRIDE, HEAD_DIM, CMP_HIDDEN).transpose(1, 2, 0, 3).reshape(CMP_STRIDE, HEAD_DIM, 2 * CMP_HIDDEN)
    z = jnp.zeros_like(w)
    top = jnp.concatenate([w, z], axis=-1)
    bot = jnp.concatenate([z, w], axis=-1)
    w1bd = jnp.stack([top, bot], axis=1).reshape(CMP_STRIDE * 2 * HEAD_DIM, 4 * CMP_HIDDEN).astype(BF16)
    z2 = jnp.zeros_like(w2)
    if per_group_out:
        w2bd = jnp.concatenate([jnp.concatenate([w2, z2, z2, z2], axis=1),
                                jnp.concatenate([z2, z2, w2, z2], axis=1)], axis=0).astype(BF16)
    else:
        w2bd = jnp.concatenate([jnp.concatenate([w2, z2], axis=1), jnp.concatenate([z2, w2], axis=1)], axis=0).astype(BF16)
    pe_rows = jnp.broadcast_to(pe.reshape(2, CMP_STRIDE, 1, HEAD_DIM), (2, CMP_STRIDE, N_KV, HEAD_DIM))
    pe_rows = pe_rows.reshape(2, CMP_STRIDE * N_KV * HEAD_DIM)
    pe_rows = jnp.pad(pe_rows, ((0, SUBLANE - 2), (0, 0)))
    return w1bd, w2bd, pe_rows


def _compress_kernel(*refs, n_pages, n_prefetch):
    refs = refs[n_prefetch:]
    page_refs = refs[:n_pages]
    w1_ref, w2k_ref, w2v_ref, pe_ref, ck_ref, cvt_ref, carry_ref, tok_ref = refs[n_pages:]
    step = pl.program_id(1)
    m = n_pages * (PAGE_SIZE // CMP_STRIDE)

    @pl.when(step == 0)
    def _():
        carry_ref[...] = jnp.zeros(carry_ref.shape, F32)

    row0 = lax.broadcasted_iota(jnp.int32, (m, CMP_HIDDEN), 0) == 0
    for t, w2_ref in enumerate((w2k_ref, w2v_ref)):
        for p, pr in enumerate(page_refs):
            slab = pr[0, t].reshape(N_KV * HEAD_DIM, PAGE_SIZE)
            tok_ref[t, p * PAGE_SIZE:(p + 1) * PAGE_SIZE, :] = slab.T
        pieces = [tok_ref[t, pl.ds(j, m, stride=CMP_STRIDE), :] for j in range(CMP_STRIDE)]
        lhs = jnp.concatenate([jnp.concatenate(pieces, axis=1), pe_ref[t]], axis=0).astype(BF16)
        a = _mm(lhs, w1_ref[t])
        hs = []
        for g in range(N_KV):
            c0 = g * 2 * CMP_HIDDEN
            a0 = a[0:m, c0:c0 + CMP_HIDDEN]
            a1 = a[0:m, c0 + CMP_HIDDEN:c0 + 2 * CMP_HIDDEN]
            pe_term = a[m:m + 1, c0:c0 + CMP_HIDDEN] + a[m + 1:m + 2, c0 + CMP_HIDDEN:c0 + 2 * CMP_HIDDEN]
            prev = carry_ref[t, 0:1, g * CMP_HIDDEN:(g + 1) * CMP_HIDDEN]
            shifted = jnp.where(row0, prev, pltpu.roll(a0, 1, axis=0))
            carry_ref[t, 0:1, g * CMP_HIDDEN:(g + 1) * CMP_HIDDEN] = a0[m - 1:m, :]
            hs.append(jax.nn.gelu(shifted + a1 + pe_term))
        out = _mm(jnp.concatenate(hs, axis=1).astype(BF16), w2_ref[...])
        if t == 0:
            ck_ref[0, 0] = out[:, 0:LANE].astype(BF16)
            ck_ref[0, 1] = out[:, LANE:2 * LANE].astype(BF16)
        else:
            cvt_ref[0] = out.T.astype(BF16)


def _compress_call(page_specs, page_args, prefetch, grid, batch, n_chunks, m, w1, w2k, w2v, pe, name):
    n_pages = len(page_specs)
    n_pf = len(prefetch)
    cspec = lambda shape: pl.BlockSpec(shape, lambda *_: (0,) * len(shape))
    grid_spec = pltpu.PrefetchScalarGridSpec(
        num_scalar_prefetch=n_pf,
        grid=grid,
        in_specs=list(page_specs) + [cspec(w1.shape), cspec(w2k.shape), cspec(w2v.shape), cspec(pe.shape)],
        out_specs=(pl.BlockSpec((1, N_KV, m, LANE), lambda b, s, *_: (b, 0, s, 0)),
                   pl.BlockSpec((1, LANE, m), lambda b, s, *_: (b, 0, s))),
        scratch_shapes=[pltpu.VMEM((2, SUBLANE, 2 * CMP_HIDDEN), F32),
                        pltpu.VMEM((2, n_pages * PAGE_SIZE, N_KV * HEAD_DIM), F32)],
    )
    return pl.pallas_call(
        functools.partial(_compress_kernel, n_pages=n_pages, n_prefetch=n_pf),
        grid_spec=grid_spec,
        out_shape=(jax.ShapeDtypeStruct((batch, N_KV, n_chunks, LANE), BF16),
                   jax.ShapeDtypeStruct((batch, LANE, n_chunks), BF16)),
        compiler_params=_cparams(2),
        name=name,
    )(*prefetch, *page_args, w1, w2k, w2v, pe)


_PAGE_BLOCK = (1, 2, N_KV, HEAD_DIM, PAGE_SIZE)


def _compress_prompt(kv_t, batch, seq, w1, w2k, w2v, pe):
    n_pages = seq // PAGE_SIZE
    rows = PAGE_SIZE // CMP_STRIDE
    specs = [pl.BlockSpec(_PAGE_BLOCK, lambda b, s, p=p: (b, 0, 0, 0, p)) for p in range(n_pages)]
    return _compress_call(specs, [kv_t] * n_pages, (), (batch, 1), batch, n_pages * rows, n_pages * rows,
                          w1, w2k, w2v, pe, "compress_prompt")


def _compress_sample(cache_t, page_table, w1, w2k, w2v, pe):
    batch, n_pages = page_table.shape
    rows = PAGE_SIZE // CMP_STRIDE
    specs = [pl.BlockSpec(_PAGE_BLOCK, lambda b, s, pt, k=k: (pt[b, s * SAMPLE_PAGES + k], 0, 0, 0, 0))
             for k in range(SAMPLE_PAGES)]
    return _compress_call(specs, [cache_t] * SAMPLE_PAGES, (page_table,), (batch, n_pages // SAMPLE_PAGES), batch,
                          n_pages * rows, SAMPLE_PAGES * rows, w1, w2k, w2v, pe, "compress_sample")


def _tile4(x):
    return jnp.concatenate([x] * GQ, axis=1)


def _softmax_fold(state, pieces, vt):
    m, l, acc = state
    m_new = m
    for pc in pieces:
        m_new = jnp.maximum(m_new, jnp.max(pc, axis=0, keepdims=True))
    alpha = jnp.exp(m - m_new)
    es = [jnp.exp(pc - m_new) for pc in pieces]
    l = alpha * l
    for e in es:
        l = l + jnp.sum(e, axis=0, keepdims=True)
    e_all = es[0] if len(es) == 1 else jnp.concatenate(es, axis=0)
    acc = alpha * acc + _mm(vt, e_all.astype(BF16))
    return m_new, l, acc


def _attn_prompt_kernel(q_ref, gate_ref, ck_ref, cvt_ref, ks_ref, sv_ref, kw_ref, wv_ref,
                        near_ref, up_ref, tabc_ref, sp_ref, gn_ref, o_ref, *, n_slc):
    qi = pl.program_id(1)
    q0 = pl.multiple_of(qi * TQ, TQ)
    krow = lax.broadcasted_iota(jnp.int32, (LANE, TQ), 0)
    qcol = lax.broadcasted_iota(jnp.int32, (LANE, TQ), 1)
    cmp_ok4 = _tile4((krow >= 1) & (CMP_STRIDE * krow + (CMP_STRIDE - 1) <= q0 + qcol))
    sidx = lax.broadcasted_iota(jnp.int32, (n_slc, TQ), 0)
    qblk = lax.shift_right_logical(q0 + lax.broadcasted_iota(jnp.int32, (n_slc, TQ), 1), 6)
    sel_valid = sidx <= qblk
    sel_forced = (sidx == 0) | (sidx == qblk) | (sidx == qblk - 1)
    n_far = lax.shift_right_logical(jnp.maximum(qi - 1, 0), 2)
    up4 = _tile4(up_ref[...])
    cols = GQ * TQ
    fresh = (jnp.full((1, cols), NEG, F32), jnp.zeros((1, cols), F32), jnp.zeros((LANE, cols), F32))

    groups = range(N_KV)
    qa, near, o_cmp = [], [], []
    for g in groups:
        heads = [GQ * g + r for r in range(GQ)]
        q_t = [q_ref[h] for h in heads]
        near.append(jnp.concatenate([near_ref[h] for h in heads], axis=1))

        bias_c = jnp.concatenate([tabc_ref[0, h] for h in heads], axis=1)
        s = jnp.where(cmp_ok4, _mm(ck_ref[0, g], jnp.concatenate(q_t, axis=1)) + bias_c, NEG)
        e = jnp.exp(s - jnp.max(s, axis=0, keepdims=True))
        p = (e / jnp.sum(e, axis=0, keepdims=True) * cmp_ok4.astype(F32)).astype(BF16)
        o_cmp.append(_mm(cvt_ref[0], p))
        imp4 = _mm(sp_ref[...], p)
        imp = imp4[:, 0:TQ] + imp4[:, TQ:2 * TQ] + imp4[:, 2 * TQ:3 * TQ] + imp4[:, 3 * TQ:4 * TQ]

        score = jnp.where(sel_valid, imp[0:n_slc, :] + jnp.where(sel_forced, FORCE, 0.0), -FORCE)
        rank = jnp.zeros((n_slc, TQ), F32)
        for k in range(1, n_slc):
            other = pltpu.roll(score, k, axis=0)
            rank = rank + jnp.where(sidx >= k, (other >= score).astype(F32), (other > score).astype(F32))
        sel_t = (rank < float(N_SELECT)).astype(F32)
        aug = jnp.concatenate([jnp.zeros((SEL_LANE0, TQ), F32), sel_t - 1.0,
                               jnp.zeros((LANE - SEL_LANE0 - n_slc, TQ), F32)], axis=0)
        aug = jnp.where(krow == PAD_LANE, -1.0, aug).astype(BF16)
        qa.append(jnp.concatenate([qh + aug for qh in q_t], axis=1))

    def far_tile(it, states):
        start = pl.multiple_of(q0 - FAR * (n_far - it), LANE)
        vt = sv_ref[0, :, pl.ds(start, FAR)]
        return tuple(_softmax_fold(states[g], [_mm(ks_ref[g, 0, pl.ds(start, FAR), :], qa[g])], vt) for g in groups)

    states = lax.fori_loop(0, n_far, far_tile, (fresh,) * N_KV)
    v_tail = sv_ref[0, :, pl.ds(q0, TAIL)]
    w_tail = wv_ref[0, :, pl.ds(q0, TAIL)]
    o_slc, o_win = [], []
    for g in groups:
        sc = _mm(ks_ref[g, 0, pl.ds(q0, TAIL), :], qa[g])
        pieces = [sc[0:TAIL - 2 * LANE], sc[TAIL - 2 * LANE:TAIL] + near[g]]
        _, l, acc = _softmax_fold(states[g], pieces, v_tail)
        o_slc.append(acc / l)
        sc = _mm(kw_ref[g, 0, pl.ds(q0, TAIL), :], qa[g])
        pieces = [sc[0:LANE] + up4, sc[LANE:TAIL - 2 * LANE], sc[TAIL - 2 * LANE:TAIL] + near[g]]
        _, l, acc = _softmax_fold(fresh, pieces, w_tail)
        o_win.append(acc / l)

    outs = []
    for g in groups:
        gt = gate_ref[g]
        own = lax.shift_right_logical(krow, 6) == g
        for r in range(GQ):
            cs = slice(r * TQ, (r + 1) * TQ)
            comb = gt[3 * r:3 * r + 1, :] * o_cmp[g][:, cs] + gt[3 * r + 1:3 * r + 2, :] * o_slc[g][:, cs] \
                + gt[3 * r + 2:3 * r + 3, :] * o_win[g][:, cs]
            outs.append(jnp.where(own, comb, 0.0))
    o = jnp.concatenate(outs, axis=0)
    ms = jnp.sum(o * o, axis=0, keepdims=True) / (N_HEADS * HEAD_DIM)
    o = o * lax.rsqrt(ms + 1e-6) * gn_ref[...]
    o_ref[...] = jnp.concatenate([o[h * LANE:(h + 1) * LANE, :].T for h in range(N_HEADS)], axis=1).astype(BF16)


def _attn_prompt(q, gates, cmpk, cmpv_t, ks, sv_t, kw, wv_t, near_t, up_t, tab_c, sp_t, gn_rep, batch, seq):
    nq = seq // TQ
    t = batch * seq
    rows = ks.shape[2]
    n_slc = seq // SLC_LEN
    n_cmp = seq // CMP_STRIDE
    assert SEL_LANE0 + n_slc <= PAD_LANE < LANE
    return pl.pallas_call(
        functools.partial(_attn_prompt_kernel, n_slc=n_slc),
        grid=(batch, nq),
        in_specs=[
            pl.BlockSpec((N_HEADS, LANE, TQ), lambda b, i: (0, 0, b * nq + i)),
            pl.BlockSpec((N_KV, LANE, TQ), lambda b, i: (0, 0, b * nq + i)),
            pl.BlockSpec((1, N_KV, n_cmp, LANE), lambda b, i: (b, 0, 0, 0)),
            pl.BlockSpec((1, LANE, n_cmp), lambda b, i: (b, 0, 0)),
            pl.BlockSpec((N_KV, 1, rows, LANE), lambda b, i: (0, b, 0, 0)),
            pl.BlockSpec((1, LANE, rows), lambda b, i: (b, 0, 0)),
            pl.BlockSpec((N_KV, 1, rows, LANE), lambda b, i: (0, b, 0, 0)),
            pl.BlockSpec((1, LANE, rows), lambda b, i: (b, 0, 0)),
            _const_spec(near_t.shape),
            _const_spec(up_t.shape),
            pl.BlockSpec((1, N_HEADS, LANE, LANE), lambda b, i: (i, 0, 0, 0)),
            _const_spec(sp_t.shape),
            _const_spec(gn_rep.shape),
        ],
        out_specs=pl.BlockSpec((TQ, N_HEADS * LANE), lambda b, i: (b * nq + i, 0)),
        out_shape=jax.ShapeDtypeStruct((t, N_HEADS * LANE), BF16),
        compiler_params=_cparams(2),
        name="attn_prompt",
    )(q, gates, cmpk, cmpv_t, ks, sv_t, kw, wv_t, near_t, up_t, tab_c, sp_t, gn_rep)


def _sample_cw_kernel(qlo_ref, qgl_ref, ck_ref, cvt_ref, cwin_ref, wnew_ref, tabc_ref, tabw_ref, rb0_ref, sp_ref,
                      ocmp_ref, owin_ref, idx_ref, *, n_slc, qblk, win_buf):
    qlo = qlo_ref[0]
    qgl = qgl_ref[0]
    n_cmp_rows = ck_ref.shape[2]
    col = lax.broadcasted_iota(jnp.int32, (N_HEADS, n_cmp_rows), 1)
    grp_of_head = lax.shift_right_logical(lax.broadcasted_iota(jnp.int32, (N_HEADS, n_cmp_rows), 0), 2)
    ok = col >= 1
    raw = jnp.where(grp_of_head == 0, _qk(qlo, ck_ref[0, 0]), _qk(qlo, ck_ref[0, 1]))
    s = jnp.where(ok, raw + tabc_ref[...], NEG)
    e = jnp.exp(s - jnp.max(s, axis=1, keepdims=True))
    p = (e / jnp.sum(e, axis=1, keepdims=True) * ok.astype(F32)).astype(BF16)
    ocmp_ref[0] = _qk(p, cvt_ref[0])
    imp8 = _mm(p, sp_ref[...])

    width = imp8.shape[1]
    head_grp = lax.shift_right_logical(lax.broadcasted_iota(jnp.int32, (N_HEADS, width), 0), 2)
    sidx = lax.broadcasted_iota(jnp.int32, (1, width), 1)
    sidx_f = sidx.astype(F32)
    forced = (sidx == 0) | (sidx == qblk) | (sidx == qblk - 1)
    lane = lax.broadcasted_iota(jnp.int32, (1, LANE), 1)
    for g in range(N_KV):
        imp = jnp.sum(jnp.where(head_grp == g, imp8, 0.0), axis=0, keepdims=True)
        score = jnp.where(sidx <= qblk, imp + jnp.where(forced, FORCE, 0.0), -FORCE)
        score = jnp.where(sidx < n_slc, score, -jnp.inf)
        picked = jnp.zeros((1, LANE), F32)
        for k in range(N_SELECT):
            best = jnp.max(score, axis=1, keepdims=True)
            ik = jnp.min(jnp.where(score == best, sidx_f, float(width)), axis=1, keepdims=True)
            picked = jnp.where(lane == k, ik, picked)
            score = jnp.where(sidx_f == ik, -jnp.inf, score)
        idx_ref[0, g:g + 1, :] = picked.astype(jnp.int32)

    kt = cwin_ref[0, 0].reshape(N_KV * HEAD_DIM, win_buf).astype(BF16)
    vt = cwin_ref[0, 1].reshape(N_KV * HEAD_DIM, win_buf).astype(BF16)
    knew = wnew_ref[0, :, 0:LANE].astype(BF16).astype(F32)
    vnew = wnew_ref[0, :, LANE:2 * LANE].astype(BF16).astype(F32)
    colw = lax.broadcasted_iota(jnp.int32, (N_HEADS, win_buf), 1)
    okw = colw > win_buf - WINDOW
    sw = jnp.where(okw, _mm(qgl, kt) + tabw_ref[...], NEG)
    s_new = jnp.sum(qgl.astype(F32) * knew, axis=1, keepdims=True) + rb0_ref[:, 0:1]
    mw = jnp.maximum(jnp.max(sw, axis=1, keepdims=True), s_new)
    ew = jnp.where(okw, jnp.exp(sw - mw), 0.0)
    e_new = jnp.exp(s_new - mw)
    lw = jnp.sum(ew, axis=1, keepdims=True) + e_new
    owin_ref[0] = (_qk(ew.astype(BF16), vt) + e_new.astype(BF16).astype(F32) * vnew) / lw


def _sample_cw(qlo, qgl, cmpk, cmpv_t, cwin, wnew, tab_c, tab_w, rb0, sp, n_slc, qblk):
    n = qlo.shape[0]
    win_buf = cwin.shape[-1]
    blk = lambda a: pl.BlockSpec((1,) + a.shape[1:], lambda b: (b,) + (0,) * (a.ndim - 1))
    outs = (jax.ShapeDtypeStruct((n, N_HEADS, LANE), F32), jax.ShapeDtypeStruct((n, N_HEADS, LANE), F32),
            jax.ShapeDtypeStruct((n, N_KV, LANE), jnp.int32))
    return pl.pallas_call(
        functools.partial(_sample_cw_kernel, n_slc=n_slc, qblk=qblk, win_buf=win_buf),
        grid=(n,),
        in_specs=[blk(qlo), blk(qgl), blk(cmpk), blk(cmpv_t), blk(cwin), blk(wnew), _const_spec(tab_c.shape),
                  _const_spec(tab_w.shape), _const_spec(rb0.shape), _const_spec(sp.shape)],
        out_specs=tuple(pl.BlockSpec((1,) + o.shape[1:], lambda b: (b, 0, 0)) for o in outs),
        out_shape=outs,
        compiler_params=_cparams(1),
        name="sample_cmp_win",
    )(qlo, qgl, cmpk, cmpv_t, cwin, wnew, tab_c, tab_w, rb0, sp)


def _sample_slc_kernel(pg_ref, hf_ref, bid_ref, *refs, new_block, near_block):
    n_blk = N_KV * N_SELECT
    blocks = refs[:n_blk]
    (q_ref, kvnew_ref, ocmp_ref, owin_ref, gate_ref, tnear_ref, tlast_ref, rb0_ref, gn_ref, o_ref) = refs[n_blk:]
    b = pl.program_id(0)
    n_keys = N_SELECT * PAGE_SIZE
    lane = lax.broadcasted_iota(jnp.int32, (1, n_keys), 1)
    slot = lax.shift_right_logical(lane, 7)
    half = lax.shift_right_logical(lane, 6) & 1
    head_grp = lax.shift_right_logical(lax.broadcasted_iota(jnp.int32, (N_HEADS, LANE), 0), 2)
    lane_grp = lax.shift_right_logical(lax.broadcasted_iota(jnp.int32, (N_HEADS, LANE), 1), 6)
    knew = kvnew_ref[0, :, 256:384].astype(BF16).astype(F32)
    vnew = kvnew_ref[0, :, 384:512].astype(BF16).astype(F32)
    qs = q_ref[0]
    o_slc = jnp.zeros((N_HEADS, LANE), F32)
    for g in range(N_KV):
        mine = blocks[g * N_SELECT:(g + 1) * N_SELECT]
        kt = jnp.concatenate([blk[0, 0].reshape(N_KV * HEAD_DIM, PAGE_SIZE) for blk in mine], axis=1).astype(BF16)
        vt = jnp.concatenate([blk[0, 1].reshape(N_KV * HEAD_DIM, PAGE_SIZE) for blk in mine], axis=1).astype(BF16)
        bid = jnp.zeros((1, n_keys), jnp.int32)
        hsel = jnp.zeros((1, n_keys), jnp.int32)
        has_new = bid_ref[b, g * N_SELECT] == new_block
        for kk in range(N_SELECT):
            bid = jnp.where(slot == kk, bid_ref[b, g * N_SELECT + kk], bid)
            hsel = jnp.where(slot == kk, hf_ref[b, g * N_SELECT + kk], hsel)
            if kk:
                has_new = has_new | (bid_ref[b, g * N_SELECT + kk] == new_block)
        bias = jnp.where(bid == near_block + 1, tlast_ref[...], jnp.where(bid == near_block, tnear_ref[...], 0.0))
        ok = (bid != new_block) & (half == hsel)
        s = jnp.where(ok, _mm(qs, kt) + bias, NEG)
        s_new = jnp.sum(qs.astype(F32) * knew, axis=1, keepdims=True) + rb0_ref[:, 0:1]
        s_new = jnp.where(has_new, s_new, NEG)
        m = jnp.maximum(jnp.max(s, axis=1, keepdims=True), s_new)
        e = jnp.where(ok, jnp.exp(s - m), 0.0)
        e_new = jnp.where(has_new, jnp.exp(s_new - m), 0.0)
        l = jnp.sum(e, axis=1, keepdims=True) + e_new
        og = (_qk(e.astype(BF16), vt) + e_new.astype(BF16).astype(F32) * vnew) / l
        o_slc = jnp.where(head_grp == g, og, o_slc)
    gt = gate_ref[0]
    comb = gt[:, 0:1] * ocmp_ref[0] + gt[:, 1:2] * o_slc + gt[:, 2:3] * owin_ref[0]
    comb = jnp.where(lane_grp == head_grp, comb, 0.0)
    o = jnp.concatenate([comb[h:h + 1, :] for h in range(N_HEADS)], axis=1)
    o_ref[0] = _rms(o, gn_ref[...], N_HEADS * HEAD_DIM).astype(BF16)


def _sample_slc(pages, halves, bids, cache_t, q, kvnew, ocmp, owin, gates, t_near, t_last, rb0, gn_pad,
                new_block, near_block):
    n = q.shape[0]
    n_blk = N_KV * N_SELECT
    blk_specs = [pl.BlockSpec(_PAGE_BLOCK, lambda b, pg, hf, bi, k=k: (pg[b, k], 1, 0, 0, 0))
                 for k in range(n_blk)]
    per_b = lambda a: pl.BlockSpec((1,) + a.shape[1:], lambda b, *_: (b,) + (0,) * (a.ndim - 1))
    cst = lambda a: pl.BlockSpec(a.shape, lambda *_: (0,) * a.ndim)
    grid_spec = pltpu.PrefetchScalarGridSpec(
        num_scalar_prefetch=3,
        grid=(n,),
        in_specs=blk_specs + [per_b(q), per_b(kvnew), per_b(ocmp), per_b(owin), per_b(gates),
                              cst(t_near), cst(t_last), cst(rb0), cst(gn_pad)],
        out_specs=pl.BlockSpec((1, 1, N_HEADS * LANE), lambda b, *_: (b, 0, 0)),
    )
    return pl.pallas_call(
        functools.partial(_sample_slc_kernel, new_block=new_block, near_block=near_block),
        grid_spec=grid_spec,
        out_shape=jax.ShapeDtypeStruct((n, 1, N_HEADS * LANE), BF16),
        compiler_params=_cparams(1),
        name="sample_selected",
    )(pages, halves, bids, *([cache_t] * n_blk), q, kvnew, ocmp, owin, gates, t_near, t_last, rb0, gn_pad)


def _proj_kernel(on_ref, zn_ref, x_ref, wa_ref, wb_ref, g1_ref, b1_ref, wr_ref, br_ref,
                 x1_ref, gate_ref, *, alpha):
    mix = _mm(on_ref[...], wa_ref[...]) + _mm(zn_ref[...], wb_ref[...])
    x1 = _layernorm(alpha * x_ref[...] + mix, g1_ref[...], b1_ref[...])
    x1_ref[...] = x1
    logits = _mm(x1.astype(BF16), wr_ref[...]) + br_ref[...]
    tm = logits.shape[0]
    lane = lax.broadcasted_iota(jnp.int32, (tm, LANE), 1).astype(F32)

    def first_argmax(v, vmax):
        return jnp.min(jnp.where(v == vmax, lane, float(LANE)), axis=1, keepdims=True)

    lg = jnp.where(lane < N_GROUPS, logits[:, 0:LANE], -jnp.inf)
    lg_max = jnp.max(lg, axis=1, keepdims=True)
    eg = jnp.exp(lg - lg_max)
    pg = eg / jnp.sum(eg, axis=1, keepdims=True)
    gidx = first_argmax(lg, lg_max)
    pg_sel = jnp.sum(jnp.where(lane == gidx, pg, 0.0), axis=1, keepdims=True)
    le = jnp.zeros((tm, LANE), F32)
    for gi in range(N_GROUPS):
        le = le + jnp.where(gidx == gi, logits[:, (gi + 1) * LANE:(gi + 2) * LANE], 0.0)
    le = jnp.where(lane < N_EXP, le, -jnp.inf)
    ee = jnp.exp(le - jnp.max(le, axis=1, keepdims=True))
    pe = jnp.where(lane < N_EXP, ee / jnp.sum(ee, axis=1, keepdims=True), -1.0)
    v1 = jnp.max(pe, axis=1, keepdims=True)
    i1 = first_argmax(pe, v1)
    pe2 = jnp.where(lane == i1, -1.0, pe)
    v2 = jnp.max(pe2, axis=1, keepdims=True)
    i2 = first_argmax(pe2, v2)
    tot = v1 + v2
    gate_e = jnp.where(lane == i1, v1 / tot * pg_sel, jnp.where(lane == i2, v2 / tot * pg_sel, 0.0))
    gate_ref[...] = jnp.where(lane == GROUP_LANE, gidx, gate_e)


def _proj(on, zn, x, wa, wb, g1, b1, wr, br, alpha, tm):
    t, d = x.shape
    row = lambda n: pl.BlockSpec((tm, n), lambda i: (i, 0))
    outs = (jax.ShapeDtypeStruct((t, d), F32), jax.ShapeDtypeStruct((t, LANE), F32))
    return pl.pallas_call(
        functools.partial(_proj_kernel, alpha=alpha),
        grid=(t // tm,),
        in_specs=[row(on.shape[1]), row(zn.shape[1]), row(d)] + [_const_spec(a.shape) for a in (wa, wb, g1, b1, wr, br)],
        out_specs=(row(d), row(LANE)),
        out_shape=outs,
        compiler_params=_cparams(1),
        name=f"proj_ln_router_{tm}",
    )(on, zn, x, wa, wb, g1, b1, wr, br)


def _moe_kernel(x1_ref, gate_ref, wgu_ref, wd_ref, g2_ref, b2_ref, out_ref, acc_ref, *, alpha, d_ff):
    gi = pl.program_id(1)

    @pl.when(gi == 0)
    def _():
        acc_ref[...] = jnp.zeros(acc_ref.shape, F32)

    x = x1_ref[...].astype(BF16)
    gate = gate_ref[...]
    gt = jnp.where(gate[:, GROUP_LANE:GROUP_LANE + 1] == gi.astype(F32), gate, 0.0)
    acc = acc_ref[...]
    for e in range(N_EXP):
        au = _mm(x, wgu_ref[0, e])
        h = jax.nn.silu(au[:, 0:d_ff]) * au[:, d_ff:2 * d_ff]
        acc = acc + _mm((h * gt[:, e:e + 1]).astype(BF16), wd_ref[0, e])
    acc_ref[...] = acc

    @pl.when(gi == N_GROUPS - 1)
    def _():
        out_ref[...] = _layernorm(alpha * x1_ref[...] + acc_ref[...], g2_ref[...], b2_ref[...])


def _moe(x1, gate, wgu, wd, g2, b2, alpha, tm):
    t, d = x1.shape
    d_ff = wd.shape[2]
    return pl.pallas_call(
        functools.partial(_moe_kernel, alpha=alpha, d_ff=d_ff),
        grid=(t // tm, N_GROUPS),
        in_specs=[
            pl.BlockSpec((tm, d), lambda i, g: (i, 0)),
            pl.BlockSpec((tm, LANE), lambda i, g: (i, 0)),
            pl.BlockSpec((1, N_EXP, d, 2 * d_ff), lambda i, g: (g, 0, 0, 0)),
            pl.BlockSpec((1, N_EXP, d_ff, d), lambda i, g: (g, 0, 0, 0)),
            _const_spec(g2.shape), _const_spec(b2.shape),
        ],
        out_specs=pl.BlockSpec((tm, d), lambda i, g: (i, 0)),
        out_shape=jax.ShapeDtypeStruct((t, d), F32),
        scratch_shapes=[pltpu.VMEM((tm, d), F32)],
        compiler_params=_cparams(2),
        name=f"moe_ln_{tm}",
    )(x1, gate, wgu, wd, g2, b2)


def _split_bf16(x):
    hi = x.astype(BF16)
    return hi, (x - hi.astype(F32)).astype(BF16)


def _moe_sorted_kernel(cnt_ref, off_ref, x1_ref, gate_ref, drow_ref, dcol_ref, wgu_ref, wd_ref, g2_ref, b2_ref,
                       out_ref, xs_ref, gs_ref, ys_ref, *, alpha, d_ff):
    i = pl.program_id(0)
    gi = pl.program_id(1)
    half = pl.program_id(2)
    n_half = N_EXP // EXPERT_SPLIT
    tmoe = x1_ref.shape[0]

    @pl.when((i == 0) & (gi == 0) & (half == 0))
    def _():
        xs_ref[...] = jnp.zeros(xs_ref.shape, BF16)
        gs_ref[...] = jnp.zeros(gs_ref.shape, F32)
        ys_ref[...] = jnp.zeros(ys_ref.shape, F32)

    @pl.when((gi == 0) & (half == 0))
    def _():
        perm = (lax.broadcasted_iota(jnp.int32, (PERM_ROWS, tmoe), 0) == drow_ref[0]).astype(BF16)
        xs_ref[0:PERM_ROWS, :] = _mm(perm, x1_ref[...].astype(BF16)).astype(BF16)
        g_hi, g_lo = _split_bf16(gate_ref[...])
        gs_ref[0:PERM_ROWS, :] = _mm(perm, g_hi) + _mm(perm, g_lo)

    n = cnt_ref[i * N_GROUPS + gi]
    off = off_ref[i * N_GROUPS + gi]

    def run_pass(c, carry):
        base = pl.multiple_of(off + c * MOE_CHUNK, SEG_ALIGN)
        xc = xs_ref[pl.ds(base, MOE_CHUNK), :]
        gc = gs_ref[pl.ds(base, MOE_CHUNK), :]
        gc = jnp.where(half == 0, gc, pltpu.roll(gc, LANE - n_half, axis=1))
        y = jnp.zeros((MOE_CHUNK, out_ref.shape[1]), F32)
        for e in range(n_half):
            au = _mm(xc, wgu_ref[0, e])
            h = jax.nn.silu(au[:, 0:d_ff]) * au[:, d_ff:2 * d_ff]
            y = y + _mm((h * gc[:, e:e + 1]).astype(BF16), wd_ref[0, e])

        @pl.when(half == 0)
        def _():
            ys_ref[pl.ds(base, MOE_CHUNK), :] = y

        @pl.when(half != 0)
        def _():
            ys_ref[pl.ds(base, MOE_CHUNK), :] = ys_ref[pl.ds(base, MOE_CHUNK), :] + y

        return carry

    lax.fori_loop(0, (n + MOE_CHUNK - 1) // MOE_CHUNK, run_pass, 0)

    @pl.when((gi == N_GROUPS - 1) & (half == EXPERT_SPLIT - 1))
    def _():
        unperm = (dcol_ref[...] == lax.broadcasted_iota(jnp.int32, (tmoe, PERM_ROWS), 1)).astype(BF16)
        y_hi, y_lo = _split_bf16(ys_ref[0:PERM_ROWS, :])
        moe = _mm(unperm, y_hi) + _mm(unperm, y_lo)
        out_ref[...] = _layernorm(alpha * x1_ref[...] + moe, g2_ref[...], b2_ref[...])


def _moe_sorted(x1, gate, wgu, wd, g2, b2, alpha):
    t, d = x1.shape
    d_ff = wd.shape[2]
    n_tiles = t // TMOE
    n_half = N_EXP // EXPERT_SPLIT
    gid = gate[:, GROUP_LANE].astype(jnp.int32).reshape(n_tiles, TMOE)
    onehot = (gid[:, :, None] == jnp.arange(N_GROUPS)[None, None, :]).astype(jnp.int32)
    cnt = jnp.sum(onehot, axis=1)
    rank = jnp.sum((jnp.cumsum(onehot, axis=1) - onehot) * onehot, axis=2)
    seg = (cnt + SEG_ALIGN - 1) // SEG_ALIGN * SEG_ALIGN
    off = jnp.cumsum(seg, axis=1) - seg
    dest = jnp.sum(onehot * off[:, None, :], axis=2) + rank
    assert N_GROUPS * (SEG_ALIGN - 1) <= PERM_ROWS - TMOE
    wgu_h = wgu.reshape(N_GROUPS * EXPERT_SPLIT, n_half, d, 2 * d_ff)
    wd_h = wd.reshape(N_GROUPS * EXPERT_SPLIT, n_half, d_ff, d)
    grid_spec = pltpu.PrefetchScalarGridSpec(
        num_scalar_prefetch=2,
        grid=(n_tiles, N_GROUPS, EXPERT_SPLIT),
        in_specs=[
            pl.BlockSpec((TMOE, d), lambda i, g, h, *_: (i, 0)),
            pl.BlockSpec((TMOE, LANE), lambda i, g, h, *_: (i, 0)),
            pl.BlockSpec((1, 1, TMOE), lambda i, g, h, *_: (i, 0, 0)),
            pl.BlockSpec((TMOE, 1), lambda i, g, h, *_: (i, 0)),
            pl.BlockSpec((1, n_half, d, 2 * d_ff), lambda i, g, h, *_: (g * EXPERT_SPLIT + h, 0, 0, 0)),
            pl.BlockSpec((1, n_half, d_ff, d), lambda i, g, h, *_: (g * EXPERT_SPLIT + h, 0, 0, 0)),
            pl.BlockSpec(g2.shape, lambda *_: (0, 0)), pl.BlockSpec(b2.shape, lambda *_: (0, 0)),
        ],
        out_specs=pl.BlockSpec((TMOE, d), lambda i, g, h, *_: (i, 0)),
        scratch_shapes=[pltpu.VMEM((SORT_ROWS, d), BF16), pltpu.VMEM((SORT_ROWS, LANE), F32),
                        pltpu.VMEM((SORT_ROWS, d), F32)],
    )
    return pl.pallas_call(
        functools.partial(_moe_sorted_kernel, alpha=alpha, d_ff=d_ff),
        grid_spec=grid_spec,
        out_shape=jax.ShapeDtypeStruct((t, d), F32),
        compiler_params=_cparams(3),
        name="moe_sorted_ln",
    )(cnt.reshape(-1), off.reshape(-1), x1, gate, dest.reshape(n_tiles, 1, TMOE), dest.reshape(t, 1), wgu_h, wd_h,
      g2, b2)


def _pad_head_lanes(v):
    vh = v.reshape(N_HEADS, HEAD_DIM)
    z = jnp.zeros_like(vh)
    grp = (jnp.arange(N_HEADS) // GQ)[:, None]
    return jnp.where(grp == 0, jnp.concatenate([vh, z], axis=1), jnp.concatenate([z, vh], axis=1)).reshape(1, -1)


def _pad_head_rows(w):
    wh = w.reshape(N_HEADS, HEAD_DIM, -1)
    z = jnp.zeros_like(wh)
    grp = (jnp.arange(N_HEADS) // GQ)[:, None, None]
    return jnp.where(grp == 0, jnp.concatenate([wh, z], axis=1), jnp.concatenate([z, wh], axis=1)).reshape(
        N_HEADS * LANE, -1)


def _front_pad(a, axis, pad_row):
    shape = list(a.shape)
    shape[axis] = WINDOW
    return jnp.concatenate([jnp.broadcast_to(pad_row.astype(a.dtype), shape), a], axis=axis)


def kernel(x_prompt, x_sample, cache_kv, page_table, cache_win, state_conv, w_in, conv_w, pe_k, pe_v, w_ck1, w_ck2, w_cv1, w_cv2, g_nsa, g_conv, w_out, ln1_g, ln1_b, w_rg, b_rg, w_re, b_re, w_eg, w_eu, w_ed, ln2_g, ln2_b, rel_bias):
    batch, seq, d_model = x_prompt.shape
    dec_batch, dec_seq = x_sample.shape[0], x_sample.shape[1]
    depth = w_in.shape[0]
    n_pages = page_table.shape[1]
    past = n_pages * PAGE_SIZE
    win_buf = cache_win.shape[2]
    d_nsa = N_HEADS * HEAD_DIM
    d_conv = w_out.shape[1] - d_nsa
    assert dec_seq == 1 and seq % TM == 0 and min(WINDOW, seq) == TM and win_buf == WINDOW
    assert d_conv == 512 and seq // CMP_STRIDE == LANE and n_pages % SAMPLE_PAGES == 0
    alpha = (2 * depth) ** 0.25
    n_slc_s = -(-(past + dec_seq) // SLC_LEN)
    qblk_s = past // SLC_LEN

    idx_np, sp_p_np, sp_s_np = _static_tables(seq, past, win_buf)
    tabs = _bias_tables(rel_bias, jnp.asarray(idx_np))
    nq = seq // TQ
    near_t = jnp.transpose(tabs[0:2], (1, 0, 2, 3)).reshape(N_HEADS, 2 * LANE, LANE)
    up_t = tabs[2, 0]
    tab_c = tabs[_N_NEAR_TILES:_N_NEAR_TILES + nq]
    smp = tabs[_N_NEAR_TILES + nq]
    tab_w_s = smp[:, 0:5, :].reshape(N_HEADS, 5 * LANE)[:, :win_buf]
    tab_c_s = smp[:, 5:13, :].reshape(N_HEADS, 8 * LANE)
    reps = N_SELECT * PAGE_SIZE // SLC_LEN
    t_near = jnp.tile(smp[:, 13, 0:SLC_LEN], (1, reps))
    t_last = jnp.tile(smp[:, 13, SLC_LEN:2 * SLC_LEN], (1, reps))
    rb0_rep = jnp.broadcast_to(smp[:, 14, 0:1], (N_HEADS, LANE))
    sp_p = jnp.asarray(sp_p_np, BF16)
    sp_s = jnp.asarray(sp_s_np, BF16)
    key_pad = jnp.zeros((LANE,), F32).at[PAD_LANE].set(KEY_MASK)
    val_pad = jnp.zeros((LANE,), F32)

    xp = x_prompt.reshape(batch * seq, d_model)
    xs = x_sample.reshape(dec_batch * dec_seq, d_model)
    outs = [[] for _ in range(6)]
    for l in range(depth):
        w_pack = _pack_in_weights(w_in[l])
        gc = g_conv[l].reshape(1, -1)
        gn_pad = _pad_head_lanes(g_nsa[l])
        w1k, w2k, pek = _pack_compress_weights(pe_k[l], w_ck1[l], w_ck2[l], True)
        w1v, w2v, pev = _pack_compress_weights(pe_v[l], w_cv1[l], w_cv2[l], False)
        w1 = jnp.stack([w1k, w1v])
        pe = jnp.stack([pek, pev])
        wa = _pad_head_rows(w_out[l][:d_nsa]).astype(BF16)
        wb = w_out[l][d_nsa:].astype(BF16)
        wr = jnp.concatenate(
            [jnp.pad(w_rg[l], ((0, 0), (0, LANE - N_GROUPS)))]
            + [jnp.pad(w_re[l][:, gi * N_EXP:(gi + 1) * N_EXP], ((0, 0), (0, LANE - N_EXP))) for gi in range(N_GROUPS)],
            axis=1).astype(BF16)
        br = jnp.concatenate(
            [jnp.pad(b_rg[l], (0, LANE - N_GROUPS))]
            + [jnp.pad(b_re[l][gi * N_EXP:(gi + 1) * N_EXP], (0, LANE - N_EXP)) for gi in range(N_GROUPS)]).reshape(1, -1)
        wgu = jnp.concatenate([w_eg[l], w_eu[l]], axis=-1).astype(BF16)
        wd = w_ed[l].astype(BF16)
        g1, b1 = ln1_g[l].reshape(1, -1), ln1_b[l].reshape(1, -1)
        g2, b2 = ln2_g[l].reshape(1, -1), ln2_b[l].reshape(1, -1)

        q_p, kvt_p, ks, sv, kw, wv, wtail_t, gates_p, zn_p, ctail = _inproj_prompt(xp, w_pack, conv_w[l], gc, batch, seq)
        kvt_p = kvt_p.reshape(batch, 4, N_KV, HEAD_DIM, seq)
        cmpk_p, cmpvt_p = _compress_prompt(kvt_p, batch, seq, w1, w2k, w2v, pe)
        ks = _front_pad(ks.reshape(N_KV, batch, seq, LANE), 2, key_pad)
        kw = _front_pad(kw.reshape(N_KV, batch, seq, LANE), 2, key_pad)
        sv = jnp.pad(sv, ((0, 0), (0, 0), (WINDOW, 0)))
        wv = jnp.pad(wv, ((0, 0), (0, 0), (WINDOW, 0)))
        gn_rep = jnp.broadcast_to(gn_pad.reshape(-1, 1), (N_HEADS * LANE, TQ))
        on_p = _attn_prompt(q_p, gates_p, cmpk_p, cmpvt_p, ks, sv, kw, wv, near_t, up_t, tab_c, sp_p, gn_rep,
                            batch, seq)
        x1_p, gate_p = _proj(on_p, zn_p, xp, wa, wb, g1, b1, wr, br, alpha, TM)
        y_p = _moe_sorted(x1_p, gate_p, wgu, wd, g2, b2, alpha)

        st = state_conv[l]
        qlo_s, qgl_s, kv_s, win_s, gates_s, zn_s, u_s = _inproj_sample(xs, w_pack, conv_w[l], gc, st[:, 0], st[:, 1])
        cache_t = jnp.transpose(cache_kv[l], (0, 2, 3, 4, 1))
        cwin_t = jnp.transpose(cache_win[l], (0, 2, 3, 4, 1))
        cmpk_s, cmpvt_s = _compress_sample(cache_t, page_table, w1, w2k, w2v, pe)
        qlo_s3 = qlo_s.reshape(dec_batch, N_HEADS, LANE)
        qgl_s3 = qgl_s.reshape(dec_batch, N_HEADS, LANE)
        ocmp, owin, sel_idx = _sample_cw(qlo_s3, qgl_s3, cmpk_s, cmpvt_s, cwin_t, win_s.reshape(dec_batch, 1, 256), tab_c_s,
                                         tab_w_s, rb0_rep, sp_s, n_slc_s, qblk_s)
        bids = sel_idx[:, :, :N_SELECT].reshape(dec_batch, N_KV * N_SELECT)
        blk_pages = jnp.take_along_axis(page_table, jnp.minimum(bids // 2, n_pages - 1), axis=1)
        gates_s3 = jnp.pad(gates_s.reshape(dec_batch, N_KV, LANE)[:, :, :3 * GQ].reshape(dec_batch, N_HEADS, 3),
                           ((0, 0), (0, 0), (0, LANE - 3)))
        on_s = _sample_slc(blk_pages, bids % 2, bids, cache_t, qgl_s3, kv_s.reshape(dec_batch, 1, 512), ocmp, owin,
                           gates_s3, t_near, t_last, rb0_rep, gn_pad, n_slc_s - 1, qblk_s - 2)
        on_s = on_s.reshape(dec_batch, N_HEADS * LANE)
        x1_s, gate_s = _proj(on_s, zn_s, xs, wa, wb, g1, b1, wr, br, alpha, dec_batch)
        y_s = _moe(x1_s, gate_s, wgu, wd, g2, b2, alpha, dec_batch)

        to_token_major = lambda a: jnp.transpose(a, (0, 4, 1, 2, 3))
        outs[0].append(to_token_major(kvt_p))
        outs[1].append(kv_s.reshape(dec_batch, dec_seq, 4, N_KV, HEAD_DIM))
        outs[2].append(to_token_major(wtail_t.reshape(batch, 2, N_KV, HEAD_DIM, TM)))
        win_all_t = jnp.concatenate([cwin_t, win_s.reshape(dec_batch, 2, N_KV, HEAD_DIM, dec_seq)], axis=-1)
        outs[3].append(to_token_major(win_all_t[..., win_all_t.shape[-1] - min(WINDOW, past + dec_seq):]))
        outs[4].append(ctail)
        outs[5].append(jnp.concatenate([st, u_s[:, None, :]], axis=1)[:, dec_seq:])
        xp, xs = y_p, y_s
    return (xp.reshape(batch, seq, d_model), xs.reshape(dec_batch, dec_seq, d_model),
            jnp.stack(outs[0]), jnp.stack(outs[1]), jnp.stack(outs[2]), jnp.stack(outs[3]),
            jnp.stack(outs[4]), jnp.stack(outs[5]))
```

```python
import functools
import math

import numpy as np
import jax
import jax.numpy as jnp
from jax import lax
from jax.experimental import pallas as pl
from jax.experimental.pallas import tpu as pltpu

F32 = jnp.float32
BF16 = jnp.bfloat16

HEAD_DIM = 64
N_KV = 2
GQ = 4
N_HEADS = N_KV * GQ
CONV_W = 3
CMP_STRIDE = 16
CMP_LEN = 32
CMP_HIDDEN = 2 * HEAD_DIM
SLC_LEN = 64
N_SELECT = 16
WINDOW = 512
N_BUCKETS = 32
MAX_DISTANCE = 128
N_GROUPS = 4
N_EXP = 8
PAGE_SIZE = 128
NEG = -1e30
FORCE = 1e6

LANE = 128
SUBLANE = 8
VMEM_LIMIT = 52 * 1024 * 1024

TQ = 128
TM = 512
FAR = 512
TAIL = WINDOW + TQ
SAMPLE_PAGES = 64
TMOE = 1024
MOE_CHUNK = 320
SEG_ALIGN = 16
PERM_ROWS = TMOE + LANE
SORT_ROWS = PERM_ROWS + MOE_CHUNK
EXPERT_SPLIT = 2
GROUP_LANE = N_EXP

KEY_MASK = 2.0 ** 100
TABLE_MASK = 2 * NEG
MASKED_BUCKET = N_BUCKETS
SEL_LANE0 = HEAD_DIM
PAD_LANE = HEAD_DIM + 32


def _cparams(n_axes):
    return pltpu.CompilerParams(dimension_semantics=("arbitrary",) * n_axes, vmem_limit_bytes=VMEM_LIMIT)


def _const_spec(shape):
    nd = len(shape)
    return pl.BlockSpec(shape, lambda *_, nd=nd: (0,) * nd)


def _qk(a, b):
    return lax.dot_general(a, b, (((1,), (1,)), ((), ())), preferred_element_type=F32)


def _mm(a, b):
    return jnp.dot(a, b, preferred_element_type=F32)


def _bucket_np(dist):
    n = np.maximum(dist, 0)
    max_exact = N_BUCKETS // 2
    nf = np.maximum(n, 1).astype(np.float32)
    large = max_exact + (np.log(nf / np.float32(max_exact)) / np.float32(math.log(MAX_DISTANCE / max_exact))
                         * np.float32(N_BUCKETS - max_exact)).astype(np.int32)
    large = np.minimum(large, N_BUCKETS - 1)
    return np.where(n < max_exact, n, large).astype(np.int32)


def _overlap_np(c, s):
    c0 = c * CMP_STRIDE
    s0 = s * SLC_LEN
    return np.maximum(np.minimum(c0 + CMP_LEN, s0 + SLC_LEN) - np.maximum(c0, s0), 0)


_N_NEAR_TILES = 3


def _static_tables(seq, past, win_buf):
    nq = seq // TQ
    i = np.arange(LANE)[:, None]
    j = np.arange(LANE)[None, :]
    far_bucket = N_BUCKETS - 1
    tiles = [_bucket_np(LANE + j - i),
             np.where(j >= i, _bucket_np(j - i), MASKED_BUCKET),
             np.where(i > j, far_bucket, MASKED_BUCKET) + 0 * i]
    tiles += [_bucket_np(TQ * q + j - (CMP_STRIDE * i + CMP_STRIDE - 1)) for q in range(nq)]
    smp = np.full((LANE, LANE), 10 * MAX_DISTANCE, np.int64)
    k = np.arange(5 * LANE)
    smp[0:5] = np.where(k < win_buf, win_buf - k, 0).reshape(5, LANE)
    ci = np.arange(8 * LANE)
    smp[5:13] = np.maximum(past - (CMP_STRIDE * ci + CMP_STRIDE - 1), 0).reshape(8, LANE)
    nb = past // SLC_LEN
    smp[13] = past - (SLC_LEN * (nb - 2) + np.arange(LANE))
    smp[14] = 0
    tiles.append(_bucket_np(smp))
    idx = np.stack(tiles).astype(np.int32)

    def sp(nrows, ncols, n_slc):
        r = np.arange(nrows)[:, None]
        s = np.arange(ncols)[None, :]
        ov = _overlap_np(r - 1, s)
        return np.where((r >= 1) & (s < n_slc), ov, 0).astype(np.float32)

    sp_p = sp(seq // CMP_STRIDE, LANE, -(-seq // SLC_LEN)).T
    n_slc_s = -(-(past + 1) // SLC_LEN)
    sp_s = sp(past // CMP_STRIDE, 3 * LANE, n_slc_s)
    return idx, sp_p, sp_s


def _bias_kernel(rb_ref, idx_ref, out_ref):
    idx = idx_ref[0]
    accs = [jnp.full((LANE, LANE), TABLE_MASK, F32) for _ in range(N_HEADS)]
    for b in range(N_BUCKETS):
        hit = idx == b
        for h in range(N_HEADS):
            accs[h] = jnp.where(hit, rb_ref[b, h] - rb_ref[N_BUCKETS - 1, h], accs[h])
    for h in range(N_HEADS):
        out_ref[0, h] = accs[h]


def _bias_tables(rel_bias, idx):
    nt = idx.shape[0]
    return pl.pallas_call(
        _bias_kernel,
        grid=(nt,),
        in_specs=[pl.BlockSpec(memory_space=pltpu.SMEM),
                  pl.BlockSpec((1, LANE, LANE), lambda t: (t, 0, 0))],
        out_specs=pl.BlockSpec((1, N_HEADS, LANE, LANE), lambda t: (t, 0, 0, 0)),
        out_shape=jax.ShapeDtypeStruct((nt, N_HEADS, LANE, LANE), F32),
        compiler_params=_cparams(1),
        name="bias_tables",
    )(rel_bias, idx)


_Q0, _Q1 = 0, N_HEADS * HEAD_DIM
_KV0, _KV1 = _Q1, _Q1 + 512
_WN0, _WN1 = _KV1, _KV1 + 256
_GT0, _GT1 = _WN1, _WN1 + 2 * LANE
_CV0, _CV1 = _GT1, _GT1 + 3 * 512


def _pack_in_weights(w):
    d = w.shape[0]
    d_nsa = N_HEADS * HEAD_DIM
    wq_pad = w[:, :d_nsa]
    o = d_nsa
    w_kv = w[:, o:o + 512]
    w_win = w[:, o + 512:o + 768]
    wg = w[:, o + 768:o + 768 + 3 * N_HEADS].reshape(d, N_KV, 3 * GQ)
    wg_pad = jnp.pad(wg, ((0, 0), (0, 0), (0, LANE - 3 * GQ))).reshape(d, N_KV * LANE)
    w_conv = w[:, o + 768 + 3 * N_HEADS:]
    return jnp.concatenate([wq_pad, w_kv, w_win, wg_pad, w_conv], axis=1).astype(BF16)


def _rms(x, gain, n, eps=1e-6):
    ms = jnp.sum(x * x, axis=-1, keepdims=True) / n
    return x * lax.rsqrt(ms + eps) * gain


def _layernorm(y, gain, bias, eps=1e-5):
    mu = jnp.mean(y, axis=-1, keepdims=True)
    d = y - mu
    var = jnp.mean(d * d, axis=-1, keepdims=True)
    return d * lax.rsqrt(var + eps) * gain + bias


def _inproj_prompt_kernel(x_ref, w_ref, cw_ref, gc_ref,
                          q_ref, kv_ref, ks_ref, sv_ref, kw_ref, wv_ref, wtail_ref, gate_ref, zn_ref, ctail_ref,
                          uext_ref, *, tiles_per_batch):
    i = pl.program_id(0)
    tm = x_ref.shape[0]
    xb = x_ref[...].astype(BF16)

    def seg(a, b):
        return _mm(xb, w_ref[:, a:b])

    qp_t = (seg(_Q0, _Q1) * (HEAD_DIM ** -0.5)).T
    q_fill = jnp.zeros((LANE - HEAD_DIM, tm), F32)
    for h in range(N_HEADS):
        q_ref[h] = jnp.concatenate([qp_t[h * HEAD_DIM:(h + 1) * HEAD_DIM, :], q_fill], axis=0).astype(BF16)
    kv = seg(_KV0, _KV1)
    kv_t = kv.T
    kv_ref[0] = kv_t
    win = seg(_WN0, _WN1)
    win_t = win.T
    wtail_ref[0] = win_t
    pos = (i % tiles_per_batch) * tm + lax.broadcasted_iota(jnp.int32, (tm, LANE - HEAD_DIM), 0)
    lane = lax.broadcasted_iota(jnp.int32, (tm, LANE - HEAD_DIM), 1)
    blk_flag = jnp.where(lane == lax.shift_right_logical(pos, 6), KEY_MASK, 0.0).astype(BF16)
    no_flag = jnp.zeros((tm, LANE - HEAD_DIM), BF16)
    for g in range(N_KV):
        ks_ref[g] = jnp.concatenate([kv[:, 256 + g * HEAD_DIM:256 + (g + 1) * HEAD_DIM].astype(BF16), blk_flag], axis=1)
        kw_ref[g] = jnp.concatenate([win[:, g * HEAD_DIM:(g + 1) * HEAD_DIM].astype(BF16), no_flag], axis=1)
    sv_ref[0] = kv_t[384:512, :].astype(BF16)
    wv_ref[0] = win_t[LANE:2 * LANE, :].astype(BF16)
    gt_t = jax.nn.sigmoid(seg(_GT0, _GT1)).T
    gate_ref[0] = gt_t[0:LANE, :]
    gate_ref[1] = gt_t[LANE:2 * LANE, :]

    conv = seg(_CV0, _CV1)
    cb = conv[:, 0:512]
    u = conv[:, 512:1024] * conv[:, 1024:1536]
    first = (i % tiles_per_batch) == 0

    @pl.when(first)
    def _():
        uext_ref[0:SUBLANE, :] = jnp.zeros((SUBLANE, 512), F32)

    @pl.when(jnp.logical_not(first))
    def _():
        uext_ref[0:SUBLANE, :] = uext_ref[tm:tm + SUBLANE, :]

    uext_ref[SUBLANE:tm + SUBLANE, :] = u
    um1 = uext_ref[SUBLANE - 1:tm + SUBLANE - 1, :]
    um2 = uext_ref[SUBLANE - 2:tm + SUBLANE - 2, :]
    z = cb * (um2 * cw_ref[0:1, :] + um1 * cw_ref[1:2, :] + u * cw_ref[2:3, :])
    zn_ref[...] = _rms(z, gc_ref[...], 512).astype(BF16)
    ctail_ref[0] = uext_ref[tm + SUBLANE - 2:tm + SUBLANE, :]


def _inproj_prompt(x, w, conv_w, g_conv, batch, seq):
    t, d = x.shape
    n_tiles = t // TM
    tpb = seq // TM
    outs = (
        jax.ShapeDtypeStruct((N_HEADS, LANE, t), BF16),
        jax.ShapeDtypeStruct((batch, 512, seq), F32),
        jax.ShapeDtypeStruct((N_KV, t, LANE), BF16),
        jax.ShapeDtypeStruct((batch, LANE, seq), BF16),
        jax.ShapeDtypeStruct((N_KV, t, LANE), BF16),
        jax.ShapeDtypeStruct((batch, LANE, seq), BF16),
        jax.ShapeDtypeStruct((batch, 256, TM), F32),
        jax.ShapeDtypeStruct((N_KV, LANE, t), F32),
        jax.ShapeDtypeStruct((t, 512), BF16),
        jax.ShapeDtypeStruct((batch, CONV_W - 1, 512), F32),
    )
    row = lambda n: pl.BlockSpec((TM, n), lambda i: (i, 0))
    planes = lambda n: pl.BlockSpec((n, TM, LANE), lambda i: (0, i, 0))
    planes_t = lambda n: pl.BlockSpec((n, LANE, TM), lambda i: (0, 0, i))
    per_batch_t = pl.BlockSpec((1, LANE, TM), lambda i: (i // tpb, 0, i % tpb))
    return pl.pallas_call(
        functools.partial(_inproj_prompt_kernel, tiles_per_batch=tpb),
        grid=(n_tiles,),
        in_specs=[row(d), _const_spec(w.shape), _const_spec(conv_w.shape), _const_spec(g_conv.shape)],
        out_specs=(
            planes_t(N_HEADS),
            pl.BlockSpec((1, 512, TM), lambda i: (i // tpb, 0, i % tpb)),
            planes(N_KV), per_batch_t, planes(N_KV), per_batch_t,
            pl.BlockSpec((1, 256, TM), lambda i: (i // tpb, 0, 0)),
            planes_t(N_KV),
            row(512),
            pl.BlockSpec((1, CONV_W - 1, 512), lambda i: (i // tpb, 0, 0)),
        ),
        out_shape=outs,
        scratch_shapes=[pltpu.VMEM((TM + SUBLANE, 512), F32)],
        compiler_params=_cparams(1),
        name="inproj_prompt",
    )(x, w, conv_w, g_conv)


def _inproj_sample_kernel(x_ref, w_ref, cw_ref, gc_ref, s0_ref, s1_ref,
                          qlo_ref, qgl_ref, kv_ref, win_ref, gate_ref, zn_ref, u_ref):
    xb = x_ref[...].astype(BF16)

    def seg(a, b):
        return _mm(xb, w_ref[:, a:b])

    qp = seg(_Q0, _Q1) * (HEAD_DIM ** -0.5)
    fill = jnp.zeros((qp.shape[0], LANE - HEAD_DIM), F32)
    heads = [qp[:, h * HEAD_DIM:(h + 1) * HEAD_DIM] for h in range(N_HEADS)]
    qlo_ref[...] = jnp.concatenate([x for qh in heads for x in (qh, fill)], axis=1).astype(BF16)
    qgl_ref[...] = jnp.concatenate([x for h, qh in enumerate(heads) for x in ((qh, fill) if h < GQ else (fill, qh))],
                                   axis=1).astype(BF16)
    kv_ref[...] = seg(_KV0, _KV1)
    win_ref[...] = seg(_WN0, _WN1)
    gate_ref[...] = jax.nn.sigmoid(seg(_GT0, _GT1))
    conv = seg(_CV0, _CV1)
    cb = conv[:, 0:512]
    u = conv[:, 512:1024] * conv[:, 1024:1536]
    z = cb * (s0_ref[...] * cw_ref[0:1, :] + s1_ref[...] * cw_ref[1:2, :] + u * cw_ref[2:3, :])
    zn_ref[...] = _rms(z, gc_ref[...], 512).astype(BF16)
    u_ref[...] = u


def _inproj_sample(x, w, conv_w, g_conv, s0, s1):
    n = x.shape[0]
    outs = (
        jax.ShapeDtypeStruct((n, N_HEADS * LANE), BF16),
        jax.ShapeDtypeStruct((n, N_HEADS * LANE), BF16),
        jax.ShapeDtypeStruct((n, 512), F32),
        jax.ShapeDtypeStruct((n, 256), F32),
        jax.ShapeDtypeStruct((n, N_KV * LANE), F32),
        jax.ShapeDtypeStruct((n, 512), BF16),
        jax.ShapeDtypeStruct((n, 512), F32),
    )
    args = (x, w, conv_w, g_conv, s0, s1)
    return pl.pallas_call(
        _inproj_sample_kernel,
        grid=(1,),
        in_specs=[_const_spec(a.shape) for a in args],
        out_specs=tuple(_const_spec(o.shape) for o in outs),
        out_shape=outs,
        compiler_params=_cparams(1),
        name="inproj_sample",
    )(*args)


def _pack_compress_weights(pe, w1, w2, per_group_out):
    w = w1.reshape(2, CMP_STRIDE, HEAD_DIM, CMP_HIDDEN).transpose(1, 2, 0, 3).reshape(CMP_STRIDE, HEAD_DIM, 2 * CMP_HIDDEN)
    z = jnp.zeros_like(w)
    top = jnp.concatenate([w, z], axis=-1)
    bot = jnp.concatenate([z, w], axis=-1)
    w1bd = jnp.stack([top, bot], axis=1).reshape(CMP_STRIDE * 2 * HEAD_DIM, 4 * CMP_HIDDEN).astype(BF16)
    z2 = jnp.zeros_like(w2)
    if per_group_out:
        w2bd = jnp.concatenate([jnp.concatenate([w2, z2, z2, z2], axis=1),
                                jnp.concatenate([z2, z2, w2, z2], axis=1)], axis=0).astype(BF16)
    else:
        w2bd = jnp.concatenate([jnp.concatenate([w2, z2], axis=1), jnp.concatenate([z2, w2], axis=1)], axis=0).astype(BF16)
    pe_rows = jnp.broadcast_to(pe.reshape(2, CMP_STRIDE, 1, HEAD_DIM), (2, CMP_STRIDE, N_KV, HEAD_DIM))
    pe_rows = pe_rows.reshape(2, CMP_STRIDE * N_KV * HEAD_DIM)
    pe_rows = jnp.pad(pe_rows, ((0, SUBLANE - 2), (0, 0)))
    return w1bd, w2bd, pe_rows


def _compress_kernel(*refs, n_pages, n_prefetch):
    refs = refs[n_prefetch:]
    page_refs = refs[:n_pages]
    w1_ref, w2k_ref, w2v_ref, pe_ref, ck_ref, cvt_ref, carry_ref, tok_ref = refs[n_pages:]
    step = pl.program_id(1)
    m = n_pages * (PAGE_SIZE // CMP_STRIDE)

    @pl.when(step == 0)
    def _():
        carry_ref[...] = jnp.zeros(carry_ref.shape, F32)

    row0 = lax.broadcasted_iota(jnp.int32, (m, CMP_HIDDEN), 0) == 0
    for t, w2_ref in enumerate((w2k_ref, w2v_ref)):
        for p, pr in enumerate(page_refs):
            slab = pr[0, t].reshape(N_KV * HEAD_DIM, PAGE_SIZE)
            tok_ref[t, p * PAGE_SIZE:(p + 1) * PAGE_SIZE, :] = slab.T
        pieces = [tok_ref[t, pl.ds(j, m, stride=CMP_STRIDE), :] for j in range(CMP_STRIDE)]
        lhs = jnp.concatenate([jnp.concatenate(pieces, axis=1), pe_ref[t]], axis=0).astype(BF16)
        a = _mm(lhs, w1_ref[t])
        hs = []
        for g in range(N_KV):
            c0 = g * 2 * CMP_HIDDEN
            a0 = a[0:m, c0:c0 + CMP_HIDDEN]
            a1 = a[0:m, c0 + CMP_HIDDEN:c0 + 2 * CMP_HIDDEN]
            pe_term = a[m:m + 1, c0:c0 + CMP_HIDDEN] + a[m + 1:m + 2, c0 + CMP_HIDDEN:c0 + 2 * CMP_HIDDEN]
            prev = carry_ref[t, 0:1, g * CMP_HIDDEN:(g + 1) * CMP_HIDDEN]
            shifted = jnp.where(row0, prev, pltpu.roll(a0, 1, axis=0))
            carry_ref[t, 0:1, g * CMP_HIDDEN:(g + 1) * CMP_HIDDEN] = a0[m - 1:m, :]
            hs.append(jax.nn.gelu(shifted + a1 + pe_term))
        out = _mm(jnp.concatenate(hs, axis=1).astype(BF16), w2_ref[...])
        if t == 0:
            ck_ref[0, 0] = out[:, 0:LANE].astype(BF16)
            ck_ref[0, 1] = out[:, LANE:2 * LANE].astype(BF16)
        else:
            cvt_ref[0] = out.T.astype(BF16)


def _compress_call(page_specs, page_args, prefetch, grid, batch, n_chunks, m, w1, w2k, w2v, pe, name):
    n_pages = len(page_specs)
    n_pf = len(prefetch)
    cspec = lambda shape: pl.BlockSpec(shape, lambda *_: (0,) * len(shape))
    grid_spec = pltpu.PrefetchScalarGridSpec(
        num_scalar_prefetch=n_pf,
        grid=grid,
        in_specs=list(page_specs) + [cspec(w1.shape), cspec(w2k.shape), cspec(w2v.shape), cspec(pe.shape)],
        out_specs=(pl.BlockSpec((1, N_KV, m, LANE), lambda b, s, *_: (b, 0, s, 0)),
                   pl.BlockSpec((1, LANE, m), lambda b, s, *_: (b, 0, s))),
        scratch_shapes=[pltpu.VMEM((2, SUBLANE, 2 * CMP_HIDDEN), F32),
                        pltpu.VMEM((2, n_pages * PAGE_SIZE, N_KV * HEAD_DIM), F32)],
    )
    return pl.pallas_call(
        functools.partial(_compress_kernel, n_pages=n_pages, n_prefetch=n_pf),
        grid_spec=grid_spec,
        out_shape=(jax.ShapeDtypeStruct((batch, N_KV, n_chunks, LANE), BF16),
                   jax.ShapeDtypeStruct((batch, LANE, n_chunks), BF16)),
        compiler_params=_cparams(2),
        name=name,
    )(*prefetch, *page_args, w1, w2k, w2v, pe)


_PAGE_BLOCK = (1, 2, N_KV, HEAD_DIM, PAGE_SIZE)


def _compress_prompt(kv_t, batch, seq, w1, w2k, w2v, pe):
    n_pages = seq // PAGE_SIZE
    rows = PAGE_SIZE // CMP_STRIDE
    specs = [pl.BlockSpec(_PAGE_BLOCK, lambda b, s, p=p: (b, 0, 0, 0, p)) for p in range(n_pages)]
    return _compress_call(specs, [kv_t] * n_pages, (), (batch, 1), batch, n_pages * rows, n_pages * rows,
                          w1, w2k, w2v, pe, "compress_prompt")


def _compress_sample(cache_t, page_table, w1, w2k, w2v, pe):
    batch, n_pages = page_table.shape
    rows = PAGE_SIZE // CMP_STRIDE
    specs = [pl.BlockSpec(_PAGE_BLOCK, lambda b, s, pt, k=k: (pt[b, s * SAMPLE_PAGES + k], 0, 0, 0, 0))
             for k in range(SAMPLE_PAGES)]
    return _compress_call(specs, [cache_t] * SAMPLE_PAGES, (page_table,), (batch, n_pages // SAMPLE_PAGES), batch,
                          n_pages * rows, SAMPLE_PAGES * rows, w1, w2k, w2v, pe, "compress_sample")


def _tile4(x):
    return jnp.concatenate([x] * GQ, axis=1)


def _softmax_fold(state, pieces, vt):
    m, l, acc = state
    m_new = m
    for pc in pieces:
        m_new = jnp.maximum(m_new, jnp.max(pc, axis=0, keepdims=True))
    alpha = jnp.exp(m - m_new)
    es = [jnp.exp(pc - m_new) for pc in pieces]
    l = alpha * l
    for e in es:
        l = l + jnp.sum(e, axis=0, keepdims=True)
    e_all = es[0] if len(es) == 1 else jnp.concatenate(es, axis=0)
    acc = alpha * acc + _mm(vt, e_all.astype(BF16))
    return m_new, l, acc


def _attn_prompt_kernel(q_ref, gate_ref, ck_ref, cvt_ref, ks_ref, sv_ref, kw_ref, wv_ref,
                        near_ref, up_ref, tabc_ref, sp_ref, gn_ref, o_ref, *, n_slc):
    qi = pl.program_id(1)
    q0 = pl.multiple_of(qi * TQ, TQ)
    krow = lax.broadcasted_iota(jnp.int32, (LANE, TQ), 0)
    qcol = lax.broadcasted_iota(jnp.int32, (LANE, TQ), 1)
    cmp_ok4 = _tile4((krow >= 1) & (CMP_STRIDE * krow + (CMP_STRIDE - 1) <= q0 + qcol))
    sidx = lax.broadcasted_iota(jnp.int32, (n_slc, TQ), 0)
    qblk = lax.shift_right_logical(q0 + lax.broadcasted_iota(jnp.int32, (n_slc, TQ), 1), 6)
    sel_valid = sidx <= qblk
    sel_forced = (sidx == 0) | (sidx == qblk) | (sidx == qblk - 1)
    n_far = lax.shift_right_logical(jnp.maximum(qi - 1, 0), 2)
    up4 = _tile4(up_ref[...])
    cols = GQ * TQ
    fresh = (jnp.full((1, cols), NEG, F32), jnp.zeros((1, cols), F32), jnp.zeros((LANE, cols), F32))

    groups = range(N_KV)
    qa, near, o_cmp = [], [], []
    for g in groups:
        heads = [GQ * g + r for r in range(GQ)]
        q_t = [q_ref[h] for h in heads]
        near.append(jnp.concatenate([near_ref[h] for h in heads], axis=1))

        bias_c = jnp.concatenate([tabc_ref[0, h] for h in heads], axis=1)
        s = jnp.where(cmp_ok4, _mm(ck_ref[0, g], jnp.concatenate(q_t, axis=1)) + bias_c, NEG)
        e = jnp.exp(s - jnp.max(s, axis=0, keepdims=True))
        p = (e / jnp.sum(e, axis=0, keepdims=True) * cmp_ok4.astype(F32)).astype(BF16)
        o_cmp.append(_mm(cvt_ref[0], p))
        imp4 = _mm(sp_ref[...], p)
        imp = imp4[:, 0:TQ] + imp4[:, TQ:2 * TQ] + imp4[:, 2 * TQ:3 * TQ] + imp4[:, 3 * TQ:4 * TQ]

        score = jnp.where(sel_valid, imp[0:n_slc, :] + jnp.where(sel_forced, FORCE, 0.0), -FORCE)
        rank = jnp.zeros((n_slc, TQ), F32)
        for k in range(1, n_slc):
            other = pltpu.roll(score, k, axis=0)
            rank = rank + jnp.where(sidx >= k, (other >= score).astype(F32), (other > score).astype(F32))
        sel_t = (rank < float(N_SELECT)).astype(F32)
        aug = jnp.concatenate([jnp.zeros((SEL_LANE0, TQ), F32), sel_t - 1.0,
                               jnp.zeros((LANE - SEL_LANE0 - n_slc, TQ), F32)], axis=0)
        aug = jnp.where(krow == PAD_LANE, -1.0, aug).astype(BF16)
        qa.append(jnp.concatenate([qh + aug for qh in q_t], axis=1))

    def far_tile(it, states):
        start = pl.multiple_of(q0 - FAR * (n_far - it), LANE)
        vt = sv_ref[0, :, pl.ds(start, FAR)]
        return tuple(_softmax_fold(states[g], [_mm(ks_ref[g, 0, pl.ds(start, FAR), :], qa[g])], vt) for g in groups)

    states = lax.fori_loop(0, n_far, far_tile, (fresh,) * N_KV)
    v_tail = sv_ref[0, :, pl.ds(q0, TAIL)]
    w_tail = wv_ref[0, :, pl.ds(q0, TAIL)]
    o_slc, o_win = [], []
    for g in groups:
        sc = _mm(ks_ref[g, 0, pl.ds(q0, TAIL), :], qa[g])
        pieces = [sc[0:TAIL - 2 * LANE], sc[TAIL - 2 * LANE:TAIL] + near[g]]
        _, l, acc = _softmax_fold(states[g], pieces, v_tail)
        o_slc.append(acc / l)
        sc = _mm(kw_ref[g, 0, pl.ds(q0, TAIL), :], qa[g])
        pieces = [sc[0:LANE] + up4, sc[LANE:TAIL - 2 * LANE], sc[TAIL - 2 * LANE:TAIL] + near[g]]
        _, l, acc = _softmax_fold(fresh, pieces, w_tail)
        o_win.append(acc / l)

    outs = []
    for g in groups:
        gt = gate_ref[g]
        own = lax.shift_right_logical(krow, 6) == g
        for r in range(GQ):
            cs = slice(r * TQ, (r + 1) * TQ)
            comb = gt[3 * r:3 * r + 1, :] * o_cmp[g][:, cs] + gt[3 * r + 1:3 * r + 2, :] * o_slc[g][:, cs] \
                + gt[3 * r + 2:3 * r + 3, :] * o_win[g][:, cs]
            outs.append(jnp.where(own, comb, 0.0))
    o = jnp.concatenate(outs, axis=0)
    ms = jnp.sum(o * o, axis=0, keepdims=True) / (N_HEADS * HEAD_DIM)
    o = o * lax.rsqrt(ms + 1e-6) * gn_ref[...]
    o_ref[...] = jnp.concatenate([o[h * LANE:(h + 1) * LANE, :].T for h in range(N_HEADS)], axis=1).astype(BF16)


def _attn_prompt(q, gates, cmpk, cmpv_t, ks, sv_t, kw, wv_t, near_t, up_t, tab_c, sp_t, gn_rep, batch, seq):
    nq = seq // TQ
    t = batch * seq
    rows = ks.shape[2]
    n_slc = seq // SLC_LEN
    n_cmp = seq // CMP_STRIDE
    assert SEL_LANE0 + n_slc <= PAD_LANE < LANE
    return pl.pallas_call(
        functools.partial(_attn_prompt_kernel, n_slc=n_slc),
        grid=(batch, nq),
        in_specs=[
            pl.BlockSpec((N_HEADS, LANE, TQ), lambda b, i: (0, 0, b * nq + i)),
            pl.BlockSpec((N_KV, LANE, TQ), lambda b, i: (0, 0, b * nq + i)),
            pl.BlockSpec((1, N_KV, n_cmp, LANE), lambda b, i: (b, 0, 0, 0)),
            pl.BlockSpec((1, LANE, n_cmp), lambda b, i: (b, 0, 0)),
            pl.BlockSpec((N_KV, 1, rows, LANE), lambda b, i: (0, b, 0, 0)),
            pl.BlockSpec((1, LANE, rows), lambda b, i: (b, 0, 0)),
            pl.BlockSpec((N_KV, 1, rows, LANE), lambda b, i: (0, b, 0, 0)),
            pl.BlockSpec((1, LANE, rows), lambda b, i: (b, 0, 0)),
            _const_spec(near_t.shape),
            _const_spec(up_t.shape),
            pl.BlockSpec((1, N_HEADS, LANE, LANE), lambda b, i: (i, 0, 0, 0)),
            _const_spec(sp_t.shape),
            _const_spec(gn_rep.shape),
        ],
        out_specs=pl.BlockSpec((TQ, N_HEADS * LANE), lambda b, i: (b * nq + i, 0)),
        out_shape=jax.ShapeDtypeStruct((t, N_HEADS * LANE), BF16),
        compiler_params=_cparams(2),
        name="attn_prompt",
    )(q, gates, cmpk, cmpv_t, ks, sv_t, kw, wv_t, near_t, up_t, tab_c, sp_t, gn_rep)


def _sample_cw_kernel(qlo_ref, qgl_ref, ck_ref, cvt_ref, cwin_ref, wnew_ref, tabc_ref, tabw_ref, rb0_ref, sp_ref,
                      ocmp_ref, owin_ref, idx_ref, *, n_slc, qblk, win_buf):
    qlo = qlo_ref[0]
    qgl = qgl_ref[0]
    n_cmp_rows = ck_ref.shape[2]
    col = lax.broadcasted_iota(jnp.int32, (N_HEADS, n_cmp_rows), 1)
    grp_of_head = lax.shift_right_logical(lax.broadcasted_iota(jnp.int32, (N_HEADS, n_cmp_rows), 0), 2)
    ok = col >= 1
    raw = jnp.where(grp_of_head == 0, _qk(qlo, ck_ref[0, 0]), _qk(qlo, ck_ref[0, 1]))
    s = jnp.where(ok, raw + tabc_ref[...], NEG)
    e = jnp.exp(s - jnp.max(s, axis=1, keepdims=True))
    p = (e / jnp.sum(e, axis=1, keepdims=True) * ok.astype(F32)).astype(BF16)
    ocmp_ref[0] = _qk(p, cvt_ref[0])
    imp8 = _mm(p, sp_ref[...])

    width = imp8.shape[1]
    head_grp = lax.shift_right_logical(lax.broadcasted_iota(jnp.int32, (N_HEADS, width), 0), 2)
    sidx = lax.broadcasted_iota(jnp.int32, (N_HEADS, width), 1)
    sidx_f = sidx.astype(F32)
    forced = (sidx == 0) | (sidx == qblk) | (sidx == qblk - 1)
    lane = lax.broadcasted_iota(jnp.int32, (N_HEADS, LANE), 1)
    imp_g = [jnp.sum(jnp.where(head_grp == g, imp8, 0.0), axis=0, keepdims=True) for g in range(N_KV)]
    imp = jnp.where(head_grp == 0, imp_g[0], imp_g[1])
    score = jnp.where(sidx <= qblk, imp + jnp.where(forced, FORCE, 0.0), -FORCE)
    score = jnp.where(sidx < n_slc, score, -jnp.inf)
    picked = jnp.zeros((N_HEADS, LANE), F32)
    for k in range(N_SELECT):
        best = jnp.max(score, axis=1, keepdims=True)
        ik = jnp.min(jnp.where(score == best, sidx_f, float(width)), axis=1, keepdims=True)
        picked = jnp.where(lane == k, ik, picked)
        score = jnp.where(sidx_f == ik, -jnp.inf, score)
    for g in range(N_KV):
        idx_ref[0, g:g + 1, :] = picked[GQ * g:GQ * g + 1, :].astype(jnp.int32)

    kt = cwin_ref[0, 0].reshape(N_KV * HEAD_DIM, win_buf).astype(BF16)
    vt = cwin_ref[0, 1].reshape(N_KV * HEAD_DIM, win_buf).astype(BF16)
    knew = wnew_ref[0, :, 0:LANE].astype(BF16).astype(F32)
    vnew = wnew_ref[0, :, LANE:2 * LANE].astype(BF16).astype(F32)
    colw = lax.broadcasted_iota(jnp.int32, (N_HEADS, win_buf), 1)
    okw = colw > win_buf - WINDOW
    sw = jnp.where(okw, _mm(qgl, kt) + tabw_ref[...], NEG)
    s_new = jnp.sum(qgl.astype(F32) * knew, axis=1, keepdims=True) + rb0_ref[:, 0:1]
    mw = jnp.maximum(jnp.max(sw, axis=1, keepdims=True), s_new)
    ew = jnp.where(okw, jnp.exp(sw - mw), 0.0)
    e_new = jnp.exp(s_new - mw)
    lw = jnp.sum(ew, axis=1, keepdims=True) + e_new
    owin_ref[0] = (_qk(ew.astype(BF16), vt) + e_new.astype(BF16).astype(F32) * vnew) / lw


def _sample_cw(qlo, qgl, cmpk, cmpv_t, cwin, wnew, tab_c, tab_w, rb0, sp, n_slc, qblk):
    n = qlo.shape[0]
    win_buf = cwin.shape[-1]
    blk = lambda a: pl.BlockSpec((1,) + a.shape[1:], lambda b: (b,) + (0,) * (a.ndim - 1))
    outs = (jax.ShapeDtypeStruct((n, N_HEADS, LANE), F32), jax.ShapeDtypeStruct((n, N_HEADS, LANE), F32),
            jax.ShapeDtypeStruct((n, N_KV, LANE), jnp.int32))
    return pl.pallas_call(
        functools.partial(_sample_cw_kernel, n_slc=n_slc, qblk=qblk, win_buf=win_buf),
        grid=(n,),
        in_specs=[blk(qlo), blk(qgl), blk(cmpk), blk(cmpv_t), blk(cwin), blk(wnew), _const_spec(tab_c.shape),
                  _const_spec(tab_w.shape), _const_spec(rb0.shape), _const_spec(sp.shape)],
        out_specs=tuple(pl.BlockSpec((1,) + o.shape[1:], lambda b: (b, 0, 0)) for o in outs),
        out_shape=outs,
        compiler_params=_cparams(1),
        name="sample_cmp_win",
    )(qlo, qgl, cmpk, cmpv_t, cwin, wnew, tab_c, tab_w, rb0, sp)


def _sample_slc_kernel(pg_ref, hf_ref, bid_ref, *refs, new_block, near_block):
    n_blk = N_KV * N_SELECT
    blocks = refs[:n_blk]
    (q_ref, kvnew_ref, ocmp_ref, owin_ref, gate_ref, tnear_ref, tlast_ref, rb0_ref, gn_ref, o_ref) = refs[n_blk:]
    b = pl.program_id(0)
    n_keys = N_SELECT * PAGE_SIZE
    lane = lax.broadcasted_iota(jnp.int32, (1, n_keys), 1)
    slot = lax.shift_right_logical(lane, 7)
    half = lax.shift_right_logical(lane, 6) & 1
    head_grp = lax.shift_right_logical(lax.broadcasted_iota(jnp.int32, (N_HEADS, LANE), 0), 2)
    lane_grp = lax.shift_right_logical(lax.broadcasted_iota(jnp.int32, (N_HEADS, LANE), 1), 6)
    knew = kvnew_ref[0, :, 256:384].astype(BF16).astype(F32)
    vnew = kvnew_ref[0, :, 384:512].astype(BF16).astype(F32)
    qs = q_ref[0]
    o_slc = jnp.zeros((N_HEADS, LANE), F32)
    for g in range(N_KV):
        mine = blocks[g * N_SELECT:(g + 1) * N_SELECT]
        kt = jnp.concatenate([blk[0, 0].reshape(N_KV * HEAD_DIM, PAGE_SIZE) for blk in mine], axis=1).astype(BF16)
        vt = jnp.concatenate([blk[0, 1].reshape(N_KV * HEAD_DIM, PAGE_SIZE) for blk in mine], axis=1).astype(BF16)
        bid = jnp.zeros((1, n_keys), jnp.int32)
        hsel = jnp.zeros((1, n_keys), jnp.int32)
        has_new = bid_ref[b, g * N_SELECT] == new_block
        for kk in range(N_SELECT):
            bid = jnp.where(slot == kk, bid_ref[b, g * N_SELECT + kk], bid)
            hsel = jnp.where(slot == kk, hf_ref[b, g * N_SELECT + kk], hsel)
            if kk:
                has_new = has_new | (bid_ref[b, g * N_SELECT + kk] == new_block)
        bias = jnp.where(bid == near_block + 1, tlast_ref[...], jnp.where(bid == near_block, tnear_ref[...], 0.0))
        ok = (bid != new_block) & (half == hsel)
        s = jnp.where(ok, _mm(qs, kt) + bias, NEG)
        s_new = jnp.sum(qs.astype(F32) * knew, axis=1, keepdims=True) + rb0_ref[:, 0:1]
        s_new = jnp.where(has_new, s_new, NEG)
        m = jnp.maximum(jnp.max(s, axis=1, keepdims=True), s_new)
        e = jnp.where(ok, jnp.exp(s - m), 0.0)
        e_new = jnp.where(has_new, jnp.exp(s_new - m), 0.0)
        l = jnp.sum(e, axis=1, keepdims=True) + e_new
        og = (_qk(e.astype(BF16), vt) + e_new.astype(BF16).astype(F32) * vnew) / l
        o_slc = jnp.where(head_grp == g, og, o_slc)
    gt = gate_ref[0]
    comb = gt[:, 0:1] * ocmp_ref[0] + gt[:, 1:2] * o_slc + gt[:, 2:3] * owin_ref[0]
    comb = jnp.where(lane_grp == head_grp, comb, 0.0)
    o = jnp.concatenate([comb[h:h + 1, :] for h in range(N_HEADS)], axis=1)
    o_ref[0] = _rms(o, gn_ref[...], N_HEADS * HEAD_DIM).astype(BF16)


def _sample_slc(pages, halves, bids, cache_t, q, kvnew, ocmp, owin, gates, t_near, t_last, rb0, gn_pad,
                new_block, near_block):
    n = q.shape[0]
    n_blk = N_KV * N_SELECT
    blk_specs = [pl.BlockSpec(_PAGE_BLOCK, lambda b, pg, hf, bi, k=k: (pg[b, k], 1, 0, 0, 0))
                 for k in range(n_blk)]
    per_b = lambda a: pl.BlockSpec((1,) + a.shape[1:], lambda b, *_: (b,) + (0,) * (a.ndim - 1))
    cst = lambda a: pl.BlockSpec(a.shape, lambda *_: (0,) * a.ndim)
    grid_spec = pltpu.PrefetchScalarGridSpec(
        num_scalar_prefetch=3,
        grid=(n,),
        in_specs=blk_specs + [per_b(q), per_b(kvnew), per_b(ocmp), per_b(owin), per_b(gates),
                              cst(t_near), cst(t_last), cst(rb0), cst(gn_pad)],
        out_specs=pl.BlockSpec((1, 1, N_HEADS * LANE), lambda b, *_: (b, 0, 0)),
    )
    return pl.pallas_call(
        functools.partial(_sample_slc_kernel, new_block=new_block, near_block=near_block),
        grid_spec=grid_spec,
        out_shape=jax.ShapeDtypeStruct((n, 1, N_HEADS * LANE), BF16),
        compiler_params=_cparams(1),
        name="sample_selected",
    )(pages, halves, bids, *([cache_t] * n_blk), q, kvnew, ocmp, owin, gates, t_near, t_last, rb0, gn_pad)


def _proj_kernel(on_ref, zn_ref, x_ref, wa_ref, wb_ref, g1_ref, b1_ref, wr_ref, br_ref,
                 x1_ref, gate_ref, *, alpha):
    mix = _mm(on_ref[...], wa_ref[...]) + _mm(zn_ref[...], wb_ref[...])
    x1 = _layernorm(alpha * x_ref[...] + mix, g1_ref[...], b1_ref[...])
    x1_ref[...] = x1
    logits = _mm(x1.astype(BF16), wr_ref[...]) + br_ref[...]
    tm = logits.shape[0]
    lane = lax.broadcasted_iota(jnp.int32, (tm, LANE), 1).astype(F32)

    def first_argmax(v, vmax):
        return jnp.min(jnp.where(v == vmax, lane, float(LANE)), axis=1, keepdims=True)

    lg = jnp.where(lane < N_GROUPS, logits, -jnp.inf)
    lg_max = jnp.max(lg, axis=1, keepdims=True)
    eg = jnp.exp(lg - lg_max)
    pg = eg / jnp.sum(eg, axis=1, keepdims=True)
    gidx = first_argmax(lg, lg_max)
    pg_sel = jnp.sum(jnp.where(lane == gidx, pg, 0.0), axis=1, keepdims=True)
    le = jnp.zeros((tm, LANE), F32)
    for gi in range(N_GROUPS):
        le = le + jnp.where(gidx == gi, pltpu.roll(logits, LANE - N_EXP * (1 + gi), axis=1), 0.0)
    le = jnp.where(lane < N_EXP, le, -jnp.inf)
    ee = jnp.exp(le - jnp.max(le, axis=1, keepdims=True))
    pe = jnp.where(lane < N_EXP, ee / jnp.sum(ee, axis=1, keepdims=True), -1.0)
    v1 = jnp.max(pe, axis=1, keepdims=True)
    i1 = first_argmax(pe, v1)
    pe2 = jnp.where(lane == i1, -1.0, pe)
    v2 = jnp.max(pe2, axis=1, keepdims=True)
    i2 = first_argmax(pe2, v2)
    tot = v1 + v2
    gate_e = jnp.where(lane == i1, v1 / tot * pg_sel, jnp.where(lane == i2, v2 / tot * pg_sel, 0.0))
    gate_ref[...] = jnp.where(lane == GROUP_LANE, gidx, gate_e)


def _proj(on, zn, x, wa, wb, g1, b1, wr, br, alpha, tm):
    t, d = x.shape
    row = lambda n: pl.BlockSpec((tm, n), lambda i: (i, 0))
    outs = (jax.ShapeDtypeStruct((t, d), F32), jax.ShapeDtypeStruct((t, LANE), F32))
    return pl.pallas_call(
        functools.partial(_proj_kernel, alpha=alpha),
        grid=(t // tm,),
        in_specs=[row(on.shape[1]), row(zn.shape[1]), row(d)] + [_const_spec(a.shape) for a in (wa, wb, g1, b1, wr, br)],
        out_specs=(row(d), row(LANE)),
        out_shape=outs,
        compiler_params=_cparams(1),
        name=f"proj_ln_router_{tm}",
    )(on, zn, x, wa, wb, g1, b1, wr, br)


def _moe_kernel(x1_ref, gate_ref, wgu_ref, wd_ref, g2_ref, b2_ref, out_ref, acc_ref, *, alpha, d_ff):
    gi = pl.program_id(1)

    @pl.when(gi == 0)
    def _():
        acc_ref[...] = jnp.zeros(acc_ref.shape, F32)

    x = x1_ref[...].astype(BF16)
    gate = gate_ref[...]
    gt = jnp.where(gate[:, GROUP_LANE:GROUP_LANE + 1] == gi.astype(F32), gate, 0.0)
    acc = acc_ref[...]
    for e in range(N_EXP):
        au = _mm(x, wgu_ref[0, e])
        h = jax.nn.silu(au[:, 0:d_ff]) * au[:, d_ff:2 * d_ff]
        acc = acc + _mm((h * gt[:, e:e + 1]).astype(BF16), wd_ref[0, e])
    acc_ref[...] = acc

    @pl.when(gi == N_GROUPS - 1)
    def _():
        out_ref[...] = _layernorm(alpha * x1_ref[...] + acc_ref[...], g2_ref[...], b2_ref[...])


def _moe(x1, gate, wgu, wd, g2, b2, alpha, tm):
    t, d = x1.shape
    d_ff = wd.shape[2]
    return pl.pallas_call(
        functools.partial(_moe_kernel, alpha=alpha, d_ff=d_ff),
        grid=(t // tm, N_GROUPS),
        in_specs=[
            pl.BlockSpec((tm, d), lambda i, g: (i, 0)),
            pl.BlockSpec((tm, LANE), lambda i, g: (i, 0)),
            pl.BlockSpec((1, N_EXP, d, 2 * d_ff), lambda i, g: (g, 0, 0, 0)),
            pl.BlockSpec((1, N_EXP, d_ff, d), lambda i, g: (g, 0, 0, 0)),
            _const_spec(g2.shape), _const_spec(b2.shape),
        ],
        out_specs=pl.BlockSpec((tm, d), lambda i, g: (i, 0)),
        out_shape=jax.ShapeDtypeStruct((t, d), F32),
        scratch_shapes=[pltpu.VMEM((tm, d), F32)],
        compiler_params=_cparams(2),
        name=f"moe_ln_{tm}",
    )(x1, gate, wgu, wd, g2, b2)


def _split_bf16(x):
    hi = x.astype(BF16)
    return hi, (x - hi.astype(F32)).astype(BF16)


def _moe_sorted_kernel(cnt_ref, off_ref, x1_ref, gate_ref, drow_ref, dcol_ref, wgu_ref, wd_ref, g2_ref, b2_ref,
                       out_ref, xs_ref, gs_ref, ys_ref, *, alpha, d_ff):
    i = pl.program_id(0)
    gi = pl.program_id(1)
    half = pl.program_id(2)
    n_half = N_EXP // EXPERT_SPLIT
    tmoe = x1_ref.shape[0]

    @pl.when((i == 0) & (gi == 0) & (half == 0))
    def _():
        xs_ref[...] = jnp.zeros(xs_ref.shape, BF16)
        gs_ref[...] = jnp.zeros(gs_ref.shape, F32)
        ys_ref[...] = jnp.zeros(ys_ref.shape, F32)

    @pl.when((gi == 0) & (half == 0))
    def _():
        perm = (lax.broadcasted_iota(jnp.int32, (PERM_ROWS, tmoe), 0) == drow_ref[0]).astype(BF16)
        xs_ref[0:PERM_ROWS, :] = _mm(perm, x1_ref[...].astype(BF16)).astype(BF16)
        g_hi, g_lo = _split_bf16(gate_ref[...])
        gs_ref[0:PERM_ROWS, :] = _mm(perm, g_hi) + _mm(perm, g_lo)

    n = cnt_ref[i * N_GROUPS + gi]
    off = off_ref[i * N_GROUPS + gi]

    def run_pass(c, carry):
        base = pl.multiple_of(off + c * MOE_CHUNK, SEG_ALIGN)
        xc = xs_ref[pl.ds(base, MOE_CHUNK), :]
        gc = gs_ref[pl.ds(base, MOE_CHUNK), :]
        gc = jnp.where(half == 0, gc, pltpu.roll(gc, LANE - n_half, axis=1))
        y = jnp.zeros((MOE_CHUNK, out_ref.shape[1]), F32)
        for e in range(n_half):
            au = _mm(xc, wgu_ref[0, e])
            h = jax.nn.silu(au[:, 0:d_ff]) * au[:, d_ff:2 * d_ff]
            y = y + _mm((h * gc[:, e:e + 1]).astype(BF16), wd_ref[0, e])

        @pl.when(half == 0)
        def _():
            ys_ref[pl.ds(base, MOE_CHUNK), :] = y

        @pl.when(half != 0)
        def _():
            ys_ref[pl.ds(base, MOE_CHUNK), :] = ys_ref[pl.ds(base, MOE_CHUNK), :] + y

        return carry

    lax.fori_loop(0, (n + MOE_CHUNK - 1) // MOE_CHUNK, run_pass, 0)

    @pl.when((gi == N_GROUPS - 1) & (half == EXPERT_SPLIT - 1))
    def _():
        unperm = (dcol_ref[...] == lax.broadcasted_iota(jnp.int32, (tmoe, PERM_ROWS), 1)).astype(BF16)
        y_hi, y_lo = _split_bf16(ys_ref[0:PERM_ROWS, :])
        moe = _mm(unperm, y_hi) + _mm(unperm, y_lo)
        out_ref[...] = _layernorm(alpha * x1_ref[...] + moe, g2_ref[...], b2_ref[...])


def _moe_sorted(x1, gate, wgu, wd, g2, b2, alpha):
    t, d = x1.shape
    d_ff = wd.shape[2]
    n_tiles = t // TMOE
    n_half = N_EXP // EXPERT_SPLIT
    gid = gate[:, GROUP_LANE].astype(jnp.int32).reshape(n_tiles, TMOE)
    onehot = (gid[:, :, None] == jnp.arange(N_GROUPS)[None, None, :]).astype(jnp.int32)
    cnt = jnp.sum(onehot, axis=1)
    rank = jnp.sum((jnp.cumsum(onehot, axis=1) - onehot) * onehot, axis=2)
    seg = (cnt + SEG_ALIGN - 1) // SEG_ALIGN * SEG_ALIGN
    off = jnp.cumsum(seg, axis=1) - seg
    dest = jnp.sum(onehot * off[:, None, :], axis=2) + rank
    assert N_GROUPS * (SEG_ALIGN - 1) <= PERM_ROWS - TMOE
    wgu_h = wgu.reshape(N_GROUPS * EXPERT_SPLIT, n_half, d, 2 * d_ff)
    wd_h = wd.reshape(N_GROUPS * EXPERT_SPLIT, n_half, d_ff, d)
    grid_spec = pltpu.PrefetchScalarGridSpec(
        num_scalar_prefetch=2,
        grid=(n_tiles, N_GROUPS, EXPERT_SPLIT),
        in_specs=[
            pl.BlockSpec((TMOE, d), lambda i, g, h, *_: (i, 0)),
            pl.BlockSpec((TMOE, LANE), lambda i, g, h, *_: (i, 0)),
            pl.BlockSpec((1, 1, TMOE), lambda i, g, h, *_: (i, 0, 0)),
            pl.BlockSpec((TMOE, 1), lambda i, g, h, *_: (i, 0)),
            pl.BlockSpec((1, n_half, d, 2 * d_ff), lambda i, g, h, *_: (g * EXPERT_SPLIT + h, 0, 0, 0)),
            pl.BlockSpec((1, n_half, d_ff, d), lambda i, g, h, *_: (g * EXPERT_SPLIT + h, 0, 0, 0)),
            pl.BlockSpec(g2.shape, lambda *_: (0, 0)), pl.BlockSpec(b2.shape, lambda *_: (0, 0)),
        ],
        out_specs=pl.BlockSpec((TMOE, d), lambda i, g, h, *_: (i, 0)),
        scratch_shapes=[pltpu.VMEM((SORT_ROWS, d), BF16), pltpu.VMEM((SORT_ROWS, LANE), F32),
                        pltpu.VMEM((SORT_ROWS, d), F32)],
    )
    return pl.pallas_call(
        functools.partial(_moe_sorted_kernel, alpha=alpha, d_ff=d_ff),
        grid_spec=grid_spec,
        out_shape=jax.ShapeDtypeStruct((t, d), F32),
        compiler_params=_cparams(3),
        name="moe_sorted_ln",
    )(cnt.reshape(-1), off.reshape(-1), x1, gate, dest.reshape(n_tiles, 1, TMOE), dest.reshape(t, 1), wgu_h, wd_h,
      g2, b2)


def _pad_head_lanes(v):
    vh = v.reshape(N_HEADS, HEAD_DIM)
    z = jnp.zeros_like(vh)
    grp = (jnp.arange(N_HEADS) // GQ)[:, None]
    return jnp.where(grp == 0, jnp.concatenate([vh, z], axis=1), jnp.concatenate([z, vh], axis=1)).reshape(1, -1)


def _pad_head_rows(w):
    wh = w.reshape(N_HEADS, HEAD_DIM, -1)
    z = jnp.zeros_like(wh)
    grp = (jnp.arange(N_HEADS) // GQ)[:, None, None]
    return jnp.where(grp == 0, jnp.concatenate([wh, z], axis=1), jnp.concatenate([z, wh], axis=1)).reshape(
        N_HEADS * LANE, -1)


def _front_pad(a, axis, pad_row):
    shape = list(a.shape)
    shape[axis] = WINDOW
    return jnp.concatenate([jnp.broadcast_to(pad_row.astype(a.dtype), shape), a], axis=axis)


def kernel(x_prompt, x_sample, cache_kv, page_table, cache_win, state_conv, w_in, conv_w, pe_k, pe_v, w_ck1, w_ck2, w_cv1, w_cv2, g_nsa, g_conv, w_out, ln1_g, ln1_b, w_rg, b_rg, w_re, b_re, w_eg, w_eu, w_ed, ln2_g, ln2_b, rel_bias):
    batch, seq, d_model = x_prompt.shape
    dec_batch, dec_seq = x_sample.shape[0], x_sample.shape[1]
    depth = w_in.shape[0]
    n_pages = page_table.shape[1]
    past = n_pages * PAGE_SIZE
    win_buf = cache_win.shape[2]
    d_nsa = N_HEADS * HEAD_DIM
    d_conv = w_out.shape[1] - d_nsa
    assert dec_seq == 1 and seq % TM == 0 and min(WINDOW, seq) == TM and win_buf == WINDOW
    assert d_conv == 512 and seq // CMP_STRIDE == LANE and n_pages % SAMPLE_PAGES == 0
    alpha = (2 * depth) ** 0.25
    n_slc_s = -(-(past + dec_seq) // SLC_LEN)
    qblk_s = past // SLC_LEN

    idx_np, sp_p_np, sp_s_np = _static_tables(seq, past, win_buf)
    tabs = _bias_tables(rel_bias, jnp.asarray(idx_np))
    nq = seq // TQ
    near_t = jnp.transpose(tabs[0:2], (1, 0, 2, 3)).reshape(N_HEADS, 2 * LANE, LANE)
    up_t = tabs[2, 0]
    tab_c = tabs[_N_NEAR_TILES:_N_NEAR_TILES + nq]
    smp = tabs[_N_NEAR_TILES + nq]
    tab_w_s = smp[:, 0:5, :].reshape(N_HEADS, 5 * LANE)[:, :win_buf]
    tab_c_s = smp[:, 5:13, :].reshape(N_HEADS, 8 * LANE)
    reps = N_SELECT * PAGE_SIZE // SLC_LEN
    t_near = jnp.tile(smp[:, 13, 0:SLC_LEN], (1, reps))
    t_last = jnp.tile(smp[:, 13, SLC_LEN:2 * SLC_LEN], (1, reps))
    rb0_rep = jnp.broadcast_to(smp[:, 14, 0:1], (N_HEADS, LANE))
    sp_p = jnp.asarray(sp_p_np, BF16)
    sp_s = jnp.asarray(sp_s_np, BF16)
    key_pad = jnp.zeros((LANE,), F32).at[PAD_LANE].set(KEY_MASK)

    xp = x_prompt.reshape(batch * seq, d_model)
    xs = x_sample.reshape(dec_batch * dec_seq, d_model)
    outs = [[] for _ in range(6)]
    for l in range(depth):
        w_pack = _pack_in_weights(w_in[l])
        gc = g_conv[l].reshape(1, -1)
        gn_pad = _pad_head_lanes(g_nsa[l])
        w1k, w2k, pek = _pack_compress_weights(pe_k[l], w_ck1[l], w_ck2[l], True)
        w1v, w2v, pev = _pack_compress_weights(pe_v[l], w_cv1[l], w_cv2[l], False)
        w1 = jnp.stack([w1k, w1v])
        pe = jnp.stack([pek, pev])
        wa = _pad_head_rows(w_out[l][:d_nsa]).astype(BF16)
        wb = w_out[l][d_nsa:].astype(BF16)
        tail = LANE - N_EXP - N_GROUPS * N_EXP
        wr = jnp.concatenate([jnp.pad(w_rg[l], ((0, 0), (0, N_EXP - N_GROUPS))),
                              jnp.pad(w_re[l], ((0, 0), (0, tail)))], axis=1).astype(BF16)
        br = jnp.concatenate([jnp.pad(b_rg[l], (0, N_EXP - N_GROUPS)), jnp.pad(b_re[l], (0, tail))]).reshape(1, -1)
        wgu = jnp.concatenate([w_eg[l], w_eu[l]], axis=-1).astype(BF16)
        wd = w_ed[l].astype(BF16)
        g1, b1 = ln1_g[l].reshape(1, -1), ln1_b[l].reshape(1, -1)
        g2, b2 = ln2_g[l].reshape(1, -1), ln2_b[l].reshape(1, -1)

        q_p, kvt_p, ks, sv, kw, wv, wtail_t, gates_p, zn_p, ctail = _inproj_prompt(xp, w_pack, conv_w[l], gc, batch, seq)
        kvt_p = kvt_p.reshape(batch, 4, N_KV, HEAD_DIM, seq)
        cmpk_p, cmpvt_p = _compress_prompt(kvt_p, batch, seq, w1, w2k, w2v, pe)
        ks = _front_pad(ks.reshape(N_KV, batch, seq, LANE), 2, key_pad)
        kw = _front_pad(kw.reshape(N_KV, batch, seq, LANE), 2, key_pad)
        sv = jnp.pad(sv, ((0, 0), (0, 0), (WINDOW, 0)))
        wv = jnp.pad(wv, ((0, 0), (0, 0), (WINDOW, 0)))
        gn_rep = jnp.broadcast_to(gn_pad.reshape(-1, 1), (N_HEADS * LANE, TQ))
        on_p = _attn_prompt(q_p, gates_p, cmpk_p, cmpvt_p, ks, sv, kw, wv, near_t, up_t, tab_c, sp_p, gn_rep,
                            batch, seq)
        x1_p, gate_p = _proj(on_p, zn_p, xp, wa, wb, g1, b1, wr, br, alpha, TM)
        y_p = _moe_sorted(x1_p, gate_p, wgu, wd, g2, b2, alpha)

        st = state_conv[l]
        qlo_s, qgl_s, kv_s, win_s, gates_s, zn_s, u_s = _inproj_sample(xs, w_pack, conv_w[l], gc, st[:, 0], st[:, 1])
        cache_t = jnp.transpose(cache_kv[l], (0, 2, 3, 4, 1))
        cwin_t = jnp.transpose(cache_win[l], (0, 2, 3, 4, 1))
        cmpk_s, cmpvt_s = _compress_sample(cache_t, page_table, w1, w2k, w2v, pe)
        qlo_s3 = qlo_s.reshape(dec_batch, N_HEADS, LANE)
        qgl_s3 = qgl_s.reshape(dec_batch, N_HEADS, LANE)
        ocmp, owin, sel_idx = _sample_cw(qlo_s3, qgl_s3, cmpk_s, cmpvt_s, cwin_t, win_s.reshape(dec_batch, 1, 256), tab_c_s,
                                         tab_w_s, rb0_rep, sp_s, n_slc_s, qblk_s)
        bids = sel_idx[:, :, :N_SELECT].reshape(dec_batch, N_KV * N_SELECT)
        blk_pages = jnp.take_along_axis(page_table, jnp.minimum(bids // 2, n_pages - 1), axis=1)
        gates_s3 = jnp.pad(gates_s.reshape(dec_batch, N_KV, LANE)[:, :, :3 * GQ].reshape(dec_batch, N_HEADS, 3),
                           ((0, 0), (0, 0), (0, LANE - 3)))
        on_s = _sample_slc(blk_pages, bids % 2, bids, cache_t, qgl_s3, kv_s.reshape(dec_batch, 1, 512), ocmp, owin,
                           gates_s3, t_near, t_last, rb0_rep, gn_pad, n_slc_s - 1, qblk_s - 2)
        on_s = on_s.reshape(dec_batch, N_HEADS * LANE)
        x1_s, gate_s = _proj(on_s, zn_s, xs, wa, wb, g1, b1, wr, br, alpha, dec_batch)
        y_s = _moe(x1_s, gate_s, wgu, wd, g2, b2, alpha, dec_batch)

        to_token_major = lambda a: jnp.transpose(a, (0, 4, 1, 2, 3))
        outs[0].append(to_token_major(kvt_p))
        outs[1].append(kv_s.reshape(dec_batch, dec_seq, 4, N_KV, HEAD_DIM))
        outs[2].append(to_token_major(wtail_t.reshape(batch, 2, N_KV, HEAD_DIM, TM)))
        win_all_t = jnp.concatenate([cwin_t, win_s.reshape(dec_batch, 2, N_KV, HEAD_DIM, dec_seq)], axis=-1)
        outs[3].append(to_token_major(win_all_t[..., win_all_t.shape[-1] - min(WINDOW, past + dec_seq):]))
        outs[4].append(ctail)
        outs[5].append(jnp.concatenate([st, u_s[:, None, :]], axis=1)[:, dec_seq:])
        xp, xs = y_p, y_s
    return (xp.reshape(batch, seq, d_model), xs.reshape(dec_batch, dec_seq, d_model),
            jnp.stack(outs[0]), jnp.stack(outs[1]), jnp.stack(outs[2]), jnp.stack(outs[3]),
            jnp.stack(outs[4]), jnp.stack(outs[5]))
```

```python
import functools
import math

import numpy as np
import jax
import jax.numpy as jnp
from jax import lax
from jax.experimental import pallas as pl
from jax.experimental.pallas import tpu as pltpu

F32 = jnp.float32
BF16 = jnp.bfloat16

HEAD_DIM = 64
N_KV = 2
GQ = 4
N_HEADS = N_KV * GQ
CONV_W = 3
CMP_STRIDE = 16
CMP_LEN = 32
CMP_HIDDEN = 2 * HEAD_DIM
SLC_LEN = 64
N_SELECT = 16
WINDOW = 512
N_BUCKETS = 32
MAX_DISTANCE = 128
N_GROUPS = 4
N_EXP = 8
PAGE_SIZE = 128
NEG = -1e30
FORCE = 1e6

LANE = 128
SUBLANE = 8
VMEM_LIMIT = 52 * 1024 * 1024

TQ = 128
TM = 512
FAR = 512
TAIL = WINDOW + TQ
SAMPLE_PAGES = 64
SAMPLE_BB = 4
TMOE = 1024
MOE_CHUNK = 320
SEG_ALIGN = 16
PERM_ROWS = TMOE + LANE
SORT_ROWS = PERM_ROWS + MOE_CHUNK
EXPERT_SPLIT = 2
GROUP_LANE = N_EXP

KEY_MASK = 2.0 ** 100
TABLE_MASK = 2 * NEG
MASKED_BUCKET = N_BUCKETS
SEL_LANE0 = HEAD_DIM
PAD_LANE = HEAD_DIM + 32


def _cparams(n_axes):
    return pltpu.CompilerParams(dimension_semantics=("arbitrary",) * n_axes, vmem_limit_bytes=VMEM_LIMIT)


def _const_spec(shape):
    nd = len(shape)
    return pl.BlockSpec(shape, lambda *_, nd=nd: (0,) * nd)


def _qk(a, b):
    return lax.dot_general(a, b, (((1,), (1,)), ((), ())), preferred_element_type=F32)


def _mm(a, b):
    return jnp.dot(a, b, preferred_element_type=F32)


def _bucket_np(dist):
    n = np.maximum(dist, 0)
    max_exact = N_BUCKETS // 2
    nf = np.maximum(n, 1).astype(np.float32)
    large = max_exact + (np.log(nf / np.float32(max_exact)) / np.float32(math.log(MAX_DISTANCE / max_exact))
                         * np.float32(N_BUCKETS - max_exact)).astype(np.int32)
    large = np.minimum(large, N_BUCKETS - 1)
    return np.where(n < max_exact, n, large).astype(np.int32)


def _overlap_np(c, s):
    c0 = c * CMP_STRIDE
    s0 = s * SLC_LEN
    return np.maximum(np.minimum(c0 + CMP_LEN, s0 + SLC_LEN) - np.maximum(c0, s0), 0)


_N_NEAR_TILES = 3


def _static_tables(seq, past, win_buf):
    nq = seq // TQ
    i = np.arange(LANE)[:, None]
    j = np.arange(LANE)[None, :]
    far_bucket = N_BUCKETS - 1
    tiles = [_bucket_np(LANE + j - i),
             np.where(j >= i, _bucket_np(j - i), MASKED_BUCKET),
             np.where(i > j, far_bucket, MASKED_BUCKET) + 0 * i]
    tiles += [_bucket_np(TQ * q + j - (CMP_STRIDE * i + CMP_STRIDE - 1)) for q in range(nq)]
    smp = np.full((LANE, LANE), 10 * MAX_DISTANCE, np.int64)
    k = np.arange(5 * LANE)
    smp[0:5] = np.where(k < win_buf, win_buf - k, 0).reshape(5, LANE)
    ci = np.arange(8 * LANE)
    smp[5:13] = np.maximum(past - (CMP_STRIDE * ci + CMP_STRIDE - 1), 0).reshape(8, LANE)
    nb = past // SLC_LEN
    smp[13] = past - (SLC_LEN * (nb - 2) + np.arange(LANE))
    smp[14] = 0
    tiles.append(_bucket_np(smp))
    idx = np.stack(tiles).astype(np.int32)

    def sp(nrows, ncols, n_slc):
        r = np.arange(nrows)[:, None]
        s = np.arange(ncols)[None, :]
        ov = _overlap_np(r - 1, s)
        return np.where((r >= 1) & (s < n_slc), ov, 0).astype(np.float32)

    sp_p = sp(seq // CMP_STRIDE, LANE, -(-seq // SLC_LEN)).T
    n_slc_s = -(-(past + 1) // SLC_LEN)
    sp_s = sp(past // CMP_STRIDE, 3 * LANE, n_slc_s)
    return idx, sp_p, sp_s


def _bias_kernel(rb_ref, idx_ref, out_ref):
    idx = idx_ref[0]
    accs = [jnp.full((LANE, LANE), TABLE_MASK, F32) for _ in range(N_HEADS)]
    for b in range(N_BUCKETS):
        hit = idx == b
        for h in range(N_HEADS):
            accs[h] = jnp.where(hit, rb_ref[b, h] - rb_ref[N_BUCKETS - 1, h], accs[h])
    for h in range(N_HEADS):
        out_ref[0, h] = accs[h]


def _bias_tables(rel_bias, idx):
    nt = idx.shape[0]
    return pl.pallas_call(
        _bias_kernel,
        grid=(nt,),
        in_specs=[pl.BlockSpec(memory_space=pltpu.SMEM),
                  pl.BlockSpec((1, LANE, LANE), lambda t: (t, 0, 0))],
        out_specs=pl.BlockSpec((1, N_HEADS, LANE, LANE), lambda t: (t, 0, 0, 0)),
        out_shape=jax.ShapeDtypeStruct((nt, N_HEADS, LANE, LANE), F32),
        compiler_params=_cparams(1),
        name="bias_tables",
    )(rel_bias, idx)


_Q0, _Q1 = 0, N_HEADS * HEAD_DIM
_KV0, _KV1 = _Q1, _Q1 + 512
_WN0, _WN1 = _KV1, _KV1 + 256
_GT0, _GT1 = _WN1, _WN1 + 2 * LANE
_CV0, _CV1 = _GT1, _GT1 + 3 * 512


def _pack_in_weights(w):
    d = w.shape[0]
    d_nsa = N_HEADS * HEAD_DIM
    wq_pad = w[:, :d_nsa]
    o = d_nsa
    w_kv = w[:, o:o + 512]
    w_win = w[:, o + 512:o + 768]
    wg = w[:, o + 768:o + 768 + 3 * N_HEADS].reshape(d, N_KV, 3 * GQ)
    wg_pad = jnp.pad(wg, ((0, 0), (0, 0), (0, LANE - 3 * GQ))).reshape(d, N_KV * LANE)
    w_conv = w[:, o + 768 + 3 * N_HEADS:]
    return jnp.concatenate([wq_pad, w_kv, w_win, wg_pad, w_conv], axis=1).astype(BF16)


def _rms(x, gain, n, eps=1e-6):
    ms = jnp.sum(x * x, axis=-1, keepdims=True) / n
    return x * lax.rsqrt(ms + eps) * gain


def _layernorm(y, gain, bias, eps=1e-5):
    mu = jnp.mean(y, axis=-1, keepdims=True)
    d = y - mu
    var = jnp.mean(d * d, axis=-1, keepdims=True)
    return d * lax.rsqrt(var + eps) * gain + bias


def _inproj_prompt_kernel(x_ref, w_ref, cw_ref, gc_ref, ks_fill, kw_fill, sv_fill, wv_fill,
                          q_ref, kv_ref, ks_ref, sv_ref, kw_ref, wv_ref, wtail_ref, gate_ref, zn_ref, ctail_ref,
                          uext_ref, *, tiles_per_batch):
    del ks_fill, kw_fill, sv_fill, wv_fill
    i = pl.program_id(0)
    tm = x_ref.shape[0]
    xb = x_ref[...].astype(BF16)

    def seg(a, b):
        return _mm(xb, w_ref[:, a:b])

    qp_t = (seg(_Q0, _Q1) * (HEAD_DIM ** -0.5)).T
    q_fill = jnp.zeros((LANE - HEAD_DIM, tm), F32)
    for h in range(N_HEADS):
        q_ref[h] = jnp.concatenate([qp_t[h * HEAD_DIM:(h + 1) * HEAD_DIM, :], q_fill], axis=0).astype(BF16)
    kv = seg(_KV0, _KV1)
    kv_t = kv.T
    kv_ref[0] = kv_t
    win = seg(_WN0, _WN1)
    win_t = win.T
    wtail_ref[0] = win_t
    pos = (i % tiles_per_batch) * tm + lax.broadcasted_iota(jnp.int32, (tm, LANE - HEAD_DIM), 0)
    lane = lax.broadcasted_iota(jnp.int32, (tm, LANE - HEAD_DIM), 1)
    blk_flag = jnp.where(lane == lax.shift_right_logical(pos, 6), KEY_MASK, 0.0).astype(BF16)
    no_flag = jnp.zeros((tm, LANE - HEAD_DIM), BF16)
    for g in range(N_KV):
        ks_ref[g, 0] = jnp.concatenate([kv[:, 256 + g * HEAD_DIM:256 + (g + 1) * HEAD_DIM].astype(BF16), blk_flag], axis=1)
        kw_ref[g, 0] = jnp.concatenate([win[:, g * HEAD_DIM:(g + 1) * HEAD_DIM].astype(BF16), no_flag], axis=1)
    sv_ref[0] = kv_t[384:512, :].astype(BF16)
    wv_ref[0] = win_t[LANE:2 * LANE, :].astype(BF16)
    gt_t = jax.nn.sigmoid(seg(_GT0, _GT1)).T
    gate_ref[0] = gt_t[0:LANE, :]
    gate_ref[1] = gt_t[LANE:2 * LANE, :]

    conv = seg(_CV0, _CV1)
    cb = conv[:, 0:512]
    u = conv[:, 512:1024] * conv[:, 1024:1536]
    first = (i % tiles_per_batch) == 0

    @pl.when(first)
    def _():
        uext_ref[0:SUBLANE, :] = jnp.zeros((SUBLANE, 512), F32)

    @pl.when(jnp.logical_not(first))
    def _():
        uext_ref[0:SUBLANE, :] = uext_ref[tm:tm + SUBLANE, :]

    uext_ref[SUBLANE:tm + SUBLANE, :] = u
    um1 = uext_ref[SUBLANE - 1:tm + SUBLANE - 1, :]
    um2 = uext_ref[SUBLANE - 2:tm + SUBLANE - 2, :]
    z = cb * (um2 * cw_ref[0:1, :] + um1 * cw_ref[1:2, :] + u * cw_ref[2:3, :])
    zn_ref[...] = _rms(z, gc_ref[...], 512).astype(BF16)
    ctail_ref[0] = uext_ref[tm + SUBLANE - 2:tm + SUBLANE, :]


def _inproj_prompt(x, w, conv_w, g_conv, batch, seq):
    t, d = x.shape
    n_tiles = t // TM
    tpb = seq // TM
    assert WINDOW == TM
    rows = WINDOW + seq
    key_pad = jnp.zeros((LANE,), BF16).at[PAD_LANE].set(KEY_MASK)
    k_fill = jnp.broadcast_to(key_pad, (N_KV, batch, rows, LANE))
    v_fill = jnp.zeros((batch, LANE, rows), BF16)
    outs = (
        jax.ShapeDtypeStruct((N_HEADS, LANE, t), BF16),
        jax.ShapeDtypeStruct((batch, 512, seq), F32),
        jax.ShapeDtypeStruct(k_fill.shape, BF16),
        jax.ShapeDtypeStruct(v_fill.shape, BF16),
        jax.ShapeDtypeStruct(k_fill.shape, BF16),
        jax.ShapeDtypeStruct(v_fill.shape, BF16),
        jax.ShapeDtypeStruct((batch, 256, TM), F32),
        jax.ShapeDtypeStruct((N_KV, LANE, t), F32),
        jax.ShapeDtypeStruct((t, 512), BF16),
        jax.ShapeDtypeStruct((batch, CONV_W - 1, 512), F32),
    )
    row = lambda n: pl.BlockSpec((TM, n), lambda i: (i, 0))
    planes_t = lambda n: pl.BlockSpec((n, LANE, TM), lambda i: (0, 0, i))
    keys_out = pl.BlockSpec((N_KV, 1, TM, LANE), lambda i: (0, i // tpb, i % tpb + 1, 0))
    vals_out = pl.BlockSpec((1, LANE, TM), lambda i: (i // tpb, 0, i % tpb + 1))
    in_place = pl.BlockSpec(memory_space=pl.ANY)
    return pl.pallas_call(
        functools.partial(_inproj_prompt_kernel, tiles_per_batch=tpb),
        grid=(n_tiles,),
        in_specs=[row(d), _const_spec(w.shape), _const_spec(conv_w.shape), _const_spec(g_conv.shape),
                  in_place, in_place, in_place, in_place],
        input_output_aliases={4: 2, 5: 4, 6: 3, 7: 5},
        out_specs=(
            planes_t(N_HEADS),
            pl.BlockSpec((1, 512, TM), lambda i: (i // tpb, 0, i % tpb)),
            keys_out, vals_out, keys_out, vals_out,
            pl.BlockSpec((1, 256, TM), lambda i: (i // tpb, 0, 0)),
            planes_t(N_KV),
            row(512),
            pl.BlockSpec((1, CONV_W - 1, 512), lambda i: (i // tpb, 0, 0)),
        ),
        out_shape=outs,
        scratch_shapes=[pltpu.VMEM((TM + SUBLANE, 512), F32)],
        compiler_params=_cparams(1),
        name="inproj_prompt",
    )(x, w, conv_w, g_conv, k_fill, k_fill, v_fill, v_fill)


def _inproj_sample_kernel(x_ref, w_ref, cw_ref, gc_ref, s0_ref, s1_ref,
                          qlo_ref, qgl_ref, kv_ref, win_ref, gate_ref, zn_ref, u_ref):
    xb = x_ref[...].astype(BF16)

    def seg(a, b):
        return _mm(xb, w_ref[:, a:b])

    qp = seg(_Q0, _Q1) * (HEAD_DIM ** -0.5)
    fill = jnp.zeros((qp.shape[0], LANE - HEAD_DIM), F32)
    heads = [qp[:, h * HEAD_DIM:(h + 1) * HEAD_DIM] for h in range(N_HEADS)]
    qlo_ref[...] = jnp.concatenate([x for qh in heads for x in (qh, fill)], axis=1).astype(BF16)
    qgl_ref[...] = jnp.concatenate([x for h, qh in enumerate(heads) for x in ((qh, fill) if h < GQ else (fill, qh))],
                                   axis=1).astype(BF16)
    kv_ref[...] = seg(_KV0, _KV1)
    win_ref[...] = seg(_WN0, _WN1)
    gate_ref[...] = jax.nn.sigmoid(seg(_GT0, _GT1))
    conv = seg(_CV0, _CV1)
    cb = conv[:, 0:512]
    u = conv[:, 512:1024] * conv[:, 1024:1536]
    z = cb * (s0_ref[...] * cw_ref[0:1, :] + s1_ref[...] * cw_ref[1:2, :] + u * cw_ref[2:3, :])
    zn_ref[...] = _rms(z, gc_ref[...], 512).astype(BF16)
    u_ref[...] = u


def _inproj_sample(x, w, conv_w, g_conv, s0, s1):
    n = x.shape[0]
    outs = (
        jax.ShapeDtypeStruct((n, N_HEADS * LANE), BF16),
        jax.ShapeDtypeStruct((n, N_HEADS * LANE), BF16),
        jax.ShapeDtypeStruct((n, 512), F32),
        jax.ShapeDtypeStruct((n, 256), F32),
        jax.ShapeDtypeStruct((n, N_KV * LANE), F32),
        jax.ShapeDtypeStruct((n, 512), BF16),
        jax.ShapeDtypeStruct((n, 512), F32),
    )
    args = (x, w, conv_w, g_conv, s0, s1)
    return pl.pallas_call(
        _inproj_sample_kernel,
        grid=(1,),
        in_specs=[_const_spec(a.shape) for a in args],
        out_specs=tuple(_const_spec(o.shape) for o in outs),
        out_shape=outs,
        compiler_params=_cparams(1),
        name="inproj_sample",
    )(*args)


def _pack_compress_weights(pe, w1, w2, per_group_out):
    w = w1.reshape(2, CMP_STRIDE, HEAD_DIM, CMP_HIDDEN).transpose(1, 2, 0, 3).reshape(CMP_STRIDE, HEAD_DIM, 2 * CMP_HIDDEN)
    z = jnp.zeros_like(w)
    top = jnp.concatenate([w, z], axis=-1)
    bot = jnp.concatenate([z, w], axis=-1)
    w1bd = jnp.stack([top, bot], axis=1).reshape(CMP_STRIDE * 2 * HEAD_DIM, 4 * CMP_HIDDEN).astype(BF16)
    z2 = jnp.zeros_like(w2)
    if per_group_out:
        w2bd = jnp.concatenate([jnp.concatenate([w2, z2, z2, z2], axis=1),
                                jnp.concatenate([z2, z2, w2, z2], axis=1)], axis=0).astype(BF16)
    else:
        w2bd = jnp.concatenate([jnp.concatenate([w2, z2], axis=1), jnp.concatenate([z2, w2], axis=1)], axis=0).astype(BF16)
    pe_rows = jnp.broadcast_to(pe.reshape(2, CMP_STRIDE, 1, HEAD_DIM), (2, CMP_STRIDE, N_KV, HEAD_DIM))
    pe_rows = pe_rows.reshape(2, CMP_STRIDE * N_KV * HEAD_DIM)
    pe_rows = jnp.pad(pe_rows, ((0, SUBLANE - 2), (0, 0)))
    return w1bd, w2bd, pe_rows


def _compress_kernel(*refs, n_pages, n_prefetch):
    refs = refs[n_prefetch:]
    page_refs = refs[:n_pages]
    w1_ref, w2k_ref, w2v_ref, pe_ref, ck_ref, cvt_ref, carry_ref, tok_ref = refs[n_pages:]
    step = pl.program_id(1)
    m = n_pages * (PAGE_SIZE // CMP_STRIDE)

    @pl.when(step == 0)
    def _():
        carry_ref[...] = jnp.zeros(carry_ref.shape, F32)

    row0 = lax.broadcasted_iota(jnp.int32, (m, CMP_HIDDEN), 0) == 0
    for t, w2_ref in enumerate((w2k_ref, w2v_ref)):
        for p, pr in enumerate(page_refs):
            slab = pr[0, t].reshape(N_KV * HEAD_DIM, PAGE_SIZE)
            tok_ref[t, p * PAGE_SIZE:(p + 1) * PAGE_SIZE, :] = slab.T
        pieces = [tok_ref[t, pl.ds(j, m, stride=CMP_STRIDE), :] for j in range(CMP_STRIDE)]
        lhs = jnp.concatenate([jnp.concatenate(pieces, axis=1), pe_ref[t]], axis=0).astype(BF16)
        a = _mm(lhs, w1_ref[t])
        hs = []
        for g in range(N_KV):
            c0 = g * 2 * CMP_HIDDEN
            a0 = a[0:m, c0:c0 + CMP_HIDDEN]
            a1 = a[0:m, c0 + CMP_HIDDEN:c0 + 2 * CMP_HIDDEN]
            pe_term = a[m:m + 1, c0:c0 + CMP_HIDDEN] + a[m + 1:m + 2, c0 + CMP_HIDDEN:c0 + 2 * CMP_HIDDEN]
            prev = carry_ref[t, 0:1, g * CMP_HIDDEN:(g + 1) * CMP_HIDDEN]
            shifted = jnp.where(row0, prev, pltpu.roll(a0, 1, axis=0))
            carry_ref[t, 0:1, g * CMP_HIDDEN:(g + 1) * CMP_HIDDEN] = a0[m - 1:m, :]
            hs.append(jax.nn.gelu(shifted + a1 + pe_term))
        out = _mm(jnp.concatenate(hs, axis=1).astype(BF16), w2_ref[...])
        if t == 0:
            ck_ref[0, 0] = out[:, 0:LANE].astype(BF16)
            ck_ref[0, 1] = out[:, LANE:2 * LANE].astype(BF16)
        else:
            cvt_ref[0] = out.T.astype(BF16)


def _compress_call(page_specs, page_args, prefetch, grid, batch, n_chunks, m, w1, w2k, w2v, pe, name):
    n_pages = len(page_specs)
    n_pf = len(prefetch)
    cspec = lambda shape: pl.BlockSpec(shape, lambda *_: (0,) * len(shape))
    grid_spec = pltpu.PrefetchScalarGridSpec(
        num_scalar_prefetch=n_pf,
        grid=grid,
        in_specs=list(page_specs) + [cspec(w1.shape), cspec(w2k.shape), cspec(w2v.shape), cspec(pe.shape)],
        out_specs=(pl.BlockSpec((1, N_KV, m, LANE), lambda b, s, *_: (b, 0, s, 0)),
                   pl.BlockSpec((1, LANE, m), lambda b, s, *_: (b, 0, s))),
        scratch_shapes=[pltpu.VMEM((2, SUBLANE, 2 * CMP_HIDDEN), F32),
                        pltpu.VMEM((2, n_pages * PAGE_SIZE, N_KV * HEAD_DIM), F32)],
    )
    return pl.pallas_call(
        functools.partial(_compress_kernel, n_pages=n_pages, n_prefetch=n_pf),
        grid_spec=grid_spec,
        out_shape=(jax.ShapeDtypeStruct((batch, N_KV, n_chunks, LANE), BF16),
                   jax.ShapeDtypeStruct((batch, LANE, n_chunks), BF16)),
        compiler_params=_cparams(2),
        name=name,
    )(*prefetch, *page_args, w1, w2k, w2v, pe)


_PAGE_BLOCK = (1, 2, N_KV, HEAD_DIM, PAGE_SIZE)


def _compress_prompt(kv_t, batch, seq, w1, w2k, w2v, pe):
    n_pages = seq // PAGE_SIZE
    rows = PAGE_SIZE // CMP_STRIDE
    specs = [pl.BlockSpec(_PAGE_BLOCK, lambda b, s, p=p: (b, 0, 0, 0, p)) for p in range(n_pages)]
    return _compress_call(specs, [kv_t] * n_pages, (), (batch, 1), batch, n_pages * rows, n_pages * rows,
                          w1, w2k, w2v, pe, "compress_prompt")


def _compress_sample(cache_t, page_table, w1, w2k, w2v, pe):
    batch, n_pages = page_table.shape
    rows = PAGE_SIZE // CMP_STRIDE
    specs = [pl.BlockSpec(_PAGE_BLOCK, lambda b, s, pt, k=k: (pt[b, s * SAMPLE_PAGES + k], 0, 0, 0, 0))
             for k in range(SAMPLE_PAGES)]
    return _compress_call(specs, [cache_t] * SAMPLE_PAGES, (page_table,), (batch, n_pages // SAMPLE_PAGES), batch,
                          n_pages * rows, SAMPLE_PAGES * rows, w1, w2k, w2v, pe, "compress_sample")


def _tile4(x):
    return jnp.concatenate([x] * GQ, axis=1)


def _softmax_fold(state, pieces, vt):
    m, l, acc = state
    m_new = m
    for pc in pieces:
        m_new = jnp.maximum(m_new, jnp.max(pc, axis=0, keepdims=True))
    alpha = jnp.exp(m - m_new)
    es = [jnp.exp(pc - m_new) for pc in pieces]
    l = alpha * l
    for e in es:
        l = l + jnp.sum(e, axis=0, keepdims=True)
    e_all = es[0] if len(es) == 1 else jnp.concatenate(es, axis=0)
    acc = alpha * acc + _mm(vt, e_all.astype(BF16))
    return m_new, l, acc


def _attn_prompt_kernel(q_ref, gate_ref, ck_ref, cvt_ref, ks_ref, sv_ref, kw_ref, wv_ref,
                        near_ref, up_ref, tabc_ref, sp_ref, gn_ref, o_ref, *, n_slc):
    qi = pl.program_id(1)
    q0 = pl.multiple_of(qi * TQ, TQ)
    krow = lax.broadcasted_iota(jnp.int32, (LANE, TQ), 0)
    qcol = lax.broadcasted_iota(jnp.int32, (LANE, TQ), 1)
    cmp_ok4 = _tile4((krow >= 1) & (CMP_STRIDE * krow + (CMP_STRIDE - 1) <= q0 + qcol))
    sidx = lax.broadcasted_iota(jnp.int32, (n_slc, TQ), 0)
    qblk = lax.shift_right_logical(q0 + lax.broadcasted_iota(jnp.int32, (n_slc, TQ), 1), 6)
    sel_valid = sidx <= qblk
    sel_forced = (sidx == 0) | (sidx == qblk) | (sidx == qblk - 1)
    n_far = lax.shift_right_logical(jnp.maximum(qi - 1, 0), 2)
    up4 = _tile4(up_ref[...])
    cols = GQ * TQ
    fresh = (jnp.full((1, cols), NEG, F32), jnp.zeros((1, cols), F32), jnp.zeros((LANE, cols), F32))

    groups = range(N_KV)
    qa, near, o_cmp = [], [], []
    for g in groups:
        heads = [GQ * g + r for r in range(GQ)]
        q_t = [q_ref[h] for h in heads]
        near.append(jnp.concatenate([near_ref[h] for h in heads], axis=1))

        bias_c = jnp.concatenate([tabc_ref[0, h] for h in heads], axis=1)
        s = jnp.where(cmp_ok4, _mm(ck_ref[0, g], jnp.concatenate(q_t, axis=1)) + bias_c, NEG)
        e = jnp.exp(s - jnp.max(s, axis=0, keepdims=True))
        p = (e / jnp.sum(e, axis=0, keepdims=True) * cmp_ok4.astype(F32)).astype(BF16)
        o_cmp.append(_mm(cvt_ref[0], p))
        imp4 = _mm(sp_ref[...], p)
        imp = imp4[:, 0:TQ] + imp4[:, TQ:2 * TQ] + imp4[:, 2 * TQ:3 * TQ] + imp4[:, 3 * TQ:4 * TQ]

        score = jnp.where(sel_valid, imp[0:n_slc, :] + jnp.where(sel_forced, FORCE, 0.0), -FORCE)
        rank = jnp.zeros((n_slc, TQ), F32)
        for k in range(1, n_slc):
            other = pltpu.roll(score, k, axis=0)
            rank = rank + jnp.where(sidx >= k, (other >= score).astype(F32), (other > score).astype(F32))
        sel_t = (rank < float(N_SELECT)).astype(F32)
        aug = jnp.concatenate([jnp.zeros((SEL_LANE0, TQ), F32), sel_t - 1.0,
                               jnp.zeros((LANE - SEL_LANE0 - n_slc, TQ), F32)], axis=0)
        aug = jnp.where(krow == PAD_LANE, -1.0, aug).astype(BF16)
        qa.append(jnp.concatenate([qh + aug for qh in q_t], axis=1))

    def far_tile(it, states):
        start = pl.multiple_of(q0 - FAR * (n_far - it), LANE)
        vt = sv_ref[0, :, pl.ds(start, FAR)]
        return tuple(_softmax_fold(states[g], [_mm(ks_ref[g, 0, pl.ds(start, FAR), :], qa[g])], vt) for g in groups)

    states = lax.fori_loop(0, n_far, far_tile, (fresh,) * N_KV)
    v_tail = sv_ref[0, :, pl.ds(q0, TAIL)]
    w_tail = wv_ref[0, :, pl.ds(q0, TAIL)]
    o_slc, o_win = [], []
    for g in groups:
        sc = _mm(ks_ref[g, 0, pl.ds(q0, TAIL), :], qa[g])
        pieces = [sc[0:TAIL - 2 * LANE], sc[TAIL - 2 * LANE:TAIL] + near[g]]
        _, l, acc = _softmax_fold(states[g], pieces, v_tail)
        o_slc.append(acc / l)
        sc = _mm(kw_ref[g, 0, pl.ds(q0, TAIL), :], qa[g])
        pieces = [sc[0:LANE] + up4, sc[LANE:TAIL - 2 * LANE], sc[TAIL - 2 * LANE:TAIL] + near[g]]
        _, l, acc = _softmax_fold(fresh, pieces, w_tail)
        o_win.append(acc / l)

    outs = []
    for g in groups:
        gt = gate_ref[g]
        own = lax.shift_right_logical(krow, 6) == g
        for r in range(GQ):
            cs = slice(r * TQ, (r + 1) * TQ)
            comb = gt[3 * r:3 * r + 1, :] * o_cmp[g][:, cs] + gt[3 * r + 1:3 * r + 2, :] * o_slc[g][:, cs] \
                + gt[3 * r + 2:3 * r + 3, :] * o_win[g][:, cs]
            outs.append(jnp.where(own, comb, 0.0))
    o = jnp.concatenate(outs, axis=0)
    ms = jnp.sum(o * o, axis=0, keepdims=True) / (N_HEADS * HEAD_DIM)
    o = o * lax.rsqrt(ms + 1e-6) * gn_ref[...]
    o_ref[...] = jnp.concatenate([o[h * LANE:(h + 1) * LANE, :].T for h in range(N_HEADS)], axis=1).astype(BF16)


def _attn_prompt(q, gates, cmpk, cmpv_t, ks, sv_t, kw, wv_t, near_t, up_t, tab_c, sp_t, gn_rep, batch, seq):
    nq = seq // TQ
    t = batch * seq
    rows = ks.shape[2]
    n_slc = seq // SLC_LEN
    n_cmp = seq // CMP_STRIDE
    assert SEL_LANE0 + n_slc <= PAD_LANE < LANE
    return pl.pallas_call(
        functools.partial(_attn_prompt_kernel, n_slc=n_slc),
        grid=(batch, nq),
        in_specs=[
            pl.BlockSpec((N_HEADS, LANE, TQ), lambda b, i: (0, 0, b * nq + i)),
            pl.BlockSpec((N_KV, LANE, TQ), lambda b, i: (0, 0, b * nq + i)),
            pl.BlockSpec((1, N_KV, n_cmp, LANE), lambda b, i: (b, 0, 0, 0)),
            pl.BlockSpec((1, LANE, n_cmp), lambda b, i: (b, 0, 0)),
            pl.BlockSpec((N_KV, 1, rows, LANE), lambda b, i: (0, b, 0, 0)),
            pl.BlockSpec((1, LANE, rows), lambda b, i: (b, 0, 0)),
            pl.BlockSpec((N_KV, 1, rows, LANE), lambda b, i: (0, b, 0, 0)),
            pl.BlockSpec((1, LANE, rows), lambda b, i: (b, 0, 0)),
            _const_spec(near_t.shape),
            _const_spec(up_t.shape),
            pl.BlockSpec((1, N_HEADS, LANE, LANE), lambda b, i: (i, 0, 0, 0)),
            _const_spec(sp_t.shape),
            _const_spec(gn_rep.shape),
        ],
        out_specs=pl.BlockSpec((TQ, N_HEADS * LANE), lambda b, i: (b * nq + i, 0)),
        out_shape=jax.ShapeDtypeStruct((t, N_HEADS * LANE), BF16),
        compiler_params=_cparams(2),
        name="attn_prompt",
    )(q, gates, cmpk, cmpv_t, ks, sv_t, kw, wv_t, near_t, up_t, tab_c, sp_t, gn_rep)


def _sample_cw_kernel(qlo_ref, qgl_ref, ck_ref, cvt_ref, cwin_ref, wnew_ref, tabc_ref, tabw_ref, rb0_ref, sp_ref,
                      ocmp_ref, owin_ref, idx_ref, **static):
    for bb in range(qlo_ref.shape[0]):
        one = lambda r, bb=bb: r.at[pl.ds(bb, 1)]
        _sample_cw_one(one(qlo_ref), one(qgl_ref), one(ck_ref), one(cvt_ref), one(cwin_ref), one(wnew_ref),
                       tabc_ref, tabw_ref, rb0_ref, sp_ref, one(ocmp_ref), one(owin_ref), one(idx_ref), **static)


def _sample_cw_one(qlo_ref, qgl_ref, ck_ref, cvt_ref, cwin_ref, wnew_ref, tabc_ref, tabw_ref, rb0_ref, sp_ref,
                   ocmp_ref, owin_ref, idx_ref, *, n_slc, qblk, win_buf):
    qlo = qlo_ref[0]
    qgl = qgl_ref[0]
    n_cmp_rows = ck_ref.shape[2]
    col = lax.broadcasted_iota(jnp.int32, (N_HEADS, n_cmp_rows), 1)
    grp_of_head = lax.shift_right_logical(lax.broadcasted_iota(jnp.int32, (N_HEADS, n_cmp_rows), 0), 2)
    ok = col >= 1
    raw = jnp.where(grp_of_head == 0, _qk(qlo, ck_ref[0, 0]), _qk(qlo, ck_ref[0, 1]))
    s = jnp.where(ok, raw + tabc_ref[...], NEG)
    e = jnp.exp(s - jnp.max(s, axis=1, keepdims=True))
    p = (e / jnp.sum(e, axis=1, keepdims=True) * ok.astype(F32)).astype(BF16)
    ocmp_ref[0] = _qk(p, cvt_ref[0])
    imp8 = _mm(p, sp_ref[...])

    width = imp8.shape[1]
    head_grp = lax.shift_right_logical(lax.broadcasted_iota(jnp.int32, (N_HEADS, width), 0), 2)
    sidx = lax.broadcasted_iota(jnp.int32, (N_HEADS, width), 1)
    sidx_f = sidx.astype(F32)
    forced = (sidx == 0) | (sidx == qblk) | (sidx == qblk - 1)
    lane = lax.broadcasted_iota(jnp.int32, (N_HEADS, LANE), 1)
    imp_g = [jnp.sum(jnp.where(head_grp == g, imp8, 0.0), axis=0, keepdims=True) for g in range(N_KV)]
    imp = jnp.where(head_grp == 0, imp_g[0], imp_g[1])
    score = jnp.where(sidx <= qblk, imp + jnp.where(forced, FORCE, 0.0), -FORCE)
    score = jnp.where(sidx < n_slc, score, -jnp.inf)
    picked = jnp.zeros((N_HEADS, LANE), F32)
    for k in range(N_SELECT):
        best = jnp.max(score, axis=1, keepdims=True)
        ik = jnp.min(jnp.where(score == best, sidx_f, float(width)), axis=1, keepdims=True)
        picked = jnp.where(lane == k, ik, picked)
        score = jnp.where(sidx_f == ik, -jnp.inf, score)
    for g in range(N_KV):
        idx_ref[0, g:g + 1, :] = picked[GQ * g:GQ * g + 1, :].astype(jnp.int32)

    kt = cwin_ref[0, 0].reshape(N_KV * HEAD_DIM, win_buf).astype(BF16)
    vt = cwin_ref[0, 1].reshape(N_KV * HEAD_DIM, win_buf).astype(BF16)
    knew = wnew_ref[0, :, 0:LANE].astype(BF16).astype(F32)
    vnew = wnew_ref[0, :, LANE:2 * LANE].astype(BF16).astype(F32)
    colw = lax.broadcasted_iota(jnp.int32, (N_HEADS, win_buf), 1)
    okw = colw > win_buf - WINDOW
    sw = jnp.where(okw, _mm(qgl, kt) + tabw_ref[...], NEG)
    s_new = jnp.sum(qgl.astype(F32) * knew, axis=1, keepdims=True) + rb0_ref[:, 0:1]
    mw = jnp.maximum(jnp.max(sw, axis=1, keepdims=True), s_new)
    ew = jnp.where(okw, jnp.exp(sw - mw), 0.0)
    e_new = jnp.exp(s_new - mw)
    lw = jnp.sum(ew, axis=1, keepdims=True) + e_new
    owin_ref[0] = (_qk(ew.astype(BF16), vt) + e_new.astype(BF16).astype(F32) * vnew) / lw


def _sample_cw(qlo, qgl, cmpk, cmpv_t, cwin, wnew, tab_c, tab_w, rb0, sp, n_slc, qblk):
    n = qlo.shape[0]
    win_buf = cwin.shape[-1]
    blk = lambda a: pl.BlockSpec((SAMPLE_BB,) + a.shape[1:], lambda b: (b,) + (0,) * (a.ndim - 1))
    outs = (jax.ShapeDtypeStruct((n, N_HEADS, LANE), F32), jax.ShapeDtypeStruct((n, N_HEADS, LANE), F32),
            jax.ShapeDtypeStruct((n, N_KV, LANE), jnp.int32))
    return pl.pallas_call(
        functools.partial(_sample_cw_kernel, n_slc=n_slc, qblk=qblk, win_buf=win_buf),
        grid=(n // SAMPLE_BB,),
        in_specs=[blk(qlo), blk(qgl), blk(cmpk), blk(cmpv_t), blk(cwin), blk(wnew), _const_spec(tab_c.shape),
                  _const_spec(tab_w.shape), _const_spec(rb0.shape), _const_spec(sp.shape)],
        out_specs=tuple(pl.BlockSpec((SAMPLE_BB,) + o.shape[1:], lambda b: (b, 0, 0)) for o in outs),
        out_shape=outs,
        compiler_params=_cparams(1),
        name="sample_cmp_win",
    )(qlo, qgl, cmpk, cmpv_t, cwin, wnew, tab_c, tab_w, rb0, sp)


def _sample_slc_kernel(pg_ref, hf_ref, bid_ref, *refs, new_block, near_block):
    n_blk = N_KV * N_SELECT
    blocks = refs[:n_blk]
    (q_ref, kvnew_ref, ocmp_ref, owin_ref, gate_ref, tnear_ref, tlast_ref, rb0_ref, gn_ref, o_ref) = refs[n_blk:]
    b = pl.program_id(0)
    n_keys = N_SELECT * PAGE_SIZE
    lane = lax.broadcasted_iota(jnp.int32, (1, n_keys), 1)
    slot = lax.shift_right_logical(lane, 7)
    half = lax.shift_right_logical(lane, 6) & 1
    head_grp = lax.shift_right_logical(lax.broadcasted_iota(jnp.int32, (N_HEADS, LANE), 0), 2)
    lane_grp = lax.shift_right_logical(lax.broadcasted_iota(jnp.int32, (N_HEADS, LANE), 1), 6)
    knew = kvnew_ref[0, :, 256:384].astype(BF16).astype(F32)
    vnew = kvnew_ref[0, :, 384:512].astype(BF16).astype(F32)
    qs = q_ref[0]
    o_slc = jnp.zeros((N_HEADS, LANE), F32)
    for g in range(N_KV):
        mine = blocks[g * N_SELECT:(g + 1) * N_SELECT]
        kt = jnp.concatenate([blk[0, 0].reshape(N_KV * HEAD_DIM, PAGE_SIZE) for blk in mine], axis=1).astype(BF16)
        vt = jnp.concatenate([blk[0, 1].reshape(N_KV * HEAD_DIM, PAGE_SIZE) for blk in mine], axis=1).astype(BF16)
        bid = jnp.zeros((1, n_keys), jnp.int32)
        hsel = jnp.zeros((1, n_keys), jnp.int32)
        has_new = bid_ref[b, g * N_SELECT] == new_block
        for kk in range(N_SELECT):
            bid = jnp.where(slot == kk, bid_ref[b, g * N_SELECT + kk], bid)
            hsel = jnp.where(slot == kk, hf_ref[b, g * N_SELECT + kk], hsel)
            if kk:
                has_new = has_new | (bid_ref[b, g * N_SELECT + kk] == new_block)
        bias = jnp.where(bid == near_block + 1, tlast_ref[...], jnp.where(bid == near_block, tnear_ref[...], 0.0))
        ok = (bid != new_block) & (half == hsel)
        s = jnp.where(ok, _mm(qs, kt) + bias, NEG)
        s_new = jnp.sum(qs.astype(F32) * knew, axis=1, keepdims=True) + rb0_ref[:, 0:1]
        s_new = jnp.where(has_new, s_new, NEG)
        m = jnp.maximum(jnp.max(s, axis=1, keepdims=True), s_new)
        e = jnp.where(ok, jnp.exp(s - m), 0.0)
        e_new = jnp.where(has_new, jnp.exp(s_new - m), 0.0)
        l = jnp.sum(e, axis=1, keepdims=True) + e_new
        og = (_qk(e.astype(BF16), vt) + e_new.astype(BF16).astype(F32) * vnew) / l
        o_slc = jnp.where(head_grp == g, og, o_slc)
    gt = gate_ref[0]
    comb = gt[:, 0:1] * ocmp_ref[0] + gt[:, 1:2] * o_slc + gt[:, 2:3] * owin_ref[0]
    comb = jnp.where(lane_grp == head_grp, comb, 0.0)
    o = jnp.concatenate([comb[h:h + 1, :] for h in range(N_HEADS)], axis=1)
    o_ref[0] = _rms(o, gn_ref[...], N_HEADS * HEAD_DIM).astype(BF16)


def _sample_slc(pages, halves, bids, cache_t, q, kvnew, ocmp, owin, gates, t_near, t_last, rb0, gn_pad,
                new_block, near_block):
    n = q.shape[0]
    n_blk = N_KV * N_SELECT
    blk_specs = [pl.BlockSpec(_PAGE_BLOCK, lambda b, pg, hf, bi, k=k: (pg[b, k], 1, 0, 0, 0))
                 for k in range(n_blk)]
    per_b = lambda a: pl.BlockSpec((1,) + a.shape[1:], lambda b, *_: (b,) + (0,) * (a.ndim - 1))
    cst = lambda a: pl.BlockSpec(a.shape, lambda *_: (0,) * a.ndim)
    grid_spec = pltpu.PrefetchScalarGridSpec(
        num_scalar_prefetch=3,
        grid=(n,),
        in_specs=blk_specs + [per_b(q), per_b(kvnew), per_b(ocmp), per_b(owin), per_b(gates),
                              cst(t_near), cst(t_last), cst(rb0), cst(gn_pad)],
        out_specs=pl.BlockSpec((1, 1, N_HEADS * LANE), lambda b, *_: (b, 0, 0)),
    )
    return pl.pallas_call(
        functools.partial(_sample_slc_kernel, new_block=new_block, near_block=near_block),
        grid_spec=grid_spec,
        out_shape=jax.ShapeDtypeStruct((n, 1, N_HEADS * LANE), BF16),
        compiler_params=_cparams(1),
        name="sample_selected",
    )(pages, halves, bids, *([cache_t] * n_blk), q, kvnew, ocmp, owin, gates, t_near, t_last, rb0, gn_pad)


def _proj_kernel(on_ref, zn_ref, x_ref, wa_ref, wb_ref, g1_ref, b1_ref, wr_ref, br_ref,
                 x1_ref, gate_ref, *, alpha):
    mix = _mm(on_ref[...], wa_ref[...]) + _mm(zn_ref[...], wb_ref[...])
    x1 = _layernorm(alpha * x_ref[...] + mix, g1_ref[...], b1_ref[...])
    x1_ref[...] = x1
    logits = _mm(x1.astype(BF16), wr_ref[...]) + br_ref[...]
    tm = logits.shape[0]
    lane = lax.broadcasted_iota(jnp.int32, (tm, LANE), 1).astype(F32)

    def first_argmax(v, vmax):
        return jnp.min(jnp.where(v == vmax, lane, float(LANE)), axis=1, keepdims=True)

    lg = jnp.where(lane < N_GROUPS, logits[:, 0:LANE], -jnp.inf)
    lg_max = jnp.max(lg, axis=1, keepdims=True)
    eg = jnp.exp(lg - lg_max)
    pg = eg / jnp.sum(eg, axis=1, keepdims=True)
    gidx = first_argmax(lg, lg_max)
    pg_sel = jnp.sum(jnp.where(lane == gidx, pg, 0.0), axis=1, keepdims=True)
    le = jnp.zeros((tm, LANE), F32)
    for gi in range(N_GROUPS):
        le = le + jnp.where(gidx == gi, logits[:, (gi + 1) * LANE:(gi + 2) * LANE], 0.0)
    le = jnp.where(lane < N_EXP, le, -jnp.inf)
    ee = jnp.exp(le - jnp.max(le, axis=1, keepdims=True))
    pe = jnp.where(lane < N_EXP, ee / jnp.sum(ee, axis=1, keepdims=True), -1.0)
    v1 = jnp.max(pe, axis=1, keepdims=True)
    i1 = first_argmax(pe, v1)
    pe2 = jnp.where(lane == i1, -1.0, pe)
    v2 = jnp.max(pe2, axis=1, keepdims=True)
    i2 = first_argmax(pe2, v2)
    tot = v1 + v2
    gate_e = jnp.where(lane == i1, v1 / tot * pg_sel, jnp.where(lane == i2, v2 / tot * pg_sel, 0.0))
    gate_ref[...] = jnp.where(lane == GROUP_LANE, gidx, gate_e)


def _proj(on, zn, x, wa, wb, g1, b1, wr, br, alpha, tm):
    t, d = x.shape
    row = lambda n: pl.BlockSpec((tm, n), lambda i: (i, 0))
    outs = (jax.ShapeDtypeStruct((t, d), F32), jax.ShapeDtypeStruct((t, LANE), F32))
    return pl.pallas_call(
        functools.partial(_proj_kernel, alpha=alpha),
        grid=(t // tm,),
        in_specs=[row(on.shape[1]), row(zn.shape[1]), row(d)] + [_const_spec(a.shape) for a in (wa, wb, g1, b1, wr, br)],
        out_specs=(row(d), row(LANE)),
        out_shape=outs,
        compiler_params=_cparams(1),
        name=f"proj_ln_router_{tm}",
    )(on, zn, x, wa, wb, g1, b1, wr, br)


def _moe_kernel(x1_ref, gate_ref, wgu_ref, wd_ref, g2_ref, b2_ref, out_ref, acc_ref, *, alpha, d_ff):
    gi = pl.program_id(1)

    @pl.when(gi == 0)
    def _():
        acc_ref[...] = jnp.zeros(acc_ref.shape, F32)

    x = x1_ref[...].astype(BF16)
    gate = gate_ref[...]
    gt = jnp.where(gate[:, GROUP_LANE:GROUP_LANE + 1] == gi.astype(F32), gate, 0.0)
    acc = acc_ref[...]
    for e in range(N_EXP):
        au = _mm(x, wgu_ref[0, e])
        h = jax.nn.silu(au[:, 0:d_ff]) * au[:, d_ff:2 * d_ff]
        acc = acc + _mm((h * gt[:, e:e + 1]).astype(BF16), wd_ref[0, e])
    acc_ref[...] = acc

    @pl.when(gi == N_GROUPS - 1)
    def _():
        out_ref[...] = _layernorm(alpha * x1_ref[...] + acc_ref[...], g2_ref[...], b2_ref[...])


def _moe(x1, gate, wgu, wd, g2, b2, alpha, tm):
    t, d = x1.shape
    d_ff = wd.shape[2]
    return pl.pallas_call(
        functools.partial(_moe_kernel, alpha=alpha, d_ff=d_ff),
        grid=(t // tm, N_GROUPS),
        in_specs=[
            pl.BlockSpec((tm, d), lambda i, g: (i, 0)),
            pl.BlockSpec((tm, LANE), lambda i, g: (i, 0)),
            pl.BlockSpec((1, N_EXP, d, 2 * d_ff), lambda i, g: (g, 0, 0, 0)),
            pl.BlockSpec((1, N_EXP, d_ff, d), lambda i, g: (g, 0, 0, 0)),
            _const_spec(g2.shape), _const_spec(b2.shape),
        ],
        out_specs=pl.BlockSpec((tm, d), lambda i, g: (i, 0)),
        out_shape=jax.ShapeDtypeStruct((t, d), F32),
        scratch_shapes=[pltpu.VMEM((tm, d), F32)],
        compiler_params=_cparams(2),
        name=f"moe_ln_{tm}",
    )(x1, gate, wgu, wd, g2, b2)


def _split_bf16(x):
    hi = x.astype(BF16)
    return hi, (x - hi.astype(F32)).astype(BF16)


def _moe_sorted_kernel(cnt_ref, off_ref, x1_ref, gate_ref, drow_ref, dcol_ref, wgu_ref, wd_ref, g2_ref, b2_ref,
                       out_ref, xs_ref, gs_ref, ys_ref, *, alpha, d_ff):
    i = pl.program_id(0)
    gi = pl.program_id(1)
    half = pl.program_id(2)
    n_half = N_EXP // EXPERT_SPLIT
    tmoe = x1_ref.shape[0]

    @pl.when((i == 0) & (gi == 0) & (half == 0))
    def _():
        xs_ref[...] = jnp.zeros(xs_ref.shape, BF16)
        gs_ref[...] = jnp.zeros(gs_ref.shape, F32)
        ys_ref[...] = jnp.zeros(ys_ref.shape, F32)

    @pl.when((gi == 0) & (half == 0))
    def _():
        perm = (lax.broadcasted_iota(jnp.int32, (PERM_ROWS, tmoe), 0) == drow_ref[0]).astype(BF16)
        xs_ref[0:PERM_ROWS, :] = _mm(perm, x1_ref[...].astype(BF16)).astype(BF16)
        g_hi, g_lo = _split_bf16(gate_ref[...])
        gs_ref[0:PERM_ROWS, :] = _mm(perm, g_hi) + _mm(perm, g_lo)

    n = cnt_ref[i * N_GROUPS + gi]
    off = off_ref[i * N_GROUPS + gi]

    def run_pass(c, carry):
        base = pl.multiple_of(off + c * MOE_CHUNK, SEG_ALIGN)
        xc = xs_ref[pl.ds(base, MOE_CHUNK), :]
        gc = gs_ref[pl.ds(base, MOE_CHUNK), :]
        gc = jnp.where(half == 0, gc, pltpu.roll(gc, LANE - n_half, axis=1))
        y = jnp.zeros((MOE_CHUNK, out_ref.shape[1]), F32)
        for e in range(n_half):
            au = _mm(xc, wgu_ref[0, e])
            h = jax.nn.silu(au[:, 0:d_ff]) * au[:, d_ff:2 * d_ff]
            y = y + _mm((h * gc[:, e:e + 1]).astype(BF16), wd_ref[0, e])

        @pl.when(half == 0)
        def _():
            ys_ref[pl.ds(base, MOE_CHUNK), :] = y

        @pl.when(half != 0)
        def _():
            ys_ref[pl.ds(base, MOE_CHUNK), :] = ys_ref[pl.ds(base, MOE_CHUNK), :] + y

        return carry

    lax.fori_loop(0, (n + MOE_CHUNK - 1) // MOE_CHUNK, run_pass, 0)

    @pl.when((gi == N_GROUPS - 1) & (half == EXPERT_SPLIT - 1))
    def _():
        unperm = (dcol_ref[...] == lax.broadcasted_iota(jnp.int32, (tmoe, PERM_ROWS), 1)).astype(BF16)
        y_hi, y_lo = _split_bf16(ys_ref[0:PERM_ROWS, :])
        moe = _mm(unperm, y_hi) + _mm(unperm, y_lo)
        out_ref[...] = _layernorm(alpha * x1_ref[...] + moe, g2_ref[...], b2_ref[...])


def _moe_sorted(x1, gate, wgu, wd, g2, b2, alpha):
    t, d = x1.shape
    d_ff = wd.shape[2]
    n_tiles = t // TMOE
    n_half = N_EXP // EXPERT_SPLIT
    gid = gate[:, GROUP_LANE].astype(jnp.int32).reshape(n_tiles, TMOE)
    onehot = (gid[:, :, None] == jnp.arange(N_GROUPS)[None, None, :]).astype(jnp.int32)
    cnt = jnp.sum(onehot, axis=1)
    rank = jnp.sum((jnp.cumsum(onehot, axis=1) - onehot) * onehot, axis=2)
    seg = (cnt + SEG_ALIGN - 1) // SEG_ALIGN * SEG_ALIGN
    off = jnp.cumsum(seg, axis=1) - seg
    dest = jnp.sum(onehot * off[:, None, :], axis=2) + rank
    assert N_GROUPS * (SEG_ALIGN - 1) <= PERM_ROWS - TMOE
    wgu_h = wgu.reshape(N_GROUPS * EXPERT_SPLIT, n_half, d, 2 * d_ff)
    wd_h = wd.reshape(N_GROUPS * EXPERT_SPLIT, n_half, d_ff, d)
    grid_spec = pltpu.PrefetchScalarGridSpec(
        num_scalar_prefetch=2,
        grid=(n_tiles, N_GROUPS, EXPERT_SPLIT),
        in_specs=[
            pl.BlockSpec((TMOE, d), lambda i, g, h, *_: (i, 0)),
            pl.BlockSpec((TMOE, LANE), lambda i, g, h, *_: (i, 0)),
            pl.BlockSpec((1, 1, TMOE), lambda i, g, h, *_: (i, 0, 0)),
            pl.BlockSpec((TMOE, 1), lambda i, g, h, *_: (i, 0)),
            pl.BlockSpec((1, n_half, d, 2 * d_ff), lambda i, g, h, *_: (g * EXPERT_SPLIT + h, 0, 0, 0)),
            pl.BlockSpec((1, n_half, d_ff, d), lambda i, g, h, *_: (g * EXPERT_SPLIT + h, 0, 0, 0)),
            pl.BlockSpec(g2.shape, lambda *_: (0, 0)), pl.BlockSpec(b2.shape, lambda *_: (0, 0)),
        ],
        out_specs=pl.BlockSpec((TMOE, d), lambda i, g, h, *_: (i, 0)),
        scratch_shapes=[pltpu.VMEM((SORT_ROWS, d), BF16), pltpu.VMEM((SORT_ROWS, LANE), F32),
                        pltpu.VMEM((SORT_ROWS, d), F32)],
    )
    return pl.pallas_call(
        functools.partial(_moe_sorted_kernel, alpha=alpha, d_ff=d_ff),
        grid_spec=grid_spec,
        out_shape=jax.ShapeDtypeStruct((t, d), F32),
        compiler_params=_cparams(3),
        name="moe_sorted_ln",
    )(cnt.reshape(-1), off.reshape(-1), x1, gate, dest.reshape(n_tiles, 1, TMOE), dest.reshape(t, 1), wgu_h, wd_h,
      g2, b2)


def _pad_head_lanes(v):
    vh = v.reshape(N_HEADS, HEAD_DIM)
    z = jnp.zeros_like(vh)
    grp = (jnp.arange(N_HEADS) // GQ)[:, None]
    return jnp.where(grp == 0, jnp.concatenate([vh, z], axis=1), jnp.concatenate([z, vh], axis=1)).reshape(1, -1)


def _pad_head_rows(w):
    wh = w.reshape(N_HEADS, HEAD_DIM, -1)
    z = jnp.zeros_like(wh)
    grp = (jnp.arange(N_HEADS) // GQ)[:, None, None]
    return jnp.where(grp == 0, jnp.concatenate([wh, z], axis=1), jnp.concatenate([z, wh], axis=1)).reshape(
        N_HEADS * LANE, -1)


def kernel(x_prompt, x_sample, cache_kv, page_table, cache_win, state_conv, w_in, conv_w, pe_k, pe_v, w_ck1, w_ck2, w_cv1, w_cv2, g_nsa, g_conv, w_out, ln1_g, ln1_b, w_rg, b_rg, w_re, b_re, w_eg, w_eu, w_ed, ln2_g, ln2_b, rel_bias):
    batch, seq, d_model = x_prompt.shape
    dec_batch, dec_seq = x_sample.shape[0], x_sample.shape[1]
    depth = w_in.shape[0]
    n_pages = page_table.shape[1]
    past = n_pages * PAGE_SIZE
    win_buf = cache_win.shape[2]
    d_nsa = N_HEADS * HEAD_DIM
    d_conv = w_out.shape[1] - d_nsa
    assert dec_seq == 1 and seq % TM == 0 and min(WINDOW, seq) == TM and win_buf == WINDOW
    assert d_conv == 512 and seq // CMP_STRIDE == LANE and n_pages % SAMPLE_PAGES == 0
    alpha = (2 * depth) ** 0.25
    n_slc_s = -(-(past + dec_seq) // SLC_LEN)
    qblk_s = past // SLC_LEN

    idx_np, sp_p_np, sp_s_np = _static_tables(seq, past, win_buf)
    tabs = _bias_tables(rel_bias, jnp.asarray(idx_np))
    nq = seq // TQ
    near_t = jnp.transpose(tabs[0:2], (1, 0, 2, 3)).reshape(N_HEADS, 2 * LANE, LANE)
    up_t = tabs[2, 0]
    tab_c = tabs[_N_NEAR_TILES:_N_NEAR_TILES + nq]
    smp = tabs[_N_NEAR_TILES + nq]
    tab_w_s = smp[:, 0:5, :].reshape(N_HEADS, 5 * LANE)[:, :win_buf]
    tab_c_s = smp[:, 5:13, :].reshape(N_HEADS, 8 * LANE)
    reps = N_SELECT * PAGE_SIZE // SLC_LEN
    t_near = jnp.tile(smp[:, 13, 0:SLC_LEN], (1, reps))
    t_last = jnp.tile(smp[:, 13, SLC_LEN:2 * SLC_LEN], (1, reps))
    rb0_rep = jnp.broadcast_to(smp[:, 14, 0:1], (N_HEADS, LANE))
    sp_p = jnp.asarray(sp_p_np, BF16)
    sp_s = jnp.asarray(sp_s_np, BF16)

    xp = x_prompt.reshape(batch * seq, d_model)
    xs = x_sample.reshape(dec_batch * dec_seq, d_model)
    outs = [[] for _ in range(6)]
    for l in range(depth):
        w_pack = _pack_in_weights(w_in[l])
        gc = g_conv[l].reshape(1, -1)
        gn_pad = _pad_head_lanes(g_nsa[l])
        w1k, w2k, pek = _pack_compress_weights(pe_k[l], w_ck1[l], w_ck2[l], True)
        w1v, w2v, pev = _pack_compress_weights(pe_v[l], w_cv1[l], w_cv2[l], False)
        w1 = jnp.stack([w1k, w1v])
        pe = jnp.stack([pek, pev])
        wa = _pad_head_rows(w_out[l][:d_nsa]).astype(BF16)
        wb = w_out[l][d_nsa:].astype(BF16)
        wr = jnp.concatenate(
            [jnp.pad(w_rg[l], ((0, 0), (0, LANE - N_GROUPS)))]
            + [jnp.pad(w_re[l][:, gi * N_EXP:(gi + 1) * N_EXP], ((0, 0), (0, LANE - N_EXP))) for gi in range(N_GROUPS)],
            axis=1).astype(BF16)
        br = jnp.concatenate(
            [jnp.pad(b_rg[l], (0, LANE - N_GROUPS))]
            + [jnp.pad(b_re[l][gi * N_EXP:(gi + 1) * N_EXP], (0, LANE - N_EXP)) for gi in range(N_GROUPS)]).reshape(1, -1)
        wgu = jnp.concatenate([w_eg[l], w_eu[l]], axis=-1).astype(BF16)
        wd = w_ed[l].astype(BF16)
        g1, b1 = ln1_g[l].reshape(1, -1), ln1_b[l].reshape(1, -1)
        g2, b2 = ln2_g[l].reshape(1, -1), ln2_b[l].reshape(1, -1)

        q_p, kvt_p, ks, sv, kw, wv, wtail_t, gates_p, zn_p, ctail = _inproj_prompt(xp, w_pack, conv_w[l], gc, batch, seq)
        kvt_p = kvt_p.reshape(batch, 4, N_KV, HEAD_DIM, seq)
        cmpk_p, cmpvt_p = _compress_prompt(kvt_p, batch, seq, w1, w2k, w2v, pe)
        gn_rep = jnp.broadcast_to(gn_pad.reshape(-1, 1), (N_HEADS * LANE, TQ))
        on_p = _attn_prompt(q_p, gates_p, cmpk_p, cmpvt_p, ks, sv, kw, wv, near_t, up_t, tab_c, sp_p, gn_rep,
                            batch, seq)
        x1_p, gate_p = _proj(on_p, zn_p, xp, wa, wb, g1, b1, wr, br, alpha, TM)
        y_p = _moe_sorted(x1_p, gate_p, wgu, wd, g2, b2, alpha)

        st = state_conv[l]
        qlo_s, qgl_s, kv_s, win_s, gates_s, zn_s, u_s = _inproj_sample(xs, w_pack, conv_w[l], gc, st[:, 0], st[:, 1])
        cache_t = jnp.transpose(cache_kv[l], (0, 2, 3, 4, 1))
        cwin_t = jnp.transpose(cache_win[l], (0, 2, 3, 4, 1))
        cmpk_s, cmpvt_s = _compress_sample(cache_t, page_table, w1, w2k, w2v, pe)
        qlo_s3 = qlo_s.reshape(dec_batch, N_HEADS, LANE)
        qgl_s3 = qgl_s.reshape(dec_batch, N_HEADS, LANE)
        ocmp, owin, sel_idx = _sample_cw(qlo_s3, qgl_s3, cmpk_s, cmpvt_s, cwin_t, win_s.reshape(dec_batch, 1, 256), tab_c_s,
                                         tab_w_s, rb0_rep, sp_s, n_slc_s, qblk_s)
        bids = sel_idx[:, :, :N_SELECT].reshape(dec_batch, N_KV * N_SELECT)
        blk_pages = jnp.take_along_axis(page_table, jnp.minimum(bids // 2, n_pages - 1), axis=1)
        gates_s3 = jnp.pad(gates_s.reshape(dec_batch, N_KV, LANE)[:, :, :3 * GQ].reshape(dec_batch, N_HEADS, 3),
                           ((0, 0), (0, 0), (0, LANE - 3)))
        on_s = _sample_slc(blk_pages, bids % 2, bids, cache_t, qgl_s3, kv_s.reshape(dec_batch, 1, 512), ocmp, owin,
                           gates_s3, t_near, t_last, rb0_rep, gn_pad, n_slc_s - 1, qblk_s - 2)
        on_s = on_s.reshape(dec_batch, N_HEADS * LANE)
        x1_s, gate_s = _proj(on_s, zn_s, xs, wa, wb, g1, b1, wr, br, alpha, dec_batch)
        y_s = _moe(x1_s, gate_s, wgu, wd, g2, b2, alpha, dec_batch)

        to_token_major = lambda a: jnp.transpose(a, (0, 4, 1, 2, 3))
        outs[0].append(to_token_major(kvt_p))
        outs[1].append(kv_s.reshape(dec_batch, dec_seq, 4, N_KV, HEAD_DIM))
        outs[2].append(to_token_major(wtail_t.reshape(batch, 2, N_KV, HEAD_DIM, TM)))
        win_all_t = jnp.concatenate([cwin_t, win_s.reshape(dec_batch, 2, N_KV, HEAD_DIM, dec_seq)], axis=-1)
        outs[3].append(to_token_major(win_all_t[..., win_all_t.shape[-1] - min(WINDOW, past + dec_seq):]))
        outs[4].append(ctail)
        outs[5].append(jnp.concatenate([st, u_s[:, None, :]], axis=1)[:, dec_seq:])
        xp, xs = y_p, y_s
    return (xp.reshape(batch, seq, d_model), xs.reshape(dec_batch, dec_seq, d_model),
            jnp.stack(outs[0]), jnp.stack(outs[1]), jnp.stack(outs[2]), jnp.stack(outs[3]),
            jnp.stack(outs[4]), jnp.stack(outs[5]))
```

```python
import functools
import math

import numpy as np
import jax
import jax.numpy as jnp
from jax import lax
from jax.experimental import pallas as pl
from jax.experimental.pallas import tpu as pltpu

F32 = jnp.float32
BF16 = jnp.bfloat16

HEAD_DIM = 64
N_KV = 2
GQ = 4
N_HEADS = N_KV * GQ
CONV_W = 3
CMP_STRIDE = 16
CMP_LEN = 32
CMP_HIDDEN = 2 * HEAD_DIM
SLC_LEN = 64
N_SELECT = 16
WINDOW = 512
N_BUCKETS = 32
MAX_DISTANCE = 128
N_GROUPS = 4
N_EXP = 8
PAGE_SIZE = 128
NEG = -1e30
FORCE = 1e6

LANE = 128
SUBLANE = 8
VMEM_LIMIT = 52 * 1024 * 1024

TQ = 128
TM = 512
FAR = 512
TAIL = WINDOW + TQ
SAMPLE_PAGES = 64
SAMPLE_BB = 4
TMOE = 1024
MOE_CHUNK = 320
SEG_ALIGN = 16
PERM_ROWS = TMOE + LANE
SORT_ROWS = PERM_ROWS + MOE_CHUNK
EXPERT_SPLIT = 2
GROUP_LANE = N_EXP

KEY_MASK = 2.0 ** 100
TABLE_MASK = 2 * NEG
MASKED_BUCKET = N_BUCKETS
SEL_LANE0 = HEAD_DIM
PAD_LANE = HEAD_DIM + 32


def _cparams(n_axes):
    return pltpu.CompilerParams(dimension_semantics=("arbitrary",) * n_axes, vmem_limit_bytes=VMEM_LIMIT)


def _const_spec(shape):
    nd = len(shape)
    return pl.BlockSpec(shape, lambda *_, nd=nd: (0,) * nd)


def _qk(a, b):
    return lax.dot_general(a, b, (((1,), (1,)), ((), ())), preferred_element_type=F32)


def _mm(a, b):
    return jnp.dot(a, b, preferred_element_type=F32)


def _bucket_np(dist):
    n = np.maximum(dist, 0)
    max_exact = N_BUCKETS // 2
    nf = np.maximum(n, 1).astype(np.float32)
    large = max_exact + (np.log(nf / np.float32(max_exact)) / np.float32(math.log(MAX_DISTANCE / max_exact))
                         * np.float32(N_BUCKETS - max_exact)).astype(np.int32)
    large = np.minimum(large, N_BUCKETS - 1)
    return np.where(n < max_exact, n, large).astype(np.int32)


def _overlap_np(c, s):
    c0 = c * CMP_STRIDE
    s0 = s * SLC_LEN
    return np.maximum(np.minimum(c0 + CMP_LEN, s0 + SLC_LEN) - np.maximum(c0, s0), 0)


_N_NEAR_TILES = 3


def _static_tables(seq, past, win_buf):
    nq = seq // TQ
    i = np.arange(LANE)[:, None]
    j = np.arange(LANE)[None, :]
    far_bucket = N_BUCKETS - 1
    tiles = [_bucket_np(LANE + j - i),
             np.where(j >= i, _bucket_np(j - i), MASKED_BUCKET),
             np.where(i > j, far_bucket, MASKED_BUCKET) + 0 * i]
    tiles += [_bucket_np(TQ * q + j - (CMP_STRIDE * i + CMP_STRIDE - 1)) for q in range(nq)]
    smp = np.full((LANE, LANE), 10 * MAX_DISTANCE, np.int64)
    k = np.arange(5 * LANE)
    smp[0:5] = np.where(k < win_buf, win_buf - k, 0).reshape(5, LANE)
    ci = np.arange(8 * LANE)
    smp[5:13] = np.maximum(past - (CMP_STRIDE * ci + CMP_STRIDE - 1), 0).reshape(8, LANE)
    nb = past // SLC_LEN
    smp[13] = past - (SLC_LEN * (nb - 2) + np.arange(LANE))
    smp[14] = 0
    tiles.append(_bucket_np(smp))
    idx = np.stack(tiles).astype(np.int32)

    def sp(nrows, ncols, n_slc):
        r = np.arange(nrows)[:, None]
        s = np.arange(ncols)[None, :]
        ov = _overlap_np(r - 1, s)
        return np.where((r >= 1) & (s < n_slc), ov, 0).astype(np.float32)

    sp_p = sp(seq // CMP_STRIDE, LANE, -(-seq // SLC_LEN)).T
    n_slc_s = -(-(past + 1) // SLC_LEN)
    sp_s = sp(past // CMP_STRIDE, 3 * LANE, n_slc_s)
    return idx, sp_p, sp_s


def _bias_kernel(rb_ref, idx_ref, out_ref):
    idx = idx_ref[0]
    accs = [jnp.full((LANE, LANE), TABLE_MASK, F32) for _ in range(N_HEADS)]
    for b in range(N_BUCKETS):
        hit = idx == b
        for h in range(N_HEADS):
            accs[h] = jnp.where(hit, rb_ref[b, h] - rb_ref[N_BUCKETS - 1, h], accs[h])
    for h in range(N_HEADS):
        out_ref[0, h] = accs[h]


def _bias_tables(rel_bias, idx):
    nt = idx.shape[0]
    return pl.pallas_call(
        _bias_kernel,
        grid=(nt,),
        in_specs=[pl.BlockSpec(memory_space=pltpu.SMEM),
                  pl.BlockSpec((1, LANE, LANE), lambda t: (t, 0, 0))],
        out_specs=pl.BlockSpec((1, N_HEADS, LANE, LANE), lambda t: (t, 0, 0, 0)),
        out_shape=jax.ShapeDtypeStruct((nt, N_HEADS, LANE, LANE), F32),
        compiler_params=_cparams(1),
        name="bias_tables",
    )(rel_bias, idx)


_Q0, _Q1 = 0, N_HEADS * HEAD_DIM
_KV0, _KV1 = _Q1, _Q1 + 512
_WN0, _WN1 = _KV1, _KV1 + 256
_GT0, _GT1 = _WN1, _WN1 + 2 * LANE
_CV0, _CV1 = _GT1, _GT1 + 3 * 512


def _pack_in_weights(w):
    d = w.shape[0]
    d_nsa = N_HEADS * HEAD_DIM
    wq_pad = w[:, :d_nsa]
    o = d_nsa
    w_kv = w[:, o:o + 512]
    w_win = w[:, o + 512:o + 768]
    wg = w[:, o + 768:o + 768 + 3 * N_HEADS].reshape(d, N_KV, 3 * GQ)
    wg_pad = jnp.pad(wg, ((0, 0), (0, 0), (0, LANE - 3 * GQ))).reshape(d, N_KV * LANE)
    w_conv = w[:, o + 768 + 3 * N_HEADS:]
    return jnp.concatenate([wq_pad, w_kv, w_win, wg_pad, w_conv], axis=1).astype(BF16)


def _rms(x, gain, n, eps=1e-6):
    ms = jnp.sum(x * x, axis=-1, keepdims=True) / n
    return x * lax.rsqrt(ms + eps) * gain


def _layernorm(y, gain, bias, eps=1e-5):
    mu = jnp.mean(y, axis=-1, keepdims=True)
    d = y - mu
    var = jnp.mean(d * d, axis=-1, keepdims=True)
    return d * lax.rsqrt(var + eps) * gain + bias


def _inproj_prompt_kernel(x_ref, w_ref, cw_ref, gc_ref, ks_fill, kw_fill, sv_fill, wv_fill,
                          q_ref, kv_ref, ks_ref, sv_ref, kw_ref, wv_ref, wtail_ref, gate_ref, zn_ref, ctail_ref,
                          uext_ref, *, tiles_per_batch):
    del ks_fill, kw_fill, sv_fill, wv_fill
    i = pl.program_id(0)
    tm = x_ref.shape[0]
    xb = x_ref[...].astype(BF16)

    def seg(a, b):
        return _mm(xb, w_ref[:, a:b])

    qp_t = (seg(_Q0, _Q1) * (HEAD_DIM ** -0.5)).T
    q_fill = jnp.zeros((LANE - HEAD_DIM, tm), F32)
    for h in range(N_HEADS):
        q_ref[h] = jnp.concatenate([qp_t[h * HEAD_DIM:(h + 1) * HEAD_DIM, :], q_fill], axis=0).astype(BF16)
    kv = seg(_KV0, _KV1)
    kv_t = kv.T
    kv_ref[0] = kv_t
    win = seg(_WN0, _WN1)
    win_t = win.T
    wtail_ref[0] = win_t
    pos = (i % tiles_per_batch) * tm + lax.broadcasted_iota(jnp.int32, (tm, LANE - HEAD_DIM), 0)
    lane = lax.broadcasted_iota(jnp.int32, (tm, LANE - HEAD_DIM), 1)
    blk_flag = jnp.where(lane == lax.shift_right_logical(pos, 6), KEY_MASK, 0.0).astype(BF16)
    no_flag = jnp.zeros((tm, LANE - HEAD_DIM), BF16)
    for g in range(N_KV):
        ks_ref[g, 0] = jnp.concatenate([kv[:, 256 + g * HEAD_DIM:256 + (g + 1) * HEAD_DIM].astype(BF16), blk_flag], axis=1)
        kw_ref[g, 0] = jnp.concatenate([win[:, g * HEAD_DIM:(g + 1) * HEAD_DIM].astype(BF16), no_flag], axis=1)
    sv_ref[0] = kv_t[384:512, :].astype(BF16)
    wv_ref[0] = win_t[LANE:2 * LANE, :].astype(BF16)
    gt_t = jax.nn.sigmoid(seg(_GT0, _GT1)).T
    gate_ref[0] = gt_t[0:LANE, :]
    gate_ref[1] = gt_t[LANE:2 * LANE, :]

    conv = seg(_CV0, _CV1)
    cb = conv[:, 0:512]
    u = conv[:, 512:1024] * conv[:, 1024:1536]
    first = (i % tiles_per_batch) == 0

    @pl.when(first)
    def _():
        uext_ref[0:SUBLANE, :] = jnp.zeros((SUBLANE, 512), F32)

    @pl.when(jnp.logical_not(first))
    def _():
        uext_ref[0:SUBLANE, :] = uext_ref[tm:tm + SUBLANE, :]

    uext_ref[SUBLANE:tm + SUBLANE, :] = u
    um1 = uext_ref[SUBLANE - 1:tm + SUBLANE - 1, :]
    um2 = uext_ref[SUBLANE - 2:tm + SUBLANE - 2, :]
    z = cb * (um2 * cw_ref[0:1, :] + um1 * cw_ref[1:2, :] + u * cw_ref[2:3, :])
    zn_ref[...] = _rms(z, gc_ref[...], 512).astype(BF16)
    ctail_ref[0] = uext_ref[tm + SUBLANE - 2:tm + SUBLANE, :]


def _inproj_prompt(x, w, conv_w, g_conv, batch, seq):
    t, d = x.shape
    n_tiles = t // TM
    tpb = seq // TM
    assert WINDOW == TM
    rows = WINDOW + seq
    key_pad = jnp.zeros((LANE,), BF16).at[PAD_LANE].set(KEY_MASK)
    k_fill = jnp.broadcast_to(key_pad, (N_KV, batch, rows, LANE))
    v_fill = jnp.zeros((batch, LANE, rows), BF16)
    outs = (
        jax.ShapeDtypeStruct((N_HEADS, LANE, t), BF16),
        jax.ShapeDtypeStruct((batch, 512, seq), F32),
        jax.ShapeDtypeStruct(k_fill.shape, BF16),
        jax.ShapeDtypeStruct(v_fill.shape, BF16),
        jax.ShapeDtypeStruct(k_fill.shape, BF16),
        jax.ShapeDtypeStruct(v_fill.shape, BF16),
        jax.ShapeDtypeStruct((batch, 256, TM), F32),
        jax.ShapeDtypeStruct((N_KV, LANE, t), F32),
        jax.ShapeDtypeStruct((t, 512), BF16),
        jax.ShapeDtypeStruct((batch, CONV_W - 1, 512), F32),
    )
    row = lambda n: pl.BlockSpec((TM, n), lambda i: (i, 0))
    planes_t = lambda n: pl.BlockSpec((n, LANE, TM), lambda i: (0, 0, i))
    keys_out = pl.BlockSpec((N_KV, 1, TM, LANE), lambda i: (0, i // tpb, i % tpb + 1, 0))
    vals_out = pl.BlockSpec((1, LANE, TM), lambda i: (i // tpb, 0, i % tpb + 1))
    in_place = pl.BlockSpec(memory_space=pl.ANY)
    return pl.pallas_call(
        functools.partial(_inproj_prompt_kernel, tiles_per_batch=tpb),
        grid=(n_tiles,),
        in_specs=[row(d), _const_spec(w.shape), _const_spec(conv_w.shape), _const_spec(g_conv.shape),
                  in_place, in_place, in_place, in_place],
        input_output_aliases={4: 2, 5: 4, 6: 3, 7: 5},
        out_specs=(
            planes_t(N_HEADS),
            pl.BlockSpec((1, 512, TM), lambda i: (i // tpb, 0, i % tpb)),
            keys_out, vals_out, keys_out, vals_out,
            pl.BlockSpec((1, 256, TM), lambda i: (i // tpb, 0, 0)),
            planes_t(N_KV),
            row(512),
            pl.BlockSpec((1, CONV_W - 1, 512), lambda i: (i // tpb, 0, 0)),
        ),
        out_shape=outs,
        scratch_shapes=[pltpu.VMEM((TM + SUBLANE, 512), F32)],
        compiler_params=_cparams(1),
        name="inproj_prompt",
    )(x, w, conv_w, g_conv, k_fill, k_fill, v_fill, v_fill)


def _inproj_sample_kernel(x_ref, w_ref, cw_ref, gc_ref, s0_ref, s1_ref,
                          qlo_ref, qgl_ref, kv_ref, win_ref, gate_ref, zn_ref, u_ref):
    xb = x_ref[...].astype(BF16)

    def seg(a, b):
        return _mm(xb, w_ref[:, a:b])

    qp = seg(_Q0, _Q1) * (HEAD_DIM ** -0.5)
    fill = jnp.zeros((qp.shape[0], LANE - HEAD_DIM), F32)
    heads = [qp[:, h * HEAD_DIM:(h + 1) * HEAD_DIM] for h in range(N_HEADS)]
    qlo_ref[...] = jnp.concatenate([x for qh in heads for x in (qh, fill)], axis=1).astype(BF16)
    qgl_ref[...] = jnp.concatenate([x for h, qh in enumerate(heads) for x in ((qh, fill) if h < GQ else (fill, qh))],
                                   axis=1).astype(BF16)
    kv_ref[...] = seg(_KV0, _KV1)
    win_ref[...] = seg(_WN0, _WN1)
    gate_ref[...] = jax.nn.sigmoid(seg(_GT0, _GT1))
    conv = seg(_CV0, _CV1)
    cb = conv[:, 0:512]
    u = conv[:, 512:1024] * conv[:, 1024:1536]
    z = cb * (s0_ref[...] * cw_ref[0:1, :] + s1_ref[...] * cw_ref[1:2, :] + u * cw_ref[2:3, :])
    zn_ref[...] = _rms(z, gc_ref[...], 512).astype(BF16)
    u_ref[...] = u


def _inproj_sample(x, w, conv_w, g_conv, s0, s1):
    n = x.shape[0]
    outs = (
        jax.ShapeDtypeStruct((n, N_HEADS * LANE), BF16),
        jax.ShapeDtypeStruct((n, N_HEADS * LANE), BF16),
        jax.ShapeDtypeStruct((n, 512), F32),
        jax.ShapeDtypeStruct((n, 256), F32),
        jax.ShapeDtypeStruct((n, N_KV * LANE), F32),
        jax.ShapeDtypeStruct((n, 512), BF16),
        jax.ShapeDtypeStruct((n, 512), F32),
    )
    args = (x, w, conv_w, g_conv, s0, s1)
    return pl.pallas_call(
        _inproj_sample_kernel,
        grid=(1,),
        in_specs=[_const_spec(a.shape) for a in args],
        out_specs=tuple(_const_spec(o.shape) for o in outs),
        out_shape=outs,
        compiler_params=_cparams(1),
        name="inproj_sample",
    )(*args)


def _pack_compress_weights(pe, w1, w2, per_group_out):
    w = w1.reshape(2, CMP_STRIDE, HEAD_DIM, CMP_HIDDEN).transpose(1, 2, 0, 3).reshape(CMP_STRIDE, HEAD_DIM, 2 * CMP_HIDDEN)
    z = jnp.zeros_like(w)
    top = jnp.concatenate([w, z], axis=-1)
    bot = jnp.concatenate([z, w], axis=-1)
    w1bd = jnp.stack([top, bot], axis=1).reshape(CMP_STRIDE * 2 * HEAD_DIM, 4 * CMP_HIDDEN).astype(BF16)
    z2 = jnp.zeros_like(w2)
    if per_group_out:
        w2bd = jnp.concatenate([jnp.concatenate([w2, z2, z2, z2], axis=1),
                                jnp.concatenate([z2, z2, w2, z2], axis=1)], axis=0).astype(BF16)
    else:
        w2bd = jnp.concatenate([jnp.concatenate([w2, z2], axis=1), jnp.concatenate([z2, w2], axis=1)], axis=0).astype(BF16)
    pe_rows = jnp.broadcast_to(pe.reshape(2, CMP_STRIDE, 1, HEAD_DIM), (2, CMP_STRIDE, N_KV, HEAD_DIM))
    pe_rows = pe_rows.reshape(2, CMP_STRIDE * N_KV * HEAD_DIM)
    pe_rows = jnp.pad(pe_rows, ((0, SUBLANE - 2), (0, 0)))
    return w1bd, w2bd, pe_rows


def _compress_kernel(*refs, n_pages, n_prefetch):
    refs = refs[n_prefetch:]
    page_refs = refs[:n_pages]
    w1_ref, w2k_ref, w2v_ref, pe_ref, ck_ref, cvt_ref, carry_ref, tok_ref = refs[n_pages:]
    step = pl.program_id(1)
    m = n_pages * (PAGE_SIZE // CMP_STRIDE)

    @pl.when(step == 0)
    def _():
        carry_ref[...] = jnp.zeros(carry_ref.shape, F32)

    row0 = lax.broadcasted_iota(jnp.int32, (m, CMP_HIDDEN), 0) == 0
    for t, w2_ref in enumerate((w2k_ref, w2v_ref)):
        for p, pr in enumerate(page_refs):
            slab = pr[0, t].reshape(N_KV * HEAD_DIM, PAGE_SIZE)
            tok_ref[t, p * PAGE_SIZE:(p + 1) * PAGE_SIZE, :] = slab.T
        pieces = [tok_ref[t, pl.ds(j, m, stride=CMP_STRIDE), :] for j in range(CMP_STRIDE)]
        lhs = jnp.concatenate([jnp.concatenate(pieces, axis=1), pe_ref[t]], axis=0).astype(BF16)
        a = _mm(lhs, w1_ref[t])
        hs = []
        for g in range(N_KV):
            c0 = g * 2 * CMP_HIDDEN
            a0 = a[0:m, c0:c0 + CMP_HIDDEN]
            a1 = a[0:m, c0 + CMP_HIDDEN:c0 + 2 * CMP_HIDDEN]
            pe_term = a[m:m + 1, c0:c0 + CMP_HIDDEN] + a[m + 1:m + 2, c0 + CMP_HIDDEN:c0 + 2 * CMP_HIDDEN]
            prev = carry_ref[t, 0:1, g * CMP_HIDDEN:(g + 1) * CMP_HIDDEN]
            shifted = jnp.where(row0, prev, pltpu.roll(a0, 1, axis=0))
            carry_ref[t, 0:1, g * CMP_HIDDEN:(g + 1) * CMP_HIDDEN] = a0[m - 1:m, :]
            hs.append(jax.nn.gelu(shifted + a1 + pe_term))
        out = _mm(jnp.concatenate(hs, axis=1).astype(BF16), w2_ref[...])
        if t == 0:
            ck_ref[0, 0] = out[:, 0:LANE].astype(BF16)
            ck_ref[0, 1] = out[:, LANE:2 * LANE].astype(BF16)
        else:
            cvt_ref[0] = out.T.astype(BF16)


def _compress_call(page_specs, page_args, prefetch, grid, batch, n_chunks, m, w1, w2k, w2v, pe, name):
    n_pages = len(page_specs)
    n_pf = len(prefetch)
    cspec = lambda shape: pl.BlockSpec(shape, lambda *_: (0,) * len(shape))
    grid_spec = pltpu.PrefetchScalarGridSpec(
        num_scalar_prefetch=n_pf,
        grid=grid,
        in_specs=list(page_specs) + [cspec(w1.shape), cspec(w2k.shape), cspec(w2v.shape), cspec(pe.shape)],
        out_specs=(pl.BlockSpec((1, N_KV, m, LANE), lambda b, s, *_: (b, 0, s, 0)),
                   pl.BlockSpec((1, LANE, m), lambda b, s, *_: (b, 0, s))),
        scratch_shapes=[pltpu.VMEM((2, SUBLANE, 2 * CMP_HIDDEN), F32),
                        pltpu.VMEM((2, n_pages * PAGE_SIZE, N_KV * HEAD_DIM), F32)],
    )
    return pl.pallas_call(
        functools.partial(_compress_kernel, n_pages=n_pages, n_prefetch=n_pf),
        grid_spec=grid_spec,
        out_shape=(jax.ShapeDtypeStruct((batch, N_KV, n_chunks, LANE), BF16),
                   jax.ShapeDtypeStruct((batch, LANE, n_chunks), BF16)),
        compiler_params=_cparams(2),
        name=name,
    )(*prefetch, *page_args, w1, w2k, w2v, pe)


_PAGE_BLOCK = (1, 2, N_KV, HEAD_DIM, PAGE_SIZE)


def _compress_prompt(kv_t, batch, seq, w1, w2k, w2v, pe):
    n_pages = seq // PAGE_SIZE
    rows = PAGE_SIZE // CMP_STRIDE
    specs = [pl.BlockSpec(_PAGE_BLOCK, lambda b, s, p=p: (b, 0, 0, 0, p)) for p in range(n_pages)]
    return _compress_call(specs, [kv_t] * n_pages, (), (batch, 1), batch, n_pages * rows, n_pages * rows,
                          w1, w2k, w2v, pe, "compress_prompt")


def _compress_sample(cache_t, page_table, w1, w2k, w2v, pe):
    batch, n_pages = page_table.shape
    rows = PAGE_SIZE // CMP_STRIDE
    specs = [pl.BlockSpec(_PAGE_BLOCK, lambda b, s, pt, k=k: (pt[b, s * SAMPLE_PAGES + k], 0, 0, 0, 0))
             for k in range(SAMPLE_PAGES)]
    return _compress_call(specs, [cache_t] * SAMPLE_PAGES, (page_table,), (batch, n_pages // SAMPLE_PAGES), batch,
                          n_pages * rows, SAMPLE_PAGES * rows, w1, w2k, w2v, pe, "compress_sample")


def _tile4(x):
    return jnp.concatenate([x] * GQ, axis=1)


def _softmax_fold(state, pieces, vt):
    m, l, acc = state
    m_new = m
    for pc in pieces:
        m_new = jnp.maximum(m_new, jnp.max(pc, axis=0, keepdims=True))
    alpha = jnp.exp(m - m_new)
    es = [jnp.exp(pc - m_new) for pc in pieces]
    l = alpha * l
    for e in es:
        l = l + jnp.sum(e, axis=0, keepdims=True)
    e_all = es[0] if len(es) == 1 else jnp.concatenate(es, axis=0)
    acc = alpha * acc + _mm(vt, e_all.astype(BF16))
    return m_new, l, acc


def _attn_prompt_kernel(q_ref, gate_ref, ck_ref, cvt_ref, ks_ref, sv_ref, kw_ref, wv_ref,
                        near_ref, up_ref, tabc_ref, sp_ref, gn_ref, o_ref, *, n_slc):
    qi = pl.program_id(1)
    q0 = pl.multiple_of(qi * TQ, TQ)
    krow = lax.broadcasted_iota(jnp.int32, (LANE, TQ), 0)
    qcol = lax.broadcasted_iota(jnp.int32, (LANE, TQ), 1)
    cmp_ok4 = _tile4((krow >= 1) & (CMP_STRIDE * krow + (CMP_STRIDE - 1) <= q0 + qcol))
    sidx = lax.broadcasted_iota(jnp.int32, (n_slc, TQ), 0)
    qblk = lax.shift_right_logical(q0 + lax.broadcasted_iota(jnp.int32, (n_slc, TQ), 1), 6)
    sel_valid = sidx <= qblk
    sel_forced = (sidx == 0) | (sidx == qblk) | (sidx == qblk - 1)
    n_far = lax.shift_right_logical(jnp.maximum(qi - 1, 0), 2)
    up4 = _tile4(up_ref[...])
    cols = GQ * TQ
    fresh = (jnp.full((1, cols), NEG, F32), jnp.zeros((1, cols), F32), jnp.zeros((LANE, cols), F32))

    groups = range(N_KV)
    qa, near, o_cmp = [], [], []
    for g in groups:
        heads = [GQ * g + r for r in range(GQ)]
        q_t = [q_ref[h] for h in heads]
        near.append(jnp.concatenate([near_ref[h] for h in heads], axis=1))

        bias_c = jnp.concatenate([tabc_ref[0, h] for h in heads], axis=1)
        s = jnp.where(cmp_ok4, _mm(ck_ref[0, g], jnp.concatenate(q_t, axis=1)) + bias_c, NEG)
        e = jnp.exp(s - jnp.max(s, axis=0, keepdims=True))
        p = (e / jnp.sum(e, axis=0, keepdims=True) * cmp_ok4.astype(F32)).astype(BF16)
        o_cmp.append(_mm(cvt_ref[0], p))
        imp4 = _mm(sp_ref[...], p)
        imp = imp4[:, 0:TQ] + imp4[:, TQ:2 * TQ] + imp4[:, 2 * TQ:3 * TQ] + imp4[:, 3 * TQ:4 * TQ]

        score = jnp.where(sel_valid, imp[0:n_slc, :] + jnp.where(sel_forced, FORCE, 0.0), -FORCE)
        rank = jnp.zeros((n_slc, TQ), F32)
        for k in range(1, n_slc):
            other = pltpu.roll(score, k, axis=0)
            rank = rank + jnp.where(sidx >= k, (other >= score).astype(F32), (other > score).astype(F32))
        sel_t = (rank < float(N_SELECT)).astype(F32)
        aug = jnp.concatenate([jnp.zeros((SEL_LANE0, TQ), F32), sel_t - 1.0,
                               jnp.zeros((LANE - SEL_LANE0 - n_slc, TQ), F32)], axis=0)
        aug = jnp.where(krow == PAD_LANE, -1.0, aug).astype(BF16)
        qa.append(jnp.concatenate([qh + aug for qh in q_t], axis=1))

    def far_tile(it, states):
        start = pl.multiple_of(q0 - FAR * (n_far - it), LANE)
        vt = sv_ref[0, :, pl.ds(start, FAR)]
        return tuple(_softmax_fold(states[g], [_mm(ks_ref[g, 0, pl.ds(start, FAR), :], qa[g])], vt) for g in groups)

    states = lax.fori_loop(0, n_far, far_tile, (fresh,) * N_KV)
    v_tail = sv_ref[0, :, pl.ds(q0, TAIL)]
    w_tail = wv_ref[0, :, pl.ds(q0, TAIL)]
    o_slc, o_win = [], []
    for g in groups:
        sc = _mm(ks_ref[g, 0, pl.ds(q0, TAIL), :], qa[g])
        pieces = [sc[0:TAIL - 2 * LANE], sc[TAIL - 2 * LANE:TAIL] + near[g]]
        _, l, acc = _softmax_fold(states[g], pieces, v_tail)
        o_slc.append(acc / l)
        sc = _mm(kw_ref[g, 0, pl.ds(q0, TAIL), :], qa[g])
        pieces = [sc[0:LANE] + up4, sc[LANE:TAIL - 2 * LANE], sc[TAIL - 2 * LANE:TAIL] + near[g]]
        _, l, acc = _softmax_fold(fresh, pieces, w_tail)
        o_win.append(acc / l)

    outs = []
    for g in groups:
        gt = gate_ref[g]
        own = lax.shift_right_logical(krow, 6) == g
        for r in range(GQ):
            cs = slice(r * TQ, (r + 1) * TQ)
            comb = gt[3 * r:3 * r + 1, :] * o_cmp[g][:, cs] + gt[3 * r + 1:3 * r + 2, :] * o_slc[g][:, cs] \
                + gt[3 * r + 2:3 * r + 3, :] * o_win[g][:, cs]
            outs.append(jnp.where(own, comb, 0.0))
    o = jnp.concatenate(outs, axis=0)
    ms = jnp.sum(o * o, axis=0, keepdims=True) / (N_HEADS * HEAD_DIM)
    o = o * lax.rsqrt(ms + 1e-6) * gn_ref[...]
    o_ref[...] = jnp.concatenate([o[h * LANE:(h + 1) * LANE, :].T for h in range(N_HEADS)], axis=1).astype(BF16)


def _attn_prompt(q, gates, cmpk, cmpv_t, ks, sv_t, kw, wv_t, near_t, up_t, tab_c, sp_t, gn_rep, batch, seq):
    nq = seq // TQ
    t = batch * seq
    rows = ks.shape[2]
    n_slc = seq // SLC_LEN
    n_cmp = seq // CMP_STRIDE
    assert SEL_LANE0 + n_slc <= PAD_LANE < LANE
    return pl.pallas_call(
        functools.partial(_attn_prompt_kernel, n_slc=n_slc),
        grid=(batch, nq),
        in_specs=[
            pl.BlockSpec((N_HEADS, LANE, TQ), lambda b, i: (0, 0, b * nq + i)),
            pl.BlockSpec((N_KV, LANE, TQ), lambda b, i: (0, 0, b * nq + i)),
            pl.BlockSpec((1, N_KV, n_cmp, LANE), lambda b, i: (b, 0, 0, 0)),
            pl.BlockSpec((1, LANE, n_cmp), lambda b, i: (b, 0, 0)),
            pl.BlockSpec((N_KV, 1, rows, LANE), lambda b, i: (0, b, 0, 0)),
            pl.BlockSpec((1, LANE, rows), lambda b, i: (b, 0, 0)),
            pl.BlockSpec((N_KV, 1, rows, LANE), lambda b, i: (0, b, 0, 0)),
            pl.BlockSpec((1, LANE, rows), lambda b, i: (b, 0, 0)),
            _const_spec(near_t.shape),
            _const_spec(up_t.shape),
            pl.BlockSpec((1, N_HEADS, LANE, LANE), lambda b, i: (i, 0, 0, 0)),
            _const_spec(sp_t.shape),
            _const_spec(gn_rep.shape),
        ],
        out_specs=pl.BlockSpec((TQ, N_HEADS * LANE), lambda b, i: (b * nq + i, 0)),
        out_shape=jax.ShapeDtypeStruct((t, N_HEADS * LANE), BF16),
        compiler_params=_cparams(2),
        name="attn_prompt",
    )(q, gates, cmpk, cmpv_t, ks, sv_t, kw, wv_t, near_t, up_t, tab_c, sp_t, gn_rep)


def _sample_cw_kernel(qlo_ref, qgl_ref, ck_ref, cvt_ref, cwin_ref, wnew_ref, tabc_ref, tabw_ref, rb0_ref, sp_ref,
                      ocmp_ref, owin_ref, idx_ref, **static):
    for bb in range(qlo_ref.shape[0]):
        one = lambda r, bb=bb: r.at[pl.ds(bb, 1)]
        _sample_cw_one(one(qlo_ref), one(qgl_ref), one(ck_ref), one(cvt_ref), one(cwin_ref), one(wnew_ref),
                       tabc_ref, tabw_ref, rb0_ref, sp_ref, one(ocmp_ref), one(owin_ref), one(idx_ref), **static)


def _sample_cw_one(qlo_ref, qgl_ref, ck_ref, cvt_ref, cwin_ref, wnew_ref, tabc_ref, tabw_ref, rb0_ref, sp_ref,
                   ocmp_ref, owin_ref, idx_ref, *, n_slc, qblk, win_buf):
    qlo = qlo_ref[0]
    qgl = qgl_ref[0]
    n_cmp_rows = ck_ref.shape[2]
    col = lax.broadcasted_iota(jnp.int32, (N_HEADS, n_cmp_rows), 1)
    grp_of_head = lax.shift_right_logical(lax.broadcasted_iota(jnp.int32, (N_HEADS, n_cmp_rows), 0), 2)
    ok = col >= 1
    raw = jnp.where(grp_of_head == 0, _qk(qlo, ck_ref[0, 0]), _qk(qlo, ck_ref[0, 1]))
    s = jnp.where(ok, raw + tabc_ref[...], NEG)
    e = jnp.exp(s - jnp.max(s, axis=1, keepdims=True))
    p = (e / jnp.sum(e, axis=1, keepdims=True) * ok.astype(F32)).astype(BF16)
    ocmp_ref[0] = _qk(p, cvt_ref[0])
    imp8 = _mm(p, sp_ref[...])

    width = imp8.shape[1]
    head_grp = lax.shift_right_logical(lax.broadcasted_iota(jnp.int32, (N_HEADS, width), 0), 2)
    sidx = lax.broadcasted_iota(jnp.int32, (N_HEADS, width), 1)
    sidx_f = sidx.astype(F32)
    forced = (sidx == 0) | (sidx == qblk) | (sidx == qblk - 1)
    lane = lax.broadcasted_iota(jnp.int32, (N_HEADS, LANE), 1)
    imp_g = [jnp.sum(jnp.where(head_grp == g, imp8, 0.0), axis=0, keepdims=True) for g in range(N_KV)]
    imp = jnp.where(head_grp == 0, imp_g[0], imp_g[1])
    score = jnp.where(sidx <= qblk, imp + jnp.where(forced, FORCE, 0.0), -FORCE)
    score = jnp.where(sidx < n_slc, score, -jnp.inf)
    picked = jnp.zeros((N_HEADS, LANE), F32)
    for k in range(N_SELECT):
        best = jnp.max(score, axis=1, keepdims=True)
        ik = jnp.min(jnp.where(score == best, sidx_f, float(width)), axis=1, keepdims=True)
        picked = jnp.where(lane == k, ik, picked)
        score = jnp.where(sidx_f == ik, -jnp.inf, score)
    for g in range(N_KV):
        idx_ref[0, g:g + 1, :] = picked[GQ * g:GQ * g + 1, :].astype(jnp.int32)

    kt = cwin_ref[0, 0].reshape(N_KV * HEAD_DIM, win_buf).astype(BF16)
    vt = cwin_ref[0, 1].reshape(N_KV * HEAD_DIM, win_buf).astype(BF16)
    knew = wnew_ref[0, :, 0:LANE].astype(BF16).astype(F32)
    vnew = wnew_ref[0, :, LANE:2 * LANE].astype(BF16).astype(F32)
    colw = lax.broadcasted_iota(jnp.int32, (N_HEADS, win_buf), 1)
    okw = colw > win_buf - WINDOW
    sw = jnp.where(okw, _mm(qgl, kt) + tabw_ref[...], NEG)
    s_new = jnp.sum(qgl.astype(F32) * knew, axis=1, keepdims=True) + rb0_ref[:, 0:1]
    mw = jnp.maximum(jnp.max(sw, axis=1, keepdims=True), s_new)
    ew = jnp.where(okw, jnp.exp(sw - mw), 0.0)
    e_new = jnp.exp(s_new - mw)
    lw = jnp.sum(ew, axis=1, keepdims=True) + e_new
    owin_ref[0] = (_qk(ew.astype(BF16), vt) + e_new.astype(BF16).astype(F32) * vnew) / lw


def _sample_cw(qlo, qgl, cmpk, cmpv_t, cwin, wnew, tab_c, tab_w, rb0, sp, n_slc, qblk):
    n = qlo.shape[0]
    win_buf = cwin.shape[-1]
    blk = lambda a: pl.BlockSpec((SAMPLE_BB,) + a.shape[1:], lambda b: (b,) + (0,) * (a.ndim - 1))
    outs = (jax.ShapeDtypeStruct((n, N_HEADS, LANE), F32), jax.ShapeDtypeStruct((n, N_HEADS, LANE), F32),
            jax.ShapeDtypeStruct((n, N_KV, LANE), jnp.int32))
    return pl.pallas_call(
        functools.partial(_sample_cw_kernel, n_slc=n_slc, qblk=qblk, win_buf=win_buf),
        grid=(n // SAMPLE_BB,),
        in_specs=[blk(qlo), blk(qgl), blk(cmpk), blk(cmpv_t), blk(cwin), blk(wnew), _const_spec(tab_c.shape),
                  _const_spec(tab_w.shape), _const_spec(rb0.shape), _const_spec(sp.shape)],
        out_specs=tuple(pl.BlockSpec((SAMPLE_BB,) + o.shape[1:], lambda b: (b, 0, 0)) for o in outs),
        out_shape=outs,
        compiler_params=_cparams(1),
        name="sample_cmp_win",
    )(qlo, qgl, cmpk, cmpv_t, cwin, wnew, tab_c, tab_w, rb0, sp)


def _sample_slc_kernel(pg_ref, hf_ref, bid_ref, *refs, new_block, near_block):
    n_blk = N_KV * N_SELECT
    blocks = refs[:n_blk]
    (q_ref, kvnew_ref, ocmp_ref, owin_ref, gate_ref, tnear_ref, tlast_ref, rb0_ref, gn_ref, o_ref) = refs[n_blk:]
    b = pl.program_id(0)
    n_keys = N_SELECT * PAGE_SIZE
    lane = lax.broadcasted_iota(jnp.int32, (1, n_keys), 1)
    slot = lax.shift_right_logical(lane, 7)
    half = lax.shift_right_logical(lane, 6) & 1
    head_grp = lax.shift_right_logical(lax.broadcasted_iota(jnp.int32, (N_HEADS, LANE), 0), 2)
    lane_grp = lax.shift_right_logical(lax.broadcasted_iota(jnp.int32, (N_HEADS, LANE), 1), 6)
    knew = kvnew_ref[0, :, 256:384].astype(BF16).astype(F32)
    vnew = kvnew_ref[0, :, 384:512].astype(BF16).astype(F32)
    qs = q_ref[0]
    o_slc = jnp.zeros((N_HEADS, LANE), F32)
    for g in range(N_KV):
        mine = blocks[g * N_SELECT:(g + 1) * N_SELECT]
        kt = jnp.concatenate([blk[0, 0].reshape(N_KV * HEAD_DIM, PAGE_SIZE) for blk in mine], axis=1).astype(BF16)
        vt = jnp.concatenate([blk[0, 1].reshape(N_KV * HEAD_DIM, PAGE_SIZE) for blk in mine], axis=1).astype(BF16)
        bid = jnp.zeros((1, n_keys), jnp.int32)
        hsel = jnp.zeros((1, n_keys), jnp.int32)
        has_new = bid_ref[b, g * N_SELECT] == new_block
        for kk in range(N_SELECT):
            bid = jnp.where(slot == kk, bid_ref[b, g * N_SELECT + kk], bid)
            hsel = jnp.where(slot == kk, hf_ref[b, g * N_SELECT + kk], hsel)
            if kk:
                has_new = has_new | (bid_ref[b, g * N_SELECT + kk] == new_block)
        bias = jnp.where(bid == near_block + 1, tlast_ref[...], jnp.where(bid == near_block, tnear_ref[...], 0.0))
        ok = (bid != new_block) & (half == hsel)
        s = jnp.where(ok, _mm(qs, kt) + bias, NEG)
        s_new = jnp.sum(qs.astype(F32) * knew, axis=1, keepdims=True) + rb0_ref[:, 0:1]
        s_new = jnp.where(has_new, s_new, NEG)
        m = jnp.maximum(jnp.max(s, axis=1, keepdims=True), s_new)
        e = jnp.where(ok, jnp.exp(s - m), 0.0)
        e_new = jnp.where(has_new, jnp.exp(s_new - m), 0.0)
        l = jnp.sum(e, axis=1, keepdims=True) + e_new
        og = (_qk(e.astype(BF16), vt) + e_new.astype(BF16).astype(F32) * vnew) / l
        o_slc = jnp.where(head_grp == g, og, o_slc)
    gt = gate_ref[0]
    comb = gt[:, 0:1] * ocmp_ref[0] + gt[:, 1:2] * o_slc + gt[:, 2:3] * owin_ref[0]
    comb = jnp.where(lane_grp == head_grp, comb, 0.0)
    o = jnp.concatenate([comb[h:h + 1, :] for h in range(N_HEADS)], axis=1)
    o_ref[0] = _rms(o, gn_ref[...], N_HEADS * HEAD_DIM).astype(BF16)


def _sample_slc(pages, halves, bids, cache_t, q, kvnew, ocmp, owin, gates, t_near, t_last, rb0, gn_pad,
                new_block, near_block):
    n = q.shape[0]
    n_blk = N_KV * N_SELECT
    blk_specs = [pl.BlockSpec(_PAGE_BLOCK, lambda b, pg, hf, bi, k=k: (pg[b, k], 1, 0, 0, 0))
                 for k in range(n_blk)]
    per_b = lambda a: pl.BlockSpec((1,) + a.shape[1:], lambda b, *_: (b,) + (0,) * (a.ndim - 1))
    cst = lambda a: pl.BlockSpec(a.shape, lambda *_: (0,) * a.ndim)
    grid_spec = pltpu.PrefetchScalarGridSpec(
        num_scalar_prefetch=3,
        grid=(n,),
        in_specs=blk_specs + [per_b(q), per_b(kvnew), per_b(ocmp), per_b(owin), per_b(gates),
                              cst(t_near), cst(t_last), cst(rb0), cst(gn_pad)],
        out_specs=pl.BlockSpec((1, 1, N_HEADS * LANE), lambda b, *_: (b, 0, 0)),
    )
    return pl.pallas_call(
        functools.partial(_sample_slc_kernel, new_block=new_block, near_block=near_block),
        grid_spec=grid_spec,
        out_shape=jax.ShapeDtypeStruct((n, 1, N_HEADS * LANE), BF16),
        compiler_params=_cparams(1),
        name="sample_selected",
    )(pages, halves, bids, *([cache_t] * n_blk), q, kvnew, ocmp, owin, gates, t_near, t_last, rb0, gn_pad)


def _proj_kernel(on_ref, zn_ref, x_ref, wa_ref, wb_ref, g1_ref, b1_ref, wr_ref, br_ref,
                 x1_ref, gate_ref, *, alpha):
    mix = _mm(on_ref[...], wa_ref[...]) + _mm(zn_ref[...], wb_ref[...])
    x1 = _layernorm(alpha * x_ref[...] + mix, g1_ref[...], b1_ref[...])
    x1_ref[...] = x1
    logits = _mm(x1.astype(BF16), wr_ref[...]) + br_ref[...]
    tm = logits.shape[0]
    lane = lax.broadcasted_iota(jnp.int32, (tm, LANE), 1).astype(F32)

    def first_argmax(v, vmax):
        return jnp.min(jnp.where(v == vmax, lane, float(LANE)), axis=1, keepdims=True)

    lg = jnp.where(lane < N_GROUPS, logits[:, 0:LANE], -jnp.inf)
    lg_max = jnp.max(lg, axis=1, keepdims=True)
    eg = jnp.exp(lg - lg_max)
    pg = eg / jnp.sum(eg, axis=1, keepdims=True)
    gidx = first_argmax(lg, lg_max)
    pg_sel = jnp.sum(jnp.where(lane == gidx, pg, 0.0), axis=1, keepdims=True)
    le = jnp.zeros((tm, LANE), F32)
    for gi in range(N_GROUPS):
        le = le + jnp.where(gidx == gi, logits[:, (gi + 1) * LANE:(gi + 2) * LANE], 0.0)
    le = jnp.where(lane < N_EXP, le, -jnp.inf)
    ee = jnp.exp(le - jnp.max(le, axis=1, keepdims=True))
    pe = jnp.where(lane < N_EXP, ee / jnp.sum(ee, axis=1, keepdims=True), -1.0)
    v1 = jnp.max(pe, axis=1, keepdims=True)
    i1 = first_argmax(pe, v1)
    pe2 = jnp.where(lane == i1, -1.0, pe)
    v2 = jnp.max(pe2, axis=1, keepdims=True)
    i2 = first_argmax(pe2, v2)
    tot = v1 + v2
    gate_e = jnp.where(lane == i1, v1 / tot * pg_sel, jnp.where(lane == i2, v2 / tot * pg_sel, 0.0))
    gate_ref[...] = jnp.where(lane == GROUP_LANE, gidx, gate_e)


def _proj(on, zn, x, wa, wb, g1, b1, wr, br, alpha, tm):
    t, d = x.shape
    row = lambda n: pl.BlockSpec((tm, n), lambda i: (i, 0))
    outs = (jax.ShapeDtypeStruct((t, d), F32), jax.ShapeDtypeStruct((t, LANE), F32))
    return pl.pallas_call(
        functools.partial(_proj_kernel, alpha=alpha),
        grid=(t // tm,),
        in_specs=[row(on.shape[1]), row(zn.shape[1]), row(d)] + [_const_spec(a.shape) for a in (wa, wb, g1, b1, wr, br)],
        out_specs=(row(d), row(LANE)),
        out_shape=outs,
        compiler_params=_cparams(1),
        name=f"proj_ln_router_{tm}",
    )(on, zn, x, wa, wb, g1, b1, wr, br)


def _moe_kernel(x1_ref, gate_ref, wg_ref, wu_ref, wd_ref, g2_ref, b2_ref, out_ref, acc_ref, *, alpha):
    gi = pl.program_id(1)

    @pl.when(gi == 0)
    def _():
        acc_ref[...] = jnp.zeros(acc_ref.shape, F32)

    x = x1_ref[...].astype(BF16)
    gate = gate_ref[...]
    gt = jnp.where(gate[:, GROUP_LANE:GROUP_LANE + 1] == gi.astype(F32), gate, 0.0)
    acc = acc_ref[...]
    for e in range(N_EXP):
        h = jax.nn.silu(_mm(x, wg_ref[0, e])) * _mm(x, wu_ref[0, e])
        acc = acc + _mm((h * gt[:, e:e + 1]).astype(BF16), wd_ref[0, e])
    acc_ref[...] = acc

    @pl.when(gi == N_GROUPS - 1)
    def _():
        out_ref[...] = _layernorm(alpha * x1_ref[...] + acc_ref[...], g2_ref[...], b2_ref[...])


def _moe(x1, gate, wg, wu, wd, g2, b2, alpha, tm):
    t, d = x1.shape
    d_ff = wd.shape[2]
    return pl.pallas_call(
        functools.partial(_moe_kernel, alpha=alpha),
        grid=(t // tm, N_GROUPS),
        in_specs=[
            pl.BlockSpec((tm, d), lambda i, g: (i, 0)),
            pl.BlockSpec((tm, LANE), lambda i, g: (i, 0)),
            pl.BlockSpec((1, N_EXP, d, d_ff), lambda i, g: (g, 0, 0, 0)),
            pl.BlockSpec((1, N_EXP, d, d_ff), lambda i, g: (g, 0, 0, 0)),
            pl.BlockSpec((1, N_EXP, d_ff, d), lambda i, g: (g, 0, 0, 0)),
            _const_spec(g2.shape), _const_spec(b2.shape),
        ],
        out_specs=pl.BlockSpec((tm, d), lambda i, g: (i, 0)),
        out_shape=jax.ShapeDtypeStruct((t, d), F32),
        scratch_shapes=[pltpu.VMEM((tm, d), F32)],
        compiler_params=_cparams(2),
        name=f"moe_ln_{tm}",
    )(x1, gate, wg, wu, wd, g2, b2)


def _split_bf16(x):
    hi = x.astype(BF16)
    return hi, (x - hi.astype(F32)).astype(BF16)


def _moe_sorted_kernel(cnt_ref, off_ref, x1_ref, gate_ref, drow_ref, dcol_ref, wg_ref, wu_ref, wd_ref, g2_ref, b2_ref,
                       out_ref, xs_ref, gs_ref, ys_ref, *, alpha):
    i = pl.program_id(0)
    gi = pl.program_id(1)
    half = pl.program_id(2)
    n_half = N_EXP // EXPERT_SPLIT
    tmoe = x1_ref.shape[0]

    @pl.when((i == 0) & (gi == 0) & (half == 0))
    def _():
        xs_ref[...] = jnp.zeros(xs_ref.shape, BF16)
        gs_ref[...] = jnp.zeros(gs_ref.shape, F32)
        ys_ref[...] = jnp.zeros(ys_ref.shape, F32)

    @pl.when((gi == 0) & (half == 0))
    def _():
        perm = (lax.broadcasted_iota(jnp.int32, (PERM_ROWS, tmoe), 0) == drow_ref[0]).astype(BF16)
        xs_ref[0:PERM_ROWS, :] = _mm(perm, x1_ref[...].astype(BF16)).astype(BF16)
        g_hi, g_lo = _split_bf16(gate_ref[...])
        gs_ref[0:PERM_ROWS, :] = _mm(perm, g_hi) + _mm(perm, g_lo)

    n = cnt_ref[i * N_GROUPS + gi]
    off = off_ref[i * N_GROUPS + gi]

    def run_pass(c, carry):
        base = pl.multiple_of(off + c * MOE_CHUNK, SEG_ALIGN)
        xc = xs_ref[pl.ds(base, MOE_CHUNK), :]
        gc = gs_ref[pl.ds(base, MOE_CHUNK), :]
        gc = jnp.where(half == 0, gc, pltpu.roll(gc, LANE - n_half, axis=1))
        y = jnp.zeros((MOE_CHUNK, out_ref.shape[1]), F32)
        for e in range(n_half):
            h = jax.nn.silu(_mm(xc, wg_ref[0, e])) * _mm(xc, wu_ref[0, e])
            y = y + _mm((h * gc[:, e:e + 1]).astype(BF16), wd_ref[0, e])

        @pl.when(half == 0)
        def _():
            ys_ref[pl.ds(base, MOE_CHUNK), :] = y

        @pl.when(half != 0)
        def _():
            ys_ref[pl.ds(base, MOE_CHUNK), :] = ys_ref[pl.ds(base, MOE_CHUNK), :] + y

        return carry

    lax.fori_loop(0, (n + MOE_CHUNK - 1) // MOE_CHUNK, run_pass, 0)

    @pl.when((gi == N_GROUPS - 1) & (half == EXPERT_SPLIT - 1))
    def _():
        unperm = (dcol_ref[...] == lax.broadcasted_iota(jnp.int32, (tmoe, PERM_ROWS), 1)).astype(BF16)
        y_hi, y_lo = _split_bf16(ys_ref[0:PERM_ROWS, :])
        moe = _mm(unperm, y_hi) + _mm(unperm, y_lo)
        out_ref[...] = _layernorm(alpha * x1_ref[...] + moe, g2_ref[...], b2_ref[...])


def _moe_sorted(x1, gate, wg, wu, wd, g2, b2, alpha):
    t, d = x1.shape
    d_ff = wd.shape[2]
    n_tiles = t // TMOE
    n_half = N_EXP // EXPERT_SPLIT
    gid = gate[:, GROUP_LANE].astype(jnp.int32).reshape(n_tiles, TMOE)
    onehot = (gid[:, :, None] == jnp.arange(N_GROUPS)[None, None, :]).astype(jnp.int32)
    cnt = jnp.sum(onehot, axis=1)
    rank = jnp.sum((jnp.cumsum(onehot, axis=1) - onehot) * onehot, axis=2)
    seg = (cnt + SEG_ALIGN - 1) // SEG_ALIGN * SEG_ALIGN
    off = jnp.cumsum(seg, axis=1) - seg
    dest = jnp.sum(onehot * off[:, None, :], axis=2) + rank
    assert N_GROUPS * (SEG_ALIGN - 1) <= PERM_ROWS - TMOE
    wg_h = wg.reshape(N_GROUPS * EXPERT_SPLIT, n_half, d, d_ff)
    wu_h = wu.reshape(N_GROUPS * EXPERT_SPLIT, n_half, d, d_ff)
    wd_h = wd.reshape(N_GROUPS * EXPERT_SPLIT, n_half, d_ff, d)
    grid_spec = pltpu.PrefetchScalarGridSpec(
        num_scalar_prefetch=2,
        grid=(n_tiles, N_GROUPS, EXPERT_SPLIT),
        in_specs=[
            pl.BlockSpec((TMOE, d), lambda i, g, h, *_: (i, 0)),
            pl.BlockSpec((TMOE, LANE), lambda i, g, h, *_: (i, 0)),
            pl.BlockSpec((1, 1, TMOE), lambda i, g, h, *_: (i, 0, 0)),
            pl.BlockSpec((TMOE, 1), lambda i, g, h, *_: (i, 0)),
            pl.BlockSpec((1, n_half, d, d_ff), lambda i, g, h, *_: (g * EXPERT_SPLIT + h, 0, 0, 0)),
            pl.BlockSpec((1, n_half, d, d_ff), lambda i, g, h, *_: (g * EXPERT_SPLIT + h, 0, 0, 0)),
            pl.BlockSpec((1, n_half, d_ff, d), lambda i, g, h, *_: (g * EXPERT_SPLIT + h, 0, 0, 0)),
            pl.BlockSpec(g2.shape, lambda *_: (0, 0)), pl.BlockSpec(b2.shape, lambda *_: (0, 0)),
        ],
        out_specs=pl.BlockSpec((TMOE, d), lambda i, g, h, *_: (i, 0)),
        scratch_shapes=[pltpu.VMEM((SORT_ROWS, d), BF16), pltpu.VMEM((SORT_ROWS, LANE), F32),
                        pltpu.VMEM((SORT_ROWS, d), F32)],
    )
    return pl.pallas_call(
        functools.partial(_moe_sorted_kernel, alpha=alpha),
        grid_spec=grid_spec,
        out_shape=jax.ShapeDtypeStruct((t, d), F32),
        compiler_params=_cparams(3),
        name="moe_sorted_ln",
    )(cnt.reshape(-1), off.reshape(-1), x1, gate, dest.reshape(n_tiles, 1, TMOE), dest.reshape(t, 1), wg_h, wu_h, wd_h,
      g2, b2)


def _pad_head_lanes(v):
    vh = v.reshape(N_HEADS, HEAD_DIM)
    z = jnp.zeros_like(vh)
    grp = (jnp.arange(N_HEADS) // GQ)[:, None]
    return jnp.where(grp == 0, jnp.concatenate([vh, z], axis=1), jnp.concatenate([z, vh], axis=1)).reshape(1, -1)


def _pad_head_rows(w):
    wh = w.reshape(N_HEADS, HEAD_DIM, -1)
    z = jnp.zeros_like(wh)
    grp = (jnp.arange(N_HEADS) // GQ)[:, None, None]
    return jnp.where(grp == 0, jnp.concatenate([wh, z], axis=1), jnp.concatenate([z, wh], axis=1)).reshape(
        N_HEADS * LANE, -1)


def kernel(x_prompt, x_sample, cache_kv, page_table, cache_win, state_conv, w_in, conv_w, pe_k, pe_v, w_ck1, w_ck2, w_cv1, w_cv2, g_nsa, g_conv, w_out, ln1_g, ln1_b, w_rg, b_rg, w_re, b_re, w_eg, w_eu, w_ed, ln2_g, ln2_b, rel_bias):
    batch, seq, d_model = x_prompt.shape
    dec_batch, dec_seq = x_sample.shape[0], x_sample.shape[1]
    depth = w_in.shape[0]
    n_pages = page_table.shape[1]
    past = n_pages * PAGE_SIZE
    win_buf = cache_win.shape[2]
    d_nsa = N_HEADS * HEAD_DIM
    d_conv = w_out.shape[1] - d_nsa
    assert dec_seq == 1 and seq % TM == 0 and min(WINDOW, seq) == TM and win_buf == WINDOW
    assert d_conv == 512 and seq // CMP_STRIDE == LANE and n_pages % SAMPLE_PAGES == 0
    alpha = (2 * depth) ** 0.25
    n_slc_s = -(-(past + dec_seq) // SLC_LEN)
    qblk_s = past // SLC_LEN

    idx_np, sp_p_np, sp_s_np = _static_tables(seq, past, win_buf)
    tabs = _bias_tables(rel_bias, jnp.asarray(idx_np))
    nq = seq // TQ
    near_t = jnp.transpose(tabs[0:2], (1, 0, 2, 3)).reshape(N_HEADS, 2 * LANE, LANE)
    up_t = tabs[2, 0]
    tab_c = tabs[_N_NEAR_TILES:_N_NEAR_TILES + nq]
    smp = tabs[_N_NEAR_TILES + nq]
    tab_w_s = smp[:, 0:5, :].reshape(N_HEADS, 5 * LANE)[:, :win_buf]
    tab_c_s = smp[:, 5:13, :].reshape(N_HEADS, 8 * LANE)
    reps = N_SELECT * PAGE_SIZE // SLC_LEN
    t_near = jnp.tile(smp[:, 13, 0:SLC_LEN], (1, reps))
    t_last = jnp.tile(smp[:, 13, SLC_LEN:2 * SLC_LEN], (1, reps))
    rb0_rep = jnp.broadcast_to(smp[:, 14, 0:1], (N_HEADS, LANE))
    sp_p = jnp.asarray(sp_p_np, BF16)
    sp_s = jnp.asarray(sp_s_np, BF16)

    xp = x_prompt.reshape(batch * seq, d_model)
    xs = x_sample.reshape(dec_batch * dec_seq, d_model)
    outs = [[] for _ in range(6)]
    for l in range(depth):
        w_pack = _pack_in_weights(w_in[l])
        gc = g_conv[l].reshape(1, -1)
        gn_pad = _pad_head_lanes(g_nsa[l])
        w1k, w2k, pek = _pack_compress_weights(pe_k[l], w_ck1[l], w_ck2[l], True)
        w1v, w2v, pev = _pack_compress_weights(pe_v[l], w_cv1[l], w_cv2[l], False)
        w1 = jnp.stack([w1k, w1v])
        pe = jnp.stack([pek, pev])
        wa = _pad_head_rows(w_out[l][:d_nsa]).astype(BF16)
        wb = w_out[l][d_nsa:].astype(BF16)
        wr = jnp.concatenate(
            [jnp.pad(w_rg[l], ((0, 0), (0, LANE - N_GROUPS)))]
            + [jnp.pad(w_re[l][:, gi * N_EXP:(gi + 1) * N_EXP], ((0, 0), (0, LANE - N_EXP))) for gi in range(N_GROUPS)],
            axis=1).astype(BF16)
        br = jnp.concatenate(
            [jnp.pad(b_rg[l], (0, LANE - N_GROUPS))]
            + [jnp.pad(b_re[l][gi * N_EXP:(gi + 1) * N_EXP], (0, LANE - N_EXP)) for gi in range(N_GROUPS)]).reshape(1, -1)
        wg, wu, wd = w_eg[l].astype(BF16), w_eu[l].astype(BF16), w_ed[l].astype(BF16)
        g1, b1 = ln1_g[l].reshape(1, -1), ln1_b[l].reshape(1, -1)
        g2, b2 = ln2_g[l].reshape(1, -1), ln2_b[l].reshape(1, -1)

        q_p, kvt_p, ks, sv, kw, wv, wtail_t, gates_p, zn_p, ctail = _inproj_prompt(xp, w_pack, conv_w[l], gc, batch, seq)
        kvt_p = kvt_p.reshape(batch, 4, N_KV, HEAD_DIM, seq)
        cmpk_p, cmpvt_p = _compress_prompt(kvt_p, batch, seq, w1, w2k, w2v, pe)
        gn_rep = jnp.broadcast_to(gn_pad.reshape(-1, 1), (N_HEADS * LANE, TQ))
        on_p = _attn_prompt(q_p, gates_p, cmpk_p, cmpvt_p, ks, sv, kw, wv, near_t, up_t, tab_c, sp_p, gn_rep,
                            batch, seq)
        x1_p, gate_p = _proj(on_p, zn_p, xp, wa, wb, g1, b1, wr, br, alpha, TM)
        y_p = _moe_sorted(x1_p, gate_p, wg, wu, wd, g2, b2, alpha)

        st = state_conv[l]
        qlo_s, qgl_s, kv_s, win_s, gates_s, zn_s, u_s = _inproj_sample(xs, w_pack, conv_w[l], gc, st[:, 0], st[:, 1])
        cache_t = jnp.transpose(cache_kv[l], (0, 2, 3, 4, 1))
        cwin_t = jnp.transpose(cache_win[l], (0, 2, 3, 4, 1))
        cmpk_s, cmpvt_s = _compress_sample(cache_t, page_table, w1, w2k, w2v, pe)
        qlo_s3 = qlo_s.reshape(dec_batch, N_HEADS, LANE)
        qgl_s3 = qgl_s.reshape(dec_batch, N_HEADS, LANE)
        ocmp, owin, sel_idx = _sample_cw(qlo_s3, qgl_s3, cmpk_s, cmpvt_s, cwin_t, win_s.reshape(dec_batch, 1, 256), tab_c_s,
                                         tab_w_s, rb0_rep, sp_s, n_slc_s, qblk_s)
        bids = sel_idx[:, :, :N_SELECT].reshape(dec_batch, N_KV * N_SELECT)
        blk_pages = jnp.take_along_axis(page_table, jnp.minimum(bids // 2, n_pages - 1), axis=1)
        gates_s3 = jnp.pad(gates_s.reshape(dec_batch, N_KV, LANE)[:, :, :3 * GQ].reshape(dec_batch, N_HEADS, 3),
                           ((0, 0), (0, 0), (0, LANE - 3)))
        on_s = _sample_slc(blk_pages, bids % 2, bids, cache_t, qgl_s3, kv_s.reshape(dec_batch, 1, 512), ocmp, owin,
                           gates_s3, t_near, t_last, rb0_rep, gn_pad, n_slc_s - 1, qblk_s - 2)
        on_s = on_s.reshape(dec_batch, N_HEADS * LANE)
        x1_s, gate_s = _proj(on_s, zn_s, xs, wa, wb, g1, b1, wr, br, alpha, dec_batch)
        y_s = _moe(x1_s, gate_s, wg, wu, wd, g2, b2, alpha, dec_batch)

        to_token_major = lambda a: jnp.transpose(a, (0, 4, 1, 2, 3))
        outs[0].append(to_token_major(kvt_p))
        outs[1].append(kv_s.reshape(dec_batch, dec_seq, 4, N_KV, HEAD_DIM))
        outs[2].append(to_token_major(wtail_t.reshape(batch, 2, N_KV, HEAD_DIM, TM)))
        win_all_t = jnp.concatenate([cwin_t, win_s.reshape(dec_batch, 2, N_KV, HEAD_DIM, dec_seq)], axis=-1)
        outs[3].append(to_token_major(win_all_t[..., win_all_t.shape[-1] - min(WINDOW, past + dec_seq):]))
        outs[4].append(ctail)
        outs[5].append(jnp.concatenate([st, u_s[:, None, :]], axis=1)[:, dec_seq:])
        xp, xs = y_p, y_s
    return (xp.reshape(batch, seq, d_model), xs.reshape(dec_batch, dec_seq, d_model),
            jnp.stack(outs[0]), jnp.stack(outs[1]), jnp.stack(outs[2]), jnp.stack(outs[3]),
            jnp.stack(outs[4]), jnp.stack(outs[5]))
```

```python
import functools
import math

import numpy as np
import jax
import jax.numpy as jnp
from jax import lax
from jax.experimental import pallas as pl
from jax.experimental.pallas import tpu as pltpu

F32 = jnp.float32
BF16 = jnp.bfloat16

HEAD_DIM = 64
N_KV = 2
GQ = 4
N_HEADS = N_KV * GQ
CONV_W = 3
CMP_STRIDE = 16
CMP_LEN = 32
CMP_HIDDEN = 2 * HEAD_DIM
SLC_LEN = 64
N_SELECT = 16
WINDOW = 512
N_BUCKETS = 32
MAX_DISTANCE = 128
N_GROUPS = 4
N_EXP = 8
PAGE_SIZE = 128
NEG = -1e30
FORCE = 1e6

LANE = 128
SUBLANE = 8
VMEM_LIMIT = 52 * 1024 * 1024

TQ = 128
TM = 512
FAR = 512
TAIL = WINDOW + TQ
SAMPLE_PAGES = 64
SAMPLE_BB = 4
TMOE = 1024
MOE_CHUNK = 320
SEG_ALIGN = 16
PERM_ROWS = TMOE + LANE
SORT_ROWS = PERM_ROWS + MOE_CHUNK
EXPERT_SPLIT = 2
GROUP_LANE = N_EXP

KEY_MASK = 2.0 ** 100
TABLE_MASK = 2 * NEG
MASKED_BUCKET = N_BUCKETS
SEL_LANE0 = HEAD_DIM
PAD_LANE = HEAD_DIM + 32


def _cparams(n_axes):
    return pltpu.CompilerParams(dimension_semantics=("arbitrary",) * n_axes, vmem_limit_bytes=VMEM_LIMIT)


def _const_spec(shape):
    nd = len(shape)
    return pl.BlockSpec(shape, lambda *_, nd=nd: (0,) * nd)


def _qk(a, b):
    return lax.dot_general(a, b, (((1,), (1,)), ((), ())), preferred_element_type=F32)


def _mm(a, b):
    return jnp.dot(a, b, preferred_element_type=F32)


def _bucket_np(dist):
    n = np.maximum(dist, 0)
    max_exact = N_BUCKETS // 2
    nf = np.maximum(n, 1).astype(np.float32)
    large = max_exact + (np.log(nf / np.float32(max_exact)) / np.float32(math.log(MAX_DISTANCE / max_exact))
                         * np.float32(N_BUCKETS - max_exact)).astype(np.int32)
    large = np.minimum(large, N_BUCKETS - 1)
    return np.where(n < max_exact, n, large).astype(np.int32)


def _overlap_np(c, s):
    c0 = c * CMP_STRIDE
    s0 = s * SLC_LEN
    return np.maximum(np.minimum(c0 + CMP_LEN, s0 + SLC_LEN) - np.maximum(c0, s0), 0)


_N_NEAR_TILES = 3


def _static_tables(seq, past, win_buf):
    nq = seq // TQ
    i = np.arange(LANE)[:, None]
    j = np.arange(LANE)[None, :]
    far_bucket = N_BUCKETS - 1
    tiles = [_bucket_np(LANE + j - i),
             np.where(j >= i, _bucket_np(j - i), MASKED_BUCKET),
             np.where(i > j, far_bucket, MASKED_BUCKET) + 0 * i]
    tiles += [_bucket_np(TQ * q + j - (CMP_STRIDE * i + CMP_STRIDE - 1)) for q in range(nq)]
    smp = np.full((LANE, LANE), 10 * MAX_DISTANCE, np.int64)
    k = np.arange(5 * LANE)
    smp[0:5] = np.where(k < win_buf, win_buf - k, 0).reshape(5, LANE)
    ci = np.arange(8 * LANE)
    smp[5:13] = np.maximum(past - (CMP_STRIDE * ci + CMP_STRIDE - 1), 0).reshape(8, LANE)
    nb = past // SLC_LEN
    smp[13] = past - (SLC_LEN * (nb - 2) + np.arange(LANE))
    smp[14] = 0
    tiles.append(_bucket_np(smp))
    idx = np.stack(tiles).astype(np.int32)

    def sp(nrows, ncols, n_slc):
        r = np.arange(nrows)[:, None]
        s = np.arange(ncols)[None, :]
        ov = _overlap_np(r - 1, s)
        return np.where((r >= 1) & (s < n_slc), ov, 0).astype(np.float32)

    sp_p = sp(seq // CMP_STRIDE, LANE, -(-seq // SLC_LEN)).T
    n_slc_s = -(-(past + 1) // SLC_LEN)
    sp_s = sp(past // CMP_STRIDE, 3 * LANE, n_slc_s)
    return idx, sp_p, sp_s


def _bias_kernel(rb_ref, idx_ref, out_ref):
    idx = idx_ref[0]
    accs = [jnp.full((LANE, LANE), TABLE_MASK, F32) for _ in range(N_HEADS)]
    for b in range(N_BUCKETS):
        hit = idx == b
        for h in range(N_HEADS):
            accs[h] = jnp.where(hit, rb_ref[b, h] - rb_ref[N_BUCKETS - 1, h], accs[h])
    for h in range(N_HEADS):
        out_ref[0, h] = accs[h]


def _bias_tables(rel_bias, idx):
    nt = idx.shape[0]
    return pl.pallas_call(
        _bias_kernel,
        grid=(nt,),
        in_specs=[pl.BlockSpec(memory_space=pltpu.SMEM),
                  pl.BlockSpec((1, LANE, LANE), lambda t: (t, 0, 0))],
        out_specs=pl.BlockSpec((1, N_HEADS, LANE, LANE), lambda t: (t, 0, 0, 0)),
        out_shape=jax.ShapeDtypeStruct((nt, N_HEADS, LANE, LANE), F32),
        compiler_params=_cparams(1),
        name="bias_tables",
    )(rel_bias, idx)


_Q0, _Q1 = 0, N_HEADS * HEAD_DIM
_KV0, _KV1 = _Q1, _Q1 + 512
_WN0, _WN1 = _KV1, _KV1 + 256
_GT0, _GT1 = _WN1, _WN1 + 2 * LANE
_CV0, _CV1 = _GT1, _GT1 + 3 * 512


def _pack_in_weights(w):
    d = w.shape[0]
    d_nsa = N_HEADS * HEAD_DIM
    wq_pad = w[:, :d_nsa]
    o = d_nsa
    w_kv = w[:, o:o + 512]
    w_win = w[:, o + 512:o + 768]
    wg = w[:, o + 768:o + 768 + 3 * N_HEADS].reshape(d, N_KV, 3 * GQ)
    wg_pad = jnp.pad(wg, ((0, 0), (0, 0), (0, LANE - 3 * GQ))).reshape(d, N_KV * LANE)
    w_conv = w[:, o + 768 + 3 * N_HEADS:]
    return jnp.concatenate([wq_pad, w_kv, w_win, wg_pad, w_conv], axis=1).astype(BF16)


def _rms(x, gain, n, eps=1e-6):
    ms = jnp.sum(x * x, axis=-1, keepdims=True) / n
    return x * lax.rsqrt(ms + eps) * gain


def _layernorm(y, gain, bias, eps=1e-5):
    mu = jnp.mean(y, axis=-1, keepdims=True)
    d = y - mu
    var = jnp.mean(d * d, axis=-1, keepdims=True)
    return d * lax.rsqrt(var + eps) * gain + bias


def _inproj_prompt_kernel(x_ref, w_ref, cw_ref, gc_ref, ks_fill, kw_fill, sv_fill, wv_fill,
                          q_ref, kv_ref, ks_ref, sv_ref, kw_ref, wv_ref, wtail_ref, gate_ref, zn_ref, ctail_ref,
                          uext_ref, *, tiles_per_batch):
    del ks_fill, kw_fill, sv_fill, wv_fill
    i = pl.program_id(0)
    tm = x_ref.shape[0]
    xb = x_ref[...].astype(BF16)

    def seg(a, b):
        return _mm(xb, w_ref[:, a:b])

    qp_t = (seg(_Q0, _Q1) * (HEAD_DIM ** -0.5)).T
    q_fill = jnp.zeros((LANE - HEAD_DIM, tm), F32)
    for h in range(N_HEADS):
        q_ref[h] = jnp.concatenate([qp_t[h * HEAD_DIM:(h + 1) * HEAD_DIM, :], q_fill], axis=0).astype(BF16)
    kv = seg(_KV0, _KV1)
    kv_t = kv.T
    kv_ref[0] = kv_t
    win = seg(_WN0, _WN1)
    win_t = win.T
    wtail_ref[0] = win_t
    pos = (i % tiles_per_batch) * tm + lax.broadcasted_iota(jnp.int32, (tm, LANE - HEAD_DIM), 0)
    lane = lax.broadcasted_iota(jnp.int32, (tm, LANE - HEAD_DIM), 1)
    blk_flag = jnp.where(lane == lax.shift_right_logical(pos, 6), KEY_MASK, 0.0).astype(BF16)
    no_flag = jnp.zeros((tm, LANE - HEAD_DIM), BF16)
    for g in range(N_KV):
        ks_ref[g, 0] = jnp.concatenate([kv[:, 256 + g * HEAD_DIM:256 + (g + 1) * HEAD_DIM].astype(BF16), blk_flag], axis=1)
        kw_ref[g, 0] = jnp.concatenate([win[:, g * HEAD_DIM:(g + 1) * HEAD_DIM].astype(BF16), no_flag], axis=1)
    sv_ref[0] = kv_t[384:512, :].astype(BF16)
    wv_ref[0] = win_t[LANE:2 * LANE, :].astype(BF16)
    gt_t = jax.nn.sigmoid(seg(_GT0, _GT1)).T
    gate_ref[0] = gt_t[0:LANE, :]
    gate_ref[1] = gt_t[LANE:2 * LANE, :]

    conv = seg(_CV0, _CV1)
    cb = conv[:, 0:512]
    u = conv[:, 512:1024] * conv[:, 1024:1536]
    first = (i % tiles_per_batch) == 0

    @pl.when(first)
    def _():
        uext_ref[0:SUBLANE, :] = jnp.zeros((SUBLANE, 512), F32)

    @pl.when(jnp.logical_not(first))
    def _():
        uext_ref[0:SUBLANE, :] = uext_ref[tm:tm + SUBLANE, :]

    uext_ref[SUBLANE:tm + SUBLANE, :] = u
    um1 = uext_ref[SUBLANE - 1:tm + SUBLANE - 1, :]
    um2 = uext_ref[SUBLANE - 2:tm + SUBLANE - 2, :]
    z = cb * (um2 * cw_ref[0:1, :] + um1 * cw_ref[1:2, :] + u * cw_ref[2:3, :])
    zn_ref[...] = _rms(z, gc_ref[...], 512).astype(BF16)
    ctail_ref[0] = uext_ref[tm + SUBLANE - 2:tm + SUBLANE, :]


def _inproj_prompt(x, w, conv_w, g_conv, batch, seq):
    t, d = x.shape
    n_tiles = t // TM
    tpb = seq // TM
    assert WINDOW == TM
    rows = WINDOW + seq
    key_pad = jnp.zeros((LANE,), BF16).at[PAD_LANE].set(KEY_MASK)
    k_fill = jnp.broadcast_to(key_pad, (N_KV, batch, rows, LANE))
    v_fill = jnp.zeros((batch, LANE, rows), BF16)
    outs = (
        jax.ShapeDtypeStruct((N_HEADS, LANE, t), BF16),
        jax.ShapeDtypeStruct((batch, 512, seq), F32),
        jax.ShapeDtypeStruct(k_fill.shape, BF16),
        jax.ShapeDtypeStruct(v_fill.shape, BF16),
        jax.ShapeDtypeStruct(k_fill.shape, BF16),
        jax.ShapeDtypeStruct(v_fill.shape, BF16),
        jax.ShapeDtypeStruct((batch, 256, TM), F32),
        jax.ShapeDtypeStruct((N_KV, LANE, t), F32),
        jax.ShapeDtypeStruct((t, 512), BF16),
        jax.ShapeDtypeStruct((batch, CONV_W - 1, 512), F32),
    )
    row = lambda n: pl.BlockSpec((TM, n), lambda i: (i, 0))
    planes_t = lambda n: pl.BlockSpec((n, LANE, TM), lambda i: (0, 0, i))
    keys_out = pl.BlockSpec((N_KV, 1, TM, LANE), lambda i: (0, i // tpb, i % tpb + 1, 0))
    vals_out = pl.BlockSpec((1, LANE, TM), lambda i: (i // tpb, 0, i % tpb + 1))
    in_place = pl.BlockSpec(memory_space=pl.ANY)
    return pl.pallas_call(
        functools.partial(_inproj_prompt_kernel, tiles_per_batch=tpb),
        grid=(n_tiles,),
        in_specs=[row(d), _const_spec(w.shape), _const_spec(conv_w.shape), _const_spec(g_conv.shape),
                  in_place, in_place, in_place, in_place],
        input_output_aliases={4: 2, 5: 4, 6: 3, 7: 5},
        out_specs=(
            planes_t(N_HEADS),
            pl.BlockSpec((1, 512, TM), lambda i: (i // tpb, 0, i % tpb)),
            keys_out, vals_out, keys_out, vals_out,
            pl.BlockSpec((1, 256, TM), lambda i: (i // tpb, 0, 0)),
            planes_t(N_KV),
            row(512),
            pl.BlockSpec((1, CONV_W - 1, 512), lambda i: (i // tpb, 0, 0)),
        ),
        out_shape=outs,
        scratch_shapes=[pltpu.VMEM((TM + SUBLANE, 512), F32)],
        compiler_params=_cparams(1),
        name="inproj_prompt",
    )(x, w, conv_w, g_conv, k_fill, k_fill, v_fill, v_fill)


def _inproj_sample_kernel(x_ref, w_ref, cw_ref, gc_ref, s0_ref, s1_ref,
                          qlo_ref, qgl_ref, kv_ref, win_ref, gate_ref, zn_ref, u_ref):
    xb = x_ref[...].astype(BF16)

    def seg(a, b):
        return _mm(xb, w_ref[:, a:b])

    qp = seg(_Q0, _Q1) * (HEAD_DIM ** -0.5)
    fill = jnp.zeros((qp.shape[0], LANE - HEAD_DIM), F32)
    heads = [qp[:, h * HEAD_DIM:(h + 1) * HEAD_DIM] for h in range(N_HEADS)]
    qlo_ref[...] = jnp.concatenate([x for qh in heads for x in (qh, fill)], axis=1).astype(BF16)
    qgl_ref[...] = jnp.concatenate([x for h, qh in enumerate(heads) for x in ((qh, fill) if h < GQ else (fill, qh))],
                                   axis=1).astype(BF16)
    kv_ref[...] = seg(_KV0, _KV1)
    win_ref[...] = seg(_WN0, _WN1)
    gate_ref[...] = jax.nn.sigmoid(seg(_GT0, _GT1))
    conv = seg(_CV0, _CV1)
    cb = conv[:, 0:512]
    u = conv[:, 512:1024] * conv[:, 1024:1536]
    z = cb * (s0_ref[...] * cw_ref[0:1, :] + s1_ref[...] * cw_ref[1:2, :] + u * cw_ref[2:3, :])
    zn_ref[...] = _rms(z, gc_ref[...], 512).astype(BF16)
    u_ref[...] = u


def _inproj_sample(x, w, conv_w, g_conv, s0, s1):
    n = x.shape[0]
    outs = (
        jax.ShapeDtypeStruct((n, N_HEADS * LANE), BF16),
        jax.ShapeDtypeStruct((n, N_HEADS * LANE), BF16),
        jax.ShapeDtypeStruct((n, 512), F32),
        jax.ShapeDtypeStruct((n, 256), F32),
        jax.ShapeDtypeStruct((n, N_KV * LANE), F32),
        jax.ShapeDtypeStruct((n, 512), BF16),
        jax.ShapeDtypeStruct((n, 512), F32),
    )
    args = (x, w, conv_w, g_conv, s0, s1)
    return pl.pallas_call(
        _inproj_sample_kernel,
        grid=(1,),
        in_specs=[_const_spec(a.shape) for a in args],
        out_specs=tuple(_const_spec(o.shape) for o in outs),
        out_shape=outs,
        compiler_params=_cparams(1),
        name="inproj_sample",
    )(*args)


def _pack_compress_weights(pe, w1, w2, per_group_out):
    w = w1.reshape(2, CMP_STRIDE, HEAD_DIM, CMP_HIDDEN).transpose(1, 2, 0, 3).reshape(CMP_STRIDE, HEAD_DIM, 2 * CMP_HIDDEN)
    z = jnp.zeros_like(w)
    top = jnp.concatenate([w, z], axis=-1)
    bot = jnp.concatenate([z, w], axis=-1)
    w1bd = jnp.stack([top, bot], axis=1).reshape(CMP_STRIDE * 2 * HEAD_DIM, 4 * CMP_HIDDEN).astype(BF16)
    z2 = jnp.zeros_like(w2)
    if per_group_out:
        w2bd = jnp.concatenate([jnp.concatenate([w2, z2, z2, z2], axis=1),
                                jnp.concatenate([z2, z2, w2, z2], axis=1)], axis=0).astype(BF16)
    else:
        w2bd = jnp.concatenate([jnp.concatenate([w2, z2], axis=1), jnp.concatenate([z2, w2], axis=1)], axis=0).astype(BF16)
    pe_rows = jnp.broadcast_to(pe.reshape(2, CMP_STRIDE, 1, HEAD_DIM), (2, CMP_STRIDE, N_KV, HEAD_DIM))
    pe_rows = pe_rows.reshape(2, CMP_STRIDE * N_KV * HEAD_DIM)
    pe_rows = jnp.pad(pe_rows, ((0, SUBLANE - 2), (0, 0)))
    return w1bd, w2bd, pe_rows


def _compress_kernel(*refs, n_pages, n_prefetch):
    refs = refs[n_prefetch:]
    page_refs = refs[:n_pages]
    w1_ref, w2k_ref, w2v_ref, pe_ref, ck_ref, cvt_ref, carry_ref, tok_ref = refs[n_pages:]
    step = pl.program_id(1)
    m = n_pages * (PAGE_SIZE // CMP_STRIDE)

    @pl.when(step == 0)
    def _():
        carry_ref[...] = jnp.zeros(carry_ref.shape, F32)

    row0 = lax.broadcasted_iota(jnp.int32, (m, CMP_HIDDEN), 0) == 0
    for t, w2_ref in enumerate((w2k_ref, w2v_ref)):
        for p, pr in enumerate(page_refs):
            slab = pr[0, t].reshape(N_KV * HEAD_DIM, PAGE_SIZE)
            tok_ref[t, p * PAGE_SIZE:(p + 1) * PAGE_SIZE, :] = slab.T
        pieces = [tok_ref[t, pl.ds(j, m, stride=CMP_STRIDE), :] for j in range(CMP_STRIDE)]
        lhs = jnp.concatenate([jnp.concatenate(pieces, axis=1), pe_ref[t]], axis=0).astype(BF16)
        a = _mm(lhs, w1_ref[t])
        hs = []
        for g in range(N_KV):
            c0 = g * 2 * CMP_HIDDEN
            a0 = a[0:m, c0:c0 + CMP_HIDDEN]
            a1 = a[0:m, c0 + CMP_HIDDEN:c0 + 2 * CMP_HIDDEN]
            pe_term = a[m:m + 1, c0:c0 + CMP_HIDDEN] + a[m + 1:m + 2, c0 + CMP_HIDDEN:c0 + 2 * CMP_HIDDEN]
            prev = carry_ref[t, 0:1, g * CMP_HIDDEN:(g + 1) * CMP_HIDDEN]
            shifted = jnp.where(row0, prev, pltpu.roll(a0, 1, axis=0))
            carry_ref[t, 0:1, g * CMP_HIDDEN:(g + 1) * CMP_HIDDEN] = a0[m - 1:m, :]
            hs.append(jax.nn.gelu(shifted + a1 + pe_term))
        out = _mm(jnp.concatenate(hs, axis=1).astype(BF16), w2_ref[...])
        if t == 0:
            ck_ref[0, 0] = out[:, 0:LANE].astype(BF16)
            ck_ref[0, 1] = out[:, LANE:2 * LANE].astype(BF16)
        else:
            cvt_ref[0] = out.T.astype(BF16)


def _compress_call(page_specs, page_args, prefetch, grid, batch, n_chunks, m, w1, w2k, w2v, pe, name):
    n_pages = len(page_specs)
    n_pf = len(prefetch)
    cspec = lambda shape: pl.BlockSpec(shape, lambda *_: (0,) * len(shape))
    grid_spec = pltpu.PrefetchScalarGridSpec(
        num_scalar_prefetch=n_pf,
        grid=grid,
        in_specs=list(page_specs) + [cspec(w1.shape), cspec(w2k.shape), cspec(w2v.shape), cspec(pe.shape)],
        out_specs=(pl.BlockSpec((1, N_KV, m, LANE), lambda b, s, *_: (b, 0, s, 0)),
                   pl.BlockSpec((1, LANE, m), lambda b, s, *_: (b, 0, s))),
        scratch_shapes=[pltpu.VMEM((2, SUBLANE, 2 * CMP_HIDDEN), F32),
                        pltpu.VMEM((2, n_pages * PAGE_SIZE, N_KV * HEAD_DIM), F32)],
    )
    return pl.pallas_call(
        functools.partial(_compress_kernel, n_pages=n_pages, n_prefetch=n_pf),
        grid_spec=grid_spec,
        out_shape=(jax.ShapeDtypeStruct((batch, N_KV, n_chunks, LANE), BF16),
                   jax.ShapeDtypeStruct((batch, LANE, n_chunks), BF16)),
        compiler_params=_cparams(2),
        name=name,
    )(*prefetch, *page_args, w1, w2k, w2v, pe)


_PAGE_BLOCK = (1, 2, N_KV, HEAD_DIM, PAGE_SIZE)


def _compress_prompt(kv_t, batch, seq, w1, w2k, w2v, pe):
    n_pages = seq // PAGE_SIZE
    rows = PAGE_SIZE // CMP_STRIDE
    specs = [pl.BlockSpec(_PAGE_BLOCK, lambda b, s, p=p: (b, 0, 0, 0, p)) for p in range(n_pages)]
    return _compress_call(specs, [kv_t] * n_pages, (), (batch, 1), batch, n_pages * rows, n_pages * rows,
                          w1, w2k, w2v, pe, "compress_prompt")


def _compress_sample(cache_t, page_table, w1, w2k, w2v, pe):
    batch, n_pages = page_table.shape
    rows = PAGE_SIZE // CMP_STRIDE
    specs = [pl.BlockSpec(_PAGE_BLOCK, lambda b, s, pt, k=k: (pt[b, s * SAMPLE_PAGES + k], 0, 0, 0, 0))
             for k in range(SAMPLE_PAGES)]
    return _compress_call(specs, [cache_t] * SAMPLE_PAGES, (page_table,), (batch, n_pages // SAMPLE_PAGES), batch,
                          n_pages * rows, SAMPLE_PAGES * rows, w1, w2k, w2v, pe, "compress_sample")


def _tile4(x):
    return jnp.concatenate([x] * GQ, axis=1)


def _softmax_fold(state, pieces, vt):
    m, l, acc = state
    m_new = m
    for pc in pieces:
        m_new = jnp.maximum(m_new, jnp.max(pc, axis=0, keepdims=True))
    alpha = jnp.exp(m - m_new)
    es = [jnp.exp(pc - m_new) for pc in pieces]
    l = alpha * l
    for e in es:
        l = l + jnp.sum(e, axis=0, keepdims=True)
    e_all = es[0] if len(es) == 1 else jnp.concatenate(es, axis=0)
    acc = alpha * acc + _mm(vt, e_all.astype(BF16))
    return m_new, l, acc


def _attn_prompt_kernel(q_ref, gate_ref, ck_ref, cvt_ref, ks_ref, sv_ref, kw_ref, wv_ref,
                        near_ref, up_ref, tabc_ref, sp_ref, gn_ref, o_ref, *, n_slc):
    qi = pl.program_id(1)
    q0 = pl.multiple_of(qi * TQ, TQ)
    krow = lax.broadcasted_iota(jnp.int32, (LANE, TQ), 0)
    qcol = lax.broadcasted_iota(jnp.int32, (LANE, TQ), 1)
    cmp_ok4 = _tile4((krow >= 1) & (CMP_STRIDE * krow + (CMP_STRIDE - 1) <= q0 + qcol))
    sidx = lax.broadcasted_iota(jnp.int32, (n_slc, TQ), 0)
    qblk = lax.shift_right_logical(q0 + lax.broadcasted_iota(jnp.int32, (n_slc, TQ), 1), 6)
    sel_valid = sidx <= qblk
    sel_forced = (sidx == 0) | (sidx == qblk) | (sidx == qblk - 1)
    n_far = lax.shift_right_logical(jnp.maximum(qi - 1, 0), 2)
    up4 = _tile4(up_ref[...])
    cols = GQ * TQ
    fresh = (jnp.full((1, cols), NEG, F32), jnp.zeros((1, cols), F32), jnp.zeros((LANE, cols), F32))

    groups = range(N_KV)
    qa, near, o_cmp = [], [], []
    for g in groups:
        heads = [GQ * g + r for r in range(GQ)]
        q_t = [q_ref[h] for h in heads]
        near.append(jnp.concatenate([near_ref[h] for h in heads], axis=1))

        bias_c = jnp.concatenate([tabc_ref[0, h] for h in heads], axis=1)
        s = jnp.where(cmp_ok4, _mm(ck_ref[0, g], jnp.concatenate(q_t, axis=1)) + bias_c, NEG)
        e = jnp.exp(s - jnp.max(s, axis=0, keepdims=True))
        p = (e / jnp.sum(e, axis=0, keepdims=True) * cmp_ok4.astype(F32)).astype(BF16)
        o_cmp.append(_mm(cvt_ref[0], p))
        imp4 = _mm(sp_ref[...], p)
        imp = imp4[:, 0:TQ] + imp4[:, TQ:2 * TQ] + imp4[:, 2 * TQ:3 * TQ] + imp4[:, 3 * TQ:4 * TQ]

        score = jnp.where(sel_valid, imp[0:n_slc, :] + jnp.where(sel_forced, FORCE, 0.0), -FORCE)
        rank = jnp.zeros((n_slc, TQ), F32)
        for k in range(1, n_slc):
            other = pltpu.roll(score, k, axis=0)
            rank = rank + jnp.where(sidx >= k, (other >= score).astype(F32), (other > score).astype(F32))
        sel_t = (rank < float(N_SELECT)).astype(F32)
        aug = jnp.concatenate([jnp.zeros((SEL_LANE0, TQ), F32), sel_t - 1.0,
                               jnp.zeros((LANE - SEL_LANE0 - n_slc, TQ), F32)], axis=0)
        aug = jnp.where(krow == PAD_LANE, -1.0, aug).astype(BF16)
        qa.append(jnp.concatenate([qh + aug for qh in q_t], axis=1))

    def far_tile(it, states):
        start = pl.multiple_of(q0 - FAR * (n_far - it), LANE)
        vt = sv_ref[0, :, pl.ds(start, FAR)]
        return tuple(_softmax_fold(states[g], [_mm(ks_ref[g, 0, pl.ds(start, FAR), :], qa[g])], vt) for g in groups)

    states = lax.fori_loop(0, n_far, far_tile, (fresh,) * N_KV)
    v_tail = sv_ref[0, :, pl.ds(q0, TAIL)]
    w_tail = wv_ref[0, :, pl.ds(q0, TAIL)]
    o_slc, o_win = [], []
    for g in groups:
        sc = _mm(ks_ref[g, 0, pl.ds(q0, TAIL), :], qa[g])
        pieces = [sc[0:TAIL - 2 * LANE], sc[TAIL - 2 * LANE:TAIL] + near[g]]
        _, l, acc = _softmax_fold(states[g], pieces, v_tail)
        o_slc.append(acc / l)
        sc = _mm(kw_ref[g, 0, pl.ds(q0, TAIL), :], qa[g])
        pieces = [sc[0:LANE] + up4, sc[LANE:TAIL - 2 * LANE], sc[TAIL - 2 * LANE:TAIL] + near[g]]
        _, l, acc = _softmax_fold(fresh, pieces, w_tail)
        o_win.append(acc / l)

    outs = []
    for g in groups:
        gt = gate_ref[g]
        own = lax.shift_right_logical(krow, 6) == g
        for r in range(GQ):
            cs = slice(r * TQ, (r + 1) * TQ)
            comb = gt[3 * r:3 * r + 1, :] * o_cmp[g][:, cs] + gt[3 * r + 1:3 * r + 2, :] * o_slc[g][:, cs] \
                + gt[3 * r + 2:3 * r + 3, :] * o_win[g][:, cs]
            outs.append(jnp.where(own, comb, 0.0))
    o = jnp.concatenate(outs, axis=0)
    ms = jnp.sum(o * o, axis=0, keepdims=True) / (N_HEADS * HEAD_DIM)
    o = o * lax.rsqrt(ms + 1e-6) * gn_ref[...]
    o_ref[...] = jnp.concatenate([o[h * LANE:(h + 1) * LANE, :].T for h in range(N_HEADS)], axis=1).astype(BF16)


def _attn_prompt(q, gates, cmpk, cmpv_t, ks, sv_t, kw, wv_t, near_t, up_t, tab_c, sp_t, gn_rep, batch, seq):
    nq = seq // TQ
    t = batch * seq
    rows = ks.shape[2]
    n_slc = seq // SLC_LEN
    n_cmp = seq // CMP_STRIDE
    assert SEL_LANE0 + n_slc <= PAD_LANE < LANE
    return pl.pallas_call(
        functools.partial(_attn_prompt_kernel, n_slc=n_slc),
        grid=(batch, nq),
        in_specs=[
            pl.BlockSpec((N_HEADS, LANE, TQ), lambda b, i: (0, 0, b * nq + i)),
            pl.BlockSpec((N_KV, LANE, TQ), lambda b, i: (0, 0, b * nq + i)),
            pl.BlockSpec((1, N_KV, n_cmp, LANE), lambda b, i: (b, 0, 0, 0)),
            pl.BlockSpec((1, LANE, n_cmp), lambda b, i: (b, 0, 0)),
            pl.BlockSpec((N_KV, 1, rows, LANE), lambda b, i: (0, b, 0, 0)),
            pl.BlockSpec((1, LANE, rows), lambda b, i: (b, 0, 0)),
            pl.BlockSpec((N_KV, 1, rows, LANE), lambda b, i: (0, b, 0, 0)),
            pl.BlockSpec((1, LANE, rows), lambda b, i: (b, 0, 0)),
            _const_spec(near_t.shape),
            _const_spec(up_t.shape),
            pl.BlockSpec((1, N_HEADS, LANE, LANE), lambda b, i: (i, 0, 0, 0)),
            _const_spec(sp_t.shape),
            _const_spec(gn_rep.shape),
        ],
        out_specs=pl.BlockSpec((TQ, N_HEADS * LANE), lambda b, i: (b * nq + i, 0)),
        out_shape=jax.ShapeDtypeStruct((t, N_HEADS * LANE), BF16),
        compiler_params=_cparams(2),
        name="attn_prompt",
    )(q, gates, cmpk, cmpv_t, ks, sv_t, kw, wv_t, near_t, up_t, tab_c, sp_t, gn_rep)


def _sample_cw_kernel(qlo_ref, qgl_ref, ck_ref, cvt_ref, cwin_ref, wnew_ref, tabc_ref, tabw_ref, rb0_ref, sp_ref,
                      ocmp_ref, owin_ref, idx_ref, **static):
    for bb in range(qlo_ref.shape[0]):
        one = lambda r, bb=bb: r.at[pl.ds(bb, 1)]
        _sample_cw_one(one(qlo_ref), one(qgl_ref), one(ck_ref), one(cvt_ref), one(cwin_ref), one(wnew_ref),
                       tabc_ref, tabw_ref, rb0_ref, sp_ref, one(ocmp_ref), one(owin_ref), one(idx_ref), **static)


def _sample_cw_one(qlo_ref, qgl_ref, ck_ref, cvt_ref, cwin_ref, wnew_ref, tabc_ref, tabw_ref, rb0_ref, sp_ref,
                   ocmp_ref, owin_ref, idx_ref, *, n_slc, qblk, win_buf):
    qlo = qlo_ref[0]
    qgl = qgl_ref[0]
    n_cmp_rows = ck_ref.shape[2]
    col = lax.broadcasted_iota(jnp.int32, (N_HEADS, n_cmp_rows), 1)
    grp_of_head = lax.shift_right_logical(lax.broadcasted_iota(jnp.int32, (N_HEADS, n_cmp_rows), 0), 2)
    ok = col >= 1
    raw = jnp.where(grp_of_head == 0, _qk(qlo, ck_ref[0, 0]), _qk(qlo, ck_ref[0, 1]))
    s = jnp.where(ok, raw + tabc_ref[...], NEG)
    e = jnp.exp(s - jnp.max(s, axis=1, keepdims=True))
    p = (e / jnp.sum(e, axis=1, keepdims=True) * ok.astype(F32)).astype(BF16)
    ocmp_ref[0] = _qk(p, cvt_ref[0])
    imp8 = _mm(p, sp_ref[...])

    width = imp8.shape[1]
    head_grp = lax.shift_right_logical(lax.broadcasted_iota(jnp.int32, (N_HEADS, width), 0), 2)
    sidx = lax.broadcasted_iota(jnp.int32, (N_HEADS, width), 1)
    sidx_f = sidx.astype(F32)
    forced = (sidx == 0) | (sidx == qblk) | (sidx == qblk - 1)
    lane = lax.broadcasted_iota(jnp.int32, (N_HEADS, LANE), 1)
    imp_g = [jnp.sum(jnp.where(head_grp == g, imp8, 0.0), axis=0, keepdims=True) for g in range(N_KV)]
    imp = jnp.where(head_grp == 0, imp_g[0], imp_g[1])
    score = jnp.where(sidx <= qblk, imp + jnp.where(forced, FORCE, 0.0), -FORCE)
    score = jnp.where(sidx < n_slc, score, -jnp.inf)
    picked = jnp.zeros((N_HEADS, LANE), F32)
    for k in range(N_SELECT):
        best = jnp.max(score, axis=1, keepdims=True)
        ik = jnp.min(jnp.where(score == best, sidx_f, float(width)), axis=1, keepdims=True)
        picked = jnp.where(lane == k, ik, picked)
        score = jnp.where(sidx_f == ik, -jnp.inf, score)
    for g in range(N_KV):
        idx_ref[0, g:g + 1, :] = picked[GQ * g:GQ * g + 1, :].astype(jnp.int32)

    kt = cwin_ref[0, 0].reshape(N_KV * HEAD_DIM, win_buf).astype(BF16)
    vt = cwin_ref[0, 1].reshape(N_KV * HEAD_DIM, win_buf).astype(BF16)
    knew = wnew_ref[0, :, 0:LANE].astype(BF16).astype(F32)
    vnew = wnew_ref[0, :, LANE:2 * LANE].astype(BF16).astype(F32)
    colw = lax.broadcasted_iota(jnp.int32, (N_HEADS, win_buf), 1)
    okw = colw > win_buf - WINDOW
    sw = jnp.where(okw, _mm(qgl, kt) + tabw_ref[...], NEG)
    s_new = jnp.sum(qgl.astype(F32) * knew, axis=1, keepdims=True) + rb0_ref[:, 0:1]
    mw = jnp.maximum(jnp.max(sw, axis=1, keepdims=True), s_new)
    ew = jnp.where(okw, jnp.exp(sw - mw), 0.0)
    e_new = jnp.exp(s_new - mw)
    lw = jnp.sum(ew, axis=1, keepdims=True) + e_new
    owin_ref[0] = (_qk(ew.astype(BF16), vt) + e_new.astype(BF16).astype(F32) * vnew) / lw


def _sample_cw(qlo, qgl, cmpk, cmpv_t, cwin, wnew, tab_c, tab_w, rb0, sp, n_slc, qblk):
    n = qlo.shape[0]
    win_buf = cwin.shape[-1]
    blk = lambda a: pl.BlockSpec((SAMPLE_BB,) + a.shape[1:], lambda b: (b,) + (0,) * (a.ndim - 1))
    outs = (jax.ShapeDtypeStruct((n, N_HEADS, LANE), F32), jax.ShapeDtypeStruct((n, N_HEADS, LANE), F32),
            jax.ShapeDtypeStruct((n, N_KV, LANE), jnp.int32))
    return pl.pallas_call(
        functools.partial(_sample_cw_kernel, n_slc=n_slc, qblk=qblk, win_buf=win_buf),
        grid=(n // SAMPLE_BB,),
        in_specs=[blk(qlo), blk(qgl), blk(cmpk), blk(cmpv_t), blk(cwin), blk(wnew), _const_spec(tab_c.shape),
                  _const_spec(tab_w.shape), _const_spec(rb0.shape), _const_spec(sp.shape)],
        out_specs=tuple(pl.BlockSpec((SAMPLE_BB,) + o.shape[1:], lambda b: (b, 0, 0)) for o in outs),
        out_shape=outs,
        compiler_params=_cparams(1),
        name="sample_cmp_win",
    )(qlo, qgl, cmpk, cmpv_t, cwin, wnew, tab_c, tab_w, rb0, sp)


def _sample_slc_kernel(pg_ref, hf_ref, bid_ref, *refs, new_block, near_block):
    n_blk = N_KV * N_SELECT
    blocks = refs[:n_blk]
    (q_ref, kvnew_ref, ocmp_ref, owin_ref, gate_ref, tnear_ref, tlast_ref, rb0_ref, gn_ref, o_ref) = refs[n_blk:]
    b = pl.program_id(0)
    n_keys = N_SELECT * PAGE_SIZE
    lane = lax.broadcasted_iota(jnp.int32, (1, n_keys), 1)
    slot = lax.shift_right_logical(lane, 7)
    half = lax.shift_right_logical(lane, 6) & 1
    head_grp = lax.shift_right_logical(lax.broadcasted_iota(jnp.int32, (N_HEADS, LANE), 0), 2)
    lane_grp = lax.shift_right_logical(lax.broadcasted_iota(jnp.int32, (N_HEADS, LANE), 1), 6)
    knew = kvnew_ref[0, :, 256:384].astype(BF16).astype(F32)
    vnew = kvnew_ref[0, :, 384:512].astype(BF16).astype(F32)
    qs = q_ref[0]
    o_slc = jnp.zeros((N_HEADS, LANE), F32)
    for g in range(N_KV):
        mine = blocks[g * N_SELECT:(g + 1) * N_SELECT]
        kt = jnp.concatenate([blk[0, 0].reshape(N_KV * HEAD_DIM, PAGE_SIZE) for blk in mine], axis=1).astype(BF16)
        vt = jnp.concatenate([blk[0, 1].reshape(N_KV * HEAD_DIM, PAGE_SIZE) for blk in mine], axis=1).astype(BF16)
        bid = jnp.zeros((1, n_keys), jnp.int32)
        hsel = jnp.zeros((1, n_keys), jnp.int32)
        has_new = bid_ref[b, g * N_SELECT] == new_block
        for kk in range(N_SELECT):
            bid = jnp.where(slot == kk, bid_ref[b, g * N_SELECT + kk], bid)
            hsel = jnp.where(slot == kk, hf_ref[b, g * N_SELECT + kk], hsel)
            if kk:
                has_new = has_new | (bid_ref[b, g * N_SELECT + kk] == new_block)
        bias = jnp.where(bid == near_block + 1, tlast_ref[...], jnp.where(bid == near_block, tnear_ref[...], 0.0))
        ok = (bid != new_block) & (half == hsel)
        s = jnp.where(ok, _mm(qs, kt) + bias, NEG)
        s_new = jnp.sum(qs.astype(F32) * knew, axis=1, keepdims=True) + rb0_ref[:, 0:1]
        s_new = jnp.where(has_new, s_new, NEG)
        m = jnp.maximum(jnp.max(s, axis=1, keepdims=True), s_new)
        e = jnp.where(ok, jnp.exp(s - m), 0.0)
        e_new = jnp.where(has_new, jnp.exp(s_new - m), 0.0)
        l = jnp.sum(e, axis=1, keepdims=True) + e_new
        og = (_qk(e.astype(BF16), vt) + e_new.astype(BF16).astype(F32) * vnew) / l
        o_slc = jnp.where(head_grp == g, og, o_slc)
    gt = gate_ref[0]
    comb = gt[:, 0:1] * ocmp_ref[0] + gt[:, 1:2] * o_slc + gt[:, 2:3] * owin_ref[0]
    comb = jnp.where(lane_grp == head_grp, comb, 0.0)
    o = jnp.concatenate([comb[h:h + 1, :] for h in range(N_HEADS)], axis=1)
    o_ref[0] = _rms(o, gn_ref[...], N_HEADS * HEAD_DIM).astype(BF16)


def _sample_slc(pages, halves, bids, cache_t, q, kvnew, ocmp, owin, gates, t_near, t_last, rb0, gn_pad,
                new_block, near_block):
    n = q.shape[0]
    n_blk = N_KV * N_SELECT
    blk_specs = [pl.BlockSpec(_PAGE_BLOCK, lambda b, pg, hf, bi, k=k: (pg[b, k], 1, 0, 0, 0))
                 for k in range(n_blk)]
    per_b = lambda a: pl.BlockSpec((1,) + a.shape[1:], lambda b, *_: (b,) + (0,) * (a.ndim - 1))
    cst = lambda a: pl.BlockSpec(a.shape, lambda *_: (0,) * a.ndim)
    grid_spec = pltpu.PrefetchScalarGridSpec(
        num_scalar_prefetch=3,
        grid=(n,),
        in_specs=blk_specs + [per_b(q), per_b(kvnew), per_b(ocmp), per_b(owin), per_b(gates),
                              cst(t_near), cst(t_last), cst(rb0), cst(gn_pad)],
        out_specs=pl.BlockSpec((1, 1, N_HEADS * LANE), lambda b, *_: (b, 0, 0)),
    )
    return pl.pallas_call(
        functools.partial(_sample_slc_kernel, new_block=new_block, near_block=near_block),
        grid_spec=grid_spec,
        out_shape=jax.ShapeDtypeStruct((n, 1, N_HEADS * LANE), BF16),
        compiler_params=_cparams(1),
        name="sample_selected",
    )(pages, halves, bids, *([cache_t] * n_blk), q, kvnew, ocmp, owin, gates, t_near, t_last, rb0, gn_pad)


def _proj_kernel(on_ref, zn_ref, x_ref, wa_ref, wb_ref, g1_ref, b1_ref, wr_ref, br_ref,
                 x1_ref, gate_ref, *, alpha):
    mix = _mm(on_ref[...], wa_ref[...]) + _mm(zn_ref[...], wb_ref[...])
    x1 = _layernorm(alpha * x_ref[...] + mix, g1_ref[...], b1_ref[...])
    x1_ref[...] = x1
    logits = _mm(x1.astype(BF16), wr_ref[...]) + br_ref[...]
    tm = logits.shape[0]
    lane = lax.broadcasted_iota(jnp.int32, (tm, LANE), 1).astype(F32)

    def first_argmax(v, vmax):
        return jnp.min(jnp.where(v == vmax, lane, float(LANE)), axis=1, keepdims=True)

    lg = jnp.where(lane < N_GROUPS, logits[:, 0:LANE], -jnp.inf)
    lg_max = jnp.max(lg, axis=1, keepdims=True)
    eg = jnp.exp(lg - lg_max)
    pg = eg / jnp.sum(eg, axis=1, keepdims=True)
    gidx = first_argmax(lg, lg_max)
    pg_sel = jnp.sum(jnp.where(lane == gidx, pg, 0.0), axis=1, keepdims=True)
    le = jnp.zeros((tm, LANE), F32)
    for gi in range(N_GROUPS):
        le = le + jnp.where(gidx == gi, logits[:, (gi + 1) * LANE:(gi + 2) * LANE], 0.0)
    le = jnp.where(lane < N_EXP, le, -jnp.inf)
    ee = jnp.exp(le - jnp.max(le, axis=1, keepdims=True))
    pe = jnp.where(lane < N_EXP, ee / jnp.sum(ee, axis=1, keepdims=True), -1.0)
    v1 = jnp.max(pe, axis=1, keepdims=True)
    i1 = first_argmax(pe, v1)
    pe2 = jnp.where(lane == i1, -1.0, pe)
    v2 = jnp.max(pe2, axis=1, keepdims=True)
    i2 = first_argmax(pe2, v2)
    tot = v1 + v2
    gate_e = jnp.where(lane == i1, v1 / tot * pg_sel, jnp.where(lane == i2, v2 / tot * pg_sel, 0.0))
    gate_ref[...] = jnp.where(lane == GROUP_LANE, gidx, gate_e)


def _proj(on, zn, x, wa, wb, g1, b1, wr, br, alpha, tm):
    t, d = x.shape
    row = lambda n: pl.BlockSpec((tm, n), lambda i: (i, 0))
    outs = (jax.ShapeDtypeStruct((t, d), F32), jax.ShapeDtypeStruct((t, LANE), F32))
    return pl.pallas_call(
        functools.partial(_proj_kernel, alpha=alpha),
        grid=(t // tm,),
        in_specs=[row(on.shape[1]), row(zn.shape[1]), row(d)] + [_const_spec(a.shape) for a in (wa, wb, g1, b1, wr, br)],
        out_specs=(row(d), row(LANE)),
        out_shape=outs,
        compiler_params=_cparams(1),
        name=f"proj_ln_router_{tm}",
    )(on, zn, x, wa, wb, g1, b1, wr, br)


def _moe_kernel(x1_ref, gate_ref, wg_ref, wu_ref, wd_ref, g2_ref, b2_ref, out_ref, acc_ref, *, alpha):
    gi = pl.program_id(1)

    @pl.when(gi == 0)
    def _():
        acc_ref[...] = jnp.zeros(acc_ref.shape, F32)

    x = x1_ref[...].astype(BF16)
    gate = gate_ref[...]
    gt = jnp.where(gate[:, GROUP_LANE:GROUP_LANE + 1] == gi.astype(F32), gate, 0.0)
    acc = acc_ref[...]
    n_half = wg_ref.shape[1]
    for e in range(N_EXP):
        part, k = divmod(e, n_half)
        h = jax.nn.silu(_mm(x, wg_ref[part, k])) * _mm(x, wu_ref[part, k])
        acc = acc + _mm((h * gt[:, e:e + 1]).astype(BF16), wd_ref[part, k])
    acc_ref[...] = acc

    @pl.when(gi == N_GROUPS - 1)
    def _():
        out_ref[...] = _layernorm(alpha * x1_ref[...] + acc_ref[...], g2_ref[...], b2_ref[...])


def _moe(x1, gate, wg, wu, wd, g2, b2, alpha, tm):
    t, d = x1.shape
    d_ff = wd.shape[2]
    n_half = wg.shape[1]
    return pl.pallas_call(
        functools.partial(_moe_kernel, alpha=alpha),
        grid=(t // tm, N_GROUPS),
        in_specs=[
            pl.BlockSpec((tm, d), lambda i, g: (i, 0)),
            pl.BlockSpec((tm, LANE), lambda i, g: (i, 0)),
            pl.BlockSpec((EXPERT_SPLIT, n_half, d, d_ff), lambda i, g: (g, 0, 0, 0)),
            pl.BlockSpec((EXPERT_SPLIT, n_half, d, d_ff), lambda i, g: (g, 0, 0, 0)),
            pl.BlockSpec((EXPERT_SPLIT, n_half, d_ff, d), lambda i, g: (g, 0, 0, 0)),
            _const_spec(g2.shape), _const_spec(b2.shape),
        ],
        out_specs=pl.BlockSpec((tm, d), lambda i, g: (i, 0)),
        out_shape=jax.ShapeDtypeStruct((t, d), F32),
        scratch_shapes=[pltpu.VMEM((tm, d), F32)],
        compiler_params=_cparams(2),
        name=f"moe_ln_{tm}",
    )(x1, gate, wg, wu, wd, g2, b2)


def _split_bf16(x):
    hi = x.astype(BF16)
    return hi, (x - hi.astype(F32)).astype(BF16)


def _moe_sorted_kernel(cnt_ref, off_ref, x1_ref, gate_ref, drow_ref, dcol_ref, wg_ref, wu_ref, wd_ref, g2_ref, b2_ref,
                       out_ref, xs_ref, gs_ref, ys_ref, *, alpha):
    i = pl.program_id(0)
    gi = pl.program_id(1)
    half = pl.program_id(2)
    n_half = N_EXP // EXPERT_SPLIT
    tmoe = x1_ref.shape[0]

    @pl.when((i == 0) & (gi == 0) & (half == 0))
    def _():
        xs_ref[...] = jnp.zeros(xs_ref.shape, BF16)
        gs_ref[...] = jnp.zeros(gs_ref.shape, F32)
        ys_ref[...] = jnp.zeros(ys_ref.shape, F32)

    @pl.when((gi == 0) & (half == 0))
    def _():
        perm = (lax.broadcasted_iota(jnp.int32, (PERM_ROWS, tmoe), 0) == drow_ref[0]).astype(BF16)
        xs_ref[0:PERM_ROWS, :] = _mm(perm, x1_ref[...].astype(BF16)).astype(BF16)
        g_hi, g_lo = _split_bf16(gate_ref[...])
        gs_ref[0:PERM_ROWS, :] = _mm(perm, g_hi) + _mm(perm, g_lo)

    n = cnt_ref[i * N_GROUPS + gi]
    off = off_ref[i * N_GROUPS + gi]

    def run_pass(c, carry):
        base = pl.multiple_of(off + c * MOE_CHUNK, SEG_ALIGN)
        xc = xs_ref[pl.ds(base, MOE_CHUNK), :]
        gc = gs_ref[pl.ds(base, MOE_CHUNK), :]
        gc = jnp.where(half == 0, gc, pltpu.roll(gc, LANE - n_half, axis=1))
        y = jnp.zeros((MOE_CHUNK, out_ref.shape[1]), F32)
        for e in range(n_half):
            h = jax.nn.silu(_mm(xc, wg_ref[0, e])) * _mm(xc, wu_ref[0, e])
            y = y + _mm((h * gc[:, e:e + 1]).astype(BF16), wd_ref[0, e])

        @pl.when(half == 0)
        def _():
            ys_ref[pl.ds(base, MOE_CHUNK), :] = y

        @pl.when(half != 0)
        def _():
            ys_ref[pl.ds(base, MOE_CHUNK), :] = ys_ref[pl.ds(base, MOE_CHUNK), :] + y

        return carry

    lax.fori_loop(0, (n + MOE_CHUNK - 1) // MOE_CHUNK, run_pass, 0)

    @pl.when((gi == N_GROUPS - 1) & (half == EXPERT_SPLIT - 1))
    def _():
        unperm = (dcol_ref[...] == lax.broadcasted_iota(jnp.int32, (tmoe, PERM_ROWS), 1)).astype(BF16)
        y_hi, y_lo = _split_bf16(ys_ref[0:PERM_ROWS, :])
        moe = _mm(unperm, y_hi) + _mm(unperm, y_lo)
        out_ref[...] = _layernorm(alpha * x1_ref[...] + moe, g2_ref[...], b2_ref[...])


def _moe_sorted(x1, gate, wg_h, wu_h, wd_h, g2, b2, alpha):
    t, d = x1.shape
    d_ff = wd_h.shape[2]
    n_tiles = t // TMOE
    n_half = N_EXP // EXPERT_SPLIT
    gid = gate[:, GROUP_LANE].astype(jnp.int32).reshape(n_tiles, TMOE)
    onehot = (gid[:, :, None] == jnp.arange(N_GROUPS)[None, None, :]).astype(jnp.int32)
    cnt = jnp.sum(onehot, axis=1)
    rank = jnp.sum((jnp.cumsum(onehot, axis=1) - onehot) * onehot, axis=2)
    seg = (cnt + SEG_ALIGN - 1) // SEG_ALIGN * SEG_ALIGN
    off = jnp.cumsum(seg, axis=1) - seg
    dest = jnp.sum(onehot * off[:, None, :], axis=2) + rank
    assert N_GROUPS * (SEG_ALIGN - 1) <= PERM_ROWS - TMOE
    grid_spec = pltpu.PrefetchScalarGridSpec(
        num_scalar_prefetch=2,
        grid=(n_tiles, N_GROUPS, EXPERT_SPLIT),
        in_specs=[
            pl.BlockSpec((TMOE, d), lambda i, g, h, *_: (i, 0)),
            pl.BlockSpec((TMOE, LANE), lambda i, g, h, *_: (i, 0)),
            pl.BlockSpec((1, 1, TMOE), lambda i, g, h, *_: (i, 0, 0)),
            pl.BlockSpec((TMOE, 1), lambda i, g, h, *_: (i, 0)),
            pl.BlockSpec((1, n_half, d, d_ff), lambda i, g, h, *_: (g * EXPERT_SPLIT + h, 0, 0, 0)),
            pl.BlockSpec((1, n_half, d, d_ff), lambda i, g, h, *_: (g * EXPERT_SPLIT + h, 0, 0, 0)),
            pl.BlockSpec((1, n_half, d_ff, d), lambda i, g, h, *_: (g * EXPERT_SPLIT + h, 0, 0, 0)),
            pl.BlockSpec(g2.shape, lambda *_: (0, 0)), pl.BlockSpec(b2.shape, lambda *_: (0, 0)),
        ],
        out_specs=pl.BlockSpec((TMOE, d), lambda i, g, h, *_: (i, 0)),
        scratch_shapes=[pltpu.VMEM((SORT_ROWS, d), BF16), pltpu.VMEM((SORT_ROWS, LANE), F32),
                        pltpu.VMEM((SORT_ROWS, d), F32)],
    )
    return pl.pallas_call(
        functools.partial(_moe_sorted_kernel, alpha=alpha),
        grid_spec=grid_spec,
        out_shape=jax.ShapeDtypeStruct((t, d), F32),
        compiler_params=_cparams(3),
        name="moe_sorted_ln",
    )(cnt.reshape(-1), off.reshape(-1), x1, gate, dest.reshape(n_tiles, 1, TMOE), dest.reshape(t, 1), wg_h, wu_h, wd_h,
      g2, b2)


def _pad_head_lanes(v):
    vh = v.reshape(N_HEADS, HEAD_DIM)
    z = jnp.zeros_like(vh)
    grp = (jnp.arange(N_HEADS) // GQ)[:, None]
    return jnp.where(grp == 0, jnp.concatenate([vh, z], axis=1), jnp.concatenate([z, vh], axis=1)).reshape(1, -1)


def _pad_head_rows(w):
    wh = w.reshape(N_HEADS, HEAD_DIM, -1)
    z = jnp.zeros_like(wh)
    grp = (jnp.arange(N_HEADS) // GQ)[:, None, None]
    return jnp.where(grp == 0, jnp.concatenate([wh, z], axis=1), jnp.concatenate([z, wh], axis=1)).reshape(
        N_HEADS * LANE, -1)


def kernel(x_prompt, x_sample, cache_kv, page_table, cache_win, state_conv, w_in, conv_w, pe_k, pe_v, w_ck1, w_ck2, w_cv1, w_cv2, g_nsa, g_conv, w_out, ln1_g, ln1_b, w_rg, b_rg, w_re, b_re, w_eg, w_eu, w_ed, ln2_g, ln2_b, rel_bias):
    batch, seq, d_model = x_prompt.shape
    dec_batch, dec_seq = x_sample.shape[0], x_sample.shape[1]
    depth = w_in.shape[0]
    n_pages = page_table.shape[1]
    past = n_pages * PAGE_SIZE
    win_buf = cache_win.shape[2]
    d_nsa = N_HEADS * HEAD_DIM
    d_conv = w_out.shape[1] - d_nsa
    assert dec_seq == 1 and seq % TM == 0 and min(WINDOW, seq) == TM and win_buf == WINDOW
    assert d_conv == 512 and seq // CMP_STRIDE == LANE and n_pages % SAMPLE_PAGES == 0
    alpha = (2 * depth) ** 0.25
    n_slc_s = -(-(past + dec_seq) // SLC_LEN)
    qblk_s = past // SLC_LEN

    idx_np, sp_p_np, sp_s_np = _static_tables(seq, past, win_buf)
    tabs = _bias_tables(rel_bias, jnp.asarray(idx_np))
    nq = seq // TQ
    near_t = jnp.transpose(tabs[0:2], (1, 0, 2, 3)).reshape(N_HEADS, 2 * LANE, LANE)
    up_t = tabs[2, 0]
    tab_c = tabs[_N_NEAR_TILES:_N_NEAR_TILES + nq]
    smp = tabs[_N_NEAR_TILES + nq]
    tab_w_s = smp[:, 0:5, :].reshape(N_HEADS, 5 * LANE)[:, :win_buf]
    tab_c_s = smp[:, 5:13, :].reshape(N_HEADS, 8 * LANE)
    reps = N_SELECT * PAGE_SIZE // SLC_LEN
    t_near = jnp.tile(smp[:, 13, 0:SLC_LEN], (1, reps))
    t_last = jnp.tile(smp[:, 13, SLC_LEN:2 * SLC_LEN], (1, reps))
    rb0_rep = jnp.broadcast_to(smp[:, 14, 0:1], (N_HEADS, LANE))
    sp_p = jnp.asarray(sp_p_np, BF16)
    sp_s = jnp.asarray(sp_s_np, BF16)

    xp = x_prompt.reshape(batch * seq, d_model)
    xs = x_sample.reshape(dec_batch * dec_seq, d_model)
    outs = [[] for _ in range(6)]
    for l in range(depth):
        w_pack = _pack_in_weights(w_in[l])
        gc = g_conv[l].reshape(1, -1)
        gn_pad = _pad_head_lanes(g_nsa[l])
        w1k, w2k, pek = _pack_compress_weights(pe_k[l], w_ck1[l], w_ck2[l], True)
        w1v, w2v, pev = _pack_compress_weights(pe_v[l], w_cv1[l], w_cv2[l], False)
        w1 = jnp.stack([w1k, w1v])
        pe = jnp.stack([pek, pev])
        wa = _pad_head_rows(w_out[l][:d_nsa]).astype(BF16)
        wb = w_out[l][d_nsa:].astype(BF16)
        wr = jnp.concatenate(
            [jnp.pad(w_rg[l], ((0, 0), (0, LANE - N_GROUPS)))]
            + [jnp.pad(w_re[l][:, gi * N_EXP:(gi + 1) * N_EXP], ((0, 0), (0, LANE - N_EXP))) for gi in range(N_GROUPS)],
            axis=1).astype(BF16)
        br = jnp.concatenate(
            [jnp.pad(b_rg[l], (0, LANE - N_GROUPS))]
            + [jnp.pad(b_re[l][gi * N_EXP:(gi + 1) * N_EXP], (0, LANE - N_EXP)) for gi in range(N_GROUPS)]).reshape(1, -1)
        halves = lambda w: w.astype(BF16).reshape((N_GROUPS * EXPERT_SPLIT, N_EXP // EXPERT_SPLIT) + w.shape[2:])
        wg, wu, wd = halves(w_eg[l]), halves(w_eu[l]), halves(w_ed[l])
        g1, b1 = ln1_g[l].reshape(1, -1), ln1_b[l].reshape(1, -1)
        g2, b2 = ln2_g[l].reshape(1, -1), ln2_b[l].reshape(1, -1)

        q_p, kvt_p, ks, sv, kw, wv, wtail_t, gates_p, zn_p, ctail = _inproj_prompt(xp, w_pack, conv_w[l], gc, batch, seq)
        kvt_p = kvt_p.reshape(batch, 4, N_KV, HEAD_DIM, seq)
        cmpk_p, cmpvt_p = _compress_prompt(kvt_p, batch, seq, w1, w2k, w2v, pe)
        gn_rep = jnp.broadcast_to(gn_pad.reshape(-1, 1), (N_HEADS * LANE, TQ))
        on_p = _attn_prompt(q_p, gates_p, cmpk_p, cmpvt_p, ks, sv, kw, wv, near_t, up_t, tab_c, sp_p, gn_rep,
                            batch, seq)
        x1_p, gate_p = _proj(on_p, zn_p, xp, wa, wb, g1, b1, wr, br, alpha, TM)
        y_p = _moe_sorted(x1_p, gate_p, wg, wu, wd, g2, b2, alpha)

        st = state_conv[l]
        qlo_s, qgl_s, kv_s, win_s, gates_s, zn_s, u_s = _inproj_sample(xs, w_pack, conv_w[l], gc, st[:, 0], st[:, 1])
        cache_t = jnp.transpose(cache_kv[l], (0, 2, 3, 4, 1))
        cwin_t = jnp.transpose(cache_win[l], (0, 2, 3, 4, 1))
        cmpk_s, cmpvt_s = _compress_sample(cache_t, page_table, w1, w2k, w2v, pe)
        qlo_s3 = qlo_s.reshape(dec_batch, N_HEADS, LANE)
        qgl_s3 = qgl_s.reshape(dec_batch, N_HEADS, LANE)
        ocmp, owin, sel_idx = _sample_cw(qlo_s3, qgl_s3, cmpk_s, cmpvt_s, cwin_t, win_s.reshape(dec_batch, 1, 256), tab_c_s,
                                         tab_w_s, rb0_rep, sp_s, n_slc_s, qblk_s)
        bids = sel_idx[:, :, :N_SELECT].reshape(dec_batch, N_KV * N_SELECT)
        blk_pages = jnp.take_along_axis(page_table, jnp.minimum(bids // 2, n_pages - 1), axis=1)
        gates_s3 = jnp.pad(gates_s.reshape(dec_batch, N_KV, LANE)[:, :, :3 * GQ].reshape(dec_batch, N_HEADS, 3),
                           ((0, 0), (0, 0), (0, LANE - 3)))
        on_s = _sample_slc(blk_pages, bids % 2, bids, cache_t, qgl_s3, kv_s.reshape(dec_batch, 1, 512), ocmp, owin,
                           gates_s3, t_near, t_last, rb0_rep, gn_pad, n_slc_s - 1, qblk_s - 2)
        on_s = on_s.reshape(dec_batch, N_HEADS * LANE)
        x1_s, gate_s = _proj(on_s, zn_s, xs, wa, wb, g1, b1, wr, br, alpha, dec_batch)
        y_s = _moe(x1_s, gate_s, wg, wu, wd, g2, b2, alpha, dec_batch)

        to_token_major = lambda a: jnp.transpose(a, (0, 4, 1, 2, 3))
        outs[0].append(to_token_major(kvt_p))
        outs[1].append(kv_s.reshape(dec_batch, dec_seq, 4, N_KV, HEAD_DIM))
        outs[2].append(to_token_major(wtail_t.reshape(batch, 2, N_KV, HEAD_DIM, TM)))
        win_all_t = jnp.concatenate([cwin_t, win_s.reshape(dec_batch, 2, N_KV, HEAD_DIM, dec_seq)], axis=-1)
        outs[3].append(to_token_major(win_all_t[..., win_all_t.shape[-1] - min(WINDOW, past + dec_seq):]))
        outs[4].append(ctail)
        outs[5].append(jnp.concatenate([st, u_s[:, None, :]], axis=1)[:, dec_seq:])
        xp, xs = y_p, y_s
    return (xp.reshape(batch, seq, d_model), xs.reshape(dec_batch, dec_seq, d_model),
            jnp.stack(outs[0]), jnp.stack(outs[1]), jnp.stack(outs[2]), jnp.stack(outs[3]),
            jnp.stack(outs[4]), jnp.stack(outs[5]))
```

```python
import functools
import math

import numpy as np
import jax
import jax.numpy as jnp
from jax import lax
from jax.experimental import pallas as pl
from jax.experimental.pallas import tpu as pltpu

F32 = jnp.float32
BF16 = jnp.bfloat16

HEAD_DIM = 64
N_KV = 2
GQ = 4
N_HEADS = N_KV * GQ
CONV_W = 3
CMP_STRIDE = 16
CMP_LEN = 32
CMP_HIDDEN = 2 * HEAD_DIM
SLC_LEN = 64
N_SELECT = 16
WINDOW = 512
N_BUCKETS = 32
MAX_DISTANCE = 128
N_GROUPS = 4
N_EXP = 8
PAGE_SIZE = 128
NEG = -1e30
FORCE = 1e6

LANE = 128
SUBLANE = 8
VMEM_LIMIT = 52 * 1024 * 1024

TQ = 128
TM = 512
FAR = 512
TAIL = WINDOW + TQ
SAMPLE_PAGES = 64
SAMPLE_BB = 4
TMOE = 1024
MOE_CHUNK = 320
SEG_ALIGN = 16
PERM_ROWS = TMOE + LANE
SORT_ROWS = PERM_ROWS + MOE_CHUNK
EXPERT_SPLIT = 2
GROUP_LANE = N_EXP

KEY_MASK = 2.0 ** 100
TABLE_MASK = 2 * NEG
MASKED_BUCKET = N_BUCKETS
SEL_LANE0 = HEAD_DIM
PAD_LANE = HEAD_DIM + 32


def _cparams(n_axes):
    return pltpu.CompilerParams(dimension_semantics=("arbitrary",) * n_axes, vmem_limit_bytes=VMEM_LIMIT)


def _const_spec(shape):
    nd = len(shape)
    return pl.BlockSpec(shape, lambda *_, nd=nd: (0,) * nd)


def _qk(a, b):
    return lax.dot_general(a, b, (((1,), (1,)), ((), ())), preferred_element_type=F32)


def _mm(a, b):
    return jnp.dot(a, b, preferred_element_type=F32)


def _bucket_np(dist):
    n = np.maximum(dist, 0)
    max_exact = N_BUCKETS // 2
    nf = np.maximum(n, 1).astype(np.float32)
    large = max_exact + (np.log(nf / np.float32(max_exact)) / np.float32(math.log(MAX_DISTANCE / max_exact))
                         * np.float32(N_BUCKETS - max_exact)).astype(np.int32)
    large = np.minimum(large, N_BUCKETS - 1)
    return np.where(n < max_exact, n, large).astype(np.int32)


def _overlap_np(c, s):
    c0 = c * CMP_STRIDE
    s0 = s * SLC_LEN
    return np.maximum(np.minimum(c0 + CMP_LEN, s0 + SLC_LEN) - np.maximum(c0, s0), 0)


_N_NEAR_TILES = 3


def _static_tables(seq, past, win_buf):
    nq = seq // TQ
    i = np.arange(LANE)[:, None]
    j = np.arange(LANE)[None, :]
    far_bucket = N_BUCKETS - 1
    tiles = [_bucket_np(LANE + j - i),
             np.where(j >= i, _bucket_np(j - i), MASKED_BUCKET),
             np.where(i > j, far_bucket, MASKED_BUCKET) + 0 * i]
    tiles += [_bucket_np(TQ * q + j - (CMP_STRIDE * i + CMP_STRIDE - 1)) for q in range(nq)]
    smp = np.full((LANE, LANE), 10 * MAX_DISTANCE, np.int64)
    k = np.arange(5 * LANE)
    smp[0:5] = np.where(k < win_buf, win_buf - k, 0).reshape(5, LANE)
    ci = np.arange(8 * LANE)
    smp[5:13] = np.maximum(past - (CMP_STRIDE * ci + CMP_STRIDE - 1), 0).reshape(8, LANE)
    nb = past // SLC_LEN
    smp[13] = past - (SLC_LEN * (nb - 2) + np.arange(LANE))
    smp[14] = 0
    tiles.append(_bucket_np(smp))
    idx = np.stack(tiles).astype(np.int32)

    def sp(nrows, ncols, n_slc):
        r = np.arange(nrows)[:, None]
        s = np.arange(ncols)[None, :]
        ov = _overlap_np(r - 1, s)
        return np.where((r >= 1) & (s < n_slc), ov, 0).astype(np.float32)

    sp_p = sp(seq // CMP_STRIDE, LANE, -(-seq // SLC_LEN)).T
    n_slc_s = -(-(past + 1) // SLC_LEN)
    sp_s = sp(past // CMP_STRIDE, 3 * LANE, n_slc_s)
    return idx, sp_p, sp_s


def _bias_kernel(rb_ref, idx_ref, out_ref):
    idx = idx_ref[0]
    accs = [jnp.full((LANE, LANE), TABLE_MASK, F32) for _ in range(N_HEADS)]
    for b in range(N_BUCKETS):
        hit = idx == b
        for h in range(N_HEADS):
            accs[h] = jnp.where(hit, rb_ref[b, h] - rb_ref[N_BUCKETS - 1, h], accs[h])
    for h in range(N_HEADS):
        out_ref[0, h] = accs[h]


def _bias_tables(rel_bias, idx):
    nt = idx.shape[0]
    return pl.pallas_call(
        _bias_kernel,
        grid=(nt,),
        in_specs=[pl.BlockSpec(memory_space=pltpu.SMEM),
                  pl.BlockSpec((1, LANE, LANE), lambda t: (t, 0, 0))],
        out_specs=pl.BlockSpec((1, N_HEADS, LANE, LANE), lambda t: (t, 0, 0, 0)),
        out_shape=jax.ShapeDtypeStruct((nt, N_HEADS, LANE, LANE), F32),
        compiler_params=_cparams(1),
        name="bias_tables",
    )(rel_bias, idx)


_Q0, _Q1 = 0, N_HEADS * HEAD_DIM
_KV0, _KV1 = _Q1, _Q1 + 512
_WN0, _WN1 = _KV1, _KV1 + 256
_GT0, _GT1 = _WN1, _WN1 + 2 * LANE
_CV0, _CV1 = _GT1, _GT1 + 3 * 512


def _pack_in_weights(w):
    d = w.shape[0]
    d_nsa = N_HEADS * HEAD_DIM
    wq_pad = w[:, :d_nsa]
    o = d_nsa
    w_kv = w[:, o:o + 512]
    w_win = w[:, o + 512:o + 768]
    wg = w[:, o + 768:o + 768 + 3 * N_HEADS].reshape(d, N_KV, 3 * GQ)
    wg_pad = jnp.pad(wg, ((0, 0), (0, 0), (0, LANE - 3 * GQ))).reshape(d, N_KV * LANE)
    w_conv = w[:, o + 768 + 3 * N_HEADS:]
    return jnp.concatenate([wq_pad, w_kv, w_win, wg_pad, w_conv], axis=1).astype(BF16)


def _rms(x, gain, n, eps=1e-6):
    ms = jnp.sum(x * x, axis=-1, keepdims=True) / n
    return x * lax.rsqrt(ms + eps) * gain


def _layernorm(y, gain, bias, eps=1e-5):
    mu = jnp.mean(y, axis=-1, keepdims=True)
    d = y - mu
    var = jnp.mean(d * d, axis=-1, keepdims=True)
    return d * lax.rsqrt(var + eps) * gain + bias


def _inproj_prompt_kernel(x_ref, w_ref, cw_ref, gc_ref, ks_fill, kw_fill, sv_fill, wv_fill,
                          q_ref, kv_ref, ks_ref, sv_ref, kw_ref, wv_ref, wtail_ref, gate_ref, zn_ref, ctail_ref,
                          uext_ref, *, tiles_per_batch):
    del ks_fill, kw_fill, sv_fill, wv_fill
    i = pl.program_id(0)
    tm = x_ref.shape[0]
    xb = x_ref[...].astype(BF16)

    def seg(a, b):
        return _mm(xb, w_ref[:, a:b])

    qp_t = (seg(_Q0, _Q1) * (HEAD_DIM ** -0.5)).T
    q_fill = jnp.zeros((LANE - HEAD_DIM, tm), F32)
    for h in range(N_HEADS):
        q_ref[h] = jnp.concatenate([qp_t[h * HEAD_DIM:(h + 1) * HEAD_DIM, :], q_fill], axis=0).astype(BF16)
    kv = seg(_KV0, _KV1)
    kv_t = kv.T
    kv_ref[0] = kv_t
    win = seg(_WN0, _WN1)
    win_t = win.T
    wtail_ref[0] = win_t
    pos = (i % tiles_per_batch) * tm + lax.broadcasted_iota(jnp.int32, (tm, LANE - HEAD_DIM), 0)
    lane = lax.broadcasted_iota(jnp.int32, (tm, LANE - HEAD_DIM), 1)
    blk_flag = jnp.where(lane == lax.shift_right_logical(pos, 6), KEY_MASK, 0.0).astype(BF16)
    no_flag = jnp.zeros((tm, LANE - HEAD_DIM), BF16)
    for g in range(N_KV):
        ks_ref[g, 0] = jnp.concatenate([kv[:, 256 + g * HEAD_DIM:256 + (g + 1) * HEAD_DIM].astype(BF16), blk_flag], axis=1)
        kw_ref[g, 0] = jnp.concatenate([win[:, g * HEAD_DIM:(g + 1) * HEAD_DIM].astype(BF16), no_flag], axis=1)
    sv_ref[0] = kv_t[384:512, :].astype(BF16)
    wv_ref[0] = win_t[LANE:2 * LANE, :].astype(BF16)
    gt_t = jax.nn.sigmoid(seg(_GT0, _GT1)).T
    gate_ref[0] = gt_t[0:LANE, :]
    gate_ref[1] = gt_t[LANE:2 * LANE, :]

    conv = seg(_CV0, _CV1)
    cb = conv[:, 0:512]
    u = conv[:, 512:1024] * conv[:, 1024:1536]
    first = (i % tiles_per_batch) == 0

    @pl.when(first)
    def _():
        uext_ref[0:SUBLANE, :] = jnp.zeros((SUBLANE, 512), F32)

    @pl.when(jnp.logical_not(first))
    def _():
        uext_ref[0:SUBLANE, :] = uext_ref[tm:tm + SUBLANE, :]

    uext_ref[SUBLANE:tm + SUBLANE, :] = u
    um1 = uext_ref[SUBLANE - 1:tm + SUBLANE - 1, :]
    um2 = uext_ref[SUBLANE - 2:tm + SUBLANE - 2, :]
    z = cb * (um2 * cw_ref[0:1, :] + um1 * cw_ref[1:2, :] + u * cw_ref[2:3, :])
    zn_ref[...] = _rms(z, gc_ref[...], 512).astype(BF16)
    ctail_ref[0] = uext_ref[tm + SUBLANE - 2:tm + SUBLANE, :]


def _inproj_prompt(x, w, conv_w, g_conv, batch, seq):
    t, d = x.shape
    n_tiles = t // TM
    tpb = seq // TM
    assert WINDOW == TM
    rows = WINDOW + seq
    key_pad = jnp.zeros((LANE,), BF16).at[PAD_LANE].set(KEY_MASK)
    k_fill = jnp.broadcast_to(key_pad, (N_KV, batch, rows, LANE))
    v_fill = jnp.zeros((batch, LANE, rows), BF16)
    outs = (
        jax.ShapeDtypeStruct((N_HEADS, LANE, t), BF16),
        jax.ShapeDtypeStruct((batch, 512, seq), F32),
        jax.ShapeDtypeStruct(k_fill.shape, BF16),
        jax.ShapeDtypeStruct(v_fill.shape, BF16),
        jax.ShapeDtypeStruct(k_fill.shape, BF16),
        jax.ShapeDtypeStruct(v_fill.shape, BF16),
        jax.ShapeDtypeStruct((batch, 256, TM), F32),
        jax.ShapeDtypeStruct((N_KV, LANE, t), F32),
        jax.ShapeDtypeStruct((t, 512), BF16),
        jax.ShapeDtypeStruct((batch, CONV_W - 1, 512), F32),
    )
    row = lambda n: pl.BlockSpec((TM, n), lambda i: (i, 0))
    planes_t = lambda n: pl.BlockSpec((n, LANE, TM), lambda i: (0, 0, i))
    keys_out = pl.BlockSpec((N_KV, 1, TM, LANE), lambda i: (0, i // tpb, i % tpb + 1, 0))
    vals_out = pl.BlockSpec((1, LANE, TM), lambda i: (i // tpb, 0, i % tpb + 1))
    in_place = pl.BlockSpec(memory_space=pl.ANY)
    return pl.pallas_call(
        functools.partial(_inproj_prompt_kernel, tiles_per_batch=tpb),
        grid=(n_tiles,),
        in_specs=[row(d), _const_spec(w.shape), _const_spec(conv_w.shape), _const_spec(g_conv.shape),
                  in_place, in_place, in_place, in_place],
        input_output_aliases={4: 2, 5: 4, 6: 3, 7: 5},
        out_specs=(
            planes_t(N_HEADS),
            pl.BlockSpec((1, 512, TM), lambda i: (i // tpb, 0, i % tpb)),
            keys_out, vals_out, keys_out, vals_out,
            pl.BlockSpec((1, 256, TM), lambda i: (i // tpb, 0, 0)),
            planes_t(N_KV),
            row(512),
            pl.BlockSpec((1, CONV_W - 1, 512), lambda i: (i // tpb, 0, 0)),
        ),
        out_shape=outs,
        scratch_shapes=[pltpu.VMEM((TM + SUBLANE, 512), F32)],
        compiler_params=_cparams(1),
        name="inproj_prompt",
    )(x, w, conv_w, g_conv, k_fill, k_fill, v_fill, v_fill)


def _inproj_sample_kernel(x_ref, w_ref, cw_ref, gc_ref, s0_ref, s1_ref,
                          qlo_ref, qgl_ref, kv_ref, win_ref, gate_ref, zn_ref, u_ref):
    xb = x_ref[...].astype(BF16)

    def seg(a, b):
        return _mm(xb, w_ref[:, a:b])

    qp = seg(_Q0, _Q1) * (HEAD_DIM ** -0.5)
    fill = jnp.zeros((qp.shape[0], LANE - HEAD_DIM), F32)
    heads = [qp[:, h * HEAD_DIM:(h + 1) * HEAD_DIM] for h in range(N_HEADS)]
    qlo_ref[...] = jnp.concatenate([x for qh in heads for x in (qh, fill)], axis=1).astype(BF16)
    qgl_ref[...] = jnp.concatenate([x for h, qh in enumerate(heads) for x in ((qh, fill) if h < GQ else (fill, qh))],
                                   axis=1).astype(BF16)
    kv_ref[...] = seg(_KV0, _KV1)
    win_ref[...] = seg(_WN0, _WN1)
    gate_ref[...] = jax.nn.sigmoid(seg(_GT0, _GT1))
    conv = seg(_CV0, _CV1)
    cb = conv[:, 0:512]
    u = conv[:, 512:1024] * conv[:, 1024:1536]
    z = cb * (s0_ref[...] * cw_ref[0:1, :] + s1_ref[...] * cw_ref[1:2, :] + u * cw_ref[2:3, :])
    zn_ref[...] = _rms(z, gc_ref[...], 512).astype(BF16)
    u_ref[...] = u


def _inproj_sample(x, w, conv_w, g_conv, s0, s1):
    n = x.shape[0]
    outs = (
        jax.ShapeDtypeStruct((n, N_HEADS * LANE), BF16),
        jax.ShapeDtypeStruct((n, N_HEADS * LANE), BF16),
        jax.ShapeDtypeStruct((n, 512), F32),
        jax.ShapeDtypeStruct((n, 256), F32),
        jax.ShapeDtypeStruct((n, N_KV * LANE), F32),
        jax.ShapeDtypeStruct((n, 512), BF16),
        jax.ShapeDtypeStruct((n, 512), F32),
    )
    args = (x, w, conv_w, g_conv, s0, s1)
    return pl.pallas_call(
        _inproj_sample_kernel,
        grid=(1,),
        in_specs=[_const_spec(a.shape) for a in args],
        out_specs=tuple(_const_spec(o.shape) for o in outs),
        out_shape=outs,
        compiler_params=_cparams(1),
        name="inproj_sample",
    )(*args)


def _pack_compress_weights(pe, w1, w2, per_group_out):
    w = w1.reshape(2, CMP_STRIDE, HEAD_DIM, CMP_HIDDEN).transpose(1, 2, 0, 3).reshape(CMP_STRIDE, HEAD_DIM, 2 * CMP_HIDDEN)
    z = jnp.zeros_like(w)
    top = jnp.concatenate([w, z], axis=-1)
    bot = jnp.concatenate([z, w], axis=-1)
    w1bd = jnp.stack([top, bot], axis=1).reshape(CMP_STRIDE * 2 * HEAD_DIM, 4 * CMP_HIDDEN).astype(BF16)
    z2 = jnp.zeros_like(w2)
    if per_group_out:
        w2bd = jnp.concatenate([jnp.concatenate([w2, z2, z2, z2], axis=1),
                                jnp.concatenate([z2, z2, w2, z2], axis=1)], axis=0).astype(BF16)
    else:
        w2bd = jnp.concatenate([jnp.concatenate([w2, z2], axis=1), jnp.concatenate([z2, w2], axis=1)], axis=0).astype(BF16)
    pe_rows = jnp.broadcast_to(pe.reshape(2, CMP_STRIDE, 1, HEAD_DIM), (2, CMP_STRIDE, N_KV, HEAD_DIM))
    pe_rows = pe_rows.reshape(2, CMP_STRIDE * N_KV * HEAD_DIM)
    pe_rows = jnp.pad(pe_rows, ((0, SUBLANE - 2), (0, 0)))
    return w1bd, w2bd, pe_rows


def _compress_kernel(*refs, n_pages, n_prefetch):
    refs = refs[n_prefetch:]
    page_refs = refs[:n_pages]
    w1_ref, w2k_ref, w2v_ref, pe_ref, ck_ref, cvt_ref, carry_ref, tok_ref = refs[n_pages:]
    step = pl.program_id(1)
    m = n_pages * (PAGE_SIZE // CMP_STRIDE)

    @pl.when(step == 0)
    def _():
        carry_ref[...] = jnp.zeros(carry_ref.shape, F32)

    row0 = lax.broadcasted_iota(jnp.int32, (m, CMP_HIDDEN), 0) == 0
    for t, w2_ref in enumerate((w2k_ref, w2v_ref)):
        for p, pr in enumerate(page_refs):
            slab = pr[0, t].reshape(N_KV * HEAD_DIM, PAGE_SIZE)
            tok_ref[t, p * PAGE_SIZE:(p + 1) * PAGE_SIZE, :] = slab.T
        pieces = [tok_ref[t, pl.ds(j, m, stride=CMP_STRIDE), :] for j in range(CMP_STRIDE)]
        lhs = jnp.concatenate([jnp.concatenate(pieces, axis=1), pe_ref[t]], axis=0).astype(BF16)
        a = _mm(lhs, w1_ref[t])
        hs = []
        for g in range(N_KV):
            c0 = g * 2 * CMP_HIDDEN
            a0 = a[0:m, c0:c0 + CMP_HIDDEN]
            a1 = a[0:m, c0 + CMP_HIDDEN:c0 + 2 * CMP_HIDDEN]
            pe_term = a[m:m + 1, c0:c0 + CMP_HIDDEN] + a[m + 1:m + 2, c0 + CMP_HIDDEN:c0 + 2 * CMP_HIDDEN]
            prev = carry_ref[t, 0:1, g * CMP_HIDDEN:(g + 1) * CMP_HIDDEN]
            shifted = jnp.where(row0, prev, pltpu.roll(a0, 1, axis=0))
            carry_ref[t, 0:1, g * CMP_HIDDEN:(g + 1) * CMP_HIDDEN] = a0[m - 1:m, :]
            hs.append(jax.nn.gelu(shifted + a1 + pe_term))
        out = _mm(jnp.concatenate(hs, axis=1).astype(BF16), w2_ref[...])
        if t == 0:
            ck_ref[0, 0] = out[:, 0:LANE].astype(BF16)
            ck_ref[0, 1] = out[:, LANE:2 * LANE].astype(BF16)
        else:
            cvt_ref[0] = out.T.astype(BF16)


def _compress_call(page_specs, page_args, prefetch, grid, batch, n_chunks, m, w1, w2k, w2v, pe, name):
    n_pages = len(page_specs)
    n_pf = len(prefetch)
    cspec = lambda shape: pl.BlockSpec(shape, lambda *_: (0,) * len(shape))
    grid_spec = pltpu.PrefetchScalarGridSpec(
        num_scalar_prefetch=n_pf,
        grid=grid,
        in_specs=list(page_specs) + [cspec(w1.shape), cspec(w2k.shape), cspec(w2v.shape), cspec(pe.shape)],
        out_specs=(pl.BlockSpec((1, N_KV, m, LANE), lambda b, s, *_: (b, 0, s, 0)),
                   pl.BlockSpec((1, LANE, m), lambda b, s, *_: (b, 0, s))),
        scratch_shapes=[pltpu.VMEM((2, SUBLANE, 2 * CMP_HIDDEN), F32),
                        pltpu.VMEM((2, n_pages * PAGE_SIZE, N_KV * HEAD_DIM), F32)],
    )
    return pl.pallas_call(
        functools.partial(_compress_kernel, n_pages=n_pages, n_prefetch=n_pf),
        grid_spec=grid_spec,
        out_shape=(jax.ShapeDtypeStruct((batch, N_KV, n_chunks, LANE), BF16),
                   jax.ShapeDtypeStruct((batch, LANE, n_chunks), BF16)),
        compiler_params=_cparams(2),
        name=name,
    )(*prefetch, *page_args, w1, w2k, w2v, pe)


_PAGE_BLOCK = (1, 2, N_KV, HEAD_DIM, PAGE_SIZE)


def _compress_prompt(kv_t, batch, seq, w1, w2k, w2v, pe):
    n_pages = seq // PAGE_SIZE
    rows = PAGE_SIZE // CMP_STRIDE
    specs = [pl.BlockSpec(_PAGE_BLOCK, lambda b, s, p=p: (b, 0, 0, 0, p)) for p in range(n_pages)]
    return _compress_call(specs, [kv_t] * n_pages, (), (batch, 1), batch, n_pages * rows, n_pages * rows,
                          w1, w2k, w2v, pe, "compress_prompt")


def _compress_sample(cache_t, page_table, w1, w2k, w2v, pe):
    batch, n_pages = page_table.shape
    rows = PAGE_SIZE // CMP_STRIDE
    specs = [pl.BlockSpec(_PAGE_BLOCK, lambda b, s, pt, k=k: (pt[b, s * SAMPLE_PAGES + k], 0, 0, 0, 0))
             for k in range(SAMPLE_PAGES)]
    return _compress_call(specs, [cache_t] * SAMPLE_PAGES, (page_table,), (batch, n_pages // SAMPLE_PAGES), batch,
                          n_pages * rows, SAMPLE_PAGES * rows, w1, w2k, w2v, pe, "compress_sample")


def _tile4(x):
    return jnp.concatenate([x] * GQ, axis=1)


def _softmax_fold(state, pieces, vt):
    m, l, acc = state
    m_new = m
    for pc in pieces:
        m_new = jnp.maximum(m_new, jnp.max(pc, axis=0, keepdims=True))
    alpha = jnp.exp(m - m_new)
    es = [jnp.exp(pc - m_new) for pc in pieces]
    l = alpha * l
    for e in es:
        l = l + jnp.sum(e, axis=0, keepdims=True)
    e_all = es[0] if len(es) == 1 else jnp.concatenate(es, axis=0)
    acc = alpha * acc + _mm(vt, e_all.astype(BF16))
    return m_new, l, acc


def _attn_prompt_kernel(q_ref, gate_ref, ck_ref, cvt_ref, ks_ref, sv_ref, kw_ref, wv_ref,
                        near_ref, up_ref, tabc_ref, sp_ref, gn_ref, o_ref, *, n_slc):
    qi = pl.program_id(1)
    q0 = pl.multiple_of(qi * TQ, TQ)
    krow = lax.broadcasted_iota(jnp.int32, (LANE, TQ), 0)
    qcol = lax.broadcasted_iota(jnp.int32, (LANE, TQ), 1)
    cmp_ok4 = _tile4((krow >= 1) & (CMP_STRIDE * krow + (CMP_STRIDE - 1) <= q0 + qcol))
    sidx = lax.broadcasted_iota(jnp.int32, (n_slc, TQ), 0)
    qblk = lax.shift_right_logical(q0 + lax.broadcasted_iota(jnp.int32, (n_slc, TQ), 1), 6)
    sel_valid = sidx <= qblk
    sel_forced = (sidx == 0) | (sidx == qblk) | (sidx == qblk - 1)
    n_far = lax.shift_right_logical(jnp.maximum(qi - 1, 0), 2)
    up4 = _tile4(up_ref[...])
    cols = GQ * TQ
    fresh = (jnp.full((1, cols), NEG, F32), jnp.zeros((1, cols), F32), jnp.zeros((LANE, cols), F32))

    groups = range(N_KV)
    qa, near, o_cmp = [], [], []
    for g in groups:
        heads = [GQ * g + r for r in range(GQ)]
        q_t = [q_ref[h] for h in heads]
        near.append(jnp.concatenate([near_ref[h] for h in heads], axis=1))

        bias_c = jnp.concatenate([tabc_ref[0, h] for h in heads], axis=1)
        s = jnp.where(cmp_ok4, _mm(ck_ref[0, g], jnp.concatenate(q_t, axis=1)) + bias_c, NEG)
        e = jnp.exp(s - jnp.max(s, axis=0, keepdims=True))
        p = (e / jnp.sum(e, axis=0, keepdims=True) * cmp_ok4.astype(F32)).astype(BF16)
        o_cmp.append(_mm(cvt_ref[0], p))
        imp4 = _mm(sp_ref[...], p)
        imp = imp4[:, 0:TQ] + imp4[:, TQ:2 * TQ] + imp4[:, 2 * TQ:3 * TQ] + imp4[:, 3 * TQ:4 * TQ]

        score = jnp.where(sel_valid, imp[0:n_slc, :] + jnp.where(sel_forced, FORCE, 0.0), -FORCE)
        rank = jnp.zeros((n_slc, TQ), F32)
        for k in range(1, n_slc):
            other = pltpu.roll(score, k, axis=0)
            rank = rank + jnp.where(sidx >= k, (other >= score).astype(F32), (other > score).astype(F32))
        sel_t = (rank < float(N_SELECT)).astype(F32)
        aug = jnp.concatenate([jnp.zeros((SEL_LANE0, TQ), F32), sel_t - 1.0,
                               jnp.zeros((LANE - SEL_LANE0 - n_slc, TQ), F32)], axis=0)
        aug = jnp.where(krow == PAD_LANE, -1.0, aug).astype(BF16)
        qa.append(jnp.concatenate([qh + aug for qh in q_t], axis=1))

    def far_tile(it, states):
        start = pl.multiple_of(q0 - FAR * (n_far - it), LANE)
        vt = sv_ref[0, :, pl.ds(start, FAR)]
        return tuple(_softmax_fold(states[g], [_mm(ks_ref[g, 0, pl.ds(start, FAR), :], qa[g])], vt) for g in groups)

    states = lax.fori_loop(0, n_far, far_tile, (fresh,) * N_KV)
    v_tail = sv_ref[0, :, pl.ds(q0, TAIL)]
    w_tail = wv_ref[0, :, pl.ds(q0, TAIL)]
    o_slc, o_win = [], []
    for g in groups:
        sc = _mm(ks_ref[g, 0, pl.ds(q0, TAIL), :], qa[g])
        pieces = [sc[0:TAIL - 2 * LANE], sc[TAIL - 2 * LANE:TAIL] + near[g]]
        _, l, acc = _softmax_fold(states[g], pieces, v_tail)
        o_slc.append(acc / l)
        sc = _mm(kw_ref[g, 0, pl.ds(q0, TAIL), :], qa[g])
        pieces = [sc[0:LANE] + up4, sc[LANE:TAIL - 2 * LANE], sc[TAIL - 2 * LANE:TAIL] + near[g]]
        _, l, acc = _softmax_fold(fresh, pieces, w_tail)
        o_win.append(acc / l)

    outs = []
    for g in groups:
        gt = gate_ref[g]
        own = lax.shift_right_logical(krow, 6) == g
        for r in range(GQ):
            cs = slice(r * TQ, (r + 1) * TQ)
            comb = gt[3 * r:3 * r + 1, :] * o_cmp[g][:, cs] + gt[3 * r + 1:3 * r + 2, :] * o_slc[g][:, cs] \
                + gt[3 * r + 2:3 * r + 3, :] * o_win[g][:, cs]
            outs.append(jnp.where(own, comb, 0.0))
    o = jnp.concatenate(outs, axis=0)
    ms = jnp.sum(o * o, axis=0, keepdims=True) / (N_HEADS * HEAD_DIM)
    o = o * lax.rsqrt(ms + 1e-6) * gn_ref[...]
    o_ref[...] = jnp.concatenate([o[h * LANE:(h + 1) * LANE, :].T for h in range(N_HEADS)], axis=1).astype(BF16)


def _attn_prompt(q, gates, cmpk, cmpv_t, ks, sv_t, kw, wv_t, near_t, up_t, tab_c, sp_t, gn_rep, batch, seq):
    nq = seq // TQ
    t = batch * seq
    rows = ks.shape[2]
    n_slc = seq // SLC_LEN
    n_cmp = seq // CMP_STRIDE
    assert SEL_LANE0 + n_slc <= PAD_LANE < LANE
    return pl.pallas_call(
        functools.partial(_attn_prompt_kernel, n_slc=n_slc),
        grid=(batch, nq),
        in_specs=[
            pl.BlockSpec((N_HEADS, LANE, TQ), lambda b, i: (0, 0, b * nq + i)),
            pl.BlockSpec((N_KV, LANE, TQ), lambda b, i: (0, 0, b * nq + i)),
            pl.BlockSpec((1, N_KV, n_cmp, LANE), lambda b, i: (b, 0, 0, 0)),
            pl.BlockSpec((1, LANE, n_cmp), lambda b, i: (b, 0, 0)),
            pl.BlockSpec((N_KV, 1, rows, LANE), lambda b, i: (0, b, 0, 0)),
            pl.BlockSpec((1, LANE, rows), lambda b, i: (b, 0, 0)),
            pl.BlockSpec((N_KV, 1, rows, LANE), lambda b, i: (0, b, 0, 0)),
            pl.BlockSpec((1, LANE, rows), lambda b, i: (b, 0, 0)),
            _const_spec(near_t.shape),
            _const_spec(up_t.shape),
            pl.BlockSpec((1, N_HEADS, LANE, LANE), lambda b, i: (i, 0, 0, 0)),
            _const_spec(sp_t.shape),
            _const_spec(gn_rep.shape),
        ],
        out_specs=pl.BlockSpec((TQ, N_HEADS * LANE), lambda b, i: (b * nq + i, 0)),
        out_shape=jax.ShapeDtypeStruct((t, N_HEADS * LANE), BF16),
        compiler_params=_cparams(2),
        name="attn_prompt",
    )(q, gates, cmpk, cmpv_t, ks, sv_t, kw, wv_t, near_t, up_t, tab_c, sp_t, gn_rep)


def _sample_cw_kernel(qlo_ref, qgl_ref, ck_ref, cvt_ref, cwin_ref, wnew_ref, tabc_ref, tabw_ref, rb0_ref, sp_ref,
                      ocmp_ref, owin_ref, idx_ref, **static):
    for bb in range(qlo_ref.shape[0]):
        one = lambda r, bb=bb: r.at[pl.ds(bb, 1)]
        _sample_cw_one(one(qlo_ref), one(qgl_ref), one(ck_ref), one(cvt_ref), one(cwin_ref), one(wnew_ref),
                       tabc_ref, tabw_ref, rb0_ref, sp_ref, one(ocmp_ref), one(owin_ref), one(idx_ref), **static)


def _sample_cw_one(qlo_ref, qgl_ref, ck_ref, cvt_ref, cwin_ref, wnew_ref, tabc_ref, tabw_ref, rb0_ref, sp_ref,
                   ocmp_ref, owin_ref, idx_ref, *, n_slc, qblk, win_buf):
    qlo = qlo_ref[0]
    qgl = qgl_ref[0]
    n_cmp_rows = ck_ref.shape[2]
    col = lax.broadcasted_iota(jnp.int32, (N_HEADS, n_cmp_rows), 1)
    grp_of_head = lax.shift_right_logical(lax.broadcasted_iota(jnp.int32, (N_HEADS, n_cmp_rows), 0), 2)
    ok = col >= 1
    raw = jnp.where(grp_of_head == 0, _qk(qlo, ck_ref[0, 0]), _qk(qlo, ck_ref[0, 1]))
    s = jnp.where(ok, raw + tabc_ref[...], NEG)
    e = jnp.exp(s - jnp.max(s, axis=1, keepdims=True))
    p = (e / jnp.sum(e, axis=1, keepdims=True) * ok.astype(F32)).astype(BF16)
    ocmp_ref[0] = _qk(p, cvt_ref[0])
    imp8 = _mm(p, sp_ref[...])

    width = imp8.shape[1]
    head_grp = lax.shift_right_logical(lax.broadcasted_iota(jnp.int32, (N_HEADS, width), 0), 2)
    sidx = lax.broadcasted_iota(jnp.int32, (N_HEADS, width), 1)
    sidx_f = sidx.astype(F32)
    forced = (sidx == 0) | (sidx == qblk) | (sidx == qblk - 1)
    lane = lax.broadcasted_iota(jnp.int32, (N_HEADS, LANE), 1)
    imp_g = [jnp.sum(jnp.where(head_grp == g, imp8, 0.0), axis=0, keepdims=True) for g in range(N_KV)]
    imp = jnp.where(head_grp == 0, imp_g[0], imp_g[1])
    score = jnp.where(sidx <= qblk, imp + jnp.where(forced, FORCE, 0.0), -FORCE)
    score = jnp.where(sidx < n_slc, score, -jnp.inf)
    picked = jnp.zeros((N_HEADS, LANE), F32)
    for k in range(N_SELECT):
        best = jnp.max(score, axis=1, keepdims=True)
        ik = jnp.min(jnp.where(score == best, sidx_f, float(width)), axis=1, keepdims=True)
        picked = jnp.where(lane == k, ik, picked)
        score = jnp.where(sidx_f == ik, -jnp.inf, score)
    for g in range(N_KV):
        idx_ref[0, g:g + 1, :] = picked[GQ * g:GQ * g + 1, :].astype(jnp.int32)

    kt = cwin_ref[0, 0].reshape(N_KV * HEAD_DIM, win_buf).astype(BF16)
    vt = cwin_ref[0, 1].reshape(N_KV * HEAD_DIM, win_buf).astype(BF16)
    knew = wnew_ref[0, :, 0:LANE].astype(BF16).astype(F32)
    vnew = wnew_ref[0, :, LANE:2 * LANE].astype(BF16).astype(F32)
    colw = lax.broadcasted_iota(jnp.int32, (N_HEADS, win_buf), 1)
    okw = colw > win_buf - WINDOW
    sw = jnp.where(okw, _mm(qgl, kt) + tabw_ref[...], NEG)
    s_new = jnp.sum(qgl.astype(F32) * knew, axis=1, keepdims=True) + rb0_ref[:, 0:1]
    mw = jnp.maximum(jnp.max(sw, axis=1, keepdims=True), s_new)
    ew = jnp.where(okw, jnp.exp(sw - mw), 0.0)
    e_new = jnp.exp(s_new - mw)
    lw = jnp.sum(ew, axis=1, keepdims=True) + e_new
    owin_ref[0] = (_qk(ew.astype(BF16), vt) + e_new.astype(BF16).astype(F32) * vnew) / lw


def _sample_cw(qlo, qgl, cmpk, cmpv_t, cwin, wnew, tab_c, tab_w, rb0, sp, n_slc, qblk):
    n = qlo.shape[0]
    win_buf = cwin.shape[-1]
    blk = lambda a: pl.BlockSpec((SAMPLE_BB,) + a.shape[1:], lambda b: (b,) + (0,) * (a.ndim - 1))
    outs = (jax.ShapeDtypeStruct((n, N_HEADS, LANE), F32), jax.ShapeDtypeStruct((n, N_HEADS, LANE), F32),
            jax.ShapeDtypeStruct((n, N_KV, LANE), jnp.int32))
    return pl.pallas_call(
        functools.partial(_sample_cw_kernel, n_slc=n_slc, qblk=qblk, win_buf=win_buf),
        grid=(n // SAMPLE_BB,),
        in_specs=[blk(qlo), blk(qgl), blk(cmpk), blk(cmpv_t), blk(cwin), blk(wnew), _const_spec(tab_c.shape),
                  _const_spec(tab_w.shape), _const_spec(rb0.shape), _const_spec(sp.shape)],
        out_specs=tuple(pl.BlockSpec((SAMPLE_BB,) + o.shape[1:], lambda b: (b, 0, 0)) for o in outs),
        out_shape=outs,
        compiler_params=_cparams(1),
        name="sample_cmp_win",
    )(qlo, qgl, cmpk, cmpv_t, cwin, wnew, tab_c, tab_w, rb0, sp)


def _sample_slc_kernel(pg_ref, hf_ref, bid_ref, *refs, new_block, near_block):
    n_blk = N_KV * N_SELECT
    blocks = refs[:n_blk]
    (q_ref, kvnew_ref, ocmp_ref, owin_ref, gate_ref, tnear_ref, tlast_ref, rb0_ref, gn_ref, o_ref) = refs[n_blk:]
    b = pl.program_id(0)
    n_keys = N_SELECT * PAGE_SIZE
    lane = lax.broadcasted_iota(jnp.int32, (1, n_keys), 1)
    slot = lax.shift_right_logical(lane, 7)
    half = lax.shift_right_logical(lane, 6) & 1
    head_grp = lax.shift_right_logical(lax.broadcasted_iota(jnp.int32, (N_HEADS, LANE), 0), 2)
    lane_grp = lax.shift_right_logical(lax.broadcasted_iota(jnp.int32, (N_HEADS, LANE), 1), 6)
    qs = q_ref[0][:, 0:HEAD_DIM]
    fill = jnp.zeros((N_HEADS, LANE - HEAD_DIM), F32)
    o_slc = jnp.zeros((N_HEADS, LANE), F32)
    for g in range(N_KV):
        knew = kvnew_ref[0, :, 256 + g * HEAD_DIM:256 + (g + 1) * HEAD_DIM].astype(BF16).astype(F32)
        vnew = kvnew_ref[0, :, 384 + g * HEAD_DIM:384 + (g + 1) * HEAD_DIM].astype(BF16).astype(F32)
        mine = blocks[g * N_SELECT:(g + 1) * N_SELECT]
        kt = jnp.concatenate([blk[0, 0, 0] for blk in mine], axis=1).astype(BF16)
        vt = jnp.concatenate([blk[0, 1, 0] for blk in mine], axis=1).astype(BF16)
        bid = jnp.zeros((1, n_keys), jnp.int32)
        hsel = jnp.zeros((1, n_keys), jnp.int32)
        has_new = bid_ref[b, g * N_SELECT] == new_block
        for kk in range(N_SELECT):
            bid = jnp.where(slot == kk, bid_ref[b, g * N_SELECT + kk], bid)
            hsel = jnp.where(slot == kk, hf_ref[b, g * N_SELECT + kk], hsel)
            if kk:
                has_new = has_new | (bid_ref[b, g * N_SELECT + kk] == new_block)
        bias = jnp.where(bid == near_block + 1, tlast_ref[...], jnp.where(bid == near_block, tnear_ref[...], 0.0))
        ok = (bid != new_block) & (half == hsel)
        s = jnp.where(ok, _mm(qs, kt) + bias, NEG)
        s_new = jnp.sum(qs.astype(F32) * knew, axis=1, keepdims=True) + rb0_ref[:, 0:1]
        s_new = jnp.where(has_new, s_new, NEG)
        m = jnp.maximum(jnp.max(s, axis=1, keepdims=True), s_new)
        e = jnp.where(ok, jnp.exp(s - m), 0.0)
        e_new = jnp.where(has_new, jnp.exp(s_new - m), 0.0)
        l = jnp.sum(e, axis=1, keepdims=True) + e_new
        og = (_qk(e.astype(BF16), vt) + e_new.astype(BF16).astype(F32) * vnew) / l
        og = jnp.concatenate([og, fill] if g == 0 else [fill, og], axis=1)
        o_slc = jnp.where(head_grp == g, og, o_slc)
    gt = gate_ref[0]
    comb = gt[:, 0:1] * ocmp_ref[0] + gt[:, 1:2] * o_slc + gt[:, 2:3] * owin_ref[0]
    comb = jnp.where(lane_grp == head_grp, comb, 0.0)
    o = jnp.concatenate([comb[h:h + 1, :] for h in range(N_HEADS)], axis=1)
    o_ref[0] = _rms(o, gn_ref[...], N_HEADS * HEAD_DIM).astype(BF16)


def _sample_slc(pages, halves, bids, cache_t, q, kvnew, ocmp, owin, gates, t_near, t_last, rb0, gn_pad,
                new_block, near_block):
    n = q.shape[0]
    n_blk = N_KV * N_SELECT
    blk_specs = [pl.BlockSpec((1, 2, 1, HEAD_DIM, PAGE_SIZE),
                              lambda b, pg, hf, bi, k=k: (pg[b, k], 1, k // N_SELECT, 0, 0))
                 for k in range(n_blk)]
    per_b = lambda a: pl.BlockSpec((1,) + a.shape[1:], lambda b, *_: (b,) + (0,) * (a.ndim - 1))
    cst = lambda a: pl.BlockSpec(a.shape, lambda *_: (0,) * a.ndim)
    grid_spec = pltpu.PrefetchScalarGridSpec(
        num_scalar_prefetch=3,
        grid=(n,),
        in_specs=blk_specs + [per_b(q), per_b(kvnew), per_b(ocmp), per_b(owin), per_b(gates),
                              cst(t_near), cst(t_last), cst(rb0), cst(gn_pad)],
        out_specs=pl.BlockSpec((1, 1, N_HEADS * LANE), lambda b, *_: (b, 0, 0)),
    )
    return pl.pallas_call(
        functools.partial(_sample_slc_kernel, new_block=new_block, near_block=near_block),
        grid_spec=grid_spec,
        out_shape=jax.ShapeDtypeStruct((n, 1, N_HEADS * LANE), BF16),
        compiler_params=_cparams(1),
        name="sample_selected",
    )(pages, halves, bids, *([cache_t] * n_blk), q, kvnew, ocmp, owin, gates, t_near, t_last, rb0, gn_pad)


def _proj_kernel(on_ref, zn_ref, x_ref, wa_ref, wb_ref, g1_ref, b1_ref, wr_ref, br_ref,
                 x1_ref, gate_ref, *, alpha):
    mix = _mm(on_ref[...], wa_ref[...]) + _mm(zn_ref[...], wb_ref[...])
    x1 = _layernorm(alpha * x_ref[...] + mix, g1_ref[...], b1_ref[...])
    x1_ref[...] = x1
    logits = _mm(x1.astype(BF16), wr_ref[...]) + br_ref[...]
    tm = logits.shape[0]
    lane = lax.broadcasted_iota(jnp.int32, (tm, LANE), 1).astype(F32)

    def first_argmax(v, vmax):
        return jnp.min(jnp.where(v == vmax, lane, float(LANE)), axis=1, keepdims=True)

    lg = jnp.where(lane < N_GROUPS, logits[:, 0:LANE], -jnp.inf)
    lg_max = jnp.max(lg, axis=1, keepdims=True)
    eg = jnp.exp(lg - lg_max)
    pg = eg / jnp.sum(eg, axis=1, keepdims=True)
    gidx = first_argmax(lg, lg_max)
    pg_sel = jnp.sum(jnp.where(lane == gidx, pg, 0.0), axis=1, keepdims=True)
    le = jnp.zeros((tm, LANE), F32)
    for gi in range(N_GROUPS):
        le = le + jnp.where(gidx == gi, logits[:, (gi + 1) * LANE:(gi + 2) * LANE], 0.0)
    le = jnp.where(lane < N_EXP, le, -jnp.inf)
    ee = jnp.exp(le - jnp.max(le, axis=1, keepdims=True))
    pe = jnp.where(lane < N_EXP, ee / jnp.sum(ee, axis=1, keepdims=True), -1.0)
    v1 = jnp.max(pe, axis=1, keepdims=True)
    i1 = first_argmax(pe, v1)
    pe2 = jnp.where(lane == i1, -1.0, pe)
    v2 = jnp.max(pe2, axis=1, keepdims=True)
    i2 = first_argmax(pe2, v2)
    tot = v1 + v2
    gate_e = jnp.where(lane == i1, v1 / tot * pg_sel, jnp.where(lane == i2, v2 / tot * pg_sel, 0.0))
    gate_ref[...] = jnp.where(lane == GROUP_LANE, gidx, gate_e)


def _proj(on, zn, x, wa, wb, g1, b1, wr, br, alpha, tm):
    t, d = x.shape
    row = lambda n: pl.BlockSpec((tm, n), lambda i: (i, 0))
    outs = (jax.ShapeDtypeStruct((t, d), F32), jax.ShapeDtypeStruct((t, LANE), F32))
    return pl.pallas_call(
        functools.partial(_proj_kernel, alpha=alpha),
        grid=(t // tm,),
        in_specs=[row(on.shape[1]), row(zn.shape[1]), row(d)] + [_const_spec(a.shape) for a in (wa, wb, g1, b1, wr, br)],
        out_specs=(row(d), row(LANE)),
        out_shape=outs,
        compiler_params=_cparams(1),
        name=f"proj_ln_router_{tm}",
    )(on, zn, x, wa, wb, g1, b1, wr, br)


def _moe_kernel(x1_ref, gate_ref, wg_ref, wu_ref, wd_ref, g2_ref, b2_ref, out_ref, acc_ref, *, alpha):
    gi = pl.program_id(1)

    @pl.when(gi == 0)
    def _():
        acc_ref[...] = jnp.zeros(acc_ref.shape, F32)

    x = x1_ref[...].astype(BF16)
    gate = gate_ref[...]
    gt = jnp.where(gate[:, GROUP_LANE:GROUP_LANE + 1] == gi.astype(F32), gate, 0.0)
    acc = acc_ref[...]
    n_half = wg_ref.shape[1]
    for e in range(N_EXP):
        part, k = divmod(e, n_half)
        h = jax.nn.silu(_mm(x, wg_ref[part, k])) * _mm(x, wu_ref[part, k])
        acc = acc + _mm((h * gt[:, e:e + 1]).astype(BF16), wd_ref[part, k])
    acc_ref[...] = acc

    @pl.when(gi == N_GROUPS - 1)
    def _():
        out_ref[...] = _layernorm(alpha * x1_ref[...] + acc_ref[...], g2_ref[...], b2_ref[...])


def _moe(x1, gate, wg, wu, wd, g2, b2, alpha, tm):
    t, d = x1.shape
    d_ff = wd.shape[2]
    n_half = wg.shape[1]
    return pl.pallas_call(
        functools.partial(_moe_kernel, alpha=alpha),
        grid=(t // tm, N_GROUPS),
        in_specs=[
            pl.BlockSpec((tm, d), lambda i, g: (i, 0)),
            pl.BlockSpec((tm, LANE), lambda i, g: (i, 0)),
            pl.BlockSpec((EXPERT_SPLIT, n_half, d, d_ff), lambda i, g: (g, 0, 0, 0)),
            pl.BlockSpec((EXPERT_SPLIT, n_half, d, d_ff), lambda i, g: (g, 0, 0, 0)),
            pl.BlockSpec((EXPERT_SPLIT, n_half, d_ff, d), lambda i, g: (g, 0, 0, 0)),
            _const_spec(g2.shape), _const_spec(b2.shape),
        ],
        out_specs=pl.BlockSpec((tm, d), lambda i, g: (i, 0)),
        out_shape=jax.ShapeDtypeStruct((t, d), F32),
        scratch_shapes=[pltpu.VMEM((tm, d), F32)],
        compiler_params=_cparams(2),
        name=f"moe_ln_{tm}",
    )(x1, gate, wg, wu, wd, g2, b2)


def _split_bf16(x):
    hi = x.astype(BF16)
    return hi, (x - hi.astype(F32)).astype(BF16)


def _moe_sorted_kernel(cnt_ref, off_ref, x1_ref, gate_ref, drow_ref, dcol_ref, wg_ref, wu_ref, wd_ref, g2_ref, b2_ref,
                       out_ref, xs_ref, gs_ref, ys_ref, *, alpha):
    i = pl.program_id(0)
    gi = pl.program_id(1)
    half = pl.program_id(2)
    n_half = N_EXP // EXPERT_SPLIT
    tmoe = x1_ref.shape[0]

    @pl.when((i == 0) & (gi == 0) & (half == 0))
    def _():
        xs_ref[...] = jnp.zeros(xs_ref.shape, BF16)
        gs_ref[...] = jnp.zeros(gs_ref.shape, F32)
        ys_ref[...] = jnp.zeros(ys_ref.shape, F32)

    @pl.when((gi == 0) & (half == 0))
    def _():
        perm = (lax.broadcasted_iota(jnp.int32, (PERM_ROWS, tmoe), 0) == drow_ref[0]).astype(BF16)
        xs_ref[0:PERM_ROWS, :] = _mm(perm, x1_ref[...].astype(BF16)).astype(BF16)
        g_hi, g_lo = _split_bf16(gate_ref[...])
        gs_ref[0:PERM_ROWS, :] = _mm(perm, g_hi) + _mm(perm, g_lo)

    n = cnt_ref[i * N_GROUPS + gi]
    off = off_ref[i * N_GROUPS + gi]

    def run_pass(c, carry):
        base = pl.multiple_of(off + c * MOE_CHUNK, SEG_ALIGN)
        xc = xs_ref[pl.ds(base, MOE_CHUNK), :]
        gc = gs_ref[pl.ds(base, MOE_CHUNK), :]
        gc = jnp.where(half == 0, gc, pltpu.roll(gc, LANE - n_half, axis=1))
        y = jnp.zeros((MOE_CHUNK, out_ref.shape[1]), F32)
        for e in range(n_half):
            h = jax.nn.silu(_mm(xc, wg_ref[0, e])) * _mm(xc, wu_ref[0, e])
            y = y + _mm((h * gc[:, e:e + 1]).astype(BF16), wd_ref[0, e])

        @pl.when(half == 0)
        def _():
            ys_ref[pl.ds(base, MOE_CHUNK), :] = y

        @pl.when(half != 0)
        def _():
            ys_ref[pl.ds(base, MOE_CHUNK), :] = ys_ref[pl.ds(base, MOE_CHUNK), :] + y

        return carry

    lax.fori_loop(0, (n + MOE_CHUNK - 1) // MOE_CHUNK, run_pass, 0)

    @pl.when((gi == N_GROUPS - 1) & (half == EXPERT_SPLIT - 1))
    def _():
        unperm = (dcol_ref[...] == lax.broadcasted_iota(jnp.int32, (tmoe, PERM_ROWS), 1)).astype(BF16)
        y_hi, y_lo = _split_bf16(ys_ref[0:PERM_ROWS, :])
        moe = _mm(unperm, y_hi) + _mm(unperm, y_lo)
        out_ref[...] = _layernorm(alpha * x1_ref[...] + moe, g2_ref[...], b2_ref[...])


def _moe_sorted(x1, gate, wg_h, wu_h, wd_h, g2, b2, alpha):
    t, d = x1.shape
    d_ff = wd_h.shape[2]
    n_tiles = t // TMOE
    n_half = N_EXP // EXPERT_SPLIT
    gid = gate[:, GROUP_LANE].astype(jnp.int32).reshape(n_tiles, TMOE)
    onehot = (gid[:, :, None] == jnp.arange(N_GROUPS)[None, None, :]).astype(jnp.int32)
    cnt = jnp.sum(onehot, axis=1)
    rank = jnp.sum((jnp.cumsum(onehot, axis=1) - onehot) * onehot, axis=2)
    seg = (cnt + SEG_ALIGN - 1) // SEG_ALIGN * SEG_ALIGN
    off = jnp.cumsum(seg, axis=1) - seg
    dest = jnp.sum(onehot * off[:, None, :], axis=2) + rank
    assert N_GROUPS * (SEG_ALIGN - 1) <= PERM_ROWS - TMOE
    grid_spec = pltpu.PrefetchScalarGridSpec(
        num_scalar_prefetch=2,
        grid=(n_tiles, N_GROUPS, EXPERT_SPLIT),
        in_specs=[
            pl.BlockSpec((TMOE, d), lambda i, g, h, *_: (i, 0)),
            pl.BlockSpec((TMOE, LANE), lambda i, g, h, *_: (i, 0)),
            pl.BlockSpec((1, 1, TMOE), lambda i, g, h, *_: (i, 0, 0)),
            pl.BlockSpec((TMOE, 1), lambda i, g, h, *_: (i, 0)),
            pl.BlockSpec((1, n_half, d, d_ff), lambda i, g, h, *_: (g * EXPERT_SPLIT + h, 0, 0, 0)),
            pl.BlockSpec((1, n_half, d, d_ff), lambda i, g, h, *_: (g * EXPERT_SPLIT + h, 0, 0, 0)),
            pl.BlockSpec((1, n_half, d_ff, d), lambda i, g, h, *_: (g * EXPERT_SPLIT + h, 0, 0, 0)),
            pl.BlockSpec(g2.shape, lambda *_: (0, 0)), pl.BlockSpec(b2.shape, lambda *_: (0, 0)),
        ],
        out_specs=pl.BlockSpec((TMOE, d), lambda i, g, h, *_: (i, 0)),
        scratch_shapes=[pltpu.VMEM((SORT_ROWS, d), BF16), pltpu.VMEM((SORT_ROWS, LANE), F32),
                        pltpu.VMEM((SORT_ROWS, d), F32)],
    )
    return pl.pallas_call(
        functools.partial(_moe_sorted_kernel, alpha=alpha),
        grid_spec=grid_spec,
        out_shape=jax.ShapeDtypeStruct((t, d), F32),
        compiler_params=_cparams(3),
        name="moe_sorted_ln",
    )(cnt.reshape(-1), off.reshape(-1), x1, gate, dest.reshape(n_tiles, 1, TMOE), dest.reshape(t, 1), wg_h, wu_h, wd_h,
      g2, b2)


def _pad_head_lanes(v):
    vh = v.reshape(N_HEADS, HEAD_DIM)
    z = jnp.zeros_like(vh)
    grp = (jnp.arange(N_HEADS) // GQ)[:, None]
    return jnp.where(grp == 0, jnp.concatenate([vh, z], axis=1), jnp.concatenate([z, vh], axis=1)).reshape(1, -1)


def _pad_head_rows(w):
    wh = w.reshape(N_HEADS, HEAD_DIM, -1)
    z = jnp.zeros_like(wh)
    grp = (jnp.arange(N_HEADS) // GQ)[:, None, None]
    return jnp.where(grp == 0, jnp.concatenate([wh, z], axis=1), jnp.concatenate([z, wh], axis=1)).reshape(
        N_HEADS * LANE, -1)


def kernel(x_prompt, x_sample, cache_kv, page_table, cache_win, state_conv, w_in, conv_w, pe_k, pe_v, w_ck1, w_ck2, w_cv1, w_cv2, g_nsa, g_conv, w_out, ln1_g, ln1_b, w_rg, b_rg, w_re, b_re, w_eg, w_eu, w_ed, ln2_g, ln2_b, rel_bias):
    batch, seq, d_model = x_prompt.shape
    dec_batch, dec_seq = x_sample.shape[0], x_sample.shape[1]
    depth = w_in.shape[0]
    n_pages = page_table.shape[1]
    past = n_pages * PAGE_SIZE
    win_buf = cache_win.shape[2]
    d_nsa = N_HEADS * HEAD_DIM
    d_conv = w_out.shape[1] - d_nsa
    assert dec_seq == 1 and seq % TM == 0 and min(WINDOW, seq) == TM and win_buf == WINDOW
    assert d_conv == 512 and seq // CMP_STRIDE == LANE and n_pages % SAMPLE_PAGES == 0
    alpha = (2 * depth) ** 0.25
    n_slc_s = -(-(past + dec_seq) // SLC_LEN)
    qblk_s = past // SLC_LEN

    idx_np, sp_p_np, sp_s_np = _static_tables(seq, past, win_buf)
    tabs = _bias_tables(rel_bias, jnp.asarray(idx_np))
    nq = seq // TQ
    near_t = jnp.transpose(tabs[0:2], (1, 0, 2, 3)).reshape(N_HEADS, 2 * LANE, LANE)
    up_t = tabs[2, 0]
    tab_c = tabs[_N_NEAR_TILES:_N_NEAR_TILES + nq]
    smp = tabs[_N_NEAR_TILES + nq]
    tab_w_s = smp[:, 0:5, :].reshape(N_HEADS, 5 * LANE)[:, :win_buf]
    tab_c_s = smp[:, 5:13, :].reshape(N_HEADS, 8 * LANE)
    reps = N_SELECT * PAGE_SIZE // SLC_LEN
    t_near = jnp.tile(smp[:, 13, 0:SLC_LEN], (1, reps))
    t_last = jnp.tile(smp[:, 13, SLC_LEN:2 * SLC_LEN], (1, reps))
    rb0_rep = jnp.broadcast_to(smp[:, 14, 0:1], (N_HEADS, LANE))
    sp_p = jnp.asarray(sp_p_np, BF16)
    sp_s = jnp.asarray(sp_s_np, BF16)

    xp = x_prompt.reshape(batch * seq, d_model)
    xs = x_sample.reshape(dec_batch * dec_seq, d_model)
    outs = [[] for _ in range(6)]
    for l in range(depth):
        w_pack = _pack_in_weights(w_in[l])
        gc = g_conv[l].reshape(1, -1)
        gn_pad = _pad_head_lanes(g_nsa[l])
        w1k, w2k, pek = _pack_compress_weights(pe_k[l], w_ck1[l], w_ck2[l], True)
        w1v, w2v, pev = _pack_compress_weights(pe_v[l], w_cv1[l], w_cv2[l], False)
        w1 = jnp.stack([w1k, w1v])
        pe = jnp.stack([pek, pev])
        wa = _pad_head_rows(w_out[l][:d_nsa]).astype(BF16)
        wb = w_out[l][d_nsa:].astype(BF16)
        wr = jnp.concatenate(
            [jnp.pad(w_rg[l], ((0, 0), (0, LANE - N_GROUPS)))]
            + [jnp.pad(w_re[l][:, gi * N_EXP:(gi + 1) * N_EXP], ((0, 0), (0, LANE - N_EXP))) for gi in range(N_GROUPS)],
            axis=1).astype(BF16)
        br = jnp.concatenate(
            [jnp.pad(b_rg[l], (0, LANE - N_GROUPS))]
            + [jnp.pad(b_re[l][gi * N_EXP:(gi + 1) * N_EXP], (0, LANE - N_EXP)) for gi in range(N_GROUPS)]).reshape(1, -1)
        halves = lambda w: w.astype(BF16).reshape((N_GROUPS * EXPERT_SPLIT, N_EXP // EXPERT_SPLIT) + w.shape[2:])
        wg, wu, wd = halves(w_eg[l]), halves(w_eu[l]), halves(w_ed[l])
        g1, b1 = ln1_g[l].reshape(1, -1), ln1_b[l].reshape(1, -1)
        g2, b2 = ln2_g[l].reshape(1, -1), ln2_b[l].reshape(1, -1)

        q_p, kvt_p, ks, sv, kw, wv, wtail_t, gates_p, zn_p, ctail = _inproj_prompt(xp, w_pack, conv_w[l], gc, batch, seq)
        kvt_p = kvt_p.reshape(batch, 4, N_KV, HEAD_DIM, seq)
        cmpk_p, cmpvt_p = _compress_prompt(kvt_p, batch, seq, w1, w2k, w2v, pe)
        gn_rep = jnp.broadcast_to(gn_pad.reshape(-1, 1), (N_HEADS * LANE, TQ))
        on_p = _attn_prompt(q_p, gates_p, cmpk_p, cmpvt_p, ks, sv, kw, wv, near_t, up_t, tab_c, sp_p, gn_rep,
                            batch, seq)
        x1_p, gate_p = _proj(on_p, zn_p, xp, wa, wb, g1, b1, wr, br, alpha, TM)
        y_p = _moe_sorted(x1_p, gate_p, wg, wu, wd, g2, b2, alpha)

        st = state_conv[l]
        qlo_s, qgl_s, kv_s, win_s, gates_s, zn_s, u_s = _inproj_sample(xs, w_pack, conv_w[l], gc, st[:, 0], st[:, 1])
        cache_t = jnp.transpose(cache_kv[l], (0, 2, 3, 4, 1))
        cwin_t = jnp.transpose(cache_win[l], (0, 2, 3, 4, 1))
        cmpk_s, cmpvt_s = _compress_sample(cache_t, page_table, w1, w2k, w2v, pe)
        qlo_s3 = qlo_s.reshape(dec_batch, N_HEADS, LANE)
        qgl_s3 = qgl_s.reshape(dec_batch, N_HEADS, LANE)
        ocmp, owin, sel_idx = _sample_cw(qlo_s3, qgl_s3, cmpk_s, cmpvt_s, cwin_t, win_s.reshape(dec_batch, 1, 256), tab_c_s,
                                         tab_w_s, rb0_rep, sp_s, n_slc_s, qblk_s)
        bids = sel_idx[:, :, :N_SELECT].reshape(dec_batch, N_KV * N_SELECT)
        blk_pages = jnp.take_along_axis(page_table, jnp.minimum(bids // 2, n_pages - 1), axis=1)
        gates_s3 = jnp.pad(gates_s.reshape(dec_batch, N_KV, LANE)[:, :, :3 * GQ].reshape(dec_batch, N_HEADS, 3),
                           ((0, 0), (0, 0), (0, LANE - 3)))
        on_s = _sample_slc(blk_pages, bids % 2, bids, cache_t, qlo_s3, kv_s.reshape(dec_batch, 1, 512), ocmp, owin,
                           gates_s3, t_near, t_last, rb0_rep, gn_pad, n_slc_s - 1, qblk_s - 2)
        on_s = on_s.reshape(dec_batch, N_HEADS * LANE)
        x1_s, gate_s = _proj(on_s, zn_s, xs, wa, wb, g1, b1, wr, br, alpha, dec_batch)
        y_s = _moe(x1_s, gate_s, wg, wu, wd, g2, b2, alpha, dec_batch)

        to_token_major = lambda a: jnp.transpose(a, (0, 4, 1, 2, 3))
        outs[0].append(to_token_major(kvt_p))
        outs[1].append(kv_s.reshape(dec_batch, dec_seq, 4, N_KV, HEAD_DIM))
        outs[2].append(to_token_major(wtail_t.reshape(batch, 2, N_KV, HEAD_DIM, TM)))
        win_all_t = jnp.concatenate([cwin_t, win_s.reshape(dec_batch, 2, N_KV, HEAD_DIM, dec_seq)], axis=-1)
        outs[3].append(to_token_major(win_all_t[..., win_all_t.shape[-1] - min(WINDOW, past + dec_seq):]))
        outs[4].append(ctail)
        outs[5].append(jnp.concatenate([st, u_s[:, None, :]], axis=1)[:, dec_seq:])
        xp, xs = y_p, y_s
    return (xp.reshape(batch, seq, d_model), xs.reshape(dec_batch, dec_seq, d_model),
            jnp.stack(outs[0]), jnp.stack(outs[1]), jnp.stack(outs[2]), jnp.stack(outs[3]),
            jnp.stack(outs[4]), jnp.stack(outs[5]))
```

```python
import functools
import math

import numpy as np
import jax
import jax.numpy as jnp
from jax import lax
from jax.experimental import pallas as pl
from jax.experimental.pallas import tpu as pltpu

F32 = jnp.float32
BF16 = jnp.bfloat16

HEAD_DIM = 64
N_KV = 2
GQ = 4
N_HEADS = N_KV * GQ
CONV_W = 3
CMP_STRIDE = 16
CMP_LEN = 32
CMP_HIDDEN = 2 * HEAD_DIM
SLC_LEN = 64
N_SELECT = 16
WINDOW = 512
N_BUCKETS = 32
MAX_DISTANCE = 128
N_GROUPS = 4
N_EXP = 8
PAGE_SIZE = 128
NEG = -1e30
FORCE = 1e6

LANE = 128
SUBLANE = 8
VMEM_LIMIT = 52 * 1024 * 1024

TQ = 128
TM = 512
FAR = 512
TAIL = WINDOW + TQ
SAMPLE_PAGES = 64
SAMPLE_BB = 4
TMOE = 1024
MOE_CHUNK = 320
SEG_ALIGN = 16
PERM_ROWS = TMOE + LANE
SORT_ROWS = PERM_ROWS + MOE_CHUNK
EXPERT_SPLIT = 2
GROUP_LANE = N_EXP

KEY_MASK = 2.0 ** 100
TABLE_MASK = 2 * NEG
MASKED_BUCKET = N_BUCKETS
SEL_LANE0 = HEAD_DIM
PAD_LANE = HEAD_DIM + 32


def _cparams(n_axes):
    return pltpu.CompilerParams(dimension_semantics=("arbitrary",) * n_axes, vmem_limit_bytes=VMEM_LIMIT)


def _const_spec(shape):
    nd = len(shape)
    return pl.BlockSpec(shape, lambda *_, nd=nd: (0,) * nd)


def _qk(a, b):
    return lax.dot_general(a, b, (((1,), (1,)), ((), ())), preferred_element_type=F32)


def _mm(a, b):
    return jnp.dot(a, b, preferred_element_type=F32)


def _bucket_np(dist):
    n = np.maximum(dist, 0)
    max_exact = N_BUCKETS // 2
    nf = np.maximum(n, 1).astype(np.float32)
    large = max_exact + (np.log(nf / np.float32(max_exact)) / np.float32(math.log(MAX_DISTANCE / max_exact))
                         * np.float32(N_BUCKETS - max_exact)).astype(np.int32)
    large = np.minimum(large, N_BUCKETS - 1)
    return np.where(n < max_exact, n, large).astype(np.int32)


def _overlap_np(c, s):
    c0 = c * CMP_STRIDE
    s0 = s * SLC_LEN
    return np.maximum(np.minimum(c0 + CMP_LEN, s0 + SLC_LEN) - np.maximum(c0, s0), 0)


_N_NEAR_TILES = 3


def _static_tables(seq, past, win_buf):
    nq = seq // TQ
    i = np.arange(LANE)[:, None]
    j = np.arange(LANE)[None, :]
    far_bucket = N_BUCKETS - 1
    tiles = [_bucket_np(LANE + j - i),
             np.where(j >= i, _bucket_np(j - i), MASKED_BUCKET),
             np.where(i > j, far_bucket, MASKED_BUCKET) + 0 * i]
    tiles += [_bucket_np(TQ * q + j - (CMP_STRIDE * i + CMP_STRIDE - 1)) for q in range(nq)]
    smp = np.full((LANE, LANE), 10 * MAX_DISTANCE, np.int64)
    k = np.arange(5 * LANE)
    smp[0:5] = np.where(k < win_buf, win_buf - k, 0).reshape(5, LANE)
    ci = np.arange(8 * LANE)
    smp[5:13] = np.maximum(past - (CMP_STRIDE * ci + CMP_STRIDE - 1), 0).reshape(8, LANE)
    nb = past // SLC_LEN
    smp[13] = past - (SLC_LEN * (nb - 2) + np.arange(LANE))
    smp[14] = 0
    tiles.append(_bucket_np(smp))
    idx = np.stack(tiles).astype(np.int32)

    def sp(nrows, ncols, n_slc):
        r = np.arange(nrows)[:, None]
        s = np.arange(ncols)[None, :]
        ov = _overlap_np(r - 1, s)
        return np.where((r >= 1) & (s < n_slc), ov, 0).astype(np.float32)

    sp_p = sp(seq // CMP_STRIDE, LANE, -(-seq // SLC_LEN)).T
    n_slc_s = -(-(past + 1) // SLC_LEN)
    sp_s = sp(past // CMP_STRIDE, 3 * LANE, n_slc_s)
    return idx, sp_p, sp_s


def _bias_kernel(rb_ref, idx_ref, out_ref):
    idx = idx_ref[0]
    accs = [jnp.full((LANE, LANE), TABLE_MASK, F32) for _ in range(N_HEADS)]
    for b in range(N_BUCKETS):
        hit = idx == b
        for h in range(N_HEADS):
            accs[h] = jnp.where(hit, rb_ref[b, h] - rb_ref[N_BUCKETS - 1, h], accs[h])
    for h in range(N_HEADS):
        out_ref[0, h] = accs[h]


def _bias_tables(rel_bias, idx):
    nt = idx.shape[0]
    return pl.pallas_call(
        _bias_kernel,
        grid=(nt,),
        in_specs=[pl.BlockSpec(memory_space=pltpu.SMEM),
                  pl.BlockSpec((1, LANE, LANE), lambda t: (t, 0, 0))],
        out_specs=pl.BlockSpec((1, N_HEADS, LANE, LANE), lambda t: (t, 0, 0, 0)),
        out_shape=jax.ShapeDtypeStruct((nt, N_HEADS, LANE, LANE), F32),
        compiler_params=_cparams(1),
        name="bias_tables",
    )(rel_bias, idx)


_Q0, _Q1 = 0, N_HEADS * HEAD_DIM
_KV0, _KV1 = _Q1, _Q1 + 512
_WN0, _WN1 = _KV1, _KV1 + 256
_GT0, _GT1 = _WN1, _WN1 + 2 * LANE
_CV0, _CV1 = _GT1, _GT1 + 3 * 512


def _pack_in_weights(w):
    d = w.shape[0]
    d_nsa = N_HEADS * HEAD_DIM
    wq_pad = w[:, :d_nsa]
    o = d_nsa
    w_kv = w[:, o:o + 512]
    w_win = w[:, o + 512:o + 768]
    wg = w[:, o + 768:o + 768 + 3 * N_HEADS].reshape(d, N_KV, 3 * GQ)
    wg_pad = jnp.pad(wg, ((0, 0), (0, 0), (0, LANE - 3 * GQ))).reshape(d, N_KV * LANE)
    w_conv = w[:, o + 768 + 3 * N_HEADS:]
    return jnp.concatenate([wq_pad, w_kv, w_win, wg_pad, w_conv], axis=1).astype(BF16)


def _rms(x, gain, n, eps=1e-6):
    ms = jnp.sum(x * x, axis=-1, keepdims=True) / n
    return x * lax.rsqrt(ms + eps) * gain


def _layernorm(y, gain, bias, eps=1e-5):
    mu = jnp.mean(y, axis=-1, keepdims=True)
    d = y - mu
    var = jnp.mean(d * d, axis=-1, keepdims=True)
    return d * lax.rsqrt(var + eps) * gain + bias


def _inproj_prompt_kernel(x_ref, w_ref, cw_ref, gc_ref, ks_fill, kw_fill, sv_fill, wv_fill,
                          q_ref, kv_ref, ks_ref, sv_ref, kw_ref, wv_ref, wtail_ref, gate_ref, zn_ref, ctail_ref,
                          uext_ref, *, tiles_per_batch):
    del ks_fill, kw_fill, sv_fill, wv_fill
    i = pl.program_id(0)
    tm = x_ref.shape[0]
    xb = x_ref[...].astype(BF16)

    def seg(a, b):
        return _mm(xb, w_ref[:, a:b])

    qp_t = (seg(_Q0, _Q1) * (HEAD_DIM ** -0.5)).T
    q_fill = jnp.zeros((LANE - HEAD_DIM, tm), F32)
    for h in range(N_HEADS):
        q_ref[h] = jnp.concatenate([qp_t[h * HEAD_DIM:(h + 1) * HEAD_DIM, :], q_fill], axis=0).astype(BF16)
    kv = seg(_KV0, _KV1)
    kv_t = kv.T
    kv_ref[0] = kv_t
    win = seg(_WN0, _WN1)
    win_t = win.T
    wtail_ref[0] = win_t
    pos = (i % tiles_per_batch) * tm + lax.broadcasted_iota(jnp.int32, (tm, LANE - HEAD_DIM), 0)
    lane = lax.broadcasted_iota(jnp.int32, (tm, LANE - HEAD_DIM), 1)
    blk_flag = jnp.where(lane == lax.shift_right_logical(pos, 6), KEY_MASK, 0.0).astype(BF16)
    no_flag = jnp.zeros((tm, LANE - HEAD_DIM), BF16)
    for g in range(N_KV):
        ks_ref[g, 0] = jnp.concatenate([kv[:, 256 + g * HEAD_DIM:256 + (g + 1) * HEAD_DIM].astype(BF16), blk_flag], axis=1)
        kw_ref[g, 0] = jnp.concatenate([win[:, g * HEAD_DIM:(g + 1) * HEAD_DIM].astype(BF16), no_flag], axis=1)
    sv_ref[0] = kv_t[384:512, :].astype(BF16)
    wv_ref[0] = win_t[LANE:2 * LANE, :].astype(BF16)
    gt_t = jax.nn.sigmoid(seg(_GT0, _GT1)).T
    gate_ref[0] = gt_t[0:LANE, :]
    gate_ref[1] = gt_t[LANE:2 * LANE, :]

    conv = seg(_CV0, _CV1)
    cb = conv[:, 0:512]
    u = conv[:, 512:1024] * conv[:, 1024:1536]
    first = (i % tiles_per_batch) == 0

    @pl.when(first)
    def _():
        uext_ref[0:SUBLANE, :] = jnp.zeros((SUBLANE, 512), F32)

    @pl.when(jnp.logical_not(first))
    def _():
        uext_ref[0:SUBLANE, :] = uext_ref[tm:tm + SUBLANE, :]

    uext_ref[SUBLANE:tm + SUBLANE, :] = u
    um1 = uext_ref[SUBLANE - 1:tm + SUBLANE - 1, :]
    um2 = uext_ref[SUBLANE - 2:tm + SUBLANE - 2, :]
    z = cb * (um2 * cw_ref[0:1, :] + um1 * cw_ref[1:2, :] + u * cw_ref[2:3, :])
    zn_ref[...] = _rms(z, gc_ref[...], 512).astype(BF16)
    ctail_ref[0] = uext_ref[tm + SUBLANE - 2:tm + SUBLANE, :]


def _inproj_prompt(x, w, conv_w, g_conv, batch, seq):
    t, d = x.shape
    n_tiles = t // TM
    tpb = seq // TM
    assert WINDOW == TM
    rows = WINDOW + seq
    key_pad = jnp.zeros((LANE,), BF16).at[PAD_LANE].set(KEY_MASK)
    k_fill = jnp.broadcast_to(key_pad, (N_KV, batch, rows, LANE))
    v_fill = jnp.zeros((batch, LANE, rows), BF16)
    outs = (
        jax.ShapeDtypeStruct((N_HEADS, LANE, t), BF16),
        jax.ShapeDtypeStruct((batch, 512, seq), F32),
        jax.ShapeDtypeStruct(k_fill.shape, BF16),
        jax.ShapeDtypeStruct(v_fill.shape, BF16),
        jax.ShapeDtypeStruct(k_fill.shape, BF16),
        jax.ShapeDtypeStruct(v_fill.shape, BF16),
        jax.ShapeDtypeStruct((batch, 256, TM), F32),
        jax.ShapeDtypeStruct((N_KV, LANE, t), F32),
        jax.ShapeDtypeStruct((t, 512), BF16),
        jax.ShapeDtypeStruct((batch, CONV_W - 1, 512), F32),
    )
    row = lambda n: pl.BlockSpec((TM, n), lambda i: (i, 0))
    planes_t = lambda n: pl.BlockSpec((n, LANE, TM), lambda i: (0, 0, i))
    keys_out = pl.BlockSpec((N_KV, 1, TM, LANE), lambda i: (0, i // tpb, i % tpb + 1, 0))
    vals_out = pl.BlockSpec((1, LANE, TM), lambda i: (i // tpb, 0, i % tpb + 1))
    in_place = pl.BlockSpec(memory_space=pl.ANY)
    return pl.pallas_call(
        functools.partial(_inproj_prompt_kernel, tiles_per_batch=tpb),
        grid=(n_tiles,),
        in_specs=[row(d), _const_spec(w.shape), _const_spec(conv_w.shape), _const_spec(g_conv.shape),
                  in_place, in_place, in_place, in_place],
        input_output_aliases={4: 2, 5: 4, 6: 3, 7: 5},
        out_specs=(
            planes_t(N_HEADS),
            pl.BlockSpec((1, 512, TM), lambda i: (i // tpb, 0, i % tpb)),
            keys_out, vals_out, keys_out, vals_out,
            pl.BlockSpec((1, 256, TM), lambda i: (i // tpb, 0, 0)),
            planes_t(N_KV),
            row(512),
            pl.BlockSpec((1, CONV_W - 1, 512), lambda i: (i // tpb, 0, 0)),
        ),
        out_shape=outs,
        scratch_shapes=[pltpu.VMEM((TM + SUBLANE, 512), F32)],
        compiler_params=_cparams(1),
        name="inproj_prompt",
    )(x, w, conv_w, g_conv, k_fill, k_fill, v_fill, v_fill)


def _inproj_sample_kernel(x_ref, w_ref, cw_ref, gc_ref, s0_ref, s1_ref,
                          qlo_ref, qgl_ref, kv_ref, win_ref, gate_ref, zn_ref, u_ref):
    xb = x_ref[...].astype(BF16)

    def seg(a, b):
        return _mm(xb, w_ref[:, a:b])

    qp = seg(_Q0, _Q1) * (HEAD_DIM ** -0.5)
    fill = jnp.zeros((qp.shape[0], LANE - HEAD_DIM), F32)
    heads = [qp[:, h * HEAD_DIM:(h + 1) * HEAD_DIM] for h in range(N_HEADS)]
    qlo_ref[...] = jnp.concatenate([x for qh in heads for x in (qh, fill)], axis=1).astype(BF16)
    qgl_ref[...] = jnp.concatenate([x for h, qh in enumerate(heads) for x in ((qh, fill) if h < GQ else (fill, qh))],
                                   axis=1).astype(BF16)
    kv_ref[...] = seg(_KV0, _KV1)
    win_ref[...] = seg(_WN0, _WN1)
    gate_ref[...] = jax.nn.sigmoid(seg(_GT0, _GT1))
    conv = seg(_CV0, _CV1)
    cb = conv[:, 0:512]
    u = conv[:, 512:1024] * conv[:, 1024:1536]
    z = cb * (s0_ref[...] * cw_ref[0:1, :] + s1_ref[...] * cw_ref[1:2, :] + u * cw_ref[2:3, :])
    zn_ref[...] = _rms(z, gc_ref[...], 512).astype(BF16)
    u_ref[...] = u


def _inproj_sample(x, w, conv_w, g_conv, s0, s1):
    n = x.shape[0]
    outs = (
        jax.ShapeDtypeStruct((n, N_HEADS * LANE), BF16),
        jax.ShapeDtypeStruct((n, N_HEADS * LANE), BF16),
        jax.ShapeDtypeStruct((n, 512), F32),
        jax.ShapeDtypeStruct((n, 256), F32),
        jax.ShapeDtypeStruct((n, N_KV * LANE), F32),
        jax.ShapeDtypeStruct((n, 512), BF16),
        jax.ShapeDtypeStruct((n, 512), F32),
    )
    args = (x, w, conv_w, g_conv, s0, s1)
    return pl.pallas_call(
        _inproj_sample_kernel,
        grid=(1,),
        in_specs=[_const_spec(a.shape) for a in args],
        out_specs=tuple(_const_spec(o.shape) for o in outs),
        out_shape=outs,
        compiler_params=_cparams(1),
        name="inproj_sample",
    )(*args)


def _pack_compress_weights(pe, w1, w2, per_group_out):
    w = w1.reshape(2, CMP_STRIDE, HEAD_DIM, CMP_HIDDEN).transpose(1, 2, 0, 3).reshape(CMP_STRIDE, HEAD_DIM, 2 * CMP_HIDDEN)
    z = jnp.zeros_like(w)
    top = jnp.concatenate([w, z], axis=-1)
    bot = jnp.concatenate([z, w], axis=-1)
    w1bd = jnp.stack([top, bot], axis=1).reshape(CMP_STRIDE * 2 * HEAD_DIM, 4 * CMP_HIDDEN).astype(BF16)
    z2 = jnp.zeros_like(w2)
    if per_group_out:
        w2bd = jnp.concatenate([jnp.concatenate([w2, z2, z2, z2], axis=1),
                                jnp.concatenate([z2, z2, w2, z2], axis=1)], axis=0).astype(BF16)
    else:
        w2bd = jnp.concatenate([jnp.concatenate([w2, z2], axis=1), jnp.concatenate([z2, w2], axis=1)], axis=0).astype(BF16)
    pe_rows = jnp.broadcast_to(pe.reshape(2, CMP_STRIDE, 1, HEAD_DIM), (2, CMP_STRIDE, N_KV, HEAD_DIM))
    pe_rows = pe_rows.reshape(2, CMP_STRIDE * N_KV * HEAD_DIM)
    pe_rows = jnp.pad(pe_rows, ((0, SUBLANE - 2), (0, 0)))
    return w1bd, w2bd, pe_rows


def _compress_kernel(*refs, n_pages, n_prefetch):
    refs = refs[n_prefetch:]
    page_refs = refs[:n_pages]
    w1_ref, w2k_ref, w2v_ref, pe_ref, ck_ref, cvt_ref, carry_ref, tok_ref = refs[n_pages:]
    step = pl.program_id(1)
    m = n_pages * (PAGE_SIZE // CMP_STRIDE)

    @pl.when(step == 0)
    def _():
        carry_ref[...] = jnp.zeros(carry_ref.shape, F32)

    row0 = lax.broadcasted_iota(jnp.int32, (m, CMP_HIDDEN), 0) == 0
    for t, w2_ref in enumerate((w2k_ref, w2v_ref)):
        for p, pr in enumerate(page_refs):
            slab = pr[0, t].reshape(N_KV * HEAD_DIM, PAGE_SIZE)
            tok_ref[t, p * PAGE_SIZE:(p + 1) * PAGE_SIZE, :] = slab.T
        pieces = [tok_ref[t, pl.ds(j, m, stride=CMP_STRIDE), :] for j in range(CMP_STRIDE)]
        lhs = jnp.concatenate([jnp.concatenate(pieces, axis=1), pe_ref[t]], axis=0).astype(BF16)
        a = _mm(lhs, w1_ref[t])
        hs = []
        for g in range(N_KV):
            c0 = g * 2 * CMP_HIDDEN
            a0 = a[0:m, c0:c0 + CMP_HIDDEN]
            a1 = a[0:m, c0 + CMP_HIDDEN:c0 + 2 * CMP_HIDDEN]
            pe_term = a[m:m + 1, c0:c0 + CMP_HIDDEN] + a[m + 1:m + 2, c0 + CMP_HIDDEN:c0 + 2 * CMP_HIDDEN]
            prev = carry_ref[t, 0:1, g * CMP_HIDDEN:(g + 1) * CMP_HIDDEN]
            shifted = jnp.where(row0, prev, pltpu.roll(a0, 1, axis=0))
            carry_ref[t, 0:1, g * CMP_HIDDEN:(g + 1) * CMP_HIDDEN] = a0[m - 1:m, :]
            hs.append(jax.nn.gelu(shifted + a1 + pe_term))
        out = _mm(jnp.concatenate(hs, axis=1).astype(BF16), w2_ref[...])
        if t == 0:
            ck_ref[0, 0] = out[:, 0:LANE].astype(BF16)
            ck_ref[0, 1] = out[:, LANE:2 * LANE].astype(BF16)
        else:
            cvt_ref[0] = out.T.astype(BF16)


def _compress_call(page_specs, page_args, prefetch, grid, batch, n_chunks, m, w1, w2k, w2v, pe, name):
    n_pages = len(page_specs)
    n_pf = len(prefetch)
    cspec = lambda shape: pl.BlockSpec(shape, lambda *_: (0,) * len(shape))
    grid_spec = pltpu.PrefetchScalarGridSpec(
        num_scalar_prefetch=n_pf,
        grid=grid,
        in_specs=list(page_specs) + [cspec(w1.shape), cspec(w2k.shape), cspec(w2v.shape), cspec(pe.shape)],
        out_specs=(pl.BlockSpec((1, N_KV, m, LANE), lambda b, s, *_: (b, 0, s, 0)),
                   pl.BlockSpec((1, LANE, m), lambda b, s, *_: (b, 0, s))),
        scratch_shapes=[pltpu.VMEM((2, SUBLANE, 2 * CMP_HIDDEN), F32),
                        pltpu.VMEM((2, n_pages * PAGE_SIZE, N_KV * HEAD_DIM), F32)],
    )
    return pl.pallas_call(
        functools.partial(_compress_kernel, n_pages=n_pages, n_prefetch=n_pf),
        grid_spec=grid_spec,
        out_shape=(jax.ShapeDtypeStruct((batch, N_KV, n_chunks, LANE), BF16),
                   jax.ShapeDtypeStruct((batch, LANE, n_chunks), BF16)),
        compiler_params=_cparams(2),
        name=name,
    )(*prefetch, *page_args, w1, w2k, w2v, pe)


_PAGE_BLOCK = (1, 2, N_KV, HEAD_DIM, PAGE_SIZE)


def _compress_prompt(kv_t, batch, seq, w1, w2k, w2v, pe):
    n_pages = seq // PAGE_SIZE
    rows = PAGE_SIZE // CMP_STRIDE
    specs = [pl.BlockSpec(_PAGE_BLOCK, lambda b, s, p=p: (b, 0, 0, 0, p)) for p in range(n_pages)]
    return _compress_call(specs, [kv_t] * n_pages, (), (batch, 1), batch, n_pages * rows, n_pages * rows,
                          w1, w2k, w2v, pe, "compress_prompt")


def _compress_sample(cache_t, page_table, w1, w2k, w2v, pe):
    batch, n_pages = page_table.shape
    rows = PAGE_SIZE // CMP_STRIDE
    specs = [pl.BlockSpec(_PAGE_BLOCK, lambda b, s, pt, k=k: (pt[b, s * SAMPLE_PAGES + k], 0, 0, 0, 0))
             for k in range(SAMPLE_PAGES)]
    return _compress_call(specs, [cache_t] * SAMPLE_PAGES, (page_table,), (batch, n_pages // SAMPLE_PAGES), batch,
                          n_pages * rows, SAMPLE_PAGES * rows, w1, w2k, w2v, pe, "compress_sample")


def _tile4(x):
    return jnp.concatenate([x] * GQ, axis=1)


def _softmax_fold(state, pieces, vt):
    m, l, acc = state
    m_new = m
    for pc in pieces:
        m_new = jnp.maximum(m_new, jnp.max(pc, axis=0, keepdims=True))
    alpha = jnp.exp(m - m_new)
    es = [jnp.exp(pc - m_new) for pc in pieces]
    l = alpha * l
    for e in es:
        l = l + jnp.sum(e, axis=0, keepdims=True)
    e_all = es[0] if len(es) == 1 else jnp.concatenate(es, axis=0)
    acc = alpha * acc + _mm(vt, e_all.astype(BF16))
    return m_new, l, acc


def _attn_prompt_kernel(q_ref, gate_ref, ck_ref, cvt_ref, ks_ref, sv_ref, kw_ref, wv_ref,
                        near_ref, up_ref, tabc_ref, sp_ref, gn_ref, o_ref, *, n_slc):
    qi = pl.program_id(1)
    q0 = pl.multiple_of(qi * TQ, TQ)
    krow = lax.broadcasted_iota(jnp.int32, (LANE, TQ), 0)
    qcol = lax.broadcasted_iota(jnp.int32, (LANE, TQ), 1)
    cmp_ok4 = _tile4((krow >= 1) & (CMP_STRIDE * krow + (CMP_STRIDE - 1) <= q0 + qcol))
    sidx = lax.broadcasted_iota(jnp.int32, (n_slc, TQ), 0)
    qblk = lax.shift_right_logical(q0 + lax.broadcasted_iota(jnp.int32, (n_slc, TQ), 1), 6)
    sel_valid = sidx <= qblk
    sel_forced = (sidx == 0) | (sidx == qblk) | (sidx == qblk - 1)
    n_far = lax.shift_right_logical(jnp.maximum(qi - 1, 0), 2)
    up4 = _tile4(up_ref[...])
    cols = GQ * TQ
    fresh = (jnp.full((1, cols), NEG, F32), jnp.zeros((1, cols), F32), jnp.zeros((LANE, cols), F32))

    groups = range(N_KV)
    qa, near, o_cmp = [], [], []
    for g in groups:
        heads = [GQ * g + r for r in range(GQ)]
        q_t = [q_ref[h] for h in heads]
        near.append(jnp.concatenate([near_ref[h] for h in heads], axis=1))

        bias_c = jnp.concatenate([tabc_ref[0, h] for h in heads], axis=1)
        s = jnp.where(cmp_ok4, _mm(ck_ref[0, g], jnp.concatenate(q_t, axis=1)) + bias_c, NEG)
        e = jnp.exp(s - jnp.max(s, axis=0, keepdims=True))
        p = (e / jnp.sum(e, axis=0, keepdims=True) * cmp_ok4.astype(F32)).astype(BF16)
        o_cmp.append(_mm(cvt_ref[0], p))
        imp4 = _mm(sp_ref[...], p)
        imp = imp4[:, 0:TQ] + imp4[:, TQ:2 * TQ] + imp4[:, 2 * TQ:3 * TQ] + imp4[:, 3 * TQ:4 * TQ]

        score = jnp.where(sel_valid, imp[0:n_slc, :] + jnp.where(sel_forced, FORCE, 0.0), -FORCE)
        rank = jnp.zeros((n_slc, TQ), F32)
        for k in range(1, n_slc):
            other = pltpu.roll(score, k, axis=0)
            rank = rank + jnp.where(sidx >= k, (other >= score).astype(F32), (other > score).astype(F32))
        sel_t = (rank < float(N_SELECT)).astype(F32)
        aug = jnp.concatenate([jnp.zeros((SEL_LANE0, TQ), F32), sel_t - 1.0,
                               jnp.zeros((LANE - SEL_LANE0 - n_slc, TQ), F32)], axis=0)
        aug = jnp.where(krow == PAD_LANE, -1.0, aug).astype(BF16)
        qa.append(jnp.concatenate([qh + aug for qh in q_t], axis=1))

    def far_tile(it, states):
        start = pl.multiple_of(q0 - FAR * (n_far - it), LANE)
        vt = sv_ref[0, :, pl.ds(start, FAR)]
        return tuple(_softmax_fold(states[g], [_mm(ks_ref[g, 0, pl.ds(start, FAR), :], qa[g])], vt) for g in groups)

    states = lax.fori_loop(0, n_far, far_tile, (fresh,) * N_KV)
    v_tail = sv_ref[0, :, pl.ds(q0, TAIL)]
    w_tail = wv_ref[0, :, pl.ds(q0, TAIL)]
    o_slc, o_win = [], []
    for g in groups:
        sc = _mm(ks_ref[g, 0, pl.ds(q0, TAIL), :], qa[g])
        pieces = [sc[0:TAIL - 2 * LANE], sc[TAIL - 2 * LANE:TAIL] + near[g]]
        _, l, acc = _softmax_fold(states[g], pieces, v_tail)
        o_slc.append(acc / l)
        sc = _mm(kw_ref[g, 0, pl.ds(q0, TAIL), :], qa[g])
        pieces = [sc[0:LANE] + up4, sc[LANE:TAIL - 2 * LANE], sc[TAIL - 2 * LANE:TAIL] + near[g]]
        _, l, acc = _softmax_fold(fresh, pieces, w_tail)
        o_win.append(acc / l)

    outs = []
    for g in groups:
        gt = gate_ref[g]
        own = lax.shift_right_logical(krow, 6) == g
        for r in range(GQ):
            cs = slice(r * TQ, (r + 1) * TQ)
            comb = gt[3 * r:3 * r + 1, :] * o_cmp[g][:, cs] + gt[3 * r + 1:3 * r + 2, :] * o_slc[g][:, cs] \
                + gt[3 * r + 2:3 * r + 3, :] * o_win[g][:, cs]
            outs.append(jnp.where(own, comb, 0.0))
    o = jnp.concatenate(outs, axis=0)
    ms = jnp.sum(o * o, axis=0, keepdims=True) / (N_HEADS * HEAD_DIM)
    o = o * lax.rsqrt(ms + 1e-6) * gn_ref[...]
    o_ref[...] = jnp.concatenate([o[h * LANE:(h + 1) * LANE, :].T for h in range(N_HEADS)], axis=1).astype(BF16)


def _attn_prompt(q, gates, cmpk, cmpv_t, ks, sv_t, kw, wv_t, near_t, up_t, tab_c, sp_t, gn_rep, batch, seq):
    nq = seq // TQ
    t = batch * seq
    rows = ks.shape[2]
    n_slc = seq // SLC_LEN
    n_cmp = seq // CMP_STRIDE
    assert SEL_LANE0 + n_slc <= PAD_LANE < LANE
    return pl.pallas_call(
        functools.partial(_attn_prompt_kernel, n_slc=n_slc),
        grid=(batch, nq),
        in_specs=[
            pl.BlockSpec((N_HEADS, LANE, TQ), lambda b, i: (0, 0, b * nq + i)),
            pl.BlockSpec((N_KV, LANE, TQ), lambda b, i: (0, 0, b * nq + i)),
            pl.BlockSpec((1, N_KV, n_cmp, LANE), lambda b, i: (b, 0, 0, 0)),
            pl.BlockSpec((1, LANE, n_cmp), lambda b, i: (b, 0, 0)),
            pl.BlockSpec((N_KV, 1, rows, LANE), lambda b, i: (0, b, 0, 0)),
            pl.BlockSpec((1, LANE, rows), lambda b, i: (b, 0, 0)),
            pl.BlockSpec((N_KV, 1, rows, LANE), lambda b, i: (0, b, 0, 0)),
            pl.BlockSpec((1, LANE, rows), lambda b, i: (b, 0, 0)),
            _const_spec(near_t.shape),
            _const_spec(up_t.shape),
            pl.BlockSpec((1, N_HEADS, LANE, LANE), lambda b, i: (i, 0, 0, 0)),
            _const_spec(sp_t.shape),
            _const_spec(gn_rep.shape),
        ],
        out_specs=pl.BlockSpec((TQ, N_HEADS * LANE), lambda b, i: (b * nq + i, 0)),
        out_shape=jax.ShapeDtypeStruct((t, N_HEADS * LANE), BF16),
        compiler_params=_cparams(2),
        name="attn_prompt",
    )(q, gates, cmpk, cmpv_t, ks, sv_t, kw, wv_t, near_t, up_t, tab_c, sp_t, gn_rep)


def _sample_cw_kernel(qlo_ref, qgl_ref, ck_ref, cvt_ref, cwin_ref, wnew_ref, tabc_ref, tabw_ref, rb0_ref, sp_ref,
                      ocmp_ref, owin_ref, idx_ref, **static):
    for bb in range(qlo_ref.shape[0]):
        one = lambda r, bb=bb: r.at[pl.ds(bb, 1)]
        _sample_cw_one(one(qlo_ref), one(qgl_ref), one(ck_ref), one(cvt_ref), one(cwin_ref), one(wnew_ref),
                       tabc_ref, tabw_ref, rb0_ref, sp_ref, one(ocmp_ref), one(owin_ref), one(idx_ref), **static)


def _sample_cw_one(qlo_ref, qgl_ref, ck_ref, cvt_ref, cwin_ref, wnew_ref, tabc_ref, tabw_ref, rb0_ref, sp_ref,
                   ocmp_ref, owin_ref, idx_ref, *, n_slc, qblk, win_buf):
    qlo = qlo_ref[0]
    qgl = qgl_ref[0]
    n_cmp_rows = ck_ref.shape[2]
    col = lax.broadcasted_iota(jnp.int32, (N_HEADS, n_cmp_rows), 1)
    grp_of_head = lax.shift_right_logical(lax.broadcasted_iota(jnp.int32, (N_HEADS, n_cmp_rows), 0), 2)
    ok = col >= 1
    raw = jnp.where(grp_of_head == 0, _qk(qlo, ck_ref[0, 0]), _qk(qlo, ck_ref[0, 1]))
    s = jnp.where(ok, raw + tabc_ref[...], NEG)
    e = jnp.exp(s - jnp.max(s, axis=1, keepdims=True))
    p = (e / jnp.sum(e, axis=1, keepdims=True) * ok.astype(F32)).astype(BF16)
    ocmp_ref[0] = _qk(p, cvt_ref[0])
    imp8 = _mm(p, sp_ref[...])

    width = imp8.shape[1]
    head_grp = lax.shift_right_logical(lax.broadcasted_iota(jnp.int32, (N_HEADS, width), 0), 2)
    sidx = lax.broadcasted_iota(jnp.int32, (N_HEADS, width), 1)
    sidx_f = sidx.astype(F32)
    forced = (sidx == 0) | (sidx == qblk) | (sidx == qblk - 1)
    lane = lax.broadcasted_iota(jnp.int32, (N_HEADS, LANE), 1)
    imp_g = [jnp.sum(jnp.where(head_grp == g, imp8, 0.0), axis=0, keepdims=True) for g in range(N_KV)]
    imp = jnp.where(head_grp == 0, imp_g[0], imp_g[1])
    score = jnp.where(sidx <= qblk, imp + jnp.where(forced, FORCE, 0.0), -FORCE)
    score = jnp.where(sidx < n_slc, score, -jnp.inf)
    picked = jnp.zeros((N_HEADS, LANE), F32)
    for k in range(N_SELECT):
        best = jnp.max(score, axis=1, keepdims=True)
        ik = jnp.min(jnp.where(score == best, sidx_f, float(width)), axis=1, keepdims=True)
        picked = jnp.where(lane == k, ik, picked)
        score = jnp.where(sidx_f == ik, -jnp.inf, score)
    for g in range(N_KV):
        idx_ref[0, g:g + 1, :] = picked[GQ * g:GQ * g + 1, :].astype(jnp.int32)

    kt = cwin_ref[0, 0].reshape(N_KV * HEAD_DIM, win_buf).astype(BF16)
    vt = cwin_ref[0, 1].reshape(N_KV * HEAD_DIM, win_buf).astype(BF16)
    knew = wnew_ref[0, :, 0:LANE].astype(BF16).astype(F32)
    vnew = wnew_ref[0, :, LANE:2 * LANE].astype(BF16).astype(F32)
    colw = lax.broadcasted_iota(jnp.int32, (N_HEADS, win_buf), 1)
    okw = colw > win_buf - WINDOW
    sw = jnp.where(okw, _mm(qgl, kt) + tabw_ref[...], NEG)
    s_new = jnp.sum(qgl.astype(F32) * knew, axis=1, keepdims=True) + rb0_ref[:, 0:1]
    mw = jnp.maximum(jnp.max(sw, axis=1, keepdims=True), s_new)
    ew = jnp.where(okw, jnp.exp(sw - mw), 0.0)
    e_new = jnp.exp(s_new - mw)
    lw = jnp.sum(ew, axis=1, keepdims=True) + e_new
    owin_ref[0] = (_qk(ew.astype(BF16), vt) + e_new.astype(BF16).astype(F32) * vnew) / lw


def _sample_cw(qlo, qgl, cmpk, cmpv_t, cwin, wnew, tab_c, tab_w, rb0, sp, n_slc, qblk):
    n = qlo.shape[0]
    win_buf = cwin.shape[-1]
    blk = lambda a: pl.BlockSpec((SAMPLE_BB,) + a.shape[1:], lambda b: (b,) + (0,) * (a.ndim - 1))
    outs = (jax.ShapeDtypeStruct((n, N_HEADS, LANE), F32), jax.ShapeDtypeStruct((n, N_HEADS, LANE), F32),
            jax.ShapeDtypeStruct((n, N_KV, LANE), jnp.int32))
    return pl.pallas_call(
        functools.partial(_sample_cw_kernel, n_slc=n_slc, qblk=qblk, win_buf=win_buf),
        grid=(n // SAMPLE_BB,),
        in_specs=[blk(qlo), blk(qgl), blk(cmpk), blk(cmpv_t), blk(cwin), blk(wnew), _const_spec(tab_c.shape),
                  _const_spec(tab_w.shape), _const_spec(rb0.shape), _const_spec(sp.shape)],
        out_specs=tuple(pl.BlockSpec((SAMPLE_BB,) + o.shape[1:], lambda b: (b, 0, 0)) for o in outs),
        out_shape=outs,
        compiler_params=_cparams(1),
        name="sample_cmp_win",
    )(qlo, qgl, cmpk, cmpv_t, cwin, wnew, tab_c, tab_w, rb0, sp)


def _sample_slc_kernel(pg_ref, hf_ref, bid_ref, *refs, new_block, near_block):
    n_blk = N_KV * N_SELECT
    blocks = refs[:n_blk]
    (q_ref, kvnew_ref, ocmp_ref, owin_ref, gate_ref, tnear_ref, tlast_ref, rb0_ref, gn_ref, o_ref) = refs[n_blk:]
    b = pl.program_id(0)
    n_keys = N_SELECT * PAGE_SIZE
    lane = lax.broadcasted_iota(jnp.int32, (1, n_keys), 1)
    slot = lax.shift_right_logical(lane, 7)
    half = lax.shift_right_logical(lane, 6) & 1
    head_grp = lax.shift_right_logical(lax.broadcasted_iota(jnp.int32, (N_HEADS, LANE), 0), 2)
    lane_grp = lax.shift_right_logical(lax.broadcasted_iota(jnp.int32, (N_HEADS, LANE), 1), 6)
    qs = q_ref[0][:, 0:HEAD_DIM]
    fill = jnp.zeros((N_HEADS, LANE - HEAD_DIM), F32)
    o_slc = jnp.zeros((N_HEADS, LANE), F32)
    for g in range(N_KV):
        knew = kvnew_ref[0, :, 256 + g * HEAD_DIM:256 + (g + 1) * HEAD_DIM].astype(BF16).astype(F32)
        vnew = kvnew_ref[0, :, 384 + g * HEAD_DIM:384 + (g + 1) * HEAD_DIM].astype(BF16).astype(F32)
        mine = blocks[g * N_SELECT:(g + 1) * N_SELECT]
        kt = jnp.concatenate([blk[0, 0, 0] for blk in mine], axis=1).astype(BF16)
        vt = jnp.concatenate([blk[0, 1, 0] for blk in mine], axis=1).astype(BF16)
        bid = jnp.zeros((1, n_keys), jnp.int32)
        hsel = jnp.zeros((1, n_keys), jnp.int32)
        has_new = bid_ref[b, g * N_SELECT] == new_block
        for kk in range(N_SELECT):
            bid = jnp.where(slot == kk, bid_ref[b, g * N_SELECT + kk], bid)
            hsel = jnp.where(slot == kk, hf_ref[b, g * N_SELECT + kk], hsel)
            if kk:
                has_new = has_new | (bid_ref[b, g * N_SELECT + kk] == new_block)
        bias = jnp.where(bid == near_block + 1, tlast_ref[...], jnp.where(bid == near_block, tnear_ref[...], 0.0))
        ok = (bid != new_block) & (half == hsel)
        s = jnp.where(ok, _mm(qs, kt) + bias, NEG)
        s_new = jnp.sum(qs.astype(F32) * knew, axis=1, keepdims=True) + rb0_ref[:, 0:1]
        s_new = jnp.where(has_new, s_new, NEG)
        m = jnp.maximum(jnp.max(s, axis=1, keepdims=True), s_new)
        e = jnp.where(ok, jnp.exp(s - m), 0.0)
        e_new = jnp.where(has_new, jnp.exp(s_new - m), 0.0)
        l = jnp.sum(e, axis=1, keepdims=True) + e_new
        og = (_qk(e.astype(BF16), vt) + e_new.astype(BF16).astype(F32) * vnew) / l
        og = jnp.concatenate([og, fill] if g == 0 else [fill, og], axis=1)
        o_slc = jnp.where(head_grp == g, og, o_slc)
    gt = gate_ref[0]
    comb = gt[:, 0:1] * ocmp_ref[0] + gt[:, 1:2] * o_slc + gt[:, 2:3] * owin_ref[0]
    comb = jnp.where(lane_grp == head_grp, comb, 0.0)
    o = jnp.concatenate([comb[h:h + 1, :] for h in range(N_HEADS)], axis=1)
    o_ref[0] = _rms(o, gn_ref[...], N_HEADS * HEAD_DIM).astype(BF16)


def _sample_slc(pages, halves, bids, cache_t, q, kvnew, ocmp, owin, gates, t_near, t_last, rb0, gn_pad,
                new_block, near_block):
    n = q.shape[0]
    n_blk = N_KV * N_SELECT
    blk_specs = [pl.BlockSpec((1, 2, 1, HEAD_DIM, PAGE_SIZE),
                              lambda b, pg, hf, bi, k=k: (pg[b, k], 1, k // N_SELECT, 0, 0))
                 for k in range(n_blk)]
    per_b = lambda a: pl.BlockSpec((1,) + a.shape[1:], lambda b, *_: (b,) + (0,) * (a.ndim - 1))
    cst = lambda a: pl.BlockSpec(a.shape, lambda *_: (0,) * a.ndim)
    grid_spec = pltpu.PrefetchScalarGridSpec(
        num_scalar_prefetch=3,
        grid=(n,),
        in_specs=blk_specs + [per_b(q), per_b(kvnew), per_b(ocmp), per_b(owin), per_b(gates),
                              cst(t_near), cst(t_last), cst(rb0), cst(gn_pad)],
        out_specs=pl.BlockSpec((1, 1, N_HEADS * LANE), lambda b, *_: (b, 0, 0)),
    )
    return pl.pallas_call(
        functools.partial(_sample_slc_kernel, new_block=new_block, near_block=near_block),
        grid_spec=grid_spec,
        out_shape=jax.ShapeDtypeStruct((n, 1, N_HEADS * LANE), BF16),
        compiler_params=_cparams(1),
        name="sample_selected",
    )(pages, halves, bids, *([cache_t] * n_blk), q, kvnew, ocmp, owin, gates, t_near, t_last, rb0, gn_pad)


def _proj_kernel(on_ref, zn_ref, x_ref, wa_ref, wb_ref, g1_ref, b1_ref, wr_ref, br_ref,
                 x1_ref, gate_ref, *, alpha):
    mix = _mm(on_ref[...], wa_ref[...]) + _mm(zn_ref[...], wb_ref[...])
    x1 = _layernorm(alpha * x_ref[...] + mix, g1_ref[...], b1_ref[...])
    x1_ref[...] = x1
    logits = _mm(x1.astype(BF16), wr_ref[...]) + br_ref[...]
    tm = logits.shape[0]
    lane = lax.broadcasted_iota(jnp.int32, (tm, LANE), 1).astype(F32)

    def first_argmax(v, vmax):
        return jnp.min(jnp.where(v == vmax, lane, float(LANE)), axis=1, keepdims=True)

    lg = jnp.where(lane < N_GROUPS, logits[:, 0:LANE], -jnp.inf)
    lg_max = jnp.max(lg, axis=1, keepdims=True)
    eg = jnp.exp(lg - lg_max)
    pg = eg / jnp.sum(eg, axis=1, keepdims=True)
    gidx = first_argmax(lg, lg_max)
    pg_sel = jnp.sum(jnp.where(lane == gidx, pg, 0.0), axis=1, keepdims=True)
    le = jnp.zeros((tm, LANE), F32)
    for gi in range(N_GROUPS):
        le = le + jnp.where(gidx == gi, logits[:, (gi + 1) * LANE:(gi + 2) * LANE], 0.0)
    le = jnp.where(lane < N_EXP, le, -jnp.inf)
    ee = jnp.exp(le - jnp.max(le, axis=1, keepdims=True))
    pe = jnp.where(lane < N_EXP, ee / jnp.sum(ee, axis=1, keepdims=True), -1.0)
    v1 = jnp.max(pe, axis=1, keepdims=True)
    i1 = first_argmax(pe, v1)
    pe2 = jnp.where(lane == i1, -1.0, pe)
    v2 = jnp.max(pe2, axis=1, keepdims=True)
    i2 = first_argmax(pe2, v2)
    tot = v1 + v2
    gate_e = jnp.where(lane == i1, v1 / tot * pg_sel, jnp.where(lane == i2, v2 / tot * pg_sel, 0.0))
    gate_ref[...] = jnp.where(lane == GROUP_LANE, gidx, gate_e)


def _proj(on, zn, x, wa, wb, g1, b1, wr, br, alpha, tm):
    t, d = x.shape
    row = lambda n: pl.BlockSpec((tm, n), lambda i: (i, 0))
    outs = (jax.ShapeDtypeStruct((t, d), F32), jax.ShapeDtypeStruct((t, LANE), F32))
    return pl.pallas_call(
        functools.partial(_proj_kernel, alpha=alpha),
        grid=(t // tm,),
        in_specs=[row(on.shape[1]), row(zn.shape[1]), row(d)] + [_const_spec(a.shape) for a in (wa, wb, g1, b1, wr, br)],
        out_specs=(row(d), row(LANE)),
        out_shape=outs,
        compiler_params=_cparams(1),
        name=f"proj_ln_router_{tm}",
    )(on, zn, x, wa, wb, g1, b1, wr, br)


def _moe_kernel(x1_ref, gate_ref, wg_ref, wu_ref, wd_ref, g2_ref, b2_ref, out_ref, acc_ref, *, alpha):
    gi = pl.program_id(1)

    @pl.when(gi == 0)
    def _():
        acc_ref[...] = jnp.zeros(acc_ref.shape, F32)

    x = x1_ref[...].astype(BF16)
    gate = gate_ref[...]
    gt = jnp.where(gate[:, GROUP_LANE:GROUP_LANE + 1] == gi.astype(F32), gate, 0.0)
    acc = acc_ref[...]
    n_half = wg_ref.shape[1]
    for e in range(N_EXP):
        part, k = divmod(e, n_half)
        h = jax.nn.silu(_mm(x, wg_ref[part, k])) * _mm(x, wu_ref[part, k])
        acc = acc + _mm((h * gt[:, e:e + 1]).astype(BF16), wd_ref[part, k])
    acc_ref[...] = acc

    @pl.when(gi == N_GROUPS - 1)
    def _():
        out_ref[...] = _layernorm(alpha * x1_ref[...] + acc_ref[...], g2_ref[...], b2_ref[...])


def _moe(x1, gate, wg, wu, wd, g2, b2, alpha, tm):
    t, d = x1.shape
    d_ff = wd.shape[2]
    n_half = wg.shape[1]
    return pl.pallas_call(
        functools.partial(_moe_kernel, alpha=alpha),
        grid=(t // tm, N_GROUPS),
        in_specs=[
            pl.BlockSpec((tm, d), lambda i, g: (i, 0)),
            pl.BlockSpec((tm, LANE), lambda i, g: (i, 0)),
            pl.BlockSpec((EXPERT_SPLIT, n_half, d, d_ff), lambda i, g: (g, 0, 0, 0)),
            pl.BlockSpec((EXPERT_SPLIT, n_half, d, d_ff), lambda i, g: (g, 0, 0, 0)),
            pl.BlockSpec((EXPERT_SPLIT, n_half, d_ff, d), lambda i, g: (g, 0, 0, 0)),
            _const_spec(g2.shape), _const_spec(b2.shape),
        ],
        out_specs=pl.BlockSpec((tm, d), lambda i, g: (i, 0)),
        out_shape=jax.ShapeDtypeStruct((t, d), F32),
        scratch_shapes=[pltpu.VMEM((tm, d), F32)],
        compiler_params=_cparams(2),
        name=f"moe_ln_{tm}",
    )(x1, gate, wg, wu, wd, g2, b2)


def _split_bf16(x):
    hi = x.astype(BF16)
    return hi, (x - hi.astype(F32)).astype(BF16)


def _moe_sorted_kernel(cnt_ref, off_ref, x1_ref, gate_ref, drow_ref, dcol_ref, wg_ref, wu_ref, wd_ref, g2_ref, b2_ref,
                       out_ref, xs_ref, gs_ref, ys_ref, *, alpha):
    i = pl.program_id(0)
    gi = pl.program_id(1)
    half = pl.program_id(2)
    n_half = N_EXP // EXPERT_SPLIT
    tmoe = x1_ref.shape[0]

    @pl.when((i == 0) & (gi == 0) & (half == 0))
    def _():
        xs_ref[...] = jnp.zeros(xs_ref.shape, BF16)
        gs_ref[...] = jnp.zeros(gs_ref.shape, F32)
        ys_ref[...] = jnp.zeros(ys_ref.shape, F32)

    @pl.when((gi == 0) & (half == 0))
    def _():
        perm = (lax.broadcasted_iota(jnp.int32, (PERM_ROWS, tmoe), 0) == drow_ref[0]).astype(BF16)
        xs_ref[0:PERM_ROWS, :] = _mm(perm, x1_ref[...].astype(BF16)).astype(BF16)
        g_hi, g_lo = _split_bf16(gate_ref[...])
        gs_ref[0:PERM_ROWS, :] = _mm(perm, g_hi) + _mm(perm, g_lo)

    n = cnt_ref[i * N_GROUPS + gi]
    off = off_ref[i * N_GROUPS + gi]

    def run_pass(c, carry):
        base = pl.multiple_of(off + c * MOE_CHUNK, SEG_ALIGN)
        xc = xs_ref[pl.ds(base, MOE_CHUNK), :]
        gc = gs_ref[pl.ds(base, MOE_CHUNK), :]
        gc = jnp.where(half == 0, gc, pltpu.roll(gc, LANE - n_half, axis=1))
        y = jnp.zeros((MOE_CHUNK, out_ref.shape[1]), F32)
        for e in range(n_half):
            h = jax.nn.silu(_mm(xc, wg_ref[0, e])) * _mm(xc, wu_ref[0, e])
            y = y + _mm((h * gc[:, e:e + 1]).astype(BF16), wd_ref[0, e])

        @pl.when(half == 0)
        def _():
            ys_ref[pl.ds(base, MOE_CHUNK), :] = y

        @pl.when(half != 0)
        def _():
            ys_ref[pl.ds(base, MOE_CHUNK), :] = ys_ref[pl.ds(base, MOE_CHUNK), :] + y

        return carry

    lax.fori_loop(0, (n + MOE_CHUNK - 1) // MOE_CHUNK, run_pass, 0)

    @pl.when((gi == N_GROUPS - 1) & (half == EXPERT_SPLIT - 1))
    def _():
        unperm = (dcol_ref[...] == lax.broadcasted_iota(jnp.int32, (tmoe, PERM_ROWS), 1)).astype(BF16)
        y_hi, y_lo = _split_bf16(ys_ref[0:PERM_ROWS, :])
        moe = _mm(unperm, y_hi) + _mm(unperm, y_lo)
        out_ref[...] = _layernorm(alpha * x1_ref[...] + moe, g2_ref[...], b2_ref[...])


def _moe_sorted(x1, gate, wg_h, wu_h, wd_h, g2, b2, alpha):
    t, d = x1.shape
    d_ff = wd_h.shape[2]
    n_tiles = t // TMOE
    n_half = N_EXP // EXPERT_SPLIT
    gid = gate[:, GROUP_LANE].astype(jnp.int32).reshape(n_tiles, TMOE)
    onehot = (gid[:, :, None] == jnp.arange(N_GROUPS)[None, None, :]).astype(jnp.int32)
    cnt = jnp.sum(onehot, axis=1)
    before = jnp.tril(jnp.ones((TMOE, TMOE), BF16), -1)
    prefix = jnp.einsum("ij,bjg->big", before, onehot.astype(BF16), preferred_element_type=F32).astype(jnp.int32)
    rank = jnp.sum(prefix * onehot, axis=2)
    seg = (cnt + SEG_ALIGN - 1) // SEG_ALIGN * SEG_ALIGN
    off = jnp.cumsum(seg, axis=1) - seg
    dest = jnp.sum(onehot * off[:, None, :], axis=2) + rank
    assert N_GROUPS * (SEG_ALIGN - 1) <= PERM_ROWS - TMOE
    grid_spec = pltpu.PrefetchScalarGridSpec(
        num_scalar_prefetch=2,
        grid=(n_tiles, N_GROUPS, EXPERT_SPLIT),
        in_specs=[
            pl.BlockSpec((TMOE, d), lambda i, g, h, *_: (i, 0)),
            pl.BlockSpec((TMOE, LANE), lambda i, g, h, *_: (i, 0)),
            pl.BlockSpec((1, 1, TMOE), lambda i, g, h, *_: (i, 0, 0)),
            pl.BlockSpec((TMOE, 1), lambda i, g, h, *_: (i, 0)),
            pl.BlockSpec((1, n_half, d, d_ff), lambda i, g, h, *_: (g * EXPERT_SPLIT + h, 0, 0, 0)),
            pl.BlockSpec((1, n_half, d, d_ff), lambda i, g, h, *_: (g * EXPERT_SPLIT + h, 0, 0, 0)),
            pl.BlockSpec((1, n_half, d_ff, d), lambda i, g, h, *_: (g * EXPERT_SPLIT + h, 0, 0, 0)),
            pl.BlockSpec(g2.shape, lambda *_: (0, 0)), pl.BlockSpec(b2.shape, lambda *_: (0, 0)),
        ],
        out_specs=pl.BlockSpec((TMOE, d), lambda i, g, h, *_: (i, 0)),
        scratch_shapes=[pltpu.VMEM((SORT_ROWS, d), BF16), pltpu.VMEM((SORT_ROWS, LANE), F32),
                        pltpu.VMEM((SORT_ROWS, d), F32)],
    )
    return pl.pallas_call(
        functools.partial(_moe_sorted_kernel, alpha=alpha),
        grid_spec=grid_spec,
        out_shape=jax.ShapeDtypeStruct((t, d), F32),
        compiler_params=_cparams(3),
        name="moe_sorted_ln",
    )(cnt.reshape(-1), off.reshape(-1), x1, gate, dest.reshape(n_tiles, 1, TMOE), dest.reshape(t, 1), wg_h, wu_h, wd_h,
      g2, b2)


def _pad_head_lanes(v):
    vh = v.reshape(N_HEADS, HEAD_DIM)
    z = jnp.zeros_like(vh)
    grp = (jnp.arange(N_HEADS) // GQ)[:, None]
    return jnp.where(grp == 0, jnp.concatenate([vh, z], axis=1), jnp.concatenate([z, vh], axis=1)).reshape(1, -1)


def _pad_head_rows(w):
    wh = w.reshape(N_HEADS, HEAD_DIM, -1)
    z = jnp.zeros_like(wh)
    grp = (jnp.arange(N_HEADS) // GQ)[:, None, None]
    return jnp.where(grp == 0, jnp.concatenate([wh, z], axis=1), jnp.concatenate([z, wh], axis=1)).reshape(
        N_HEADS * LANE, -1)


def kernel(x_prompt, x_sample, cache_kv, page_table, cache_win, state_conv, w_in, conv_w, pe_k, pe_v, w_ck1, w_ck2, w_cv1, w_cv2, g_nsa, g_conv, w_out, ln1_g, ln1_b, w_rg, b_rg, w_re, b_re, w_eg, w_eu, w_ed, ln2_g, ln2_b, rel_bias):
    batch, seq, d_model = x_prompt.shape
    dec_batch, dec_seq = x_sample.shape[0], x_sample.shape[1]
    depth = w_in.shape[0]
    n_pages = page_table.shape[1]
    past = n_pages * PAGE_SIZE
    win_buf = cache_win.shape[2]
    d_nsa = N_HEADS * HEAD_DIM
    d_conv = w_out.shape[1] - d_nsa
    assert dec_seq == 1 and seq % TM == 0 and min(WINDOW, seq) == TM and win_buf == WINDOW
    assert d_conv == 512 and seq // CMP_STRIDE == LANE and n_pages % SAMPLE_PAGES == 0
    alpha = (2 * depth) ** 0.25
    n_slc_s = -(-(past + dec_seq) // SLC_LEN)
    qblk_s = past // SLC_LEN

    idx_np, sp_p_np, sp_s_np = _static_tables(seq, past, win_buf)
    tabs = _bias_tables(rel_bias, jnp.asarray(idx_np))
    nq = seq // TQ
    near_t = jnp.transpose(tabs[0:2], (1, 0, 2, 3)).reshape(N_HEADS, 2 * LANE, LANE)
    up_t = tabs[2, 0]
    tab_c = tabs[_N_NEAR_TILES:_N_NEAR_TILES + nq]
    smp = tabs[_N_NEAR_TILES + nq]
    tab_w_s = smp[:, 0:5, :].reshape(N_HEADS, 5 * LANE)[:, :win_buf]
    tab_c_s = smp[:, 5:13, :].reshape(N_HEADS, 8 * LANE)
    reps = N_SELECT * PAGE_SIZE // SLC_LEN
    t_near = jnp.tile(smp[:, 13, 0:SLC_LEN], (1, reps))
    t_last = jnp.tile(smp[:, 13, SLC_LEN:2 * SLC_LEN], (1, reps))
    rb0_rep = jnp.broadcast_to(smp[:, 14, 0:1], (N_HEADS, LANE))
    sp_p = jnp.asarray(sp_p_np, BF16)
    sp_s = jnp.asarray(sp_s_np, BF16)

    xp = x_prompt.reshape(batch * seq, d_model)
    xs = x_sample.reshape(dec_batch * dec_seq, d_model)
    outs = [[] for _ in range(6)]
    for l in range(depth):
        w_pack = _pack_in_weights(w_in[l])
        gc = g_conv[l].reshape(1, -1)
        gn_pad = _pad_head_lanes(g_nsa[l])
        w1k, w2k, pek = _pack_compress_weights(pe_k[l], w_ck1[l], w_ck2[l], True)
        w1v, w2v, pev = _pack_compress_weights(pe_v[l], w_cv1[l], w_cv2[l], False)
        w1 = jnp.stack([w1k, w1v])
        pe = jnp.stack([pek, pev])
        wa = _pad_head_rows(w_out[l][:d_nsa]).astype(BF16)
        wb = w_out[l][d_nsa:].astype(BF16)
        wr = jnp.concatenate(
            [jnp.pad(w_rg[l], ((0, 0), (0, LANE - N_GROUPS)))]
            + [jnp.pad(w_re[l][:, gi * N_EXP:(gi + 1) * N_EXP], ((0, 0), (0, LANE - N_EXP))) for gi in range(N_GROUPS)],
            axis=1).astype(BF16)
        br = jnp.concatenate(
            [jnp.pad(b_rg[l], (0, LANE - N_GROUPS))]
            + [jnp.pad(b_re[l][gi * N_EXP:(gi + 1) * N_EXP], (0, LANE - N_EXP)) for gi in range(N_GROUPS)]).reshape(1, -1)
        halves = lambda w: w.astype(BF16).reshape((N_GROUPS * EXPERT_SPLIT, N_EXP // EXPERT_SPLIT) + w.shape[2:])
        wg, wu, wd = halves(w_eg[l]), halves(w_eu[l]), halves(w_ed[l])
        g1, b1 = ln1_g[l].reshape(1, -1), ln1_b[l].reshape(1, -1)
        g2, b2 = ln2_g[l].reshape(1, -1), ln2_b[l].reshape(1, -1)

        q_p, kvt_p, ks, sv, kw, wv, wtail_t, gates_p, zn_p, ctail = _inproj_prompt(xp, w_pack, conv_w[l], gc, batch, seq)
        kvt_p = kvt_p.reshape(batch, 4, N_KV, HEAD_DIM, seq)
        cmpk_p, cmpvt_p = _compress_prompt(kvt_p, batch, seq, w1, w2k, w2v, pe)
        gn_rep = jnp.broadcast_to(gn_pad.reshape(-1, 1), (N_HEADS * LANE, TQ))
        on_p = _attn_prompt(q_p, gates_p, cmpk_p, cmpvt_p, ks, sv, kw, wv, near_t, up_t, tab_c, sp_p, gn_rep,
                            batch, seq)
        x1_p, gate_p = _proj(on_p, zn_p, xp, wa, wb, g1, b1, wr, br, alpha, TM)
        y_p = _moe_sorted(x1_p, gate_p, wg, wu, wd, g2, b2, alpha)

        st = state_conv[l]
        qlo_s, qgl_s, kv_s, win_s, gates_s, zn_s, u_s = _inproj_sample(xs, w_pack, conv_w[l], gc, st[:, 0], st[:, 1])
        cache_t = jnp.transpose(cache_kv[l], (0, 2, 3, 4, 1))
        cwin_t = jnp.transpose(cache_win[l], (0, 2, 3, 4, 1))
        cmpk_s, cmpvt_s = _compress_sample(cache_t, page_table, w1, w2k, w2v, pe)
        qlo_s3 = qlo_s.reshape(dec_batch, N_HEADS, LANE)
        qgl_s3 = qgl_s.reshape(dec_batch, N_HEADS, LANE)
        ocmp, owin, sel_idx = _sample_cw(qlo_s3, qgl_s3, cmpk_s, cmpvt_s, cwin_t, win_s.reshape(dec_batch, 1, 256), tab_c_s,
                                         tab_w_s, rb0_rep, sp_s, n_slc_s, qblk_s)
        bids = sel_idx[:, :, :N_SELECT].reshape(dec_batch, N_KV * N_SELECT)
        blk_pages = jnp.take_along_axis(page_table, jnp.minimum(bids // 2, n_pages - 1), axis=1)
        gates_s3 = jnp.pad(gates_s.reshape(dec_batch, N_KV, LANE)[:, :, :3 * GQ].reshape(dec_batch, N_HEADS, 3),
                           ((0, 0), (0, 0), (0, LANE - 3)))
        on_s = _sample_slc(blk_pages, bids % 2, bids, cache_t, qlo_s3, kv_s.reshape(dec_batch, 1, 512), ocmp, owin,
                           gates_s3, t_near, t_last, rb0_rep, gn_pad, n_slc_s - 1, qblk_s - 2)
        on_s = on_s.reshape(dec_batch, N_HEADS * LANE)
        x1_s, gate_s = _proj(on_s, zn_s, xs, wa, wb, g1, b1, wr, br, alpha, dec_batch)
        y_s = _moe(x1_s, gate_s, wg, wu, wd, g2, b2, alpha, dec_batch)

        to_token_major = lambda a: jnp.transpose(a, (0, 4, 1, 2, 3))
        outs[0].append(to_token_major(kvt_p))
        outs[1].append(kv_s.reshape(dec_batch, dec_seq, 4, N_KV, HEAD_DIM))
        outs[2].append(to_token_major(wtail_t.reshape(batch, 2, N_KV, HEAD_DIM, TM)))
        win_all_t = jnp.concatenate([cwin_t, win_s.reshape(dec_batch, 2, N_KV, HEAD_DIM, dec_seq)], axis=-1)
        outs[3].append(to_token_major(win_all_t[..., win_all_t.shape[-1] - min(WINDOW, past + dec_seq):]))
        outs[4].append(ctail)
        outs[5].append(jnp.concatenate([st, u_s[:, None, :]], axis=1)[:, dec_seq:])
        xp, xs = y_p, y_s
    return (xp.reshape(batch, seq, d_model), xs.reshape(dec_batch, dec_seq, d_model),
            jnp.stack(outs[0]), jnp.stack(outs[1]), jnp.stack(outs[2]), jnp.stack(outs[3]),
            jnp.stack(outs[4]), jnp.stack(outs[5]))
```
